```python
import math
import jax, jax.numpy as jnp
from jax import lax
import numpy as np

D_MODEL = 1024
BATCH = 8
SEQ = 2048
DEPTH = 2
DEC_BATCH = 128
DEC_SEQ = 1
PAST_LEN = 16384
PAGE_SIZE = 128

N_EVEN = (DEPTH + 1) // 2
N_ODD = DEPTH // 2
SSD_HEAD_DIM = 64
SSD_HEADS = 16
D_SSD = SSD_HEADS * SSD_HEAD_DIM
SSD_GROUPS = 4
SSD_HEADS_PER_GROUP = SSD_HEADS // SSD_GROUPS
SSD_STATE = 128
SSD_CONV = 4
SSD_CHUNK = 128
SSD_CONV_DIM = D_SSD + 2 * SSD_GROUPS * SSD_STATE
DT_MIN = 0.001
DT_MAX = 0.1
GMLP_CHUNK = 128
GMLP_GROUPS = 8
D_GMLP = D_MODEL
GMLP_GROUP_DIM = D_GMLP // GMLP_GROUPS
IN_DIM = D_SSD + SSD_CONV_DIM + SSD_HEADS + 2 * D_GMLP
MIX_DIM = D_SSD + D_GMLP
D_CONF = D_MODEL
CONF_KERNEL = 31
D_FF = 2816
N_EXPERTS = 8
TOP_K = 2
RMS_EPS = 1e-6
LN_EPS = 1e-5

kernel_name = "hybrid_ssd_gmlp_conformer_moe_step"


def rmsnorm(x, g):
    xf = x.astype(jnp.float32)
    y = xf * lax.rsqrt(jnp.mean(xf * xf, axis=-1, keepdims=True) + RMS_EPS)
    return (y * g.astype(jnp.float32)).astype(x.dtype)


def layernorm(x, g, b):
    xf = x.astype(jnp.float32)
    mu = jnp.mean(xf, axis=-1, keepdims=True)
    var = jnp.mean(jnp.square(xf - mu), axis=-1, keepdims=True)
    y = (xf - mu) * lax.rsqrt(var + LN_EPS) * g.astype(jnp.float32) + b.astype(jnp.float32)
    return y.astype(x.dtype)


def pad_len(a, new_len):
    return jnp.pad(a, [(0, 0), (0, new_len - a.shape[1])] + [(0, 0)] * (a.ndim - 2))


def causal_dwconv(x, buf, w, b):
    xp = jnp.concatenate([buf.astype(x.dtype), x], axis=1)
    y = lax.conv_general_dilated(xp, w[:, None, :].astype(x.dtype), window_strides=(1,), padding='VALID',
                                 dimension_numbers=('NWC', 'WIO', 'NWC'), feature_group_count=x.shape[-1])
    return y + b.astype(x.dtype), xp[:, xp.shape[1] - (w.shape[0] - 1):]


def segsum(a):
    t = a.shape[-1]
    cs = jnp.cumsum(a, axis=-1)
    seg = cs[..., :, None] - cs[..., None, :]
    return jnp.where(jnp.tril(jnp.ones((t, t), dtype=bool)), seg, -jnp.inf)


def ssd_chunked(xdt, a_dt, bm, cm, init_state):
    bsz, L = xdt.shape[:2]
    t = min(L, SSD_CHUNK)
    nc = -(-L // t)
    lp = nc * t
    xc = pad_len(xdt, lp).reshape(bsz, nc, t, SSD_GROUPS, SSD_HEADS_PER_GROUP, SSD_HEAD_DIM)
    ac = pad_len(a_dt, lp).reshape(bsz, nc, t, SSD_GROUPS, SSD_HEADS_PER_GROUP).transpose(0, 3, 4, 1, 2)
    bc = pad_len(bm, lp).reshape(bsz, nc, t, SSD_GROUPS, SSD_STATE)
    cc = pad_len(cm, lp).reshape(bsz, nc, t, SSD_GROUPS, SSD_STATE)
    a_cs = jnp.cumsum(ac, axis=-1)
    decay_in = jnp.exp(segsum(ac))
    cb = jnp.einsum('bclgn,bcsgn->bcgls', cc, bc)
    y_diag = jnp.einsum('bcgls,bgrcls,bcsgrp->bclgrp', cb, decay_in, xc)
    decay_to_end = jnp.exp(a_cs[..., -1:] - a_cs)
    chunk_states = jnp.einsum('bclgn,bgrcl,bclgrp->bcgrpn', bc, decay_to_end, xc)
    chunk_states = jnp.concatenate([init_state[:, None], chunk_states], axis=1)
    a_last = jnp.pad(a_cs[..., -1], [(0, 0), (0, 0), (0, 0), (1, 0)])
    decay_chunk = jnp.exp(segsum(a_last))
    states = jnp.einsum('bgrzc,bcgrpn->bzgrpn', decay_chunk, chunk_states)
    prev_states, final_state = states[:, :-1], states[:, -1]
    y_off = jnp.einsum('bclgn,bcgrpn,bgrcl->bclgrp', cc, prev_states, jnp.exp(a_cs))
    y = (y_diag + y_off).reshape(bsz, lp, SSD_GROUPS, SSD_HEADS_PER_GROUP, SSD_HEAD_DIM)[:, :L]
    return y, final_state


def gated_group_rmsnorm(y, z, g):
    bsz, L, _ = y.shape
    yf = y.astype(jnp.float32) * jax.nn.silu(z.astype(jnp.float32))
    yg = yf.reshape(bsz, L, SSD_GROUPS, D_SSD // SSD_GROUPS)
    yg = yg * lax.rsqrt(jnp.mean(yg * yg, axis=-1, keepdims=True) + RMS_EPS)
    return (yg.reshape(bsz, L, D_SSD) * g.astype(jnp.float32)).astype(z.dtype)


def chunk_gmlp(u, v, ln_g, ln_b, w_s, b_s):
    u = jax.nn.gelu(u)
    v = layernorm(jax.nn.gelu(v), ln_g, ln_b)
    bsz, L, _ = v.shape
    t = min(L, GMLP_CHUNK)
    nc = -(-L // t)
    lp = nc * t
    vp = pad_len(v, lp).reshape(bsz, nc, t, GMLP_GROUPS, GMLP_GROUP_DIM)
    ws = jnp.where(jnp.tril(jnp.ones((t, t), dtype=bool)), w_s[:, :t, :t], 0.0).astype(v.dtype)
    mixed = jnp.einsum('gij,bcjgd->bcigd', ws, vp) + b_s[:, :t].T.astype(v.dtype)[None, None, :, :, None]
    mixed = mixed.reshape(bsz, lp, D_GMLP)[:, :L]
    return u * mixed, v


def swiglu(h, wg, wu, wd):
    return (jax.nn.silu(h @ wg) * (h @ wu)) @ wd


def moe_swiglu(h, router, wg, wu, wd):
    logits = (h @ router).astype(jnp.float32)
    top_v, top_i = lax.top_k(logits, TOP_K)
    gates = jax.nn.softmax(top_v, axis=-1)
    out = jnp.zeros_like(h)
    for e in range(N_EXPERTS):
        gate_e = jnp.sum(jnp.where(top_i == e, gates, 0.0), axis=-1)[..., None].astype(h.dtype)
        out = out + gate_e * swiglu(h, wg[e], wu[e], wd[e])
    return out


def even_mixer(h, ssm_state, conv_buf, p, i):
    bsz, L, _ = h.shape
    f32 = jnp.float32
    proj = h @ p['w_in_even'][i]
    s1 = D_SSD
    s2 = s1 + SSD_CONV_DIM
    s3 = s2 + SSD_HEADS
    s4 = s3 + D_GMLP
    z, xbc, dt_raw, u, v = jnp.split(proj, [s1, s2, s3, s4], axis=-1)
    xbc, new_buf = causal_dwconv(xbc, conv_buf, p['ssd_conv_w'][i], p['ssd_conv_b'][i])
    xbc = jax.nn.silu(xbc)
    xs, bm, cm = jnp.split(xbc, [D_SSD, D_SSD + SSD_GROUPS * SSD_STATE], axis=-1)
    xs = xs.astype(f32).reshape(bsz, L, SSD_GROUPS, SSD_HEADS_PER_GROUP, SSD_HEAD_DIM)
    bm = bm.astype(f32).reshape(bsz, L, SSD_GROUPS, SSD_STATE)
    cm = cm.astype(f32).reshape(bsz, L, SSD_GROUPS, SSD_STATE)
    dt = jax.nn.softplus(dt_raw.astype(f32) + p['ssd_dt_bias'][i].astype(f32))
    dt = dt.reshape(bsz, L, SSD_GROUPS, SSD_HEADS_PER_GROUP)
    a = -jnp.exp(p['ssd_a_log'][i].astype(f32)).reshape(SSD_GROUPS, SSD_HEADS_PER_GROUP)
    init = ssm_state.astype(f32).reshape(bsz, SSD_GROUPS, SSD_HEADS_PER_GROUP, SSD_HEAD_DIM, SSD_STATE)
    y, final = ssd_chunked(xs * dt[..., None], dt * a, bm, cm, init)
    d_skip = p['ssd_d'][i].astype(f32).reshape(SSD_GROUPS, SSD_HEADS_PER_GROUP)[:, :, None]
    y = (y + d_skip * xs).reshape(bsz, L, D_SSD)
    y = gated_group_rmsnorm(y, z, p['ssd_norm'][i])
    g_out, v_rows = chunk_gmlp(u, v, p['gmlp_ln_g'][i], p['gmlp_ln_b'][i], p['gmlp_w_s'][i], p['gmlp_b_s'][i])
    mix = jnp.concatenate([y, g_out], axis=-1) @ p['w_out_even'][i]
    new_state = final.reshape(bsz, SSD_HEADS, SSD_HEAD_DIM, SSD_STATE).astype(ssm_state.dtype)
    return mix, new_state, new_buf.astype(conv_buf.dtype), v_rows


def conformer_conv(h, conv_buf, p, i):
    a = h @ p['conf_w1'][i] + p['conf_b1'][i]
    val, gate = jnp.split(a, 2, axis=-1)
    a = val * jax.nn.sigmoid(gate)
    c, new_buf = causal_dwconv(a, conv_buf, p['conf_dw_w'][i], p['conf_dw_b'][i])
    c = jax.nn.silu(layernorm(c, p['conf_ln_g'][i], p['conf_ln_b'][i]))
    return c @ p['conf_w2'][i] + p['conf_b2'][i], new_buf.astype(conv_buf.dtype)


def trunk(x, ssm_states, ssd_bufs, conf_bufs, p):
    new_ssm, new_ssd_buf, new_conf_buf, v_rows = [], [], [], []
    for layer in range(DEPTH):
        i = layer // 2
        if layer % 2 == 0:
            mix, s, cb, vr = even_mixer(rmsnorm(x, p['norm_mix_even'][i]), ssm_states[i], ssd_bufs[i], p, i)
            new_ssm.append(s)
            new_ssd_buf.append(cb)
            v_rows.append(vr)
            x = x + mix
            x = x + swiglu(rmsnorm(x, p['norm_ffn_even'][i]), p['ffn_w_gate'][i], p['ffn_w_up'][i], p['ffn_w_down'][i])
        else:
            mix, cb = conformer_conv(rmsnorm(x, p['norm_mix_odd'][i]), conf_bufs[i], p, i)
            new_conf_buf.append(cb)
            x = x + mix
            x = x + moe_swiglu(rmsnorm(x, p['norm_ffn_odd'][i]), p['moe_router'][i], p['moe_w_gate'][i],
                               p['moe_w_up'][i], p['moe_w_down'][i])
    y = rmsnorm(x, p['final_norm'])
    return y, jnp.stack(new_ssm), jnp.stack(new_ssd_buf), jnp.stack(new_conf_buf), jnp.stack(v_rows)


def setup_inputs(seed: int = 0) -> dict:
    key = jax.random.key(seed)
    ks = iter(jax.random.split(key, 48))

    def nrm(shape, scale):
        return scale * jax.random.normal(next(ks), shape, jnp.float32)

    def gain(shape):
        return 1.0 + nrm(shape, 0.05)

    dt0 = jnp.exp(jax.random.uniform(next(ks), (N_EVEN, SSD_HEADS), jnp.float32,
                                     minval=math.log(DT_MIN), maxval=math.log(DT_MAX)))
    dt_bias = dt0 + jnp.log(-jnp.expm1(-dt0))
    a_log = jnp.log(jax.random.uniform(next(ks), (N_EVEN, SSD_HEADS), jnp.float32, minval=1.0, maxval=16.0))
    return {
        'x_prompt': nrm((BATCH, SEQ, D_MODEL), 1.0),
        'x_sample': nrm((DEC_BATCH, DEC_SEQ, D_MODEL), 1.0),
        'state_ssm': nrm((N_EVEN, DEC_BATCH, SSD_HEADS, SSD_HEAD_DIM, SSD_STATE), 0.1),
        'state_ssd_conv': nrm((N_EVEN, DEC_BATCH, SSD_CONV - 1, SSD_CONV_DIM), 1.0),
        'state_conf_conv': nrm((N_ODD, DEC_BATCH, CONF_KERNEL - 1, D_CONF), 0.5),
        'norm_mix_even': gain((N_EVEN, D_MODEL)),
        'w_in_even': nrm((N_EVEN, D_MODEL, IN_DIM), D_MODEL ** -0.5),
        'ssd_conv_w': nrm((N_EVEN, SSD_CONV, SSD_CONV_DIM), SSD_CONV ** -0.5),
        'ssd_conv_b': nrm((N_EVEN, SSD_CONV_DIM), 0.02),
        'ssd_dt_bias': dt_bias,
        'ssd_a_log': a_log,
        'ssd_d': gain((N_EVEN, SSD_HEADS)),
        'ssd_norm': gain((N_EVEN, D_SSD)),
        'gmlp_ln_g': gain((N_EVEN, D_GMLP)),
        'gmlp_ln_b': nrm((N_EVEN, D_GMLP), 0.02),
        'gmlp_w_s': nrm((N_EVEN, GMLP_GROUPS, GMLP_CHUNK, GMLP_CHUNK), GMLP_CHUNK ** -0.5),
        'gmlp_b_s': 1.0 + nrm((N_EVEN, GMLP_GROUPS, GMLP_CHUNK), 0.1),
        'w_out_even': nrm((N_EVEN, MIX_DIM, D_MODEL), MIX_DIM ** -0.5),
        'norm_ffn_even': gain((N_EVEN, D_MODEL)),
        'ffn_w_gate': nrm((N_EVEN, D_MODEL, D_FF), D_MODEL ** -0.5),
        'ffn_w_up': nrm((N_EVEN, D_MODEL, D_FF), D_MODEL ** -0.5),
        'ffn_w_down': nrm((N_EVEN, D_FF, D_MODEL), D_FF ** -0.5),
        'norm_mix_odd': gain((N_ODD, D_MODEL)),
        'conf_w1': nrm((N_ODD, D_MODEL, 2 * D_CONF), D_MODEL ** -0.5),
        'conf_b1': nrm((N_ODD, 2 * D_CONF), 0.02),
        'conf_dw_w': nrm((N_ODD, CONF_KERNEL, D_CONF), CONF_KERNEL ** -0.5),
        'conf_dw_b': nrm((N_ODD, D_CONF), 0.02),
        'conf_ln_g': gain((N_ODD, D_CONF)),
        'conf_ln_b': nrm((N_ODD, D_CONF), 0.02),
        'conf_w2': nrm((N_ODD, D_CONF, D_MODEL), D_CONF ** -0.5),
        'conf_b2': nrm((N_ODD, D_MODEL), 0.02),
        'norm_ffn_odd': gain((N_ODD, D_MODEL)),
        'moe_router': nrm((N_ODD, D_MODEL, N_EXPERTS), D_MODEL ** -0.5),
        'moe_w_gate': nrm((N_ODD, N_EXPERTS, D_MODEL, D_FF), D_MODEL ** -0.5),
        'moe_w_up': nrm((N_ODD, N_EXPERTS, D_MODEL, D_FF), D_MODEL ** -0.5),
        'moe_w_down': nrm((N_ODD, N_EXPERTS, D_FF, D_MODEL), D_FF ** -0.5),
        'final_norm': gain((D_MODEL,)),
    }


def reference(x_prompt, x_sample, state_ssm, state_ssd_conv, state_conf_conv,
              norm_mix_even, w_in_even, ssd_conv_w, ssd_conv_b, ssd_dt_bias, ssd_a_log, ssd_d, ssd_norm,
              gmlp_ln_g, gmlp_ln_b, gmlp_w_s, gmlp_b_s, w_out_even,
              norm_ffn_even, ffn_w_gate, ffn_w_up, ffn_w_down,
              norm_mix_odd, conf_w1, conf_b1, conf_dw_w, conf_dw_b, conf_ln_g, conf_ln_b, conf_w2, conf_b2,
              norm_ffn_odd, moe_router, moe_w_gate, moe_w_up, moe_w_down, final_norm):
    p = dict(norm_mix_even=norm_mix_even, w_in_even=w_in_even, ssd_conv_w=ssd_conv_w, ssd_conv_b=ssd_conv_b,
             ssd_dt_bias=ssd_dt_bias, ssd_a_log=ssd_a_log, ssd_d=ssd_d, ssd_norm=ssd_norm,
             gmlp_ln_g=gmlp_ln_g, gmlp_ln_b=gmlp_ln_b, gmlp_w_s=gmlp_w_s, gmlp_b_s=gmlp_b_s,
             w_out_even=w_out_even, norm_ffn_even=norm_ffn_even, ffn_w_gate=ffn_w_gate, ffn_w_up=ffn_w_up,
             ffn_w_down=ffn_w_down, norm_mix_odd=norm_mix_odd, conf_w1=conf_w1, conf_b1=conf_b1,
             conf_dw_w=conf_dw_w, conf_dw_b=conf_dw_b, conf_ln_g=conf_ln_g, conf_ln_b=conf_ln_b,
             conf_w2=conf_w2, conf_b2=conf_b2, norm_ffn_odd=norm_ffn_odd, moe_router=moe_router,
             moe_w_gate=moe_w_gate, moe_w_up=moe_w_up, moe_w_down=moe_w_down, final_norm=final_norm)
    bp = x_prompt.shape[0]
    ssm0 = jnp.zeros((N_EVEN, bp) + state_ssm.shape[2:], state_ssm.dtype)
    ssd_buf0 = jnp.zeros((N_EVEN, bp) + state_ssd_conv.shape[2:], state_ssd_conv.dtype)
    conf_buf0 = jnp.zeros((N_ODD, bp) + state_conf_conv.shape[2:], state_conf_conv.dtype)
    y_prompt, ssm_p, ssd_conv_p, conf_conv_p, _v_prompt = trunk(x_prompt, ssm0, ssd_buf0, conf_buf0, p)
    y_sample, ssm_s, ssd_conv_s, conf_conv_s, gmlp_v_s = trunk(x_sample, state_ssm, state_ssd_conv,
                                                               state_conf_conv, p)
    return (y_prompt, y_sample, ssm_p, ssd_conv_p, conf_conv_p, ssm_s, ssd_conv_s, conf_conv_s, gmlp_v_s)
```

```python
import functools

import jax
import jax.numpy as jnp
from jax import lax
from jax.experimental import pallas as pl
from jax.experimental.pallas import tpu as pltpu

F32 = jnp.float32
BF16 = jnp.bfloat16
HIGHEST = lax.Precision.HIGHEST

SSD_HEAD_DIM = 64
SSD_GROUPS = 4
SSD_STATE = 128
SSD_CHUNK = 128
GMLP_CHUNK = 128
TOP_K = 2
RMS_EPS = 1e-6
LN_EPS = 1e-5

V7X_LANES = 128
V7X_SUBLANES = 8
V7X_VMEM_BYTES = 64 * 1024 * 1024
VMEM_LIMIT = (V7X_VMEM_BYTES * 7) // 8

MOE_TILE = 256
COMBINE_TILE = 128
CONF_TILE = 256
SAMPLE_STATE_TILE = 8


def _params(*semantics):
    return pltpu.CompilerParams(dimension_semantics=semantics, vmem_limit_bytes=VMEM_LIMIT)


def _resident(shape):
    zeros = (0,) * len(shape)
    return pl.BlockSpec(shape, lambda *_: zeros, pipeline_mode=pl.Buffered(1))


def _whole(shape):
    zeros = (0,) * len(shape)
    return pl.BlockSpec(shape, lambda *_: zeros)


def _row_tile(n_rows, preferred):
    for t in (preferred, 512, 256, 128):
        if t <= preferred and n_rows % t == 0:
            return t
    raise ValueError(f"row count {n_rows} is not a multiple of {V7X_LANES}")


def _dot(a, b):
    return jnp.dot(a, b, preferred_element_type=F32)


def _rms(x, g):
    return x * lax.rsqrt(jnp.mean(x * x, axis=-1, keepdims=True) + RMS_EPS) * g


def _layernorm(x, g, b):
    mu = jnp.mean(x, axis=-1, keepdims=True)
    xc = x - mu
    var = jnp.mean(xc * xc, axis=-1, keepdims=True)
    return xc * lax.rsqrt(var + LN_EPS) * g + b


def _silu(x):
    return x * jax.nn.sigmoid(x)


def _softplus(x):
    return jnp.maximum(x, 0.0) + jnp.log1p(jnp.exp(-jnp.abs(x)))


def _even_in_kernel(x_ref, g_ref, w_ref, lng_ref, lnb_ref, z_ref, xbc_ref, ug_ref, vn_ref, dt_ref,
                    *, d_ssd, conv_dim, d_gmlp):
    h = _rms(x_ref[...], g_ref[...]).astype(BF16)
    o1 = d_ssd
    o2 = o1 + conv_dim
    o3 = o2 + d_gmlp
    o4 = o3 + d_gmlp
    z_ref[...] = _dot(h, w_ref[:, 0:o1])
    xbc_ref[...] = _dot(h, w_ref[:, o1:o2])
    ug_ref[...] = jax.nn.gelu(_dot(h, w_ref[:, o2:o3])).astype(ug_ref.dtype)
    v = jax.nn.gelu(_dot(h, w_ref[:, o3:o4]))
    vn_ref[...] = _layernorm(v, lng_ref[...], lnb_ref[...])
    dt_ref[...] = _dot(h, w_ref[:, o4:o4 + V7X_LANES])


def _even_in(x, g, w, ln_g, ln_b, *, d_ssd, conv_dim, d_gmlp):
    t, d = x.shape
    tm = _row_tile(t, 256)
    row = lambda n: pl.BlockSpec((tm, n), lambda i: (i, 0))
    return pl.pallas_call(
        functools.partial(_even_in_kernel, d_ssd=d_ssd, conv_dim=conv_dim, d_gmlp=d_gmlp),
        grid=(t // tm,),
        in_specs=[row(d), _resident(g.shape), _resident(w.shape), _resident(ln_g.shape), _resident(ln_b.shape)],
        out_specs=[row(d_ssd), row(conv_dim), row(d_gmlp), row(d_gmlp), row(V7X_LANES)],
        out_shape=[jax.ShapeDtypeStruct((t, d_ssd), F32), jax.ShapeDtypeStruct((t, conv_dim), F32),
                   jax.ShapeDtypeStruct((t, d_gmlp), BF16), jax.ShapeDtypeStruct((t, d_gmlp), F32),
                   jax.ShapeDtypeStruct((t, V7X_LANES), F32)],
        compiler_params=_params("parallel"),
        name="even_in",
    )(x, g, w, ln_g, ln_b)


def _ssd_prompt_kernel(xbc_ref, z_ref, dt_ref, cw_ref, cb_ref, dtb_ref, alog_ref, dskip_ref, ng_ref,
                       yn_ref, st_ref, xp_ref, state_ref, *, d_ssd, n_taps):
    c = pl.program_id(1)
    n_pairs = state_ref.shape[0]
    ch = SSD_CHUNK
    gn = SSD_GROUPS * SSD_STATE

    @pl.when(c == 0)
    def _():
        state_ref[...] = jnp.zeros_like(state_ref)
        xp_ref[0:V7X_SUBLANES, :] = jnp.zeros((V7X_SUBLANES, xp_ref.shape[1]), F32)

    xp_ref[V7X_SUBLANES:V7X_SUBLANES + ch, :] = xbc_ref[...]
    base = V7X_SUBLANES - (n_taps - 1)
    acc = cb_ref[...] + cw_ref[0:1, :] * xp_ref[base:base + ch, :]
    for k in range(1, n_taps):
        acc = acc + cw_ref[k:k + 1, :] * xp_ref[base + k:base + k + ch, :]
    xp_ref[0:V7X_SUBLANES, :] = xp_ref[ch:ch + V7X_SUBLANES, :]
    xc = _silu(acc)

    dt = _softplus(dt_ref[...] + dtb_ref[...])
    a = dt * (-jnp.exp(alog_ref[...]))
    li = lax.broadcasted_iota(jnp.int32, (ch, ch), 0)
    si = lax.broadcasted_iota(jnp.int32, (ch, ch), 1)
    causal = li >= si
    tril = jnp.where(causal, 1.0, 0.0).astype(F32)
    acs = jnp.dot(tril, a, precision=HIGHEST, preferred_element_type=F32)
    acs_t = acs.T
    lane = lax.broadcasted_iota(jnp.int32, (ch, 2 * SSD_HEAD_DIM), 1)
    first = lane < SSD_HEAD_DIM
    first_n = lax.broadcasted_iota(jnp.int32, (SSD_STATE, 2 * SSD_HEAD_DIM), 1) < SSD_HEAD_DIM
    pairs_per_group = n_pairs // SSD_GROUPS

    for g in range(SSD_GROUPS):
        bg = xc[:, d_ssd + g * SSD_STATE:d_ssd + (g + 1) * SSD_STATE]
        cg = xc[:, d_ssd + gn + g * SSD_STATE:d_ssd + gn + (g + 1) * SSD_STATE]
        bg_t = bg.T
        cb = _dot(cg.astype(BF16), bg_t.astype(BF16))
        ys = []
        for q in range(pairs_per_group):
            pair = g * pairs_per_group + q
            h0 = 2 * pair
            lo = pair * 2 * SSD_HEAD_DIM
            xs = xc[:, lo:lo + 2 * SSD_HEAD_DIM]
            dt2 = jnp.where(first, dt[:, h0:h0 + 1], dt[:, h0 + 1:h0 + 2])
            xdt = (xs * dt2).astype(BF16)
            s_prev = state_ref[pair]
            s_prev_b = s_prev.astype(BF16)
            y2, snew2, dec2 = [], [], []
            for r in range(2):
                h = h0 + r
                col = acs[:, h:h + 1]
                row = acs_t[h:h + 1, :]
                last = acs_t[h:h + 1, ch - 1:ch]
                decay = jnp.exp(jnp.where(causal, col - row, -jnp.inf))
                y = _dot((cb * decay).astype(BF16), xdt)
                y = y + _dot((cg * jnp.exp(col)).astype(BF16), s_prev_b)
                snew = _dot((bg_t * jnp.exp(last - row)).astype(BF16), xdt)
                y2.append(y)
                snew2.append(snew)
                dec2.append(jnp.exp(last))
            y = jnp.where(first, y2[0], y2[1])
            state_ref[pair] = (s_prev * jnp.where(first_n, dec2[0], dec2[1])
                               + jnp.where(first_n, snew2[0], snew2[1]))
            ys.append(y + dskip_ref[:, lo:lo + 2 * SSD_HEAD_DIM] * xs)
        gw = pairs_per_group * 2 * SSD_HEAD_DIM
        yg = jnp.concatenate(ys, axis=-1) * _silu(z_ref[:, g * gw:(g + 1) * gw])
        yg = yg * lax.rsqrt(jnp.mean(yg * yg, axis=-1, keepdims=True) + RMS_EPS)
        yn_ref[:, g * gw:(g + 1) * gw] = (yg * ng_ref[:, g * gw:(g + 1) * gw]).astype(yn_ref.dtype)

    @pl.when(c == pl.num_programs(1) - 1)
    def _():
        st_ref[0] = state_ref[...]


def _ssd_prompt(xbc, z, dt, conv_w, conv_b, dt_bias, a_log, d_skip, norm_g, *, batch, seq):
    t, conv_dim = xbc.shape
    d_ssd = z.shape[1]
    n_pairs = d_ssd // (2 * SSD_HEAD_DIM)
    nc = seq // SSD_CHUNK
    n_taps = conv_w.shape[0]
    tile = lambda n: pl.BlockSpec((SSD_CHUNK, n), lambda b, c: (b * nc + c, 0))
    return pl.pallas_call(
        functools.partial(_ssd_prompt_kernel, d_ssd=d_ssd, n_taps=n_taps),
        grid=(batch, nc),
        in_specs=[tile(conv_dim), tile(d_ssd), tile(V7X_LANES), _resident(conv_w.shape), _resident(conv_b.shape),
                  _resident(dt_bias.shape), _resident(a_log.shape), _resident(d_skip.shape),
                  _resident(norm_g.shape)],
        out_specs=[tile(d_ssd),
                   pl.BlockSpec((1, n_pairs, SSD_STATE, 2 * SSD_HEAD_DIM), lambda b, c: (b, 0, 0, 0))],
        out_shape=[jax.ShapeDtypeStruct((t, d_ssd), BF16),
                   jax.ShapeDtypeStruct((batch, n_pairs, SSD_STATE, 2 * SSD_HEAD_DIM), F32)],
        scratch_shapes=[pltpu.VMEM((SSD_CHUNK + V7X_SUBLANES, conv_dim), F32),
                        pltpu.VMEM((n_pairs, SSD_STATE, 2 * SSD_HEAD_DIM), F32)],
        compiler_params=_params("parallel", "arbitrary"),
        name="ssd_prompt",
    )(xbc, z, dt, conv_w, conv_b, dt_bias, a_log, d_skip, norm_g)


def _gmlp_prompt_kernel(ug_ref, vn_ref, ws_ref, bs_ref, out_ref):
    ch = GMLP_CHUNK
    n_groups = ws_ref.shape[0]
    gd = vn_ref.shape[1] // n_groups
    ii = lax.broadcasted_iota(jnp.int32, (ch, ch), 0)
    jj = lax.broadcasted_iota(jnp.int32, (ch, ch), 1)
    for g in range(n_groups):
        ws = jnp.where(ii >= jj, ws_ref[g], 0.0).astype(BF16)
        mixed = _dot(ws, vn_ref[:, g * gd:(g + 1) * gd].astype(BF16)) + bs_ref[:, g:g + 1]
        out_ref[:, g * gd:(g + 1) * gd] = (ug_ref[:, g * gd:(g + 1) * gd].astype(F32) * mixed).astype(out_ref.dtype)


def _gmlp_prompt(ug, vn, w_s, b_s_t):
    t, d = vn.shape
    tile = pl.BlockSpec((GMLP_CHUNK, d), lambda i: (i, 0))
    return pl.pallas_call(
        _gmlp_prompt_kernel,
        grid=(t // GMLP_CHUNK,),
        in_specs=[tile, tile, _resident(w_s.shape), _resident(b_s_t.shape)],
        out_specs=tile,
        out_shape=jax.ShapeDtypeStruct((t, d), BF16),
        compiler_params=_params("parallel"),
        name="gmlp_prompt",
    )(ug, vn, w_s, b_s_t)


def _ssd_sample_prep_kernel(xbc_ref, buf_ref, dt_ref, cw_ref, cb_ref, dtb_ref, alog_ref, expand_ref,
                            newbuf_ref, xs_ref, xdt_ref, bc_ref, da_ref, *, d_ssd, n_taps):
    conv_dim = xbc_ref.shape[1]
    x_new = xbc_ref[...]
    acc = cb_ref[...] + cw_ref[n_taps - 1:n_taps, :] * x_new
    for k in range(n_taps - 1):
        acc = acc + cw_ref[k:k + 1, :] * buf_ref[:, k * conv_dim:(k + 1) * conv_dim]
    for k in range(n_taps - 2):
        newbuf_ref[:, k * conv_dim:(k + 1) * conv_dim] = buf_ref[:, (k + 1) * conv_dim:(k + 2) * conv_dim]
    newbuf_ref[:, (n_taps - 2) * conv_dim:(n_taps - 1) * conv_dim] = x_new
    xc = _silu(acc)
    xs = xc[:, :d_ssd]
    dt = _softplus(dt_ref[...] + dtb_ref[...])
    da_ref[...] = jnp.exp(dt * (-jnp.exp(alog_ref[...])))
    dt_wide = jnp.dot(dt, expand_ref[...], precision=HIGHEST, preferred_element_type=F32)
    xs_ref[...] = xs
    xdt_ref[...] = xs * dt_wide
    bc_ref[...] = xc[:, d_ssd:]


def _ssd_sample_prep(xbc, buf, dt, conv_w, conv_b, dt_bias, a_log, expand):
    n, conv_dim = xbc.shape
    d_ssd = expand.shape[1]
    n_taps = conv_w.shape[0]
    args = (xbc, buf, dt, conv_w, conv_b, dt_bias, a_log, expand)
    return pl.pallas_call(
        functools.partial(_ssd_sample_prep_kernel, d_ssd=d_ssd, n_taps=n_taps),
        grid=(1,),
        in_specs=[_resident(a.shape) for a in args],
        out_specs=[_whole(buf.shape), _whole((n, d_ssd)), _whole((n, d_ssd)),
                   _whole((n, conv_dim - d_ssd)), _whole((n, V7X_LANES))],
        out_shape=[jax.ShapeDtypeStruct(buf.shape, F32), jax.ShapeDtypeStruct((n, d_ssd), F32),
                   jax.ShapeDtypeStruct((n, d_ssd), F32), jax.ShapeDtypeStruct((n, conv_dim - d_ssd), F32),
                   jax.ShapeDtypeStruct((n, V7X_LANES), F32)],
        compiler_params=_params("arbitrary"),
        name="ssd_sample_prep",
    )(*args)


def _ssd_sample_state_kernel(h_ref, xdt_ref, da_ref, b_ref, c_ref, hnew_ref, y_ref):
    n_heads = h_ref.shape[1]
    per_group = n_heads // SSD_GROUPS
    for g in range(SSD_GROUPS):
        hs = slice(g * per_group, (g + 1) * per_group)
        h_new = h_ref[:, hs] * da_ref[:, hs] + xdt_ref[:, hs] * b_ref[:, g:g + 1]
        hnew_ref[:, hs] = h_new
        y_ref[:, hs] = jnp.sum(h_new * c_ref[:, g:g + 1], axis=-1, keepdims=True)


def _ssd_sample_state(h, xdt4, da4, b4, c4):
    n, n_heads, p, s = h.shape
    bs = SAMPLE_STATE_TILE
    blk = lambda shape: pl.BlockSpec((bs,) + shape, lambda i: (i, 0, 0, 0))
    return pl.pallas_call(
        _ssd_sample_state_kernel,
        grid=(n // bs,),
        in_specs=[blk((n_heads, p, s)), blk((n_heads, p, 1)), blk((n_heads, 1, 1)),
                  blk((SSD_GROUPS, 1, s)), blk((SSD_GROUPS, 1, s))],
        out_specs=[blk((n_heads, p, s)), blk((n_heads, p, 1))],
        out_shape=[jax.ShapeDtypeStruct(h.shape, F32), jax.ShapeDtypeStruct((n, n_heads, p, 1), F32)],
        compiler_params=_params("parallel"),
        name="ssd_sample_state",
    )(h, xdt4, da4, b4, c4)


def _even_sample_post_kernel(y_ref, xs_ref, z_ref, dskip_ref, ng_ref, ug_ref, vn_ref, w0_ref, b0_ref,
                             yn_ref, gout_ref):
    d_ssd = y_ref.shape[1]
    gw = d_ssd // SSD_GROUPS
    y = (y_ref[...] + dskip_ref[...] * xs_ref[...]) * _silu(z_ref[...])
    for g in range(SSD_GROUPS):
        yg = y[:, g * gw:(g + 1) * gw]
        yg = yg * lax.rsqrt(jnp.mean(yg * yg, axis=-1, keepdims=True) + RMS_EPS)
        yn_ref[:, g * gw:(g + 1) * gw] = (yg * ng_ref[:, g * gw:(g + 1) * gw]).astype(yn_ref.dtype)
    mixed = w0_ref[...] * vn_ref[...] + b0_ref[...]
    gout_ref[...] = (ug_ref[...].astype(F32) * mixed).astype(gout_ref.dtype)


def _even_sample_post(y, xs, z, d_skip, norm_g, ug, vn, w0, b0):
    args = (y, xs, z, d_skip, norm_g, ug, vn, w0, b0)
    return pl.pallas_call(
        _even_sample_post_kernel,
        grid=(1,),
        in_specs=[_resident(a.shape) for a in args],
        out_specs=[_whole(y.shape), _whole(vn.shape)],
        out_shape=[jax.ShapeDtypeStruct(y.shape, BF16), jax.ShapeDtypeStruct(vn.shape, BF16)],
        compiler_params=_params("arbitrary"),
        name="even_sample_post",
    )(*args)


def _even_out_kernel(x_ref, yn_ref, gout_ref, wo_ref, g_ref, wg_ref, wu_ref, wd_ref, out_ref):
    d_ssd = yn_ref.shape[1]
    x = x_ref[...] + _dot(yn_ref[...], wo_ref[0:d_ssd, :]) + _dot(gout_ref[...], wo_ref[d_ssd:, :])
    h = _rms(x, g_ref[...]).astype(BF16)
    mid = (_silu(_dot(h, wg_ref[...])) * _dot(h, wu_ref[...])).astype(BF16)
    out_ref[...] = x + _dot(mid, wd_ref[...])


def _even_out(x, yn, gout, w_out, g, wg, wu, wd):
    t, d = x.shape
    tm = _row_tile(t, 256)
    row = lambda n: pl.BlockSpec((tm, n), lambda i: (i, 0))
    return pl.pallas_call(
        _even_out_kernel,
        grid=(t // tm,),
        in_specs=[row(d), row(yn.shape[1]), row(gout.shape[1]), _resident(w_out.shape), _resident(g.shape),
                  _resident(wg.shape), _resident(wu.shape), _resident(wd.shape)],
        out_specs=row(d),
        out_shape=jax.ShapeDtypeStruct((t, d), F32),
        compiler_params=_params("parallel"),
        name="even_out_ffn",
    )(x, yn, gout, w_out, g, wg, wu, wd)


def _conf_in_kernel(x_ref, g_ref, w_ref, b_ref, a_ref):
    d = a_ref.shape[1]
    h = _rms(x_ref[...], g_ref[...]).astype(BF16)
    val = _dot(h, w_ref[:, 0:d]) + b_ref[:, 0:d]
    gate = _dot(h, w_ref[:, d:]) + b_ref[:, d:]
    a_ref[...] = val * jax.nn.sigmoid(gate)


def _conf_in(x, g, w1, b1):
    t, d = x.shape
    dc = w1.shape[1] // 2
    tm = _row_tile(t, 512)
    return pl.pallas_call(
        _conf_in_kernel,
        grid=(t // tm,),
        in_specs=[pl.BlockSpec((tm, d), lambda i: (i, 0)), _resident(g.shape), _resident(w1.shape),
                  _resident(b1.shape)],
        out_specs=pl.BlockSpec((tm, dc), lambda i: (i, 0)),
        out_shape=jax.ShapeDtypeStruct((t, dc), F32),
        compiler_params=_params("parallel"),
        name="conf_in",
    )(x, g, w1, b1)


def _route(logits, n_experts, route_ref):
    lane = lax.broadcasted_iota(jnp.int32, logits.shape, 1)
    lane_f = lane.astype(F32)
    big = jnp.float32(V7X_LANES)
    lg = jnp.where(lane < n_experts, logits, -jnp.inf)
    m1 = jnp.max(lg, axis=-1, keepdims=True)
    i1 = jnp.min(jnp.where(lg == m1, lane_f, big), axis=-1, keepdims=True)
    lg2 = jnp.where(lane_f == i1, -jnp.inf, lg)
    m2 = jnp.max(lg2, axis=-1, keepdims=True)
    i2 = jnp.min(jnp.where(lg2 == m2, lane_f, big), axis=-1, keepdims=True)
    e2 = jnp.exp(m2 - m1)
    g1 = 1.0 / (1.0 + e2)
    g2 = e2 / (1.0 + e2)
    route_ref[...] = jnp.where(lane == 0, i1, jnp.where(lane == 1, i2, jnp.where(lane == 2, g1,
                               jnp.where(lane == 3, g2, 0.0))))


def _conf_finish(c, x, lng_ref, lnb_ref, w2_ref, b2_ref, ng_ref, router_ref, out_ref, hn_ref, route_ref,
                 *, n_experts):
    hmid = _silu(_layernorm(c, lng_ref[...], lnb_ref[...])).astype(BF16)
    x_new = x + _dot(hmid, w2_ref[...]) + b2_ref[...]
    out_ref[...] = x_new
    hn = _rms(x_new, ng_ref[...])
    hn_ref[...] = hn
    logits = jnp.dot(hn, router_ref[...], precision=HIGHEST, preferred_element_type=F32)
    _route(logits, n_experts, route_ref)


def _conf_prompt_kernel(a_ref, x_ref, dw_ref, db_ref, lng_ref, lnb_ref, w2_ref, b2_ref, ng_ref, router_ref,
                        out_ref, hn_ref, route_ref, xp_ref, c_ref, *, n_taps, n_experts):
    tl, d = a_ref.shape
    halo = xp_ref.shape[0] - tl
    base = halo - (n_taps - 1)

    @pl.when(pl.program_id(1) == 0)
    def _():
        xp_ref[0:halo, :] = jnp.zeros((halo, d), F32)

    xp_ref[halo:halo + tl, :] = a_ref[...]

    def lane_block(j, carry):
        cols = pl.ds(pl.multiple_of(j * V7X_LANES, V7X_LANES), V7X_LANES)
        acc = db_ref[:, cols] + dw_ref[0:1, cols] * xp_ref[base:base + tl, cols]
        for k in range(1, n_taps):
            acc = acc + dw_ref[k:k + 1, cols] * xp_ref[base + k:base + k + tl, cols]
        c_ref[:, cols] = acc
        return carry

    lax.fori_loop(0, d // V7X_LANES, lane_block, 0)
    xp_ref[0:halo, :] = xp_ref[tl:tl + halo, :]
    _conf_finish(c_ref[...], x_ref[...], lng_ref, lnb_ref, w2_ref, b2_ref, ng_ref, router_ref,
                 out_ref, hn_ref, route_ref, n_experts=n_experts)


def _conf_prompt(a, x, dw_w, dw_b, ln_g, ln_b, w2, b2, ng, router, *, batch, seq, n_experts):
    t, d = a.shape
    tl = _row_tile(seq, CONF_TILE)
    nt = seq // tl
    n_taps = dw_w.shape[0]
    halo = -(-(n_taps - 1) // V7X_SUBLANES) * V7X_SUBLANES
    tile = lambda n: pl.BlockSpec((tl, n), lambda b, i: (b * nt + i, 0))
    consts = (dw_w, dw_b, ln_g, ln_b, w2, b2, ng, router)
    return pl.pallas_call(
        functools.partial(_conf_prompt_kernel, n_taps=n_taps, n_experts=n_experts),
        grid=(batch, nt),
        in_specs=[tile(d), tile(d)] + [_resident(c.shape) for c in consts],
        out_specs=[tile(d), tile(d), tile(V7X_LANES)],
        out_shape=[jax.ShapeDtypeStruct((t, d), F32), jax.ShapeDtypeStruct((t, d), F32),
                   jax.ShapeDtypeStruct((t, V7X_LANES), F32)],
        scratch_shapes=[pltpu.VMEM((tl + halo, d), F32), pltpu.VMEM((tl, d), F32)],
        compiler_params=_params("parallel", "arbitrary"),
        name="conf_prompt",
    )(a, x, *consts)


def _conf_sample_kernel(a_ref, buf_ref, x_ref, dw_ref, db_ref, lng_ref, lnb_ref, w2_ref, b2_ref, ng_ref,
                        router_ref, newbuf_ref, out_ref, hn_ref, route_ref, *, n_taps, n_experts):
    d = a_ref.shape[1]
    a_new = a_ref[...]
    acc = db_ref[...] + dw_ref[n_taps - 1:n_taps, :] * a_new
    for k in range(n_taps - 1):
        acc = acc + dw_ref[k:k + 1, :] * buf_ref[:, k * d:(k + 1) * d]
    for k in range(n_taps - 2):
        newbuf_ref[:, k * d:(k + 1) * d] = buf_ref[:, (k + 1) * d:(k + 2) * d]
    newbuf_ref[:, (n_taps - 2) * d:(n_taps - 1) * d] = a_new
    _conf_finish(acc, x_ref[...], lng_ref, lnb_ref, w2_ref, b2_ref, ng_ref, router_ref,
                 out_ref, hn_ref, route_ref, n_experts=n_experts)


def _conf_sample(a, buf, x, dw_w, dw_b, ln_g, ln_b, w2, b2, ng, router, *, n_experts):
    n, d = a.shape
    args = (a, buf, x, dw_w, dw_b, ln_g, ln_b, w2, b2, ng, router)
    return pl.pallas_call(
        functools.partial(_conf_sample_kernel, n_taps=dw_w.shape[0], n_experts=n_experts),
        grid=(1,),
        in_specs=[_resident(v.shape) for v in args],
        out_specs=[_whole(buf.shape), _whole((n, d)), _whole((n, d)), _whole((n, V7X_LANES))],
        out_shape=[jax.ShapeDtypeStruct(buf.shape, F32), jax.ShapeDtypeStruct((n, d), F32),
                   jax.ShapeDtypeStruct((n, d), F32), jax.ShapeDtypeStruct((n, V7X_LANES), F32)],
        compiler_params=_params("arbitrary"),
        name="conf_sample",
    )(*args)


def _row_copy(src_hbm, row, dst_vmem, slot, sem):
    return pltpu.make_async_copy(src_hbm.at[pl.ds(row, 1), :], dst_vmem.at[pl.ds(slot, 1), :], sem)


def _moe_ffn_kernel(tile_expert_ref, n_tiles_ref, rows_ref, hn_hbm, wg_ref, wu_ref, wd_ref, out_ref,
                    x_vmem, sem):
    i = pl.program_id(0)
    tm = x_vmem.shape[0]

    @pl.when(i < n_tiles_ref[0])
    def _():
        def issue(r, carry):
            _row_copy(hn_hbm, rows_ref[0, 0, r], x_vmem, r, sem).start()
            return carry

        lax.fori_loop(0, tm, issue, 0)
        pltpu.make_async_copy(hn_hbm.at[pl.ds(0, tm), :], x_vmem, sem).wait()
        xb = x_vmem[...].astype(BF16)
        half = wg_ref.shape[2] // 2
        acc = None
        for j in range(2):
            cols = slice(j * half, (j + 1) * half)
            mid = (_silu(_dot(xb, wg_ref[0, :, cols])) * _dot(xb, wu_ref[0, :, cols])).astype(BF16)
            part = _dot(mid, wd_ref[0, cols, :])
            acc = part if acc is None else acc + part
        out_ref[...] = acc

    @pl.when(i >= n_tiles_ref[0])
    def _():
        out_ref[...] = jnp.zeros_like(out_ref)


def _moe_ffn(tile_expert, n_tiles, rows, hn, wg, wu, wd):
    nt = tile_expert.shape[0]
    tm = rows.shape[2]
    d = hn.shape[1]
    dff = wg.shape[2]
    grid_spec = pltpu.PrefetchScalarGridSpec(
        num_scalar_prefetch=2,
        grid=(nt,),
        in_specs=[
            pl.BlockSpec((1, 1, tm), lambda i, te, n: (i, 0, 0), memory_space=pltpu.SMEM),
            pl.BlockSpec(memory_space=pl.ANY),
            pl.BlockSpec((1, d, dff), lambda i, te, n: (te[i], 0, 0)),
            pl.BlockSpec((1, d, dff), lambda i, te, n: (te[i], 0, 0)),
            pl.BlockSpec((1, dff, d), lambda i, te, n: (te[i], 0, 0)),
        ],
        out_specs=pl.BlockSpec((tm, d), lambda i, te, n: (i, 0)),
        scratch_shapes=[pltpu.VMEM((tm, d), F32), pltpu.SemaphoreType.DMA(())],
    )
    return pl.pallas_call(
        _moe_ffn_kernel,
        grid_spec=grid_spec,
        out_shape=jax.ShapeDtypeStruct((nt * tm, d), F32),
        compiler_params=_params("arbitrary"),
        name="moe_ffn",
    )(tile_expert, n_tiles, rows, hn, wg, wu, wd)


def _moe_combine_kernel(pos_ref, x_ref, route_ref, gf_ref, ys_hbm, y_ref, a_vmem, b_vmem, sem):
    tc = x_ref.shape[0]

    def issue(r, carry):
        _row_copy(ys_hbm, pos_ref[0, 0, r], a_vmem, r, sem).start()
        _row_copy(ys_hbm, pos_ref[0, 1, r], b_vmem, r, sem).start()
        return carry

    lax.fori_loop(0, tc, issue, 0)
    pltpu.make_async_copy(ys_hbm.at[pl.ds(0, tc), :], a_vmem, sem).wait()
    pltpu.make_async_copy(ys_hbm.at[pl.ds(0, tc), :], b_vmem, sem).wait()
    route = route_ref[...]
    x = x_ref[...] + route[:, 2:3] * a_vmem[...] + route[:, 3:4] * b_vmem[...]
    y_ref[...] = _rms(x, gf_ref[...])


def _moe_combine(pos, x, route, gf, ys):
    t, d = x.shape
    tc = pos.shape[2]
    return pl.pallas_call(
        _moe_combine_kernel,
        grid=(t // tc,),
        in_specs=[pl.BlockSpec((1, 2, tc), lambda i: (i, 0, 0), memory_space=pltpu.SMEM),
                  pl.BlockSpec((tc, d), lambda i: (i, 0)),
                  pl.BlockSpec((tc, V7X_LANES), lambda i: (i, 0)),
                  _resident(gf.shape),
                  pl.BlockSpec(memory_space=pl.ANY)],
        out_specs=pl.BlockSpec((tc, d), lambda i: (i, 0)),
        out_shape=jax.ShapeDtypeStruct((t, d), F32),
        scratch_shapes=[pltpu.VMEM((tc, d), F32), pltpu.VMEM((tc, d), F32), pltpu.SemaphoreType.DMA(())],
        compiler_params=_params("arbitrary"),
        name="moe_combine",
    )(pos, x, route, gf, ys)


def _moe_plan(route, n_experts, tm):
    t = route.shape[0]
    n_slots = t * TOP_K
    expert = route[:, :TOP_K].astype(jnp.int32).reshape(n_slots)
    onehot = (expert[:, None] == jnp.arange(n_experts, dtype=jnp.int32)[None, :]).astype(jnp.int32)
    rank = jnp.sum((jnp.cumsum(onehot, axis=0) - onehot) * onehot, axis=1)
    counts = jnp.sum(onehot, axis=0)
    tiles = (counts + tm - 1) // tm
    tile_end = jnp.cumsum(tiles)
    start = (tile_end - tiles) * tm
    pos = start[expert] + rank
    nt = n_slots // tm + n_experts
    n_tiles = tile_end[-1]
    tile_id = jnp.minimum(jnp.arange(nt, dtype=jnp.int32), n_tiles - 1)
    tile_expert = jnp.sum((tile_id[:, None] >= tile_end[None, :]).astype(jnp.int32), axis=1)
    token = jnp.arange(n_slots, dtype=jnp.int32) // TOP_K
    rows = jnp.zeros((nt * tm,), jnp.int32).at[pos].set(token)
    return (tile_expert.astype(jnp.int32), n_tiles.astype(jnp.int32).reshape(1),
            rows.reshape(nt, 1, tm), pos.reshape(t, TOP_K).astype(jnp.int32))


def _row(v):
    return v.reshape(1, -1).astype(F32)


def _pad_lanes(v):
    return jnp.pad(_row(v), ((0, 0), (0, V7X_LANES - v.shape[-1])))


def kernel(x_prompt, x_sample, state_ssm, state_ssd_conv, state_conf_conv, norm_mix_even, w_in_even, ssd_conv_w, ssd_conv_b, ssd_dt_bias, ssd_a_log, ssd_d, ssd_norm, gmlp_ln_g, gmlp_ln_b, gmlp_w_s, gmlp_b_s, w_out_even, norm_ffn_even, ffn_w_gate, ffn_w_up, ffn_w_down, norm_mix_odd, conf_w1, conf_b1, conf_dw_w, conf_dw_b, conf_ln_g, conf_ln_b, conf_w2, conf_b2, norm_ffn_odd, moe_router, moe_w_gate, moe_w_up, moe_w_down, final_norm):
    batch, seq, d_model = x_prompt.shape
    n_dec, dec_seq, _ = x_sample.shape
    assert dec_seq == 1, "the sample group advances one token per sequence"
    assert seq % SSD_CHUNK == 0 and n_dec % V7X_LANES == 0
    n_even, n_odd = w_in_even.shape[0], conf_w1.shape[0]
    assert (n_even, n_odd) == (1, 1), "the final norm is fused into the last (odd) layer's MoE combine"
    n_heads = ssd_dt_bias.shape[1]
    d_ssd = n_heads * SSD_HEAD_DIM
    conv_dim = ssd_conv_w.shape[2]
    d_gmlp = gmlp_ln_g.shape[1]
    n_groups_gmlp = gmlp_w_s.shape[1]
    n_experts = moe_router.shape[2]
    tp = batch * seq

    xp = x_prompt.reshape(tp, d_model)
    xs = x_sample.reshape(n_dec, d_model)
    outs = dict(ssm_p=[], conv_p=[], conf_p=[], ssm_s=[], conv_s=[], conf_s=[], v_s=[])

    for layer in range(n_even + n_odd):
        i = layer // 2
        if layer % 2 == 0:
            w = w_in_even[i]
            o1, o2, o3 = d_ssd, d_ssd + conv_dim, d_ssd + conv_dim + n_heads
            w_r = jnp.concatenate([w[:, :o2], w[:, o3:], jnp.pad(w[:, o2:o3], ((0, 0), (0, V7X_LANES - n_heads)))],
                                  axis=1).astype(BF16)
            in_args = (_row(norm_mix_even[i]), w_r, _row(gmlp_ln_g[i]), _row(gmlp_ln_b[i]))
            dims = dict(d_ssd=d_ssd, conv_dim=conv_dim, d_gmlp=d_gmlp)
            dt_bias, a_log = _pad_lanes(ssd_dt_bias[i]), _pad_lanes(ssd_a_log[i])
            d_skip = _row(jnp.repeat(ssd_d[i], SSD_HEAD_DIM))
            norm_g = _row(ssd_norm[i])
            conv_w, conv_b = ssd_conv_w[i], _row(ssd_conv_b[i])
            z, xbc, ug, vn, dt = _even_in(xp, *in_args, **dims)
            yn, st = _ssd_prompt(xbc, z, dt, conv_w, conv_b, dt_bias, a_log, d_skip, norm_g, batch=batch, seq=seq)
            gout = _gmlp_prompt(ug, vn, gmlp_w_s[i], gmlp_b_s[i].T)
            n_pairs = n_heads // 2
            st = st.reshape(batch, n_pairs, SSD_STATE, 2, SSD_HEAD_DIM).transpose(0, 1, 3, 4, 2)
            outs['ssm_p'].append(st.reshape(batch, n_heads, SSD_HEAD_DIM, SSD_STATE))
            outs['conv_p'].append(xbc.reshape(batch, seq, conv_dim)[:, seq - (conv_w.shape[0] - 1):])
            ffn = (w_out_even[i].astype(BF16), _row(norm_ffn_even[i]), ffn_w_gate[i].astype(BF16),
                   ffn_w_up[i].astype(BF16), ffn_w_down[i].astype(BF16))
            xp = _even_out(xp, yn, gout, *ffn)
            z, xbc, ug, vn, dt = _even_in(xs, *in_args, **dims)
            expand = (jnp.arange(d_ssd)[None, :] // SSD_HEAD_DIM == jnp.arange(V7X_LANES)[:, None]).astype(F32)
            buf = state_ssd_conv[i].reshape(n_dec, -1)
            newbuf, xc, xdt, bc, da = _ssd_sample_prep(xbc, buf, dt, conv_w, conv_b, dt_bias, a_log, expand)
            gn = SSD_GROUPS * SSD_STATE
            h_new, y4 = _ssd_sample_state(
                state_ssm[i],
                xdt.reshape(n_dec, n_heads, SSD_HEAD_DIM, 1),
                da[:, :n_heads].reshape(n_dec, n_heads, 1, 1),
                bc[:, :gn].reshape(n_dec, SSD_GROUPS, 1, SSD_STATE),
                bc[:, gn:].reshape(n_dec, SSD_GROUPS, 1, SSD_STATE))
            gd = d_gmlp // n_groups_gmlp
            w0 = _row(jnp.repeat(gmlp_w_s[i][:, 0, 0], gd))
            b0 = _row(jnp.repeat(gmlp_b_s[i][:, 0], gd))
            yn, gout = _even_sample_post(y4.reshape(n_dec, d_ssd), xc, z, d_skip, norm_g, ug, vn, w0, b0)
            outs['ssm_s'].append(h_new)
            outs['conv_s'].append(newbuf.reshape(state_ssd_conv[i].shape))
            outs['v_s'].append(vn.reshape(n_dec, 1, d_gmlp))
            xs = _even_out(xs, yn, gout, *ffn)
        else:
            w1, b1 = conf_w1[i].astype(BF16), _row(conf_b1[i])
            router = jnp.pad(moe_router[i], ((0, 0), (0, V7X_LANES - n_experts)))
            tail = (conf_dw_w[i], _row(conf_dw_b[i]), _row(conf_ln_g[i]), _row(conf_ln_b[i]),
                    conf_w2[i].astype(BF16), _row(conf_b2[i]), _row(norm_ffn_odd[i]), router)
            n_keep = conf_dw_w.shape[1] - 1
            a = _conf_in(xp, _row(norm_mix_odd[i]), w1, b1)
            outs['conf_p'].append(a.reshape(batch, seq, -1)[:, seq - n_keep:])
            xp, hn_p, route_p = _conf_prompt(a, xp, *tail, batch=batch, seq=seq, n_experts=n_experts)
            a = _conf_in(xs, _row(norm_mix_odd[i]), w1, b1)
            buf = state_conf_conv[i].reshape(n_dec, -1)
            newbuf, xs, hn_s, route_s = _conf_sample(a, buf, xs, *tail, n_experts=n_experts)
            outs['conf_s'].append(newbuf.reshape(state_conf_conv[i].shape))
            x_all = jnp.concatenate([xp, xs], axis=0)
            hn = jnp.concatenate([hn_p, hn_s], axis=0)
            route = jnp.concatenate([route_p, route_s], axis=0)
            t_all = x_all.shape[0]
            tile_expert, n_tiles, rows, pos = _moe_plan(route, n_experts, MOE_TILE)
            ys = _moe_ffn(tile_expert, n_tiles, rows, hn, moe_w_gate[i].astype(BF16), moe_w_up[i].astype(BF16),
                          moe_w_down[i].astype(BF16))
            tc = COMBINE_TILE
            pos_t = pos.reshape(t_all // tc, tc, TOP_K).transpose(0, 2, 1)
            y_all = _moe_combine(pos_t, x_all, route, _row(final_norm), ys)
            xp, xs = y_all[:tp], y_all[tp:]

    y_prompt = xp.reshape(batch, seq, d_model)
    y_sample = xs.reshape(n_dec, 1, d_model)
    return (y_prompt, y_sample, jnp.stack(outs['ssm_p']), jnp.stack(outs['conv_p']), jnp.stack(outs['conf_p']),
            jnp.stack(outs['ssm_s']), jnp.stack(outs['conv_s']), jnp.stack(outs['conf_s']), jnp.stack(outs['v_s']))
```

```python
import functools

import jax
import jax.numpy as jnp
from jax import lax
from jax.experimental import pallas as pl
from jax.experimental.pallas import tpu as pltpu
from jax.experimental.pallas import tpu_sc as plsc

F32 = jnp.float32
BF16 = jnp.bfloat16
HIGHEST = lax.Precision.HIGHEST

SSD_HEAD_DIM = 64
SSD_GROUPS = 4
SSD_STATE = 128
SSD_CHUNK = 128
GMLP_CHUNK = 128
TOP_K = 2
RMS_EPS = 1e-6
LN_EPS = 1e-5

V7X_LANES = 128
V7X_SUBLANES = 8
V7X_VMEM_BYTES = 64 * 1024 * 1024
VMEM_LIMIT = (V7X_VMEM_BYTES * 7) // 8

MOE_TILE = 256
COMBINE_TILE = 128
CONF_TILE = 256
CONF_ACC_ROWS = 128
SC_GATHER_ROWS = 32
SAMPLE_STATE_TILE = 8


def _params(*semantics):
    return pltpu.CompilerParams(dimension_semantics=semantics, vmem_limit_bytes=VMEM_LIMIT)


def _resident(shape):
    zeros = (0,) * len(shape)
    return pl.BlockSpec(shape, lambda *_: zeros, pipeline_mode=pl.Buffered(1))


def _whole(shape):
    zeros = (0,) * len(shape)
    return pl.BlockSpec(shape, lambda *_: zeros)


def _row_tile(n_rows, preferred):
    for t in (preferred, 512, 256, 128):
        if t <= preferred and n_rows % t == 0:
            return t
    raise ValueError(f"row count {n_rows} is not a multiple of {V7X_LANES}")


def _dot(a, b):
    return jnp.dot(a, b, preferred_element_type=F32)


def _rms(x, g):
    return x * lax.rsqrt(jnp.mean(x * x, axis=-1, keepdims=True) + RMS_EPS) * g


def _layernorm(x, g, b):
    mu = jnp.mean(x, axis=-1, keepdims=True)
    xc = x - mu
    var = jnp.mean(xc * xc, axis=-1, keepdims=True)
    return xc * lax.rsqrt(var + LN_EPS) * g + b


def _silu(x):
    return x * jax.nn.sigmoid(x)


def _softplus(x):
    return jnp.maximum(x, 0.0) + jnp.log1p(jnp.exp(-jnp.abs(x)))


def _even_in_kernel(x_ref, g_ref, w_ref, lng_ref, lnb_ref, z_ref, xbc_ref, ug_ref, vn_ref, dt_ref,
                    *, d_ssd, conv_dim, d_gmlp):
    h = _rms(x_ref[...], g_ref[...]).astype(BF16)
    o1 = d_ssd
    o2 = o1 + conv_dim
    o3 = o2 + d_gmlp
    o4 = o3 + d_gmlp
    z_ref[...] = _dot(h, w_ref[:, 0:o1])
    xbc_ref[...] = _dot(h, w_ref[:, o1:o2])
    ug_ref[...] = jax.nn.gelu(_dot(h, w_ref[:, o2:o3])).astype(ug_ref.dtype)
    v = jax.nn.gelu(_dot(h, w_ref[:, o3:o4]))
    vn_ref[...] = _layernorm(v, lng_ref[...], lnb_ref[...])
    dt_ref[...] = _dot(h, w_ref[:, o4:o4 + V7X_LANES])


def _even_in(x, g, w, ln_g, ln_b, *, d_ssd, conv_dim, d_gmlp):
    t, d = x.shape
    tm = _row_tile(t, 256)
    row = lambda n: pl.BlockSpec((tm, n), lambda i: (i, 0))
    return pl.pallas_call(
        functools.partial(_even_in_kernel, d_ssd=d_ssd, conv_dim=conv_dim, d_gmlp=d_gmlp),
        grid=(t // tm,),
        in_specs=[row(d), _resident(g.shape), _resident(w.shape), _resident(ln_g.shape), _resident(ln_b.shape)],
        out_specs=[row(d_ssd), row(conv_dim), row(d_gmlp), row(d_gmlp), row(V7X_LANES)],
        out_shape=[jax.ShapeDtypeStruct((t, d_ssd), F32), jax.ShapeDtypeStruct((t, conv_dim), F32),
                   jax.ShapeDtypeStruct((t, d_gmlp), BF16), jax.ShapeDtypeStruct((t, d_gmlp), F32),
                   jax.ShapeDtypeStruct((t, V7X_LANES), F32)],
        compiler_params=_params("parallel"),
        name="even_in",
    )(x, g, w, ln_g, ln_b)


def _ssd_prompt_kernel(xbc_ref, z_ref, dt_ref, cw_ref, cb_ref, dtb_ref, alog_ref, dskip_ref, ng_ref,
                       yn_ref, st_ref, xp_ref, state_ref, *, d_ssd, n_taps):
    c = pl.program_id(1)
    n_pairs = state_ref.shape[0]
    ch = SSD_CHUNK
    gn = SSD_GROUPS * SSD_STATE

    @pl.when(c == 0)
    def _():
        state_ref[...] = jnp.zeros_like(state_ref)
        xp_ref[0:V7X_SUBLANES, :] = jnp.zeros((V7X_SUBLANES, xp_ref.shape[1]), F32)

    xp_ref[V7X_SUBLANES:V7X_SUBLANES + ch, :] = xbc_ref[...]
    base = V7X_SUBLANES - (n_taps - 1)
    acc = cb_ref[...] + cw_ref[0:1, :] * xp_ref[base:base + ch, :]
    for k in range(1, n_taps):
        acc = acc + cw_ref[k:k + 1, :] * xp_ref[base + k:base + k + ch, :]
    xp_ref[0:V7X_SUBLANES, :] = xp_ref[ch:ch + V7X_SUBLANES, :]
    xc = _silu(acc)

    dt = _softplus(dt_ref[...] + dtb_ref[...])
    a = dt * (-jnp.exp(alog_ref[...]))
    li = lax.broadcasted_iota(jnp.int32, (ch, ch), 0)
    si = lax.broadcasted_iota(jnp.int32, (ch, ch), 1)
    causal = li >= si
    tril = jnp.where(causal, 1.0, 0.0).astype(F32)
    acs = jnp.dot(tril, a, precision=HIGHEST, preferred_element_type=F32)
    acs_t = acs.T
    lane = lax.broadcasted_iota(jnp.int32, (ch, 2 * SSD_HEAD_DIM), 1)
    first = lane < SSD_HEAD_DIM
    first_n = lax.broadcasted_iota(jnp.int32, (SSD_STATE, 2 * SSD_HEAD_DIM), 1) < SSD_HEAD_DIM
    pairs_per_group = n_pairs // SSD_GROUPS

    for g in range(SSD_GROUPS):
        bg = xc[:, d_ssd + g * SSD_STATE:d_ssd + (g + 1) * SSD_STATE]
        cg = xc[:, d_ssd + gn + g * SSD_STATE:d_ssd + gn + (g + 1) * SSD_STATE]
        bg_t = bg.T
        cb = _dot(cg.astype(BF16), bg_t.astype(BF16))
        ys = []
        for q in range(pairs_per_group):
            pair = g * pairs_per_group + q
            h0 = 2 * pair
            lo = pair * 2 * SSD_HEAD_DIM
            xs = xc[:, lo:lo + 2 * SSD_HEAD_DIM]
            dt2 = jnp.where(first, dt[:, h0:h0 + 1], dt[:, h0 + 1:h0 + 2])
            xdt = (xs * dt2).astype(BF16)
            s_prev = state_ref[pair]
            s_prev_b = s_prev.astype(BF16)
            y2, snew2, dec2 = [], [], []
            for r in range(2):
                h = h0 + r
                col = acs[:, h:h + 1]
                row = acs_t[h:h + 1, :]
                last = acs_t[h:h + 1, ch - 1:ch]
                decay = jnp.exp(jnp.where(causal, col - row, -jnp.inf))
                y = _dot((cb * decay).astype(BF16), xdt)
                y = y + _dot((cg * jnp.exp(col)).astype(BF16), s_prev_b)
                snew = _dot((bg_t * jnp.exp(last - row)).astype(BF16), xdt)
                y2.append(y)
                snew2.append(snew)
                dec2.append(jnp.exp(last))
            y = jnp.where(first, y2[0], y2[1])
            state_ref[pair] = (s_prev * jnp.where(first_n, dec2[0], dec2[1])
                               + jnp.where(first_n, snew2[0], snew2[1]))
            ys.append(y + dskip_ref[:, lo:lo + 2 * SSD_HEAD_DIM] * xs)
        gw = pairs_per_group * 2 * SSD_HEAD_DIM
        yg = jnp.concatenate(ys, axis=-1) * _silu(z_ref[:, g * gw:(g + 1) * gw])
        yg = yg * lax.rsqrt(jnp.mean(yg * yg, axis=-1, keepdims=True) + RMS_EPS)
        yn_ref[:, g * gw:(g + 1) * gw] = (yg * ng_ref[:, g * gw:(g + 1) * gw]).astype(yn_ref.dtype)

    @pl.when(c == pl.num_programs(1) - 1)
    def _():
        st_ref[0] = state_ref[...]


def _ssd_prompt(xbc, z, dt, conv_w, conv_b, dt_bias, a_log, d_skip, norm_g, *, batch, seq):
    t, conv_dim = xbc.shape
    d_ssd = z.shape[1]
    n_pairs = d_ssd // (2 * SSD_HEAD_DIM)
    nc = seq // SSD_CHUNK
    n_taps = conv_w.shape[0]
    tile = lambda n: pl.BlockSpec((SSD_CHUNK, n), lambda b, c: (b * nc + c, 0))
    return pl.pallas_call(
        functools.partial(_ssd_prompt_kernel, d_ssd=d_ssd, n_taps=n_taps),
        grid=(batch, nc),
        in_specs=[tile(conv_dim), tile(d_ssd), tile(V7X_LANES), _resident(conv_w.shape), _resident(conv_b.shape),
                  _resident(dt_bias.shape), _resident(a_log.shape), _resident(d_skip.shape),
                  _resident(norm_g.shape)],
        out_specs=[tile(d_ssd),
                   pl.BlockSpec((1, n_pairs, SSD_STATE, 2 * SSD_HEAD_DIM), lambda b, c: (b, 0, 0, 0))],
        out_shape=[jax.ShapeDtypeStruct((t, d_ssd), BF16),
                   jax.ShapeDtypeStruct((batch, n_pairs, SSD_STATE, 2 * SSD_HEAD_DIM), F32)],
        scratch_shapes=[pltpu.VMEM((SSD_CHUNK + V7X_SUBLANES, conv_dim), F32),
                        pltpu.VMEM((n_pairs, SSD_STATE, 2 * SSD_HEAD_DIM), F32)],
        compiler_params=_params("parallel", "arbitrary"),
        name="ssd_prompt",
    )(xbc, z, dt, conv_w, conv_b, dt_bias, a_log, d_skip, norm_g)


def _gmlp_prompt_kernel(ug_ref, vn_ref, ws_ref, bs_ref, out_ref):
    ch = GMLP_CHUNK
    n_groups = ws_ref.shape[0]
    gd = vn_ref.shape[1] // n_groups
    ii = lax.broadcasted_iota(jnp.int32, (ch, ch), 0)
    jj = lax.broadcasted_iota(jnp.int32, (ch, ch), 1)
    for g in range(n_groups):
        ws = jnp.where(ii >= jj, ws_ref[g], 0.0).astype(BF16)
        mixed = _dot(ws, vn_ref[:, g * gd:(g + 1) * gd].astype(BF16)) + bs_ref[:, g:g + 1]
        out_ref[:, g * gd:(g + 1) * gd] = (ug_ref[:, g * gd:(g + 1) * gd].astype(F32) * mixed).astype(out_ref.dtype)


def _gmlp_prompt(ug, vn, w_s, b_s_t):
    t, d = vn.shape
    tile = pl.BlockSpec((GMLP_CHUNK, d), lambda i: (i, 0))
    return pl.pallas_call(
        _gmlp_prompt_kernel,
        grid=(t // GMLP_CHUNK,),
        in_specs=[tile, tile, _resident(w_s.shape), _resident(b_s_t.shape)],
        out_specs=tile,
        out_shape=jax.ShapeDtypeStruct((t, d), BF16),
        compiler_params=_params("parallel"),
        name="gmlp_prompt",
    )(ug, vn, w_s, b_s_t)


def _ssd_sample_prep_kernel(xbc_ref, buf_ref, dt_ref, cw_ref, cb_ref, dtb_ref, alog_ref, expand_ref,
                            newbuf_ref, xs_ref, xdt_ref, bc_ref, da_ref, *, d_ssd, n_taps):
    conv_dim = xbc_ref.shape[1]
    x_new = xbc_ref[...]
    acc = cb_ref[...] + cw_ref[n_taps - 1:n_taps, :] * x_new
    for k in range(n_taps - 1):
        acc = acc + cw_ref[k:k + 1, :] * buf_ref[:, k * conv_dim:(k + 1) * conv_dim]
    for k in range(n_taps - 2):
        newbuf_ref[:, k * conv_dim:(k + 1) * conv_dim] = buf_ref[:, (k + 1) * conv_dim:(k + 2) * conv_dim]
    newbuf_ref[:, (n_taps - 2) * conv_dim:(n_taps - 1) * conv_dim] = x_new
    xc = _silu(acc)
    xs = xc[:, :d_ssd]
    dt = _softplus(dt_ref[...] + dtb_ref[...])
    da_ref[...] = jnp.exp(dt * (-jnp.exp(alog_ref[...])))
    dt_wide = jnp.dot(dt, expand_ref[...], precision=HIGHEST, preferred_element_type=F32)
    xs_ref[...] = xs
    xdt_ref[...] = xs * dt_wide
    bc_ref[...] = xc[:, d_ssd:]


def _ssd_sample_prep(xbc, buf, dt, conv_w, conv_b, dt_bias, a_log, expand):
    n, conv_dim = xbc.shape
    d_ssd = expand.shape[1]
    n_taps = conv_w.shape[0]
    args = (xbc, buf, dt, conv_w, conv_b, dt_bias, a_log, expand)
    return pl.pallas_call(
        functools.partial(_ssd_sample_prep_kernel, d_ssd=d_ssd, n_taps=n_taps),
        grid=(1,),
        in_specs=[_resident(a.shape) for a in args],
        out_specs=[_whole(buf.shape), _whole((n, d_ssd)), _whole((n, d_ssd)),
                   _whole((n, conv_dim - d_ssd)), _whole((n, V7X_LANES))],
        out_shape=[jax.ShapeDtypeStruct(buf.shape, F32), jax.ShapeDtypeStruct((n, d_ssd), F32),
                   jax.ShapeDtypeStruct((n, d_ssd), F32), jax.ShapeDtypeStruct((n, conv_dim - d_ssd), F32),
                   jax.ShapeDtypeStruct((n, V7X_LANES), F32)],
        compiler_params=_params("arbitrary"),
        name="ssd_sample_prep",
    )(*args)


def _ssd_sample_state_kernel(h_ref, xdt_ref, da_ref, b_ref, c_ref, hnew_ref, y_ref):
    n_heads = h_ref.shape[1]
    per_group = n_heads // SSD_GROUPS
    for g in range(SSD_GROUPS):
        hs = slice(g * per_group, (g + 1) * per_group)
        h_new = h_ref[:, hs] * da_ref[:, hs] + xdt_ref[:, hs] * b_ref[:, g:g + 1]
        hnew_ref[:, hs] = h_new
        y_ref[:, hs] = jnp.sum(h_new * c_ref[:, g:g + 1], axis=-1, keepdims=True)


def _ssd_sample_state(h, xdt4, da4, b4, c4):
    n, n_heads, p, s = h.shape
    bs = SAMPLE_STATE_TILE
    blk = lambda shape: pl.BlockSpec((bs,) + shape, lambda i: (i, 0, 0, 0))
    return pl.pallas_call(
        _ssd_sample_state_kernel,
        grid=(n // bs,),
        in_specs=[blk((n_heads, p, s)), blk((n_heads, p, 1)), blk((n_heads, 1, 1)),
                  blk((SSD_GROUPS, 1, s)), blk((SSD_GROUPS, 1, s))],
        out_specs=[blk((n_heads, p, s)), blk((n_heads, p, 1))],
        out_shape=[jax.ShapeDtypeStruct(h.shape, F32), jax.ShapeDtypeStruct((n, n_heads, p, 1), F32)],
        compiler_params=_params("parallel"),
        name="ssd_sample_state",
    )(h, xdt4, da4, b4, c4)


def _even_sample_post_kernel(y_ref, xs_ref, z_ref, dskip_ref, ng_ref, ug_ref, vn_ref, w0_ref, b0_ref,
                             yn_ref, gout_ref):
    d_ssd = y_ref.shape[1]
    gw = d_ssd // SSD_GROUPS
    y = (y_ref[...] + dskip_ref[...] * xs_ref[...]) * _silu(z_ref[...])
    for g in range(SSD_GROUPS):
        yg = y[:, g * gw:(g + 1) * gw]
        yg = yg * lax.rsqrt(jnp.mean(yg * yg, axis=-1, keepdims=True) + RMS_EPS)
        yn_ref[:, g * gw:(g + 1) * gw] = (yg * ng_ref[:, g * gw:(g + 1) * gw]).astype(yn_ref.dtype)
    mixed = w0_ref[...] * vn_ref[...] + b0_ref[...]
    gout_ref[...] = (ug_ref[...].astype(F32) * mixed).astype(gout_ref.dtype)


def _even_sample_post(y, xs, z, d_skip, norm_g, ug, vn, w0, b0):
    args = (y, xs, z, d_skip, norm_g, ug, vn, w0, b0)
    return pl.pallas_call(
        _even_sample_post_kernel,
        grid=(1,),
        in_specs=[_resident(a.shape) for a in args],
        out_specs=[_whole(y.shape), _whole(vn.shape)],
        out_shape=[jax.ShapeDtypeStruct(y.shape, BF16), jax.ShapeDtypeStruct(vn.shape, BF16)],
        compiler_params=_params("arbitrary"),
        name="even_sample_post",
    )(*args)


def _even_out_kernel(x_ref, yn_ref, gout_ref, wo_ref, g_ref, wg_ref, wu_ref, wd_ref, out_ref):
    d_ssd = yn_ref.shape[1]
    x = x_ref[...] + _dot(yn_ref[...], wo_ref[0:d_ssd, :]) + _dot(gout_ref[...], wo_ref[d_ssd:, :])
    h = _rms(x, g_ref[...]).astype(BF16)
    mid = (_silu(_dot(h, wg_ref[...])) * _dot(h, wu_ref[...])).astype(BF16)
    out_ref[...] = x + _dot(mid, wd_ref[...])


def _even_out(x, yn, gout, w_out, g, wg, wu, wd):
    t, d = x.shape
    tm = _row_tile(t, 256)
    row = lambda n: pl.BlockSpec((tm, n), lambda i: (i, 0))
    return pl.pallas_call(
        _even_out_kernel,
        grid=(t // tm,),
        in_specs=[row(d), row(yn.shape[1]), row(gout.shape[1]), _resident(w_out.shape), _resident(g.shape),
                  _resident(wg.shape), _resident(wu.shape), _resident(wd.shape)],
        out_specs=row(d),
        out_shape=jax.ShapeDtypeStruct((t, d), F32),
        compiler_params=_params("parallel"),
        name="even_out_ffn",
    )(x, yn, gout, w_out, g, wg, wu, wd)


def _conf_in_kernel(x_ref, g_ref, w_ref, b_ref, a_ref):
    d = a_ref.shape[1]
    h = _rms(x_ref[...], g_ref[...]).astype(BF16)
    val = _dot(h, w_ref[:, 0:d]) + b_ref[:, 0:d]
    gate = _dot(h, w_ref[:, d:]) + b_ref[:, d:]
    a_ref[...] = val * jax.nn.sigmoid(gate)


def _conf_in(x, g, w1, b1):
    t, d = x.shape
    dc = w1.shape[1] // 2
    tm = _row_tile(t, 512)
    return pl.pallas_call(
        _conf_in_kernel,
        grid=(t // tm,),
        in_specs=[pl.BlockSpec((tm, d), lambda i: (i, 0)), _resident(g.shape), _resident(w1.shape),
                  _resident(b1.shape)],
        out_specs=pl.BlockSpec((tm, dc), lambda i: (i, 0)),
        out_shape=jax.ShapeDtypeStruct((t, dc), F32),
        compiler_params=_params("parallel"),
        name="conf_in",
    )(x, g, w1, b1)


def _route(logits, n_experts, route_ref):
    lane = lax.broadcasted_iota(jnp.int32, logits.shape, 1)
    lane_f = lane.astype(F32)
    big = jnp.float32(V7X_LANES)
    lg = jnp.where(lane < n_experts, logits, -jnp.inf)
    m1 = jnp.max(lg, axis=-1, keepdims=True)
    i1 = jnp.min(jnp.where(lg == m1, lane_f, big), axis=-1, keepdims=True)
    lg2 = jnp.where(lane_f == i1, -jnp.inf, lg)
    m2 = jnp.max(lg2, axis=-1, keepdims=True)
    i2 = jnp.min(jnp.where(lg2 == m2, lane_f, big), axis=-1, keepdims=True)
    e2 = jnp.exp(m2 - m1)
    g1 = 1.0 / (1.0 + e2)
    g2 = e2 / (1.0 + e2)
    route_ref[...] = jnp.where(lane == 0, i1, jnp.where(lane == 1, i2, jnp.where(lane == 2, g1,
                               jnp.where(lane == 3, g2, 0.0))))


def _conf_finish(c, x, lng_ref, lnb_ref, w2_ref, b2_ref, ng_ref, rhi_ref, rlo_ref, out_ref, hn_ref, route_ref,
                 *, n_experts):
    hmid = _silu(_layernorm(c, lng_ref[...], lnb_ref[...])).astype(BF16)
    x_new = x + _dot(hmid, w2_ref[...]) + b2_ref[...]
    out_ref[...] = x_new
    hn = _rms(x_new, ng_ref[...])
    hn_ref[...] = hn
    hi = hn.astype(BF16)
    lo = (hn - hi.astype(F32)).astype(BF16)
    logits = _dot(hi, rhi_ref[...]) + (_dot(lo, rhi_ref[...]) + _dot(hi, rlo_ref[...]))
    _route(logits, n_experts, route_ref)


def _conf_prompt_kernel(a_ref, x_ref, xs_ref, hns_ref, routes_ref, dw_ref, db_ref, lng_ref, lnb_ref, w2_ref, b2_ref,
                        ng_ref, rhi_ref, rlo_ref, out_ref, hn_ref, route_ref, xp_ref, c_ref,
                        *, n_taps, n_experts, tiles_per_seq, n_prompt_tiles):
    tl, d = a_ref.shape
    halo = xp_ref.shape[0] - tl
    base = halo - (n_taps - 1)
    rows = CONF_ACC_ROWS
    step = pl.program_id(0)

    @pl.when(step < n_prompt_tiles)
    def _():
        @pl.when(step % tiles_per_seq == 0)
        def _():
            xp_ref[0:halo, :] = jnp.zeros((halo, d), F32)

        xp_ref[halo:halo + tl, :] = a_ref[...]

        def lane_block(j, carry):
            cols = pl.ds(pl.multiple_of(j * V7X_LANES, V7X_LANES), V7X_LANES)
            for r0 in range(0, tl, rows):
                acc = jnp.broadcast_to(db_ref[:, cols], (rows, V7X_LANES))
                for phase in range(V7X_SUBLANES):
                    taps = [k for k in range(n_taps) if (base + k) % V7X_SUBLANES == phase]
                    if not taps:
                        continue
                    span = max(base + k - phase for k in taps) + rows
                    win = xp_ref[pl.ds(r0 + phase, span), cols]
                    for k in taps:
                        o = base + k - phase
                        acc = acc + dw_ref[k:k + 1, cols] * win[o:o + rows]
                c_ref[r0:r0 + rows, cols] = acc
            return carry

        lax.fori_loop(0, d // V7X_LANES, lane_block, 0)
        xp_ref[0:halo, :] = xp_ref[tl:tl + halo, :]
        _conf_finish(c_ref[...], x_ref[...], lng_ref, lnb_ref, w2_ref, b2_ref, ng_ref, rhi_ref, rlo_ref,
                     out_ref, hn_ref, route_ref, n_experts=n_experts)

    @pl.when(step == n_prompt_tiles)
    def _():
        ns = xs_ref.shape[0]
        out_ref[0:ns, :] = xs_ref[...]
        hn_ref[0:ns, :] = hns_ref[...]
        route_ref[0:ns, :] = routes_ref[...]


def _conf_prompt(a, x, sample_rows, consts, *, seq, n_experts):
    t, d = a.shape
    n_s = sample_rows[0].shape[0]
    tl = _row_tile(seq, CONF_TILE)
    assert tl % CONF_ACC_ROWS == 0 and n_s <= tl
    n_prompt_tiles = t // tl
    n_taps = consts[0].shape[0]
    halo = -(-(n_taps - 1) // V7X_SUBLANES) * V7X_SUBLANES
    in_tile = lambda n: pl.BlockSpec((tl, n), lambda s: (jnp.minimum(s, n_prompt_tiles - 1), 0))
    out_tile = lambda n: pl.BlockSpec((tl, n), lambda s: (s, 0))
    small = sample_rows + tuple(consts)
    return pl.pallas_call(
        functools.partial(_conf_prompt_kernel, n_taps=n_taps, n_experts=n_experts, tiles_per_seq=seq // tl,
                          n_prompt_tiles=n_prompt_tiles),
        grid=(n_prompt_tiles + 1,),
        in_specs=[in_tile(d), in_tile(d)] + [_resident(c.shape) for c in small],
        out_specs=[out_tile(d), out_tile(d), out_tile(V7X_LANES)],
        out_shape=[jax.ShapeDtypeStruct((t + n_s, d), F32), jax.ShapeDtypeStruct((t + n_s, d), F32),
                   jax.ShapeDtypeStruct((t + n_s, V7X_LANES), F32)],
        scratch_shapes=[pltpu.VMEM((tl + halo, d), F32), pltpu.VMEM((tl, d), F32)],
        compiler_params=_params("arbitrary"),
        name="conf_prompt",
    )(a, x, *small)


def _conf_sample_kernel(a_ref, buf_ref, x_ref, dw_ref, db_ref, lng_ref,
                        lnb_ref, w2_ref, b2_ref, ng_ref, rhi_ref, rlo_ref, newbuf_ref, out_ref, hn_ref, route_ref,
                        *, n_taps, n_experts):
    d = a_ref.shape[1]
    a_new = a_ref[...]
    acc = db_ref[...] + dw_ref[n_taps - 1:n_taps, :] * a_new
    for k in range(n_taps - 1):
        acc = acc + dw_ref[k:k + 1, :] * buf_ref[:, k * d:(k + 1) * d]
    for k in range(n_taps - 2):
        newbuf_ref[:, k * d:(k + 1) * d] = buf_ref[:, (k + 1) * d:(k + 2) * d]
    newbuf_ref[:, (n_taps - 2) * d:(n_taps - 1) * d] = a_new
    _conf_finish(acc, x_ref[...], lng_ref, lnb_ref, w2_ref, b2_ref, ng_ref, rhi_ref, rlo_ref,
                 out_ref, hn_ref, route_ref, n_experts=n_experts)


def _conf_sample(a, buf, x, consts, *, n_experts):
    n, d = a.shape
    args = (a, buf, x) + tuple(consts)
    return pl.pallas_call(
        functools.partial(_conf_sample_kernel, n_taps=consts[0].shape[0], n_experts=n_experts),
        grid=(1,),
        in_specs=[_resident(v.shape) for v in args],
        out_specs=[_whole(buf.shape), _whole((n, d)), _whole((n, d)), _whole((n, V7X_LANES))],
        out_shape=[jax.ShapeDtypeStruct(buf.shape, F32), jax.ShapeDtypeStruct((n, d), F32),
                   jax.ShapeDtypeStruct((n, d), F32), jax.ShapeDtypeStruct((n, V7X_LANES), F32)],
        compiler_params=_params("arbitrary"),
        name="conf_sample",
    )(*args)


def _sc_gather_rows(x, idx):
    n, d, w = idx.shape[0], x.shape[1], SC_GATHER_ROWS
    assert n % w == 0
    mesh = plsc.VectorSubcoreMesh(core_axis_name="core", subcore_axis_name="subcore")

    @functools.partial(pl.kernel, out_type=jax.ShapeDtypeStruct((n, d), x.dtype), mesh=mesh, scratch_types=[])
    def gather(x_hbm, idx_hbm, out_hbm):
        def body(idx_vmem, out_vmem):
            pltpu.sync_copy(x_hbm.at[idx_vmem.at[0]], out_vmem)

        pltpu.emit_pipeline(
            body,
            grid=(n // w,),
            in_specs=[pl.BlockSpec((1, w), lambda i: (i, 0))],
            out_specs=[pl.BlockSpec((w, d), lambda i: (i, 0))],
            core_axis_name=("core", "subcore"),
            dimension_semantics=(pltpu.PARALLEL,),
        )(idx_hbm, out_hbm)

    return gather(x, idx.reshape(n // w, w))


def _moe_ffn_kernel(tile_expert_ref, n_tiles_ref, x_ref, wg_ref, wu_ref, wd_ref, out_ref):
    del tile_expert_ref
    i = pl.program_id(0)

    @pl.when(i < n_tiles_ref[0])
    def _():
        xb = x_ref[...].astype(BF16)
        half = wg_ref.shape[2] // 2
        acc = None
        for j in range(2):
            cols = slice(j * half, (j + 1) * half)
            mid = (_silu(_dot(xb, wg_ref[0, :, cols])) * _dot(xb, wu_ref[0, :, cols])).astype(BF16)
            part = _dot(mid, wd_ref[0, cols, :])
            acc = part if acc is None else acc + part
        out_ref[...] = acc

    @pl.when(i >= n_tiles_ref[0])
    def _():
        out_ref[...] = jnp.zeros_like(out_ref)


def _moe_ffn(tile_expert, n_tiles, xs, wg, wu, wd, *, tm):
    p, d = xs.shape
    dff = wg.shape[2]
    grid_spec = pltpu.PrefetchScalarGridSpec(
        num_scalar_prefetch=2,
        grid=(p // tm,),
        in_specs=[
            pl.BlockSpec((tm, d), lambda i, te, n: (i, 0)),
            pl.BlockSpec((1, d, dff), lambda i, te, n: (te[i], 0, 0)),
            pl.BlockSpec((1, d, dff), lambda i, te, n: (te[i], 0, 0)),
            pl.BlockSpec((1, dff, d), lambda i, te, n: (te[i], 0, 0)),
        ],
        out_specs=pl.BlockSpec((tm, d), lambda i, te, n: (i, 0)),
    )
    return pl.pallas_call(
        _moe_ffn_kernel,
        grid_spec=grid_spec,
        out_shape=jax.ShapeDtypeStruct((p, d), F32),
        compiler_params=_params("arbitrary"),
        name="moe_ffn",
    )(tile_expert, n_tiles, xs, wg, wu, wd)


def _moe_combine_kernel(x_ref, route_ref, gf_ref, a_ref, b_ref, yp_ref, ys_ref, *, n_prompt_tiles):
    i = pl.program_id(0)
    route = route_ref[...]
    x = x_ref[...] + route[:, 2:3] * a_ref[...] + route[:, 3:4] * b_ref[...]
    y = _rms(x, gf_ref[...])

    @pl.when(i < n_prompt_tiles)
    def _():
        yp_ref[...] = y

    @pl.when(i >= n_prompt_tiles)
    def _():
        ys_ref[...] = y


def _moe_combine(x, route, gf, picked, *, n_prompt, n_sample):
    t, d = x.shape
    tc = COMBINE_TILE
    assert n_sample == tc and n_prompt % tc == 0
    n_prompt_tiles = n_prompt // tc
    n_tiles = t // tc
    return pl.pallas_call(
        functools.partial(_moe_combine_kernel, n_prompt_tiles=n_prompt_tiles),
        grid=(n_tiles,),
        in_specs=[pl.BlockSpec((tc, d), lambda i: (i, 0)),
                  pl.BlockSpec((tc, V7X_LANES), lambda i: (i, 0)),
                  _resident(gf.shape),
                  pl.BlockSpec((tc, d), lambda i: (i, 0)),
                  pl.BlockSpec((tc, d), lambda i: (i + n_tiles, 0))],
        out_specs=[pl.BlockSpec((tc, d), lambda i: (jnp.minimum(i, n_prompt_tiles - 1), 0)),
                   pl.BlockSpec((tc, d), lambda i: (0, 0))],
        out_shape=[jax.ShapeDtypeStruct((n_prompt, d), F32), jax.ShapeDtypeStruct((n_sample, d), F32)],
        compiler_params=_params("arbitrary"),
        name="moe_combine",
    )(x, route, gf, picked, picked)


def _moe_plan(route, n_experts, tm):
    t = route.shape[0]
    n_slots = t * TOP_K
    expert = route[:, :TOP_K].astype(jnp.int32).reshape(n_slots)
    onehot = (expert[:, None] == jnp.arange(n_experts, dtype=jnp.int32)[None, :]).astype(jnp.int32)
    counts = jnp.sum(onehot, axis=0)
    tiles = (counts + tm - 1) // tm
    tile_end = jnp.cumsum(tiles)
    start = (tile_end - tiles) * tm
    rank = jnp.cumsum(onehot, axis=0) - onehot
    pos = jnp.sum(onehot * (rank + start[None, :]), axis=1)
    nt = -(-n_slots // tm) + n_experts
    n_tiles = tile_end[-1]
    tile_id = jnp.minimum(jnp.arange(nt, dtype=jnp.int32), n_tiles - 1)
    tile_expert = jnp.sum((tile_id[:, None] >= tile_end[None, :]).astype(jnp.int32), axis=1)
    token = jnp.arange(n_slots, dtype=jnp.int32) // TOP_K
    rows = jnp.zeros((nt * tm,), jnp.int32).at[pos].set(token)
    return (tile_expert.astype(jnp.int32), n_tiles.astype(jnp.int32).reshape(1), rows,
            pos.reshape(t, TOP_K).astype(jnp.int32))


def _row(v):
    return v.reshape(1, -1).astype(F32)


def _pad_lanes(v):
    return jnp.pad(_row(v), ((0, 0), (0, V7X_LANES - v.shape[-1])))


def kernel(x_prompt, x_sample, state_ssm, state_ssd_conv, state_conf_conv, norm_mix_even, w_in_even, ssd_conv_w, ssd_conv_b, ssd_dt_bias, ssd_a_log, ssd_d, ssd_norm, gmlp_ln_g, gmlp_ln_b, gmlp_w_s, gmlp_b_s, w_out_even, norm_ffn_even, ffn_w_gate, ffn_w_up, ffn_w_down, norm_mix_odd, conf_w1, conf_b1, conf_dw_w, conf_dw_b, conf_ln_g, conf_ln_b, conf_w2, conf_b2, norm_ffn_odd, moe_router, moe_w_gate, moe_w_up, moe_w_down, final_norm):
    batch, seq, d_model = x_prompt.shape
    n_dec, dec_seq, _ = x_sample.shape
    assert dec_seq == 1, "the sample group advances one token per sequence"
    assert seq % SSD_CHUNK == 0 and n_dec % V7X_LANES == 0
    n_even, n_odd = w_in_even.shape[0], conf_w1.shape[0]
    assert (n_even, n_odd) == (1, 1), "the final norm is fused into the last (odd) layer's MoE combine"
    n_heads = ssd_dt_bias.shape[1]
    d_ssd = n_heads * SSD_HEAD_DIM
    conv_dim = ssd_conv_w.shape[2]
    d_gmlp = gmlp_ln_g.shape[1]
    n_groups_gmlp = gmlp_w_s.shape[1]
    n_experts = moe_router.shape[2]
    tp = batch * seq

    xp = x_prompt.reshape(tp, d_model)
    xs = x_sample.reshape(n_dec, d_model)
    outs = dict(ssm_p=[], conv_p=[], conf_p=[], ssm_s=[], conv_s=[], conf_s=[], v_s=[])

    for layer in range(n_even + n_odd):
        i = layer // 2
        if layer % 2 == 0:
            w = w_in_even[i]
            o1, o2, o3 = d_ssd, d_ssd + conv_dim, d_ssd + conv_dim + n_heads
            w_r = jnp.concatenate([w[:, :o2], w[:, o3:], jnp.pad(w[:, o2:o3], ((0, 0), (0, V7X_LANES - n_heads)))],
                                  axis=1).astype(BF16)
            in_args = (_row(norm_mix_even[i]), w_r, _row(gmlp_ln_g[i]), _row(gmlp_ln_b[i]))
            dims = dict(d_ssd=d_ssd, conv_dim=conv_dim, d_gmlp=d_gmlp)
            dt_bias, a_log = _pad_lanes(ssd_dt_bias[i]), _pad_lanes(ssd_a_log[i])
            d_skip = _row(jnp.repeat(ssd_d[i], SSD_HEAD_DIM))
            norm_g = _row(ssd_norm[i])
            conv_w, conv_b = ssd_conv_w[i], _row(ssd_conv_b[i])
            z, xbc, ug, vn, dt = _even_in(xp, *in_args, **dims)
            yn, st = _ssd_prompt(xbc, z, dt, conv_w, conv_b, dt_bias, a_log, d_skip, norm_g, batch=batch, seq=seq)
            gout = _gmlp_prompt(ug, vn, gmlp_w_s[i], gmlp_b_s[i].T)
            n_pairs = n_heads // 2
            st = st.reshape(batch, n_pairs, SSD_STATE, 2, SSD_HEAD_DIM).transpose(0, 1, 3, 4, 2)
            outs['ssm_p'].append(st.reshape(batch, n_heads, SSD_HEAD_DIM, SSD_STATE))
            outs['conv_p'].append(xbc.reshape(batch, seq, conv_dim)[:, seq - (conv_w.shape[0] - 1):])
            ffn = (w_out_even[i].astype(BF16), _row(norm_ffn_even[i]), ffn_w_gate[i].astype(BF16),
                   ffn_w_up[i].astype(BF16), ffn_w_down[i].astype(BF16))
            xp = _even_out(xp, yn, gout, *ffn)
            z, xbc, ug, vn, dt = _even_in(xs, *in_args, **dims)
            expand = (jnp.arange(d_ssd)[None, :] // SSD_HEAD_DIM == jnp.arange(V7X_LANES)[:, None]).astype(F32)
            buf = state_ssd_conv[i].reshape(n_dec, -1)
            newbuf, xc, xdt, bc, da = _ssd_sample_prep(xbc, buf, dt, conv_w, conv_b, dt_bias, a_log, expand)
            gn = SSD_GROUPS * SSD_STATE
            h_new, y4 = _ssd_sample_state(
                state_ssm[i],
                xdt.reshape(n_dec, n_heads, SSD_HEAD_DIM, 1),
                da[:, :n_heads].reshape(n_dec, n_heads, 1, 1),
                bc[:, :gn].reshape(n_dec, SSD_GROUPS, 1, SSD_STATE),
                bc[:, gn:].reshape(n_dec, SSD_GROUPS, 1, SSD_STATE))
            gd = d_gmlp // n_groups_gmlp
            w0 = _row(jnp.repeat(gmlp_w_s[i][:, 0, 0], gd))
            b0 = _row(jnp.repeat(gmlp_b_s[i][:, 0], gd))
            yn, gout = _even_sample_post(y4.reshape(n_dec, d_ssd), xc, z, d_skip, norm_g, ug, vn, w0, b0)
            outs['ssm_s'].append(h_new)
            outs['conv_s'].append(newbuf.reshape(state_ssd_conv[i].shape))
            outs['v_s'].append(vn.reshape(n_dec, 1, d_gmlp))
            xs = _even_out(xs, yn, gout, *ffn)
        else:
            w1, b1 = conf_w1[i].astype(BF16), _row(conf_b1[i])
            router = jnp.pad(moe_router[i], ((0, 0), (0, V7X_LANES - n_experts)))
            router_hi = router.astype(BF16)
            router_lo = (router - router_hi.astype(F32)).astype(BF16)
            tail = (conf_dw_w[i], _row(conf_dw_b[i]), _row(conf_ln_g[i]), _row(conf_ln_b[i]),
                    conf_w2[i].astype(BF16), _row(conf_b2[i]), _row(norm_ffn_odd[i]), router_hi, router_lo)
            n_keep = conf_dw_w.shape[1] - 1
            a = _conf_in(xp, _row(norm_mix_odd[i]), w1, b1)
            outs['conf_p'].append(a.reshape(batch, seq, -1)[:, seq - n_keep:])
            a_s = _conf_in(xs, _row(norm_mix_odd[i]), w1, b1)
            buf = state_conf_conv[i].reshape(n_dec, -1)
            newbuf, *sample_rows = _conf_sample(a_s, buf, xs, tail, n_experts=n_experts)
            outs['conf_s'].append(newbuf.reshape(state_conf_conv[i].shape))
            x_all, hn, route = _conf_prompt(a, xp, tuple(sample_rows), tail, seq=seq, n_experts=n_experts)
            tile_expert, n_tiles, rows, pos = _moe_plan(route, n_experts, MOE_TILE)
            xs_sorted = _sc_gather_rows(hn, rows)
            ys = _moe_ffn(tile_expert, n_tiles, xs_sorted, moe_w_gate[i].astype(BF16), moe_w_up[i].astype(BF16),
                          moe_w_down[i].astype(BF16), tm=MOE_TILE)
            picked = _sc_gather_rows(ys, pos.T.reshape(-1))
            xp, xs = _moe_combine(x_all, route, _row(final_norm), picked, n_prompt=tp, n_sample=n_dec)

    y_prompt = xp.reshape(batch, seq, d_model)
    y_sample = xs.reshape(n_dec, 1, d_model)
    return (y_prompt, y_sample, jnp.stack(outs['ssm_p']), jnp.stack(outs['conv_p']), jnp.stack(outs['conf_p']),
            jnp.stack(outs['ssm_s']), jnp.stack(outs['conv_s']), jnp.stack(outs['conf_s']), jnp.stack(outs['v_s']))
```

```python
import functools

import jax
import jax.numpy as jnp
from jax import lax
from jax.experimental import pallas as pl
from jax.experimental.pallas import tpu as pltpu
from jax.experimental.pallas import tpu_sc as plsc

F32 = jnp.float32
BF16 = jnp.bfloat16
HIGHEST = lax.Precision.HIGHEST

SSD_HEAD_DIM = 64
SSD_GROUPS = 4
SSD_STATE = 128
SSD_CHUNK = 128
GMLP_CHUNK = 128
TOP_K = 2
RMS_EPS = 1e-6
LN_EPS = 1e-5

V7X_LANES = 128
V7X_SUBLANES = 8
V7X_VMEM_BYTES = 64 * 1024 * 1024
VMEM_LIMIT = (V7X_VMEM_BYTES * 7) // 8

MOE_TILE = 256
COMBINE_TILE = 128
CONF_TILE = 256
CONF_ACC_ROWS = 128
SC_GATHER_ROWS = 32
SAMPLE_STATE_TILE = 8


def _params(*semantics):
    return pltpu.CompilerParams(dimension_semantics=semantics, vmem_limit_bytes=VMEM_LIMIT)


def _resident(shape):
    zeros = (0,) * len(shape)
    return pl.BlockSpec(shape, lambda *_: zeros, pipeline_mode=pl.Buffered(1))


def _whole(shape):
    zeros = (0,) * len(shape)
    return pl.BlockSpec(shape, lambda *_: zeros)


def _row_tile(n_rows, preferred):
    for t in (preferred, 512, 256, 128):
        if t <= preferred and n_rows % t == 0:
            return t
    raise ValueError(f"row count {n_rows} is not a multiple of {V7X_LANES}")


def _dot(a, b):
    return jnp.dot(a, b, preferred_element_type=F32)


def _rms(x, g):
    return x * lax.rsqrt(jnp.mean(x * x, axis=-1, keepdims=True) + RMS_EPS) * g


def _layernorm(x, g, b):
    mu = jnp.mean(x, axis=-1, keepdims=True)
    xc = x - mu
    var = jnp.mean(xc * xc, axis=-1, keepdims=True)
    return xc * lax.rsqrt(var + LN_EPS) * g + b


def _sigmoid(x):
    return 0.5 * jnp.tanh(0.5 * x) + 0.5


def _silu(x):
    return x * _sigmoid(x)


def _softplus(x):
    return jnp.maximum(x, 0.0) + jnp.log1p(jnp.exp(-jnp.abs(x)))


def _even_in_kernel(x_ref, g_ref, w_ref, lng_ref, lnb_ref, z_ref, xbc_ref, ug_ref, vn_ref, dt_ref,
                    *, d_ssd, conv_dim, d_gmlp):
    h = _rms(x_ref[...], g_ref[...]).astype(BF16)
    o1 = d_ssd
    o2 = o1 + conv_dim
    o3 = o2 + d_gmlp
    o4 = o3 + d_gmlp
    z_ref[...] = _dot(h, w_ref[:, 0:o1])
    xbc_ref[...] = _dot(h, w_ref[:, o1:o2])
    ug_ref[...] = jax.nn.gelu(_dot(h, w_ref[:, o2:o3])).astype(ug_ref.dtype)
    v = jax.nn.gelu(_dot(h, w_ref[:, o3:o4]))
    vn_ref[...] = _layernorm(v, lng_ref[...], lnb_ref[...])
    dt_ref[...] = _dot(h, w_ref[:, o4:o4 + V7X_LANES])


def _even_in(x, g, w, ln_g, ln_b, *, d_ssd, conv_dim, d_gmlp):
    t, d = x.shape
    tm = _row_tile(t, 256)
    row = lambda n: pl.BlockSpec((tm, n), lambda i: (i, 0))
    return pl.pallas_call(
        functools.partial(_even_in_kernel, d_ssd=d_ssd, conv_dim=conv_dim, d_gmlp=d_gmlp),
        grid=(t // tm,),
        in_specs=[row(d), _resident(g.shape), _resident(w.shape), _resident(ln_g.shape), _resident(ln_b.shape)],
        out_specs=[row(d_ssd), row(conv_dim), row(d_gmlp), row(d_gmlp), row(V7X_LANES)],
        out_shape=[jax.ShapeDtypeStruct((t, d_ssd), F32), jax.ShapeDtypeStruct((t, conv_dim), F32),
                   jax.ShapeDtypeStruct((t, d_gmlp), BF16), jax.ShapeDtypeStruct((t, d_gmlp), F32),
                   jax.ShapeDtypeStruct((t, V7X_LANES), F32)],
        compiler_params=_params("parallel"),
        name="even_in",
    )(x, g, w, ln_g, ln_b)


def _ssd_prompt_kernel(xbc_ref, z_ref, dt_ref, cw_ref, cb_ref, dtb_ref, alog_ref, dskip_ref, ng_ref,
                       yn_ref, st_ref, xp_ref, state_ref, xc_ref, *, d_ssd, n_taps):
    c = pl.program_id(1)
    n_pairs = state_ref.shape[0]
    ch = SSD_CHUNK
    gn = SSD_GROUPS * SSD_STATE

    @pl.when(c == 0)
    def _():
        state_ref[...] = jnp.zeros_like(state_ref)
        xp_ref[0:V7X_SUBLANES, :] = jnp.zeros((V7X_SUBLANES, xp_ref.shape[1]), F32)

    xp_ref[V7X_SUBLANES:V7X_SUBLANES + ch, :] = xbc_ref[...]
    base = V7X_SUBLANES - (n_taps - 1)
    acc = cb_ref[...] + cw_ref[n_taps - 1:n_taps, :] * xbc_ref[...]
    for k in range(n_taps - 1):
        xc_ref[...] = xp_ref[base + k:base + k + ch, :]
        acc = acc + cw_ref[k:k + 1, :] * xc_ref[...]
    xp_ref[0:V7X_SUBLANES, :] = xp_ref[ch:ch + V7X_SUBLANES, :]
    xc_ref[...] = _silu(acc)
    xc = xc_ref

    dt = _softplus(dt_ref[...] + dtb_ref[...])
    a = dt * (-jnp.exp(alog_ref[...]))
    li = lax.broadcasted_iota(jnp.int32, (ch, ch), 0)
    si = lax.broadcasted_iota(jnp.int32, (ch, ch), 1)
    causal = li >= si
    tril = jnp.where(causal, 1.0, 0.0).astype(F32)
    acs = jnp.dot(tril, a, precision=HIGHEST, preferred_element_type=F32)
    acs_t = acs.T
    lane = lax.broadcasted_iota(jnp.int32, (ch, 2 * SSD_HEAD_DIM), 1)
    first = lane < SSD_HEAD_DIM
    first_n = lax.broadcasted_iota(jnp.int32, (SSD_STATE, 2 * SSD_HEAD_DIM), 1) < SSD_HEAD_DIM
    pairs_per_group = n_pairs // SSD_GROUPS

    for g in range(SSD_GROUPS):
        bg = xc[:, d_ssd + g * SSD_STATE:d_ssd + (g + 1) * SSD_STATE]
        cg = xc[:, d_ssd + gn + g * SSD_STATE:d_ssd + gn + (g + 1) * SSD_STATE]
        bg_t = bg.T
        cb = _dot(cg.astype(BF16), bg_t.astype(BF16))
        ys = []
        for q in range(pairs_per_group):
            pair = g * pairs_per_group + q
            h0 = 2 * pair
            lo = pair * 2 * SSD_HEAD_DIM
            xs = xc[:, lo:lo + 2 * SSD_HEAD_DIM]
            dt2 = jnp.where(first, dt[:, h0:h0 + 1], dt[:, h0 + 1:h0 + 2])
            xdt = (xs * dt2).astype(BF16)
            s_prev = state_ref[pair]
            s_prev_b = s_prev.astype(BF16)
            y2, snew2, dec2 = [], [], []
            for r in range(2):
                h = h0 + r
                col = acs[:, h:h + 1]
                row = acs_t[h:h + 1, :]
                last = acs_t[h:h + 1, ch - 1:ch]
                decay = jnp.exp(jnp.where(causal, col - row, -jnp.inf))
                y = _dot((cb * decay).astype(BF16), xdt)
                y = y + _dot((cg * jnp.exp(col)).astype(BF16), s_prev_b)
                snew = _dot((bg_t * jnp.exp(last - row)).astype(BF16), xdt)
                y2.append(y)
                snew2.append(snew)
                dec2.append(jnp.exp(last))
            y = jnp.where(first, y2[0], y2[1])
            state_ref[pair] = (s_prev * jnp.where(first_n, dec2[0], dec2[1])
                               + jnp.where(first_n, snew2[0], snew2[1]))
            ys.append(y + dskip_ref[:, lo:lo + 2 * SSD_HEAD_DIM] * xs)
        gw = pairs_per_group * 2 * SSD_HEAD_DIM
        yg = jnp.concatenate(ys, axis=-1) * _silu(z_ref[:, g * gw:(g + 1) * gw])
        yg = yg * lax.rsqrt(jnp.mean(yg * yg, axis=-1, keepdims=True) + RMS_EPS)
        yn_ref[:, g * gw:(g + 1) * gw] = (yg * ng_ref[:, g * gw:(g + 1) * gw]).astype(yn_ref.dtype)

    @pl.when(c == pl.num_programs(1) - 1)
    def _():
        st_ref[0] = state_ref[...]


def _ssd_prompt(xbc, z, dt, conv_w, conv_b, dt_bias, a_log, d_skip, norm_g, *, batch, seq):
    t, conv_dim = xbc.shape
    d_ssd = z.shape[1]
    n_pairs = d_ssd // (2 * SSD_HEAD_DIM)
    nc = seq // SSD_CHUNK
    n_taps = conv_w.shape[0]
    tile = lambda n: pl.BlockSpec((SSD_CHUNK, n), lambda b, c: (b * nc + c, 0))
    return pl.pallas_call(
        functools.partial(_ssd_prompt_kernel, d_ssd=d_ssd, n_taps=n_taps),
        grid=(batch, nc),
        in_specs=[tile(conv_dim), tile(d_ssd), tile(V7X_LANES), _resident(conv_w.shape), _resident(conv_b.shape),
                  _resident(dt_bias.shape), _resident(a_log.shape), _resident(d_skip.shape),
                  _resident(norm_g.shape)],
        out_specs=[tile(d_ssd),
                   pl.BlockSpec((1, n_pairs, SSD_STATE, 2 * SSD_HEAD_DIM), lambda b, c: (b, 0, 0, 0))],
        out_shape=[jax.ShapeDtypeStruct((t, d_ssd), BF16),
                   jax.ShapeDtypeStruct((batch, n_pairs, SSD_STATE, 2 * SSD_HEAD_DIM), F32)],
        scratch_shapes=[pltpu.VMEM((SSD_CHUNK + V7X_SUBLANES, conv_dim), F32),
                        pltpu.VMEM((n_pairs, SSD_STATE, 2 * SSD_HEAD_DIM), F32),
                        pltpu.VMEM((SSD_CHUNK, conv_dim), F32)],
        compiler_params=_params("parallel", "arbitrary"),
        name="ssd_prompt",
    )(xbc, z, dt, conv_w, conv_b, dt_bias, a_log, d_skip, norm_g)


def _gmlp_prompt_kernel(ug_ref, vn_ref, ws_ref, bs_ref, out_ref):
    ch = GMLP_CHUNK
    n_groups = ws_ref.shape[0]
    gd = vn_ref.shape[1] // n_groups
    ii = lax.broadcasted_iota(jnp.int32, (ch, ch), 0)
    jj = lax.broadcasted_iota(jnp.int32, (ch, ch), 1)
    for g in range(n_groups):
        ws = jnp.where(ii >= jj, ws_ref[g], 0.0).astype(BF16)
        mixed = _dot(ws, vn_ref[:, g * gd:(g + 1) * gd].astype(BF16)) + bs_ref[:, g:g + 1]
        out_ref[:, g * gd:(g + 1) * gd] = (ug_ref[:, g * gd:(g + 1) * gd].astype(F32) * mixed).astype(out_ref.dtype)


def _gmlp_prompt(ug, vn, w_s, b_s_t):
    t, d = vn.shape
    tile = pl.BlockSpec((GMLP_CHUNK, d), lambda i: (i, 0))
    return pl.pallas_call(
        _gmlp_prompt_kernel,
        grid=(t // GMLP_CHUNK,),
        in_specs=[tile, tile, _resident(w_s.shape), _resident(b_s_t.shape)],
        out_specs=tile,
        out_shape=jax.ShapeDtypeStruct((t, d), BF16),
        compiler_params=_params("parallel"),
        name="gmlp_prompt",
    )(ug, vn, w_s, b_s_t)


def _ssd_sample_prep_kernel(xbc_ref, buf_ref, dt_ref, cw_ref, cb_ref, dtb_ref, alog_ref, expand_ref,
                            newbuf_ref, xs_ref, xdt_ref, bc_ref, da_ref, *, d_ssd, n_taps):
    conv_dim = xbc_ref.shape[1]
    x_new = xbc_ref[...]
    acc = cb_ref[...] + cw_ref[n_taps - 1:n_taps, :] * x_new
    for k in range(n_taps - 1):
        acc = acc + cw_ref[k:k + 1, :] * buf_ref[:, k * conv_dim:(k + 1) * conv_dim]
    for k in range(n_taps - 2):
        newbuf_ref[:, k * conv_dim:(k + 1) * conv_dim] = buf_ref[:, (k + 1) * conv_dim:(k + 2) * conv_dim]
    newbuf_ref[:, (n_taps - 2) * conv_dim:(n_taps - 1) * conv_dim] = x_new
    xc = _silu(acc)
    xs = xc[:, :d_ssd]
    dt = _softplus(dt_ref[...] + dtb_ref[...])
    da_ref[...] = jnp.exp(dt * (-jnp.exp(alog_ref[...])))
    dt_wide = jnp.dot(dt, expand_ref[...], precision=HIGHEST, preferred_element_type=F32)
    xs_ref[...] = xs
    xdt_ref[...] = xs * dt_wide
    bc_ref[...] = xc[:, d_ssd:]


def _ssd_sample_prep(xbc, buf, dt, conv_w, conv_b, dt_bias, a_log, expand):
    n, conv_dim = xbc.shape
    d_ssd = expand.shape[1]
    n_taps = conv_w.shape[0]
    args = (xbc, buf, dt, conv_w, conv_b, dt_bias, a_log, expand)
    return pl.pallas_call(
        functools.partial(_ssd_sample_prep_kernel, d_ssd=d_ssd, n_taps=n_taps),
        grid=(1,),
        in_specs=[_resident(a.shape) for a in args],
        out_specs=[_whole(buf.shape), _whole((n, d_ssd)), _whole((n, d_ssd)),
                   _whole((n, conv_dim - d_ssd)), _whole((n, V7X_LANES))],
        out_shape=[jax.ShapeDtypeStruct(buf.shape, F32), jax.ShapeDtypeStruct((n, d_ssd), F32),
                   jax.ShapeDtypeStruct((n, d_ssd), F32), jax.ShapeDtypeStruct((n, conv_dim - d_ssd), F32),
                   jax.ShapeDtypeStruct((n, V7X_LANES), F32)],
        compiler_params=_params("arbitrary"),
        name="ssd_sample_prep",
    )(*args)


def _ssd_sample_state_kernel(h_ref, xdt_ref, da_ref, bc_ref, hnew_ref, y_ref, xcol_ref, ccol_ref):
    bs, n_heads, p, s = h_ref.shape
    n_pairs = n_heads // 2
    pairs_per_group = n_pairs // SSD_GROUPS
    gn = SSD_GROUPS * s
    w = 2 * p
    eye = lax.broadcasted_iota(jnp.int32, (w, w), 0) == lax.broadcasted_iota(jnp.int32, (w, w), 1)
    ones = jnp.ones((w, V7X_LANES), BF16)

    def stacked_diag(v):
        return jnp.concatenate(
            [jnp.where(eye, jnp.broadcast_to(v[j:j + 1], (w, w)), 0.0) for j in range(bs)], axis=0).astype(BF16)

    for q in range(n_pairs):
        x = xdt_ref[:, q * w:(q + 1) * w]
        hi = x.astype(BF16).astype(F32)
        mid = (x - hi).astype(BF16).astype(F32)
        lo = (x - hi) - mid
        xcol_ref[q] = _dot(stacked_diag(hi), ones) + (_dot(stacked_diag(mid), ones) + _dot(stacked_diag(lo), ones))
    for g in range(SSD_GROUPS):
        ccol_ref[g] = _dot(stacked_diag(bc_ref[:, gn + g * s:gn + (g + 1) * s]), ones).astype(BF16)

    for j in range(bs):
        for g in range(SSD_GROUPS):
            brow = bc_ref[j:j + 1, g * s:(g + 1) * s]
            h_new = []
            for q in range(g * pairs_per_group, (g + 1) * pairs_per_group):
                for r in range(2):
                    h = 2 * q + r
                    xcol = xcol_ref[q, j * w + r * p:j * w + (r + 1) * p, :]
                    h_new.append(h_ref[j, h] * da_ref[j:j + 1, h:h + 1] + xcol * brow)
                    hnew_ref[j, h] = h_new[-1]
            hc = _dot(jnp.concatenate(h_new, axis=0).astype(BF16), ccol_ref[g, j * s:(j + 1) * s, :])
            for i in range(pairs_per_group):
                y = jnp.sum(jnp.where(eye, hc[i * w:(i + 1) * w], 0.0), axis=0, keepdims=True)
                q = g * pairs_per_group + i
                y_ref[j, q:q + 1, :] = y


def _ssd_sample_state(h, xdt, da, bc):
    n, n_heads, p, s = h.shape
    assert 2 * p == V7X_LANES and s == V7X_LANES
    bs = SAMPLE_STATE_TILE
    rows = lambda w: pl.BlockSpec((bs, w), lambda i: (i, 0))
    return pl.pallas_call(
        _ssd_sample_state_kernel,
        grid=(n // bs,),
        in_specs=[pl.BlockSpec((bs, n_heads, p, s), lambda i: (i, 0, 0, 0)), rows(xdt.shape[1]), rows(da.shape[1]),
                  rows(bc.shape[1])],
        out_specs=[pl.BlockSpec((bs, n_heads, p, s), lambda i: (i, 0, 0, 0)),
                   pl.BlockSpec((bs, n_heads // 2, 2 * p), lambda i: (i, 0, 0))],
        out_shape=[jax.ShapeDtypeStruct(h.shape, F32), jax.ShapeDtypeStruct((n, n_heads // 2, 2 * p), F32)],
        scratch_shapes=[pltpu.VMEM((n_heads // 2, bs * 2 * p, V7X_LANES), F32),
                        pltpu.VMEM((SSD_GROUPS, bs * s, V7X_LANES), BF16)],
        compiler_params=_params("parallel"),
        name="ssd_sample_state",
    )(h, xdt, da, bc)


def _even_sample_post_kernel(y_ref, xs_ref, z_ref, dskip_ref, ng_ref, ug_ref, vn_ref, w0_ref, b0_ref,
                             yn_ref, gout_ref):
    d_ssd = y_ref.shape[1]
    gw = d_ssd // SSD_GROUPS
    y = (y_ref[...] + dskip_ref[...] * xs_ref[...]) * _silu(z_ref[...])
    for g in range(SSD_GROUPS):
        yg = y[:, g * gw:(g + 1) * gw]
        yg = yg * lax.rsqrt(jnp.mean(yg * yg, axis=-1, keepdims=True) + RMS_EPS)
        yn_ref[:, g * gw:(g + 1) * gw] = (yg * ng_ref[:, g * gw:(g + 1) * gw]).astype(yn_ref.dtype)
    mixed = w0_ref[...] * vn_ref[...] + b0_ref[...]
    gout_ref[...] = (ug_ref[...].astype(F32) * mixed).astype(gout_ref.dtype)


def _even_sample_post(y, xs, z, d_skip, norm_g, ug, vn, w0, b0):
    args = (y, xs, z, d_skip, norm_g, ug, vn, w0, b0)
    return pl.pallas_call(
        _even_sample_post_kernel,
        grid=(1,),
        in_specs=[_resident(a.shape) for a in args],
        out_specs=[_whole(y.shape), _whole(vn.shape)],
        out_shape=[jax.ShapeDtypeStruct(y.shape, BF16), jax.ShapeDtypeStruct(vn.shape, BF16)],
        compiler_params=_params("arbitrary"),
        name="even_sample_post",
    )(*args)


def _even_out_kernel(x_ref, yn_ref, gout_ref, wo_ref, g_ref, wg_ref, wu_ref, wd_ref, out_ref):
    d_ssd = yn_ref.shape[1]
    x = x_ref[...] + _dot(yn_ref[...], wo_ref[0:d_ssd, :]) + _dot(gout_ref[...], wo_ref[d_ssd:, :])
    h = _rms(x, g_ref[...]).astype(BF16)
    mid = (_silu(_dot(h, wg_ref[...])) * _dot(h, wu_ref[...])).astype(BF16)
    out_ref[...] = x + _dot(mid, wd_ref[...])


def _even_out(x, yn, gout, w_out, g, wg, wu, wd):
    t, d = x.shape
    tm = _row_tile(t, 256)
    row = lambda n: pl.BlockSpec((tm, n), lambda i: (i, 0))
    return pl.pallas_call(
        _even_out_kernel,
        grid=(t // tm,),
        in_specs=[row(d), row(yn.shape[1]), row(gout.shape[1]), _resident(w_out.shape), _resident(g.shape),
                  _resident(wg.shape), _resident(wu.shape), _resident(wd.shape)],
        out_specs=row(d),
        out_shape=jax.ShapeDtypeStruct((t, d), F32),
        compiler_params=_params("parallel"),
        name="even_out_ffn",
    )(x, yn, gout, w_out, g, wg, wu, wd)


def _conf_in_kernel(x_ref, g_ref, w_ref, b_ref, a_ref):
    d = a_ref.shape[1]
    h = _rms(x_ref[...], g_ref[...]).astype(BF16)
    val = _dot(h, w_ref[:, 0:d]) + b_ref[:, 0:d]
    gate = _dot(h, w_ref[:, d:]) + b_ref[:, d:]
    a_ref[...] = val * _sigmoid(gate)


def _conf_in(x, g, w1, b1):
    t, d = x.shape
    dc = w1.shape[1] // 2
    tm = _row_tile(t, 512)
    return pl.pallas_call(
        _conf_in_kernel,
        grid=(t // tm,),
        in_specs=[pl.BlockSpec((tm, d), lambda i: (i, 0)), _resident(g.shape), _resident(w1.shape),
                  _resident(b1.shape)],
        out_specs=pl.BlockSpec((tm, dc), lambda i: (i, 0)),
        out_shape=jax.ShapeDtypeStruct((t, dc), F32),
        compiler_params=_params("parallel"),
        name="conf_in",
    )(x, g, w1, b1)


def _route(logits, n_experts, route_ref):
    lane = lax.broadcasted_iota(jnp.int32, logits.shape, 1)
    lane_f = lane.astype(F32)
    big = jnp.float32(V7X_LANES)
    lg = jnp.where(lane < n_experts, logits, -jnp.inf)
    m1 = jnp.max(lg, axis=-1, keepdims=True)
    i1 = jnp.min(jnp.where(lg == m1, lane_f, big), axis=-1, keepdims=True)
    lg2 = jnp.where(lane_f == i1, -jnp.inf, lg)
    m2 = jnp.max(lg2, axis=-1, keepdims=True)
    i2 = jnp.min(jnp.where(lg2 == m2, lane_f, big), axis=-1, keepdims=True)
    e2 = jnp.exp(m2 - m1)
    g1 = 1.0 / (1.0 + e2)
    g2 = e2 / (1.0 + e2)
    route_ref[...] = jnp.where(lane == 0, i1, jnp.where(lane == 1, i2, jnp.where(lane == 2, g1,
                               jnp.where(lane == 3, g2, 0.0))))


def _conf_finish(c, x, lng_ref, lnb_ref, w2_ref, b2_ref, ng_ref, rhi_ref, rlo_ref, out_ref, hn_ref, route_ref,
                 *, n_experts):
    hmid = _silu(_layernorm(c, lng_ref[...], lnb_ref[...])).astype(BF16)
    x_new = x + _dot(hmid, w2_ref[...]) + b2_ref[...]
    out_ref[...] = x_new
    hn = _rms(x_new, ng_ref[...])
    hn_ref[...] = hn
    hi = hn.astype(BF16)
    lo = (hn - hi.astype(F32)).astype(BF16)
    logits = _dot(hi, rhi_ref[...]) + (_dot(lo, rhi_ref[...]) + _dot(hi, rlo_ref[...]))
    _route(logits, n_experts, route_ref)


def _conf_prompt_kernel(a_ref, x_ref, xs_ref, hns_ref, routes_ref, dw_ref, db_ref, lng_ref, lnb_ref, w2_ref, b2_ref,
                        ng_ref, rhi_ref, rlo_ref, out_ref, hn_ref, route_ref, xp_ref, c_ref, win_ref,
                        *, n_taps, n_experts, tiles_per_seq, n_prompt_tiles):
    tl, d = a_ref.shape
    halo = xp_ref.shape[0] - tl
    base = halo - (n_taps - 1)
    rows = CONF_ACC_ROWS
    step = pl.program_id(0)

    @pl.when(step < n_prompt_tiles)
    def _():
        @pl.when(step % tiles_per_seq == 0)
        def _():
            xp_ref[0:halo, :] = jnp.zeros((halo, d), F32)

        xp_ref[halo:halo + tl, :] = a_ref[...]

        def lane_block(j, carry):
            cols = pl.ds(pl.multiple_of(j * V7X_LANES, V7X_LANES), V7X_LANES)
            for r0 in range(0, tl, rows):
                acc = jnp.broadcast_to(db_ref[:, cols], (rows, V7X_LANES))
                for phase in range(V7X_SUBLANES):
                    taps = [k for k in range(n_taps) if (base + k) % V7X_SUBLANES == phase]
                    if not taps:
                        continue
                    span = max(base + k - phase for k in taps) + rows
                    win_ref[0:span, :] = xp_ref[pl.ds(r0 + phase, span), cols]
                    for k in taps:
                        o = base + k - phase
                        acc = acc + dw_ref[k:k + 1, cols] * win_ref[o:o + rows, :]
                c_ref[r0:r0 + rows, cols] = acc
            return carry

        lax.fori_loop(0, d // V7X_LANES, lane_block, 0)
        xp_ref[0:halo, :] = xp_ref[tl:tl + halo, :]
        _conf_finish(c_ref[...], x_ref[...], lng_ref, lnb_ref, w2_ref, b2_ref, ng_ref, rhi_ref, rlo_ref,
                     out_ref, hn_ref, route_ref, n_experts=n_experts)

    @pl.when(step == n_prompt_tiles)
    def _():
        ns = xs_ref.shape[0]
        out_ref[0:ns, :] = xs_ref[...]
        hn_ref[0:ns, :] = hns_ref[...]
        route_ref[0:ns, :] = routes_ref[...]


def _conf_prompt(a, x, sample_rows, consts, *, seq, n_experts):
    t, d = a.shape
    n_s = sample_rows[0].shape[0]
    tl = _row_tile(seq, CONF_TILE)
    assert tl % CONF_ACC_ROWS == 0 and n_s <= tl
    n_prompt_tiles = t // tl
    n_taps = consts[0].shape[0]
    halo = -(-(n_taps - 1) // V7X_SUBLANES) * V7X_SUBLANES
    in_tile = lambda n: pl.BlockSpec((tl, n), lambda s: (jnp.minimum(s, n_prompt_tiles - 1), 0))
    out_tile = lambda n: pl.BlockSpec((tl, n), lambda s: (s, 0))
    small = sample_rows + tuple(consts)
    return pl.pallas_call(
        functools.partial(_conf_prompt_kernel, n_taps=n_taps, n_experts=n_experts, tiles_per_seq=seq // tl,
                          n_prompt_tiles=n_prompt_tiles),
        grid=(n_prompt_tiles + 1,),
        in_specs=[in_tile(d), in_tile(d)] + [_resident(c.shape) for c in small],
        out_specs=[out_tile(d), out_tile(d), out_tile(V7X_LANES)],
        out_shape=[jax.ShapeDtypeStruct((t + n_s, d), F32), jax.ShapeDtypeStruct((t + n_s, d), F32),
                   jax.ShapeDtypeStruct((t + n_s, V7X_LANES), F32)],
        scratch_shapes=[pltpu.VMEM((tl + halo, d), F32), pltpu.VMEM((tl, d), F32),
                        pltpu.VMEM((CONF_ACC_ROWS + halo, V7X_LANES), F32)],
        compiler_params=_params("arbitrary"),
        name="conf_prompt",
    )(a, x, *small)


def _conf_sample_kernel(a_ref, buf_ref, x_ref, dw_ref, db_ref, lng_ref,
                        lnb_ref, w2_ref, b2_ref, ng_ref, rhi_ref, rlo_ref, newbuf_ref, out_ref, hn_ref, route_ref,
                        *, n_taps, n_experts):
    d = a_ref.shape[1]
    a_new = a_ref[...]
    acc = db_ref[...] + dw_ref[n_taps - 1:n_taps, :] * a_new
    for k in range(n_taps - 1):
        acc = acc + dw_ref[k:k + 1, :] * buf_ref[:, k * d:(k + 1) * d]
    for k in range(n_taps - 2):
        newbuf_ref[:, k * d:(k + 1) * d] = buf_ref[:, (k + 1) * d:(k + 2) * d]
    newbuf_ref[:, (n_taps - 2) * d:(n_taps - 1) * d] = a_new
    _conf_finish(acc, x_ref[...], lng_ref, lnb_ref, w2_ref, b2_ref, ng_ref, rhi_ref, rlo_ref,
                 out_ref, hn_ref, route_ref, n_experts=n_experts)


def _conf_sample(a, buf, x, consts, *, n_experts):
    n, d = a.shape
    args = (a, buf, x) + tuple(consts)
    return pl.pallas_call(
        functools.partial(_conf_sample_kernel, n_taps=consts[0].shape[0], n_experts=n_experts),
        grid=(1,),
        in_specs=[_resident(v.shape) for v in args],
        out_specs=[_whole(buf.shape), _whole((n, d)), _whole((n, d)), _whole((n, V7X_LANES))],
        out_shape=[jax.ShapeDtypeStruct(buf.shape, F32), jax.ShapeDtypeStruct((n, d), F32),
                   jax.ShapeDtypeStruct((n, d), F32), jax.ShapeDtypeStruct((n, V7X_LANES), F32)],
        compiler_params=_params("arbitrary"),
        name="conf_sample",
    )(*args)


def _sc_gather_rows(x, idx):
    n, d, w = idx.shape[0], x.shape[1], SC_GATHER_ROWS
    assert n % w == 0
    mesh = plsc.VectorSubcoreMesh(core_axis_name="core", subcore_axis_name="subcore")

    @functools.partial(pl.kernel, out_type=jax.ShapeDtypeStruct((n, d), x.dtype), mesh=mesh, scratch_types=[])
    def gather(x_hbm, idx_hbm, out_hbm):
        def body(idx_vmem, out_vmem):
            pltpu.sync_copy(x_hbm.at[idx_vmem.at[0]], out_vmem)

        pltpu.emit_pipeline(
            body,
            grid=(n // w,),
            in_specs=[pl.BlockSpec((1, w), lambda i: (i, 0))],
            out_specs=[pl.BlockSpec((w, d), lambda i: (i, 0))],
            core_axis_name=("core", "subcore"),
            dimension_semantics=(pltpu.PARALLEL,),
        )(idx_hbm, out_hbm)

    return gather(x, idx.reshape(n // w, w))


def _sc_dispatch_rows(x, pos, pad_pos, n_out):
    t, d = x.shape
    w = SC_GATHER_ROWS
    n_pad = pad_pos.shape[0]
    assert t % w == 0 and n_pad % w == 0 and TOP_K * t + n_pad == n_out
    mesh = plsc.VectorSubcoreMesh(core_axis_name="core", subcore_axis_name="subcore")

    @functools.partial(pl.kernel, out_type=jax.ShapeDtypeStruct((n_out, d), x.dtype), mesh=mesh, scratch_types=[])
    def dispatch(x_hbm, pos_hbm, pad_hbm, zero_hbm, out_hbm):
        def body(rows_vmem, idx_vmem):
            pltpu.sync_copy(rows_vmem, out_hbm.at[idx_vmem.at[0]])

        def scatter(src_hbm, idx_hbm, src_map):
            pltpu.emit_pipeline(
                body,
                grid=(idx_hbm.shape[0],),
                in_specs=[pl.BlockSpec((w, d), src_map), pl.BlockSpec((1, w), lambda i: (i, 0))],
                out_specs=[],
                core_axis_name=("core", "subcore"),
                dimension_semantics=(pltpu.PARALLEL,),
            )(src_hbm, idx_hbm)

        for k in range(TOP_K):
            scatter(x_hbm, pos_hbm.at[k], lambda i: (i, 0))
        scatter(zero_hbm, pad_hbm, lambda i: (0, 0))

    return dispatch(x, pos.reshape(TOP_K, t // w, w), pad_pos.reshape(n_pad // w, w), jnp.zeros((w, d), x.dtype))


def _moe_ffn_kernel(tile_expert_ref, n_tiles_ref, x_ref, wg_ref, wu_ref, wd_ref, out_ref):
    del tile_expert_ref
    i = pl.program_id(0)

    @pl.when(i < n_tiles_ref[0])
    def _():
        xb = x_ref[...].astype(BF16)
        half = wg_ref.shape[2] // 2
        acc = None
        for j in range(2):
            cols = slice(j * half, (j + 1) * half)
            mid = (_silu(_dot(xb, wg_ref[0, :, cols])) * _dot(xb, wu_ref[0, :, cols])).astype(BF16)
            part = _dot(mid, wd_ref[0, cols, :])
            acc = part if acc is None else acc + part
        out_ref[...] = acc

    @pl.when(i >= n_tiles_ref[0])
    def _():
        out_ref[...] = jnp.zeros_like(out_ref)


def _moe_ffn(tile_expert, n_tiles, xs, wg, wu, wd, *, tm):
    p, d = xs.shape
    dff = wg.shape[2]
    grid_spec = pltpu.PrefetchScalarGridSpec(
        num_scalar_prefetch=2,
        grid=(p // tm,),
        in_specs=[
            pl.BlockSpec((tm, d), lambda i, te, n: (i, 0)),
            pl.BlockSpec((1, d, dff), lambda i, te, n: (te[i], 0, 0)),
            pl.BlockSpec((1, d, dff), lambda i, te, n: (te[i], 0, 0)),
            pl.BlockSpec((1, dff, d), lambda i, te, n: (te[i], 0, 0)),
        ],
        out_specs=pl.BlockSpec((tm, d), lambda i, te, n: (i, 0)),
    )
    return pl.pallas_call(
        _moe_ffn_kernel,
        grid_spec=grid_spec,
        out_shape=jax.ShapeDtypeStruct((p, d), F32),
        compiler_params=_params("arbitrary"),
        name="moe_ffn",
    )(tile_expert, n_tiles, xs, wg, wu, wd)


def _moe_combine_kernel(x_ref, route_ref, gf_ref, a_ref, b_ref, yp_ref, ys_ref, *, n_prompt_tiles):
    i = pl.program_id(0)
    route = route_ref[...]
    x = x_ref[...] + route[:, 2:3] * a_ref[...] + route[:, 3:4] * b_ref[...]
    y = _rms(x, gf_ref[...])

    @pl.when(i < n_prompt_tiles)
    def _():
        yp_ref[...] = y

    @pl.when(i >= n_prompt_tiles)
    def _():
        ys_ref[...] = y


def _moe_combine(x, route, gf, picked, *, n_prompt, n_sample):
    t, d = x.shape
    tc = COMBINE_TILE
    assert n_sample == tc and n_prompt % tc == 0
    n_prompt_tiles = n_prompt // tc
    n_tiles = t // tc
    return pl.pallas_call(
        functools.partial(_moe_combine_kernel, n_prompt_tiles=n_prompt_tiles),
        grid=(n_tiles,),
        in_specs=[pl.BlockSpec((tc, d), lambda i: (i, 0)),
                  pl.BlockSpec((tc, V7X_LANES), lambda i: (i, 0)),
                  _resident(gf.shape),
                  pl.BlockSpec((tc, d), lambda i: (i, 0)),
                  pl.BlockSpec((tc, d), lambda i: (i + n_tiles, 0))],
        out_specs=[pl.BlockSpec((tc, d), lambda i: (jnp.minimum(i, n_prompt_tiles - 1), 0)),
                   pl.BlockSpec((tc, d), lambda i: (0, 0))],
        out_shape=[jax.ShapeDtypeStruct((n_prompt, d), F32), jax.ShapeDtypeStruct((n_sample, d), F32)],
        compiler_params=_params("arbitrary"),
        name="moe_combine",
    )(x, route, gf, picked, picked)


def _moe_plan(route, n_experts, tm):
    t = route.shape[0]
    n_slots = t * TOP_K
    i32 = jnp.int32
    expert = route[:, :TOP_K].astype(i32).reshape(n_slots)
    onehot = (expert[:, None] == jnp.arange(n_experts, dtype=i32)[None, :]).astype(i32)
    counts = jnp.sum(onehot, axis=0)
    tiles = (counts + tm - 1) // tm
    tile_end = jnp.cumsum(tiles)
    start = (tile_end - tiles) * tm
    rank = jnp.cumsum(onehot, axis=0) - onehot
    pos = jnp.sum(onehot * (rank + start[None, :]), axis=1)
    nt = -(-n_slots // tm) + n_experts
    n_tiles = tile_end[-1]
    tile_id = jnp.minimum(jnp.arange(nt, dtype=i32), n_tiles - 1)
    tile_expert = jnp.sum((tile_id[:, None] >= tile_end[None, :]).astype(i32), axis=1)
    seg_start = jnp.concatenate([start + counts, (n_tiles * tm)[None]])
    seg_len = jnp.concatenate([tiles * tm - counts, (nt * tm - n_tiles * tm)[None]])
    seg_end = jnp.cumsum(seg_len)
    q = jnp.arange(nt * tm - n_slots, dtype=i32)
    in_seg = (q[:, None] >= (seg_end - seg_len)[None, :]) & (q[:, None] < seg_end[None, :])
    pad_pos = jnp.sum(in_seg.astype(i32) * (seg_start - (seg_end - seg_len))[None, :], axis=1) + q
    return (tile_expert.astype(i32), n_tiles.astype(i32).reshape(1), pos.reshape(t, TOP_K).T.astype(i32),
            pad_pos.astype(i32), nt * tm)


def _row(v):
    return v.reshape(1, -1).astype(F32)


def _pad_lanes(v):
    return jnp.pad(_row(v), ((0, 0), (0, V7X_LANES - v.shape[-1])))


def kernel(x_prompt, x_sample, state_ssm, state_ssd_conv, state_conf_conv, norm_mix_even, w_in_even, ssd_conv_w, ssd_conv_b, ssd_dt_bias, ssd_a_log, ssd_d, ssd_norm, gmlp_ln_g, gmlp_ln_b, gmlp_w_s, gmlp_b_s, w_out_even, norm_ffn_even, ffn_w_gate, ffn_w_up, ffn_w_down, norm_mix_odd, conf_w1, conf_b1, conf_dw_w, conf_dw_b, conf_ln_g, conf_ln_b, conf_w2, conf_b2, norm_ffn_odd, moe_router, moe_w_gate, moe_w_up, moe_w_down, final_norm):
    batch, seq, d_model = x_prompt.shape
    n_dec, dec_seq, _ = x_sample.shape
    assert dec_seq == 1, "the sample group advances one token per sequence"
    assert seq % SSD_CHUNK == 0 and n_dec % V7X_LANES == 0
    n_even, n_odd = w_in_even.shape[0], conf_w1.shape[0]
    assert (n_even, n_odd) == (1, 1), "the final norm is fused into the last (odd) layer's MoE combine"
    n_heads = ssd_dt_bias.shape[1]
    d_ssd = n_heads * SSD_HEAD_DIM
    conv_dim = ssd_conv_w.shape[2]
    d_gmlp = gmlp_ln_g.shape[1]
    n_groups_gmlp = gmlp_w_s.shape[1]
    n_experts = moe_router.shape[2]
    tp = batch * seq

    xp = x_prompt.reshape(tp, d_model)
    xs = x_sample.reshape(n_dec, d_model)
    outs = dict(ssm_p=[], conv_p=[], conf_p=[], ssm_s=[], conv_s=[], conf_s=[], v_s=[])

    for layer in range(n_even + n_odd):
        i = layer // 2
        if layer % 2 == 0:
            w = w_in_even[i]
            o1, o2, o3 = d_ssd, d_ssd + conv_dim, d_ssd + conv_dim + n_heads
            w_r = jnp.concatenate([w[:, :o2], w[:, o3:], jnp.pad(w[:, o2:o3], ((0, 0), (0, V7X_LANES - n_heads)))],
                                  axis=1).astype(BF16)
            in_args = (_row(norm_mix_even[i]), w_r, _row(gmlp_ln_g[i]), _row(gmlp_ln_b[i]))
            dims = dict(d_ssd=d_ssd, conv_dim=conv_dim, d_gmlp=d_gmlp)
            dt_bias, a_log = _pad_lanes(ssd_dt_bias[i]), _pad_lanes(ssd_a_log[i])
            d_skip = _row(jnp.repeat(ssd_d[i], SSD_HEAD_DIM))
            norm_g = _row(ssd_norm[i])
            conv_w, conv_b = ssd_conv_w[i], _row(ssd_conv_b[i])
            z, xbc, ug, vn, dt = _even_in(xp, *in_args, **dims)
            yn, st = _ssd_prompt(xbc, z, dt, conv_w, conv_b, dt_bias, a_log, d_skip, norm_g, batch=batch, seq=seq)
            gout = _gmlp_prompt(ug, vn, gmlp_w_s[i], gmlp_b_s[i].T)
            n_pairs = n_heads // 2
            st = st.reshape(batch, n_pairs, SSD_STATE, 2, SSD_HEAD_DIM).transpose(0, 1, 3, 4, 2)
            outs['ssm_p'].append(st.reshape(batch, n_heads, SSD_HEAD_DIM, SSD_STATE))
            outs['conv_p'].append(xbc.reshape(batch, seq, conv_dim)[:, seq - (conv_w.shape[0] - 1):])
            ffn = (w_out_even[i].astype(BF16), _row(norm_ffn_even[i]), ffn_w_gate[i].astype(BF16),
                   ffn_w_up[i].astype(BF16), ffn_w_down[i].astype(BF16))
            xp = _even_out(xp, yn, gout, *ffn)
            z, xbc, ug, vn, dt = _even_in(xs, *in_args, **dims)
            expand = (jnp.arange(d_ssd)[None, :] // SSD_HEAD_DIM == jnp.arange(V7X_LANES)[:, None]).astype(F32)
            buf = state_ssd_conv[i].reshape(n_dec, -1)
            newbuf, xc, xdt, bc, da = _ssd_sample_prep(xbc, buf, dt, conv_w, conv_b, dt_bias, a_log, expand)
            h_new, y4 = _ssd_sample_state(state_ssm[i], xdt, da, bc)
            gd = d_gmlp // n_groups_gmlp
            w0 = _row(jnp.repeat(gmlp_w_s[i][:, 0, 0], gd))
            b0 = _row(jnp.repeat(gmlp_b_s[i][:, 0], gd))
            yn, gout = _even_sample_post(y4.reshape(n_dec, d_ssd), xc, z, d_skip, norm_g, ug, vn, w0, b0)
            outs['ssm_s'].append(h_new)
            outs['conv_s'].append(newbuf.reshape(state_ssd_conv[i].shape))
            outs['v_s'].append(vn.reshape(n_dec, 1, d_gmlp))
            xs = _even_out(xs, yn, gout, *ffn)
        else:
            w1, b1 = conf_w1[i].astype(BF16), _row(conf_b1[i])
            router = jnp.pad(moe_router[i], ((0, 0), (0, V7X_LANES - n_experts)))
            router_hi = router.astype(BF16)
            router_lo = (router - router_hi.astype(F32)).astype(BF16)
            tail = (conf_dw_w[i], _row(conf_dw_b[i]), _row(conf_ln_g[i]), _row(conf_ln_b[i]),
                    conf_w2[i].astype(BF16), _row(conf_b2[i]), _row(norm_ffn_odd[i]), router_hi, router_lo)
            n_keep = conf_dw_w.shape[1] - 1
            a = _conf_in(xp, _row(norm_mix_odd[i]), w1, b1)
            outs['conf_p'].append(a.reshape(batch, seq, -1)[:, seq - n_keep:])
            a_s = _conf_in(xs, _row(norm_mix_odd[i]), w1, b1)
            buf = state_conf_conv[i].reshape(n_dec, -1)
            newbuf, *sample_rows = _conf_sample(a_s, buf, xs, tail, n_experts=n_experts)
            outs['conf_s'].append(newbuf.reshape(state_conf_conv[i].shape))
            x_all, hn, route = _conf_prompt(a, xp, tuple(sample_rows), tail, seq=seq, n_experts=n_experts)
            tile_expert, n_tiles, pos, pad_pos, n_rows = _moe_plan(route, n_experts, MOE_TILE)
            xs_sorted = _sc_dispatch_rows(hn, pos, pad_pos, n_rows)
            ys = _moe_ffn(tile_expert, n_tiles, xs_sorted, moe_w_gate[i].astype(BF16), moe_w_up[i].astype(BF16),
                          moe_w_down[i].astype(BF16), tm=MOE_TILE)
            picked = _sc_gather_rows(ys, pos.reshape(-1))
            xp, xs = _moe_combine(x_all, route, _row(final_norm), picked, n_prompt=tp, n_sample=n_dec)

    y_prompt = xp.reshape(batch, seq, d_model)
    y_sample = xs.reshape(n_dec, 1, d_model)
    return (y_prompt, y_sample, jnp.stack(outs['ssm_p']), jnp.stack(outs['conv_p']), jnp.stack(outs['conf_p']),
            jnp.stack(outs['ssm_s']), jnp.stack(outs['conv_s']), jnp.stack(outs['conf_s']), jnp.stack(outs['v_s']))
```

```python
import functools

import jax
import jax.numpy as jnp
from jax import lax
from jax.experimental import pallas as pl
from jax.experimental.pallas import tpu as pltpu
from jax.experimental.pallas import tpu_sc as plsc

F32 = jnp.float32
BF16 = jnp.bfloat16
HIGHEST = lax.Precision.HIGHEST

SSD_HEAD_DIM = 64
SSD_GROUPS = 4
SSD_STATE = 128
SSD_CHUNK = 128
GMLP_CHUNK = 128
TOP_K = 2
RMS_EPS = 1e-6
LN_EPS = 1e-5

V7X_LANES = 128
V7X_SUBLANES = 8
V7X_VMEM_BYTES = 64 * 1024 * 1024
VMEM_LIMIT = (V7X_VMEM_BYTES * 7) // 8

MOE_TILE = 256
COMBINE_TILE = 128
GMLP_TILE = 512
MOE_FF_CHUNK = 256
CONF_TILE = 256
CONF_ACC_ROWS = 128
SC_GATHER_ROWS = 32
SAMPLE_STATE_TILE = 8


def _params(*semantics):
    return pltpu.CompilerParams(dimension_semantics=semantics, vmem_limit_bytes=VMEM_LIMIT)


def _resident(shape):
    zeros = (0,) * len(shape)
    return pl.BlockSpec(shape, lambda *_: zeros, pipeline_mode=pl.Buffered(1))


def _whole(shape):
    zeros = (0,) * len(shape)
    return pl.BlockSpec(shape, lambda *_: zeros)


def _row_tile(n_rows, preferred):
    for t in (preferred, 512, 256, 128):
        if t <= preferred and n_rows % t == 0:
            return t
    raise ValueError(f"row count {n_rows} is not a multiple of {V7X_LANES}")


def _dot(a, b):
    return jnp.dot(a, b, preferred_element_type=F32)


def _rms(x, g):
    return x * lax.rsqrt(jnp.mean(x * x, axis=-1, keepdims=True) + RMS_EPS) * g


def _layernorm(x, g, b):
    mu = jnp.mean(x, axis=-1, keepdims=True)
    xc = x - mu
    var = jnp.mean(xc * xc, axis=-1, keepdims=True)
    return xc * lax.rsqrt(var + LN_EPS) * g + b


def _sigmoid(x):
    return 0.5 * jnp.tanh(0.5 * x) + 0.5


def _silu(x):
    return x * _sigmoid(x)


def _softplus(x):
    return jnp.maximum(x, 0.0) + jnp.log1p(jnp.exp(-jnp.abs(x)))


def _even_in_kernel(x_ref, g_ref, w_ref, lng_ref, lnb_ref, z_ref, xbc_ref, ug_ref, vn_ref, dt_ref,
                    *, d_ssd, conv_dim, d_gmlp):
    h = _rms(x_ref[...], g_ref[...]).astype(BF16)
    o1 = d_ssd
    o2 = o1 + conv_dim
    o3 = o2 + d_gmlp
    o4 = o3 + d_gmlp
    z_ref[...] = _dot(h, w_ref[:, 0:o1])
    xbc_ref[...] = _dot(h, w_ref[:, o1:o2])
    ug_ref[...] = jax.nn.gelu(_dot(h, w_ref[:, o2:o3])).astype(ug_ref.dtype)
    v = jax.nn.gelu(_dot(h, w_ref[:, o3:o4]))
    vn_ref[...] = _layernorm(v, lng_ref[...], lnb_ref[...])
    dt_ref[...] = _dot(h, w_ref[:, o4:o4 + V7X_LANES])


def _even_in(x, g, w, ln_g, ln_b, *, d_ssd, conv_dim, d_gmlp):
    t, d = x.shape
    tm = _row_tile(t, 512)
    row = lambda n: pl.BlockSpec((tm, n), lambda i: (i, 0))
    return pl.pallas_call(
        functools.partial(_even_in_kernel, d_ssd=d_ssd, conv_dim=conv_dim, d_gmlp=d_gmlp),
        grid=(t // tm,),
        in_specs=[row(d), _resident(g.shape), _resident(w.shape), _resident(ln_g.shape), _resident(ln_b.shape)],
        out_specs=[row(d_ssd), row(conv_dim), row(d_gmlp), row(d_gmlp), row(V7X_LANES)],
        out_shape=[jax.ShapeDtypeStruct((t, d_ssd), F32), jax.ShapeDtypeStruct((t, conv_dim), F32),
                   jax.ShapeDtypeStruct((t, d_gmlp), BF16), jax.ShapeDtypeStruct((t, d_gmlp), F32),
                   jax.ShapeDtypeStruct((t, V7X_LANES), F32)],
        compiler_params=_params("parallel"),
        name="even_in",
    )(x, g, w, ln_g, ln_b)


def _ssd_prompt_kernel(xbc_ref, z_ref, dt_ref, cw_ref, cb_ref, dtb_ref, alog_ref, dskip_ref, ng_ref,
                       yn_ref, st_ref, xp_ref, state_ref, xc_ref, *, d_ssd, n_taps):
    c = pl.program_id(1)
    n_pairs = state_ref.shape[0]
    ch = SSD_CHUNK
    gn = SSD_GROUPS * SSD_STATE

    @pl.when(c == 0)
    def _():
        state_ref[...] = jnp.zeros_like(state_ref)
        xp_ref[0:V7X_SUBLANES, :] = jnp.zeros((V7X_SUBLANES, xp_ref.shape[1]), F32)

    xp_ref[V7X_SUBLANES:V7X_SUBLANES + ch, :] = xbc_ref[...]
    base = V7X_SUBLANES - (n_taps - 1)
    acc = cb_ref[...] + cw_ref[n_taps - 1:n_taps, :] * xbc_ref[...]
    for k in range(n_taps - 1):
        xc_ref[...] = xp_ref[base + k:base + k + ch, :]
        acc = acc + cw_ref[k:k + 1, :] * xc_ref[...]
    xp_ref[0:V7X_SUBLANES, :] = xp_ref[ch:ch + V7X_SUBLANES, :]
    xc_ref[...] = _silu(acc)
    xc = xc_ref

    dt = _softplus(dt_ref[...] + dtb_ref[...])
    a = dt * (-jnp.exp(alog_ref[...]))
    li = lax.broadcasted_iota(jnp.int32, (ch, ch), 0)
    si = lax.broadcasted_iota(jnp.int32, (ch, ch), 1)
    causal = li >= si
    tril = jnp.where(causal, 1.0, 0.0).astype(F32)
    acs = jnp.dot(tril, a, precision=HIGHEST, preferred_element_type=F32)
    acs_t = acs.T
    lane = lax.broadcasted_iota(jnp.int32, (ch, 2 * SSD_HEAD_DIM), 1)
    first = lane < SSD_HEAD_DIM
    first_n = lax.broadcasted_iota(jnp.int32, (SSD_STATE, 2 * SSD_HEAD_DIM), 1) < SSD_HEAD_DIM
    pairs_per_group = n_pairs // SSD_GROUPS

    for g in range(SSD_GROUPS):
        bg = xc[:, d_ssd + g * SSD_STATE:d_ssd + (g + 1) * SSD_STATE]
        cg = xc[:, d_ssd + gn + g * SSD_STATE:d_ssd + gn + (g + 1) * SSD_STATE]
        bg_t = bg.T
        cb = _dot(cg.astype(BF16), bg_t.astype(BF16))
        ys = []
        for q in range(pairs_per_group):
            pair = g * pairs_per_group + q
            h0 = 2 * pair
            lo = pair * 2 * SSD_HEAD_DIM
            xs = xc[:, lo:lo + 2 * SSD_HEAD_DIM]
            dt2 = jnp.where(first, dt[:, h0:h0 + 1], dt[:, h0 + 1:h0 + 2])
            xdt = (xs * dt2).astype(BF16)
            s_prev = state_ref[pair]
            s_prev_b = s_prev.astype(BF16)
            y2, snew2, dec2 = [], [], []
            for r in range(2):
                h = h0 + r
                col = acs[:, h:h + 1]
                row = acs_t[h:h + 1, :]
                last = acs_t[h:h + 1, ch - 1:ch]
                decay = jnp.exp(jnp.where(causal, col - row, -jnp.inf))
                y = _dot((cb * decay).astype(BF16), xdt)
                y = y + _dot((cg * jnp.exp(col)).astype(BF16), s_prev_b)
                snew = _dot((bg_t * jnp.exp(last - row)).astype(BF16), xdt)
                y2.append(y)
                snew2.append(snew)
                dec2.append(jnp.exp(last))
            y = jnp.where(first, y2[0], y2[1])
            state_ref[pair] = (s_prev * jnp.where(first_n, dec2[0], dec2[1])
                               + jnp.where(first_n, snew2[0], snew2[1]))
            ys.append(y + dskip_ref[:, lo:lo + 2 * SSD_HEAD_DIM] * xs)
        gw = pairs_per_group * 2 * SSD_HEAD_DIM
        yg = jnp.concatenate(ys, axis=-1) * _silu(z_ref[:, g * gw:(g + 1) * gw])
        yg = yg * lax.rsqrt(jnp.mean(yg * yg, axis=-1, keepdims=True) + RMS_EPS)
        yn_ref[:, g * gw:(g + 1) * gw] = (yg * ng_ref[:, g * gw:(g + 1) * gw]).astype(yn_ref.dtype)

    @pl.when(c == pl.num_programs(1) - 1)
    def _():
        st_ref[0] = state_ref[...]


def _ssd_prompt(xbc, z, dt, conv_w, conv_b, dt_bias, a_log, d_skip, norm_g, *, batch, seq):
    t, conv_dim = xbc.shape
    d_ssd = z.shape[1]
    n_pairs = d_ssd // (2 * SSD_HEAD_DIM)
    nc = seq // SSD_CHUNK
    n_taps = conv_w.shape[0]
    tile = lambda n: pl.BlockSpec((SSD_CHUNK, n), lambda b, c: (b * nc + c, 0))
    return pl.pallas_call(
        functools.partial(_ssd_prompt_kernel, d_ssd=d_ssd, n_taps=n_taps),
        grid=(batch, nc),
        in_specs=[tile(conv_dim), tile(d_ssd), tile(V7X_LANES), _resident(conv_w.shape), _resident(conv_b.shape),
                  _resident(dt_bias.shape), _resident(a_log.shape), _resident(d_skip.shape),
                  _resident(norm_g.shape)],
        out_specs=[tile(d_ssd),
                   pl.BlockSpec((1, n_pairs, SSD_STATE, 2 * SSD_HEAD_DIM), lambda b, c: (b, 0, 0, 0))],
        out_shape=[jax.ShapeDtypeStruct((t, d_ssd), BF16),
                   jax.ShapeDtypeStruct((batch, n_pairs, SSD_STATE, 2 * SSD_HEAD_DIM), F32)],
        scratch_shapes=[pltpu.VMEM((SSD_CHUNK + V7X_SUBLANES, conv_dim), F32),
                        pltpu.VMEM((n_pairs, SSD_STATE, 2 * SSD_HEAD_DIM), F32),
                        pltpu.VMEM((SSD_CHUNK, conv_dim), F32)],
        compiler_params=_params("parallel", "arbitrary"),
        name="ssd_prompt",
    )(xbc, z, dt, conv_w, conv_b, dt_bias, a_log, d_skip, norm_g)


def _gmlp_prompt_kernel(ug_ref, vn_ref, ws_ref, bs_ref, out_ref):
    ch = GMLP_CHUNK
    n_groups = ws_ref.shape[0]
    gd = vn_ref.shape[1] // n_groups
    ii = lax.broadcasted_iota(jnp.int32, (ch, ch), 0)
    jj = lax.broadcasted_iota(jnp.int32, (ch, ch), 1)
    for g in range(n_groups):
        ws = jnp.where(ii >= jj, ws_ref[g], 0.0).astype(BF16)
        cols = slice(g * gd, (g + 1) * gd)
        for r0 in range(0, vn_ref.shape[0], ch):
            rows = slice(r0, r0 + ch)
            mixed = _dot(ws, vn_ref[rows, cols].astype(BF16)) + bs_ref[:, g:g + 1]
            out_ref[rows, cols] = (ug_ref[rows, cols].astype(F32) * mixed).astype(out_ref.dtype)


def _gmlp_prompt(ug, vn, w_s, b_s_t, *, seq):
    t, d = vn.shape
    rows = _row_tile(seq, GMLP_TILE)
    assert rows % GMLP_CHUNK == 0
    tile = pl.BlockSpec((rows, d), lambda i: (i, 0))
    return pl.pallas_call(
        _gmlp_prompt_kernel,
        grid=(t // rows,),
        in_specs=[tile, tile, _resident(w_s.shape), _resident(b_s_t.shape)],
        out_specs=tile,
        out_shape=jax.ShapeDtypeStruct((t, d), BF16),
        compiler_params=_params("parallel"),
        name="gmlp_prompt",
    )(ug, vn, w_s, b_s_t)


def _ssd_sample_prep_kernel(xbc_ref, buf_ref, dt_ref, cw_ref, cb_ref, dtb_ref, alog_ref, expand_ref,
                            newbuf_ref, xs_ref, xdt_ref, bc_ref, da_ref, *, d_ssd, n_taps):
    conv_dim = xbc_ref.shape[1]
    x_new = xbc_ref[...]
    acc = cb_ref[...] + cw_ref[n_taps - 1:n_taps, :] * x_new
    for k in range(n_taps - 1):
        acc = acc + cw_ref[k:k + 1, :] * buf_ref[:, k * conv_dim:(k + 1) * conv_dim]
    for k in range(n_taps - 2):
        newbuf_ref[:, k * conv_dim:(k + 1) * conv_dim] = buf_ref[:, (k + 1) * conv_dim:(k + 2) * conv_dim]
    newbuf_ref[:, (n_taps - 2) * conv_dim:(n_taps - 1) * conv_dim] = x_new
    xc = _silu(acc)
    xs = xc[:, :d_ssd]
    dt = _softplus(dt_ref[...] + dtb_ref[...])
    da_ref[...] = jnp.exp(dt * (-jnp.exp(alog_ref[...])))
    dt_wide = jnp.dot(dt, expand_ref[...], precision=HIGHEST, preferred_element_type=F32)
    xs_ref[...] = xs
    xdt_ref[...] = xs * dt_wide
    bc_ref[...] = xc[:, d_ssd:]


def _ssd_sample_prep(xbc, buf, dt, conv_w, conv_b, dt_bias, a_log, expand):
    n, conv_dim = xbc.shape
    d_ssd = expand.shape[1]
    n_taps = conv_w.shape[0]
    args = (xbc, buf, dt, conv_w, conv_b, dt_bias, a_log, expand)
    return pl.pallas_call(
        functools.partial(_ssd_sample_prep_kernel, d_ssd=d_ssd, n_taps=n_taps),
        grid=(1,),
        in_specs=[_resident(a.shape) for a in args],
        out_specs=[_whole(buf.shape), _whole((n, d_ssd)), _whole((n, d_ssd)),
                   _whole((n, conv_dim - d_ssd)), _whole((n, V7X_LANES))],
        out_shape=[jax.ShapeDtypeStruct(buf.shape, F32), jax.ShapeDtypeStruct((n, d_ssd), F32),
                   jax.ShapeDtypeStruct((n, d_ssd), F32), jax.ShapeDtypeStruct((n, conv_dim - d_ssd), F32),
                   jax.ShapeDtypeStruct((n, V7X_LANES), F32)],
        compiler_params=_params("arbitrary"),
        name="ssd_sample_prep",
    )(*args)


def _ssd_sample_state_kernel(h_ref, xdt_ref, da_ref, bc_ref, hnew_ref, y_ref, xcol_ref, ccol_ref):
    bs, n_heads, p, s = h_ref.shape
    n_pairs = n_heads // 2
    pairs_per_group = n_pairs // SSD_GROUPS
    gn = SSD_GROUPS * s
    w = 2 * p
    eye = lax.broadcasted_iota(jnp.int32, (w, w), 0) == lax.broadcasted_iota(jnp.int32, (w, w), 1)
    ones = jnp.ones((w, V7X_LANES), BF16)

    def stacked_diag(v):
        return jnp.concatenate(
            [jnp.where(eye, jnp.broadcast_to(v[j:j + 1], (w, w)), 0.0) for j in range(bs)], axis=0).astype(BF16)

    for q in range(n_pairs):
        x = xdt_ref[:, q * w:(q + 1) * w]
        hi = x.astype(BF16).astype(F32)
        mid = (x - hi).astype(BF16).astype(F32)
        lo = (x - hi) - mid
        xcol_ref[q] = _dot(stacked_diag(hi), ones) + (_dot(stacked_diag(mid), ones) + _dot(stacked_diag(lo), ones))
    for g in range(SSD_GROUPS):
        ccol_ref[g] = _dot(stacked_diag(bc_ref[:, gn + g * s:gn + (g + 1) * s]), ones).astype(BF16)

    for j in range(bs):
        for g in range(SSD_GROUPS):
            brow = bc_ref[j:j + 1, g * s:(g + 1) * s]
            h_new = []
            for q in range(g * pairs_per_group, (g + 1) * pairs_per_group):
                for r in range(2):
                    h = 2 * q + r
                    xcol = xcol_ref[q, j * w + r * p:j * w + (r + 1) * p, :]
                    h_new.append(h_ref[j, h] * da_ref[j:j + 1, h:h + 1] + xcol * brow)
                    hnew_ref[j, h] = h_new[-1]
            hc = _dot(jnp.concatenate(h_new, axis=0).astype(BF16), ccol_ref[g, j * s:(j + 1) * s, :])
            for i in range(pairs_per_group):
                y = jnp.sum(jnp.where(eye, hc[i * w:(i + 1) * w], 0.0), axis=0, keepdims=True)
                q = g * pairs_per_group + i
                y_ref[j, q:q + 1, :] = y


def _ssd_sample_state(h, xdt, da, bc):
    n, n_heads, p, s = h.shape
    assert 2 * p == V7X_LANES and s == V7X_LANES
    bs = SAMPLE_STATE_TILE
    rows = lambda w: pl.BlockSpec((bs, w), lambda i: (i, 0))
    return pl.pallas_call(
        _ssd_sample_state_kernel,
        grid=(n // bs,),
        in_specs=[pl.BlockSpec((bs, n_heads, p, s), lambda i: (i, 0, 0, 0)), rows(xdt.shape[1]), rows(da.shape[1]),
                  rows(bc.shape[1])],
        out_specs=[pl.BlockSpec((bs, n_heads, p, s), lambda i: (i, 0, 0, 0)),
                   pl.BlockSpec((bs, n_heads // 2, 2 * p), lambda i: (i, 0, 0))],
        out_shape=[jax.ShapeDtypeStruct(h.shape, F32), jax.ShapeDtypeStruct((n, n_heads // 2, 2 * p), F32)],
        scratch_shapes=[pltpu.VMEM((n_heads // 2, bs * 2 * p, V7X_LANES), F32),
                        pltpu.VMEM((SSD_GROUPS, bs * s, V7X_LANES), BF16)],
        compiler_params=_params("parallel"),
        name="ssd_sample_state",
    )(h, xdt, da, bc)


def _even_sample_post_kernel(y_ref, xs_ref, z_ref, dskip_ref, ng_ref, ug_ref, vn_ref, w0_ref, b0_ref,
                             yn_ref, gout_ref):
    d_ssd = y_ref.shape[1]
    gw = d_ssd // SSD_GROUPS
    y = (y_ref[...] + dskip_ref[...] * xs_ref[...]) * _silu(z_ref[...])
    for g in range(SSD_GROUPS):
        yg = y[:, g * gw:(g + 1) * gw]
        yg = yg * lax.rsqrt(jnp.mean(yg * yg, axis=-1, keepdims=True) + RMS_EPS)
        yn_ref[:, g * gw:(g + 1) * gw] = (yg * ng_ref[:, g * gw:(g + 1) * gw]).astype(yn_ref.dtype)
    mixed = w0_ref[...] * vn_ref[...] + b0_ref[...]
    gout_ref[...] = (ug_ref[...].astype(F32) * mixed).astype(gout_ref.dtype)


def _even_sample_post(y, xs, z, d_skip, norm_g, ug, vn, w0, b0):
    args = (y, xs, z, d_skip, norm_g, ug, vn, w0, b0)
    return pl.pallas_call(
        _even_sample_post_kernel,
        grid=(1,),
        in_specs=[_resident(a.shape) for a in args],
        out_specs=[_whole(y.shape), _whole(vn.shape)],
        out_shape=[jax.ShapeDtypeStruct(y.shape, BF16), jax.ShapeDtypeStruct(vn.shape, BF16)],
        compiler_params=_params("arbitrary"),
        name="even_sample_post",
    )(*args)


def _even_out_kernel(x_ref, yn_ref, gout_ref, wo_ref, g_ref, wg_ref, wu_ref, wd_ref, out_ref):
    d_ssd = yn_ref.shape[1]
    x = x_ref[...] + _dot(yn_ref[...], wo_ref[0:d_ssd, :]) + _dot(gout_ref[...], wo_ref[d_ssd:, :])
    h = _rms(x, g_ref[...]).astype(BF16)
    mid = (_silu(_dot(h, wg_ref[...])) * _dot(h, wu_ref[...])).astype(BF16)
    out_ref[...] = x + _dot(mid, wd_ref[...])


def _even_out(x, yn, gout, w_out, g, wg, wu, wd):
    t, d = x.shape
    tm = _row_tile(t, 512)
    row = lambda n: pl.BlockSpec((tm, n), lambda i: (i, 0))
    return pl.pallas_call(
        _even_out_kernel,
        grid=(t // tm,),
        in_specs=[row(d), row(yn.shape[1]), row(gout.shape[1]), _resident(w_out.shape), _resident(g.shape),
                  _resident(wg.shape), _resident(wu.shape), _resident(wd.shape)],
        out_specs=row(d),
        out_shape=jax.ShapeDtypeStruct((t, d), F32),
        compiler_params=_params("parallel"),
        name="even_out_ffn",
    )(x, yn, gout, w_out, g, wg, wu, wd)


def _conf_in_kernel(x_ref, g_ref, w_ref, b_ref, a_ref):
    d = a_ref.shape[1]
    h = _rms(x_ref[...], g_ref[...]).astype(BF16)
    val = _dot(h, w_ref[:, 0:d]) + b_ref[:, 0:d]
    gate = _dot(h, w_ref[:, d:]) + b_ref[:, d:]
    a_ref[...] = val * _sigmoid(gate)


def _conf_in(x, g, w1, b1):
    t, d = x.shape
    dc = w1.shape[1] // 2
    tm = _row_tile(t, 512)
    return pl.pallas_call(
        _conf_in_kernel,
        grid=(t // tm,),
        in_specs=[pl.BlockSpec((tm, d), lambda i: (i, 0)), _resident(g.shape), _resident(w1.shape),
                  _resident(b1.shape)],
        out_specs=pl.BlockSpec((tm, dc), lambda i: (i, 0)),
        out_shape=jax.ShapeDtypeStruct((t, dc), F32),
        compiler_params=_params("parallel"),
        name="conf_in",
    )(x, g, w1, b1)


def _route(logits, n_experts, route_ref):
    lane = lax.broadcasted_iota(jnp.int32, logits.shape, 1)
    lane_f = lane.astype(F32)
    big = jnp.float32(V7X_LANES)
    lg = jnp.where(lane < n_experts, logits, -jnp.inf)
    m1 = jnp.max(lg, axis=-1, keepdims=True)
    i1 = jnp.min(jnp.where(lg == m1, lane_f, big), axis=-1, keepdims=True)
    lg2 = jnp.where(lane_f == i1, -jnp.inf, lg)
    m2 = jnp.max(lg2, axis=-1, keepdims=True)
    i2 = jnp.min(jnp.where(lg2 == m2, lane_f, big), axis=-1, keepdims=True)
    e2 = jnp.exp(m2 - m1)
    g1 = 1.0 / (1.0 + e2)
    g2 = e2 / (1.0 + e2)
    route_ref[...] = jnp.where(lane == 0, i1, jnp.where(lane == 1, i2, jnp.where(lane == 2, g1,
                               jnp.where(lane == 3, g2, 0.0))))


def _conf_finish(c, x, lng_ref, lnb_ref, w2_ref, b2_ref, ng_ref, rhi_ref, rlo_ref, out_ref, hn_ref, route_ref,
                 *, n_experts):
    hmid = _silu(_layernorm(c, lng_ref[...], lnb_ref[...])).astype(BF16)
    x_new = x + _dot(hmid, w2_ref[...]) + b2_ref[...]
    out_ref[...] = x_new
    hn = _rms(x_new, ng_ref[...])
    hn_ref[...] = hn
    hi = hn.astype(BF16)
    lo = (hn - hi.astype(F32)).astype(BF16)
    logits = _dot(hi, rhi_ref[...]) + (_dot(lo, rhi_ref[...]) + _dot(hi, rlo_ref[...]))
    _route(logits, n_experts, route_ref)


def _conf_prompt_kernel(a_ref, x_ref, xs_ref, hns_ref, routes_ref, dw_ref, db_ref, lng_ref, lnb_ref, w2_ref, b2_ref,
                        ng_ref, rhi_ref, rlo_ref, out_ref, hn_ref, route_ref, xp_ref, c_ref, win_ref,
                        *, n_taps, n_experts, tiles_per_seq, n_prompt_tiles):
    tl, d = a_ref.shape
    halo = xp_ref.shape[0] - tl
    base = halo - (n_taps - 1)
    rows = CONF_ACC_ROWS
    step = pl.program_id(0)

    @pl.when(step < n_prompt_tiles)
    def _():
        @pl.when(step % tiles_per_seq == 0)
        def _():
            xp_ref[0:halo, :] = jnp.zeros((halo, d), F32)

        xp_ref[halo:halo + tl, :] = a_ref[...]

        def lane_block(j, carry):
            cols = pl.ds(pl.multiple_of(j * V7X_LANES, V7X_LANES), V7X_LANES)
            for r0 in range(0, tl, rows):
                acc = jnp.broadcast_to(db_ref[:, cols], (rows, V7X_LANES))
                for phase in range(V7X_SUBLANES):
                    taps = [k for k in range(n_taps) if (base + k) % V7X_SUBLANES == phase]
                    if not taps:
                        continue
                    span = max(base + k - phase for k in taps) + rows
                    win_ref[0:span, :] = xp_ref[pl.ds(r0 + phase, span), cols]
                    for k in taps:
                        o = base + k - phase
                        acc = acc + dw_ref[k:k + 1, cols] * win_ref[o:o + rows, :]
                c_ref[r0:r0 + rows, cols] = acc
            return carry

        lax.fori_loop(0, d // V7X_LANES, lane_block, 0)
        xp_ref[0:halo, :] = xp_ref[tl:tl + halo, :]
        _conf_finish(c_ref[...], x_ref[...], lng_ref, lnb_ref, w2_ref, b2_ref, ng_ref, rhi_ref, rlo_ref,
                     out_ref, hn_ref, route_ref, n_experts=n_experts)

    @pl.when(step == n_prompt_tiles)
    def _():
        ns = xs_ref.shape[0]
        out_ref[0:ns, :] = xs_ref[...]
        hn_ref[0:ns, :] = hns_ref[...]
        route_ref[0:ns, :] = routes_ref[...]


def _conf_prompt(a, x, sample_rows, consts, *, seq, n_experts):
    t, d = a.shape
    n_s = sample_rows[0].shape[0]
    tl = _row_tile(seq, CONF_TILE)
    assert tl % CONF_ACC_ROWS == 0 and n_s <= tl
    n_prompt_tiles = t // tl
    n_taps = consts[0].shape[0]
    halo = -(-(n_taps - 1) // V7X_SUBLANES) * V7X_SUBLANES
    in_tile = lambda n: pl.BlockSpec((tl, n), lambda s: (jnp.minimum(s, n_prompt_tiles - 1), 0))
    out_tile = lambda n: pl.BlockSpec((tl, n), lambda s: (s, 0))
    small = sample_rows + tuple(consts)
    return pl.pallas_call(
        functools.partial(_conf_prompt_kernel, n_taps=n_taps, n_experts=n_experts, tiles_per_seq=seq // tl,
                          n_prompt_tiles=n_prompt_tiles),
        grid=(n_prompt_tiles + 1,),
        in_specs=[in_tile(d), in_tile(d)] + [_resident(c.shape) for c in small],
        out_specs=[out_tile(d), out_tile(d), out_tile(V7X_LANES)],
        out_shape=[jax.ShapeDtypeStruct((t + n_s, d), F32), jax.ShapeDtypeStruct((t + n_s, d), F32),
                   jax.ShapeDtypeStruct((t + n_s, V7X_LANES), F32)],
        scratch_shapes=[pltpu.VMEM((tl + halo, d), F32), pltpu.VMEM((tl, d), F32),
                        pltpu.VMEM((CONF_ACC_ROWS + halo, V7X_LANES), F32)],
        compiler_params=_params("arbitrary"),
        name="conf_prompt",
    )(a, x, *small)


def _conf_sample_kernel(a_ref, buf_ref, x_ref, dw_ref, db_ref, lng_ref,
                        lnb_ref, w2_ref, b2_ref, ng_ref, rhi_ref, rlo_ref, newbuf_ref, out_ref, hn_ref, route_ref,
                        *, n_taps, n_experts):
    d = a_ref.shape[1]
    a_new = a_ref[...]
    acc = db_ref[...] + dw_ref[n_taps - 1:n_taps, :] * a_new
    for k in range(n_taps - 1):
        acc = acc + dw_ref[k:k + 1, :] * buf_ref[:, k * d:(k + 1) * d]
    for k in range(n_taps - 2):
        newbuf_ref[:, k * d:(k + 1) * d] = buf_ref[:, (k + 1) * d:(k + 2) * d]
    newbuf_ref[:, (n_taps - 2) * d:(n_taps - 1) * d] = a_new
    _conf_finish(acc, x_ref[...], lng_ref, lnb_ref, w2_ref, b2_ref, ng_ref, rhi_ref, rlo_ref,
                 out_ref, hn_ref, route_ref, n_experts=n_experts)


def _conf_sample(a, buf, x, consts, *, n_experts):
    n, d = a.shape
    args = (a, buf, x) + tuple(consts)
    return pl.pallas_call(
        functools.partial(_conf_sample_kernel, n_taps=consts[0].shape[0], n_experts=n_experts),
        grid=(1,),
        in_specs=[_resident(v.shape) for v in args],
        out_specs=[_whole(buf.shape), _whole((n, d)), _whole((n, d)), _whole((n, V7X_LANES))],
        out_shape=[jax.ShapeDtypeStruct(buf.shape, F32), jax.ShapeDtypeStruct((n, d), F32),
                   jax.ShapeDtypeStruct((n, d), F32), jax.ShapeDtypeStruct((n, V7X_LANES), F32)],
        compiler_params=_params("arbitrary"),
        name="conf_sample",
    )(*args)


def _sc_gather_rows(x, idx):
    n, d, w = idx.shape[0], x.shape[1], SC_GATHER_ROWS
    assert n % w == 0
    mesh = plsc.VectorSubcoreMesh(core_axis_name="core", subcore_axis_name="subcore")

    @functools.partial(pl.kernel, out_type=jax.ShapeDtypeStruct((n, d), x.dtype), mesh=mesh, scratch_types=[])
    def gather(x_hbm, idx_hbm, out_hbm):
        def body(idx_vmem, out_vmem):
            pltpu.sync_copy(x_hbm.at[idx_vmem.at[0]], out_vmem)

        pltpu.emit_pipeline(
            body,
            grid=(n // w,),
            in_specs=[pl.BlockSpec((1, w), lambda i: (i, 0))],
            out_specs=[pl.BlockSpec((w, d), lambda i: (i, 0))],
            core_axis_name=("core", "subcore"),
            dimension_semantics=(pltpu.PARALLEL,),
        )(idx_hbm, out_hbm)

    return gather(x, idx.reshape(n // w, w))


def _sc_dispatch_rows(x, pos, pad_pos, n_out):
    t, d = x.shape
    w = SC_GATHER_ROWS
    n_pad = pad_pos.shape[0]
    assert t % w == 0 and n_pad % w == 0 and TOP_K * t + n_pad == n_out
    mesh = plsc.VectorSubcoreMesh(core_axis_name="core", subcore_axis_name="subcore")

    @functools.partial(pl.kernel, out_type=jax.ShapeDtypeStruct((n_out, d), x.dtype), mesh=mesh, scratch_types=[])
    def dispatch(x_hbm, pos_hbm, pad_hbm, zero_hbm, out_hbm):
        def body(rows_vmem, idx_vmem):
            pltpu.sync_copy(rows_vmem, out_hbm.at[idx_vmem.at[0]])

        def scatter(src_hbm, idx_hbm, src_map):
            pltpu.emit_pipeline(
                body,
                grid=(idx_hbm.shape[0],),
                in_specs=[pl.BlockSpec((w, d), src_map), pl.BlockSpec((1, w), lambda i: (i, 0))],
                out_specs=[],
                core_axis_name=("core", "subcore"),
                dimension_semantics=(pltpu.PARALLEL,),
            )(src_hbm, idx_hbm)

        for k in range(TOP_K):
            scatter(x_hbm, pos_hbm.at[k], lambda i: (i, 0))
        scatter(zero_hbm, pad_hbm, lambda i: (0, 0))

    return dispatch(x, pos.reshape(TOP_K, t // w, w), pad_pos.reshape(n_pad // w, w), jnp.zeros((w, d), x.dtype))


def _moe_ffn_kernel(tile_expert_ref, first_ref, n_tiles_ref, x_ref, wg_hbm, wu_hbm, wd_hbm, out_ref,
                    wg_v, wu_v, wd_v, stage_g, stage_u, stage_d, sems):
    i = pl.program_id(0)
    e = tile_expert_ref[i]
    dff = wg_v.shape[1]
    ck = stage_g.shape[2]
    n_chunks = dff // ck
    active = i < n_tiles_ref[0]

    def chunk_copies(c, slot):
        cols = slice(c * ck, (c + 1) * ck)
        return (pltpu.make_async_copy(wg_hbm.at[e, :, cols], stage_g.at[slot], sems.at[0, slot]),
                pltpu.make_async_copy(wu_hbm.at[e, :, cols], stage_u.at[slot], sems.at[1, slot]),
                pltpu.make_async_copy(wd_hbm.at[e, cols, :], stage_d.at[slot], sems.at[2, slot]))

    @pl.when(active & (first_ref[i] == 1))
    def _():
        for cp in chunk_copies(0, 0):
            cp.start()
        for c in range(n_chunks):
            slot = c % 2
            if c + 1 < n_chunks:
                for cp in chunk_copies(c + 1, 1 - slot):
                    cp.start()
            for cp in chunk_copies(c, slot):
                cp.wait()
            cols = slice(c * ck, (c + 1) * ck)
            wg_v[:, cols] = stage_g[slot].astype(BF16)
            wu_v[:, cols] = stage_u[slot].astype(BF16)
            wd_v[cols, :] = stage_d[slot].astype(BF16)

    @pl.when(active)
    def _():
        xb = x_ref[...].astype(BF16)
        half = dff // 2
        acc = None
        for j in range(2):
            cols = slice(j * half, (j + 1) * half)
            mid = (_silu(_dot(xb, wg_v[:, cols])) * _dot(xb, wu_v[:, cols])).astype(BF16)
            part = _dot(mid, wd_v[cols, :])
            acc = part if acc is None else acc + part
        out_ref[...] = acc

    @pl.when(jnp.logical_not(active))
    def _():
        out_ref[...] = jnp.zeros_like(out_ref)


def _moe_ffn(tile_expert, n_tiles, xs, wg, wu, wd, *, tm):
    p, d = xs.shape
    dff = wg.shape[2]
    ck = MOE_FF_CHUNK
    assert dff % ck == 0
    nt = p // tm
    first = jnp.concatenate([jnp.ones((1,), jnp.int32), (tile_expert[1:] != tile_expert[:-1]).astype(jnp.int32)])
    any_spec = pl.BlockSpec(memory_space=pl.ANY)
    grid_spec = pltpu.PrefetchScalarGridSpec(
        num_scalar_prefetch=3,
        grid=(nt,),
        in_specs=[pl.BlockSpec((tm, d), lambda i, te, fi, n: (i, 0)), any_spec, any_spec, any_spec],
        out_specs=pl.BlockSpec((tm, d), lambda i, te, fi, n: (i, 0)),
        scratch_shapes=[pltpu.VMEM((d, dff), BF16), pltpu.VMEM((d, dff), BF16), pltpu.VMEM((dff, d), BF16),
                        pltpu.VMEM((2, d, ck), F32), pltpu.VMEM((2, d, ck), F32), pltpu.VMEM((2, ck, d), F32),
                        pltpu.SemaphoreType.DMA((3, 2))],
    )
    return pl.pallas_call(
        _moe_ffn_kernel,
        grid_spec=grid_spec,
        out_shape=jax.ShapeDtypeStruct((p, d), F32),
        compiler_params=_params("arbitrary"),
        name="moe_ffn",
    )(tile_expert, first, n_tiles, xs, wg, wu, wd)


def _moe_combine_kernel(x_ref, route_ref, gf_ref, a_ref, b_ref, yp_ref, ys_ref, *, n_prompt_tiles):
    i = pl.program_id(0)
    route = route_ref[...]
    x = x_ref[...] + route[:, 2:3] * a_ref[...] + route[:, 3:4] * b_ref[...]
    y = _rms(x, gf_ref[...])

    @pl.when(i < n_prompt_tiles)
    def _():
        yp_ref[...] = y

    @pl.when(i >= n_prompt_tiles)
    def _():
        ys_ref[...] = y


def _moe_combine(x, route, gf, picked, *, n_prompt, n_sample):
    t, d = x.shape
    tc = COMBINE_TILE
    assert n_sample == tc and n_prompt % tc == 0
    n_prompt_tiles = n_prompt // tc
    n_tiles = t // tc
    return pl.pallas_call(
        functools.partial(_moe_combine_kernel, n_prompt_tiles=n_prompt_tiles),
        grid=(n_tiles,),
        in_specs=[pl.BlockSpec((tc, d), lambda i: (i, 0)),
                  pl.BlockSpec((tc, V7X_LANES), lambda i: (i, 0)),
                  _resident(gf.shape),
                  pl.BlockSpec((tc, d), lambda i: (i, 0)),
                  pl.BlockSpec((tc, d), lambda i: (i + n_tiles, 0))],
        out_specs=[pl.BlockSpec((tc, d), lambda i: (jnp.minimum(i, n_prompt_tiles - 1), 0)),
                   pl.BlockSpec((tc, d), lambda i: (0, 0))],
        out_shape=[jax.ShapeDtypeStruct((n_prompt, d), F32), jax.ShapeDtypeStruct((n_sample, d), F32)],
        compiler_params=_params("arbitrary"),
        name="moe_combine",
    )(x, route, gf, picked, picked)


def _moe_plan(route, n_experts, tm):
    t = route.shape[0]
    n_slots = t * TOP_K
    i32 = jnp.int32
    expert = route[:, :TOP_K].astype(i32).reshape(n_slots)
    onehot = (expert[:, None] == jnp.arange(n_experts, dtype=i32)[None, :]).astype(i32)
    counts = jnp.sum(onehot, axis=0)
    tiles = (counts + tm - 1) // tm
    tile_end = jnp.cumsum(tiles)
    start = (tile_end - tiles) * tm
    rank = jnp.cumsum(onehot, axis=0) - onehot
    pos = jnp.sum(onehot * (rank + start[None, :]), axis=1)
    nt = -(-n_slots // tm) + n_experts
    n_tiles = tile_end[-1]
    tile_id = jnp.minimum(jnp.arange(nt, dtype=i32), n_tiles - 1)
    tile_expert = jnp.sum((tile_id[:, None] >= tile_end[None, :]).astype(i32), axis=1)
    seg_start = jnp.concatenate([start + counts, (n_tiles * tm)[None]])
    seg_len = jnp.concatenate([tiles * tm - counts, (nt * tm - n_tiles * tm)[None]])
    seg_end = jnp.cumsum(seg_len)
    q = jnp.arange(nt * tm - n_slots, dtype=i32)
    in_seg = (q[:, None] >= (seg_end - seg_len)[None, :]) & (q[:, None] < seg_end[None, :])
    pad_pos = jnp.sum(in_seg.astype(i32) * (seg_start - (seg_end - seg_len))[None, :], axis=1) + q
    return (tile_expert.astype(i32), n_tiles.astype(i32).reshape(1), pos.reshape(t, TOP_K).T.astype(i32),
            pad_pos.astype(i32), nt * tm)


def _row(v):
    return v.reshape(1, -1).astype(F32)


def _pad_lanes(v):
    return jnp.pad(_row(v), ((0, 0), (0, V7X_LANES - v.shape[-1])))


def kernel(x_prompt, x_sample, state_ssm, state_ssd_conv, state_conf_conv, norm_mix_even, w_in_even, ssd_conv_w, ssd_conv_b, ssd_dt_bias, ssd_a_log, ssd_d, ssd_norm, gmlp_ln_g, gmlp_ln_b, gmlp_w_s, gmlp_b_s, w_out_even, norm_ffn_even, ffn_w_gate, ffn_w_up, ffn_w_down, norm_mix_odd, conf_w1, conf_b1, conf_dw_w, conf_dw_b, conf_ln_g, conf_ln_b, conf_w2, conf_b2, norm_ffn_odd, moe_router, moe_w_gate, moe_w_up, moe_w_down, final_norm):
    batch, seq, d_model = x_prompt.shape
    n_dec, dec_seq, _ = x_sample.shape
    assert dec_seq == 1, "the sample group advances one token per sequence"
    assert seq % SSD_CHUNK == 0 and n_dec % V7X_LANES == 0
    n_even, n_odd = w_in_even.shape[0], conf_w1.shape[0]
    assert (n_even, n_odd) == (1, 1), "the final norm is fused into the last (odd) layer's MoE combine"
    n_heads = ssd_dt_bias.shape[1]
    d_ssd = n_heads * SSD_HEAD_DIM
    conv_dim = ssd_conv_w.shape[2]
    d_gmlp = gmlp_ln_g.shape[1]
    n_groups_gmlp = gmlp_w_s.shape[1]
    n_experts = moe_router.shape[2]
    tp = batch * seq

    xp = x_prompt.reshape(tp, d_model)
    xs = x_sample.reshape(n_dec, d_model)
    outs = dict(ssm_p=[], conv_p=[], conf_p=[], ssm_s=[], conv_s=[], conf_s=[], v_s=[])

    for layer in range(n_even + n_odd):
        i = layer // 2
        if layer % 2 == 0:
            w = w_in_even[i]
            o1, o2, o3 = d_ssd, d_ssd + conv_dim, d_ssd + conv_dim + n_heads
            w_r = jnp.concatenate([w[:, :o2], w[:, o3:], jnp.pad(w[:, o2:o3], ((0, 0), (0, V7X_LANES - n_heads)))],
                                  axis=1).astype(BF16)
            in_args = (_row(norm_mix_even[i]), w_r, _row(gmlp_ln_g[i]), _row(gmlp_ln_b[i]))
            dims = dict(d_ssd=d_ssd, conv_dim=conv_dim, d_gmlp=d_gmlp)
            dt_bias, a_log = _pad_lanes(ssd_dt_bias[i]), _pad_lanes(ssd_a_log[i])
            d_skip = _row(jnp.repeat(ssd_d[i], SSD_HEAD_DIM))
            norm_g = _row(ssd_norm[i])
            conv_w, conv_b = ssd_conv_w[i], _row(ssd_conv_b[i])
            z, xbc, ug, vn, dt = _even_in(xp, *in_args, **dims)
            yn, st = _ssd_prompt(xbc, z, dt, conv_w, conv_b, dt_bias, a_log, d_skip, norm_g, batch=batch, seq=seq)
            gout = _gmlp_prompt(ug, vn, gmlp_w_s[i], gmlp_b_s[i].T, seq=seq)
            n_pairs = n_heads // 2
            st = st.reshape(batch, n_pairs, SSD_STATE, 2, SSD_HEAD_DIM).transpose(0, 1, 3, 4, 2)
            outs['ssm_p'].append(st.reshape(batch, n_heads, SSD_HEAD_DIM, SSD_STATE))
            outs['conv_p'].append(xbc.reshape(batch, seq, conv_dim)[:, seq - (conv_w.shape[0] - 1):])
            ffn = (w_out_even[i].astype(BF16), _row(norm_ffn_even[i]), ffn_w_gate[i].astype(BF16),
                   ffn_w_up[i].astype(BF16), ffn_w_down[i].astype(BF16))
            xp = _even_out(xp, yn, gout, *ffn)
            z, xbc, ug, vn, dt = _even_in(xs, *in_args, **dims)
            expand = (jnp.arange(d_ssd)[None, :] // SSD_HEAD_DIM == jnp.arange(V7X_LANES)[:, None]).astype(F32)
            buf = state_ssd_conv[i].reshape(n_dec, -1)
            newbuf, xc, xdt, bc, da = _ssd_sample_prep(xbc, buf, dt, conv_w, conv_b, dt_bias, a_log, expand)
            h_new, y4 = _ssd_sample_state(state_ssm[i], xdt, da, bc)
            gd = d_gmlp // n_groups_gmlp
            w0 = _row(jnp.repeat(gmlp_w_s[i][:, 0, 0], gd))
            b0 = _row(jnp.repeat(gmlp_b_s[i][:, 0], gd))
            yn, gout = _even_sample_post(y4.reshape(n_dec, d_ssd), xc, z, d_skip, norm_g, ug, vn, w0, b0)
            outs['ssm_s'].append(h_new)
            outs['conv_s'].append(newbuf.reshape(state_ssd_conv[i].shape))
            outs['v_s'].append(vn.reshape(n_dec, 1, d_gmlp))
            xs = _even_out(xs, yn, gout, *ffn)
        else:
            w1, b1 = conf_w1[i].astype(BF16), _row(conf_b1[i])
            router = jnp.pad(moe_router[i], ((0, 0), (0, V7X_LANES - n_experts)))
            router_hi = router.astype(BF16)
            router_lo = (router - router_hi.astype(F32)).astype(BF16)
            tail = (conf_dw_w[i], _row(conf_dw_b[i]), _row(conf_ln_g[i]), _row(conf_ln_b[i]),
                    conf_w2[i].astype(BF16), _row(conf_b2[i]), _row(norm_ffn_odd[i]), router_hi, router_lo)
            n_keep = conf_dw_w.shape[1] - 1
            a = _conf_in(xp, _row(norm_mix_odd[i]), w1, b1)
            outs['conf_p'].append(a.reshape(batch, seq, -1)[:, seq - n_keep:])
            a_s = _conf_in(xs, _row(norm_mix_odd[i]), w1, b1)
            buf = state_conf_conv[i].reshape(n_dec, -1)
            newbuf, *sample_rows = _conf_sample(a_s, buf, xs, tail, n_experts=n_experts)
            outs['conf_s'].append(newbuf.reshape(state_conf_conv[i].shape))
            x_all, hn, route = _conf_prompt(a, xp, tuple(sample_rows), tail, seq=seq, n_experts=n_experts)
            tile_expert, n_tiles, pos, pad_pos, n_rows = _moe_plan(route, n_experts, MOE_TILE)
            xs_sorted = _sc_dispatch_rows(hn, pos, pad_pos, n_rows)
            ys = _moe_ffn(tile_expert, n_tiles, xs_sorted, moe_w_gate[i], moe_w_up[i], moe_w_down[i], tm=MOE_TILE)
            picked = _sc_gather_rows(ys, pos.reshape(-1))
            xp, xs = _moe_combine(x_all, route, _row(final_norm), picked, n_prompt=tp, n_sample=n_dec)

    y_prompt = xp.reshape(batch, seq, d_model)
    y_sample = xs.reshape(n_dec, 1, d_model)
    return (y_prompt, y_sample, jnp.stack(outs['ssm_p']), jnp.stack(outs['conv_p']), jnp.stack(outs['conf_p']),
            jnp.stack(outs['ssm_s']), jnp.stack(outs['conv_s']), jnp.stack(outs['conf_s']), jnp.stack(outs['v_s']))
```

```python
import functools

import jax
import jax.numpy as jnp
from jax import lax
from jax.experimental import pallas as pl
from jax.experimental.pallas import tpu as pltpu
from jax.experimental.pallas import tpu_sc as plsc

F32 = jnp.float32
BF16 = jnp.bfloat16
HIGHEST = lax.Precision.HIGHEST

SSD_HEAD_DIM = 64
SSD_GROUPS = 4
SSD_STATE = 128
SSD_CHUNK = 128
GMLP_CHUNK = 128
TOP_K = 2
RMS_EPS = 1e-6
LN_EPS = 1e-5

V7X_LANES = 128
V7X_SUBLANES = 8
V7X_VMEM_BYTES = 64 * 1024 * 1024
VMEM_LIMIT = (V7X_VMEM_BYTES * 7) // 8

MOE_TILE = 256
COMBINE_TILE = 512
GMLP_TILE = 512
MOE_FF_CHUNK = 256
CONF_TILE = 256
CONF_ACC_ROWS = 128
SC_WINDOW_BYTES = 128 * 1024
SAMPLE_STATE_TILE = 8


def _params(*semantics):
    return pltpu.CompilerParams(dimension_semantics=semantics, vmem_limit_bytes=VMEM_LIMIT)


def _resident(shape):
    zeros = (0,) * len(shape)
    return pl.BlockSpec(shape, lambda *_: zeros, pipeline_mode=pl.Buffered(1))


def _whole(shape):
    zeros = (0,) * len(shape)
    return pl.BlockSpec(shape, lambda *_: zeros)


def _row_tile(n_rows, preferred):
    for t in (preferred, 512, 256, 128):
        if t <= preferred and n_rows % t == 0:
            return t
    raise ValueError(f"row count {n_rows} is not a multiple of {V7X_LANES}")


def _dot(a, b):
    return jnp.dot(a, b, preferred_element_type=F32)


def _pack_bf16_pairs(x):
    w = x.shape[1] // 2
    lo = pltpu.bitcast(x[:, :w].astype(BF16).astype(F32), jnp.uint32) >> 16
    hi = pltpu.bitcast(x[:, w:].astype(BF16).astype(F32), jnp.uint32) & jnp.uint32(0xFFFF0000)
    return hi | lo


def _unpack_bf16_pairs(u):
    lo = pltpu.bitcast(u << 16, F32)
    hi = pltpu.bitcast(u & jnp.uint32(0xFFFF0000), F32)
    return jnp.concatenate([lo, hi], axis=1)


def _rms(x, g):
    return x * lax.rsqrt(jnp.mean(x * x, axis=-1, keepdims=True) + RMS_EPS) * g


def _layernorm(x, g, b):
    mu = jnp.mean(x, axis=-1, keepdims=True)
    xc = x - mu
    var = jnp.mean(xc * xc, axis=-1, keepdims=True)
    return xc * lax.rsqrt(var + LN_EPS) * g + b


def _sigmoid(x):
    return 0.5 * jnp.tanh(0.5 * x) + 0.5


def _silu(x):
    return x * _sigmoid(x)


def _softplus(x):
    return jnp.maximum(x, 0.0) + jnp.log1p(jnp.exp(-jnp.abs(x)))


def _even_in_kernel(x_ref, g_ref, w_ref, lng_ref, lnb_ref, z_ref, xbc_ref, ug_ref, vn_ref, dt_ref,
                    *, d_ssd, conv_dim, d_gmlp):
    h = _rms(x_ref[...], g_ref[...]).astype(BF16)
    o1 = d_ssd
    o2 = o1 + conv_dim
    o3 = o2 + d_gmlp
    o4 = o3 + d_gmlp
    z_ref[...] = _dot(h, w_ref[:, 0:o1])
    xbc_ref[...] = _dot(h, w_ref[:, o1:o2])
    ug_ref[...] = jax.nn.gelu(_dot(h, w_ref[:, o2:o3])).astype(ug_ref.dtype)
    v = jax.nn.gelu(_dot(h, w_ref[:, o3:o4]))
    vn_ref[...] = _layernorm(v, lng_ref[...], lnb_ref[...])
    dt_ref[...] = _dot(h, w_ref[:, o4:o4 + V7X_LANES])


def _even_in(x, g, w, ln_g, ln_b, *, d_ssd, conv_dim, d_gmlp):
    t, d = x.shape
    tm = _row_tile(t, 512)
    row = lambda n: pl.BlockSpec((tm, n), lambda i: (i, 0))
    return pl.pallas_call(
        functools.partial(_even_in_kernel, d_ssd=d_ssd, conv_dim=conv_dim, d_gmlp=d_gmlp),
        grid=(t // tm,),
        in_specs=[row(d), _resident(g.shape), _resident(w.shape), _resident(ln_g.shape), _resident(ln_b.shape)],
        out_specs=[row(d_ssd), row(conv_dim), row(d_gmlp), row(d_gmlp), row(V7X_LANES)],
        out_shape=[jax.ShapeDtypeStruct((t, d_ssd), F32), jax.ShapeDtypeStruct((t, conv_dim), F32),
                   jax.ShapeDtypeStruct((t, d_gmlp), BF16), jax.ShapeDtypeStruct((t, d_gmlp), F32),
                   jax.ShapeDtypeStruct((t, V7X_LANES), F32)],
        compiler_params=_params("parallel"),
        name="even_in",
    )(x, g, w, ln_g, ln_b)


def _ssd_prompt_kernel(xbc_ref, z_ref, dt_ref, cw_ref, cb_ref, dtb_ref, alog_ref, dskip_ref, ng_ref,
                       yn_ref, st_ref, xp_ref, state_ref, xc_ref, *, d_ssd, n_taps):
    c = pl.program_id(1)
    n_pairs = state_ref.shape[0]
    ch = SSD_CHUNK
    gn = SSD_GROUPS * SSD_STATE

    @pl.when(c == 0)
    def _():
        state_ref[...] = jnp.zeros_like(state_ref)
        xp_ref[0:V7X_SUBLANES, :] = jnp.zeros((V7X_SUBLANES, xp_ref.shape[1]), F32)

    xp_ref[V7X_SUBLANES:V7X_SUBLANES + ch, :] = xbc_ref[...]
    base = V7X_SUBLANES - (n_taps - 1)
    acc = cb_ref[...] + cw_ref[n_taps - 1:n_taps, :] * xbc_ref[...]
    for k in range(n_taps - 1):
        xc_ref[...] = xp_ref[base + k:base + k + ch, :]
        acc = acc + cw_ref[k:k + 1, :] * xc_ref[...]
    xp_ref[0:V7X_SUBLANES, :] = xp_ref[ch:ch + V7X_SUBLANES, :]
    xc_ref[...] = _silu(acc)
    xc = xc_ref

    dt = _softplus(dt_ref[...] + dtb_ref[...])
    a = dt * (-jnp.exp(alog_ref[...]))
    li = lax.broadcasted_iota(jnp.int32, (ch, ch), 0)
    si = lax.broadcasted_iota(jnp.int32, (ch, ch), 1)
    causal = li >= si
    tril = jnp.where(causal, 1.0, 0.0).astype(F32)
    acs = jnp.dot(tril, a, precision=HIGHEST, preferred_element_type=F32)
    acs_t = acs.T
    lane = lax.broadcasted_iota(jnp.int32, (ch, 2 * SSD_HEAD_DIM), 1)
    first = lane < SSD_HEAD_DIM
    first_n = lax.broadcasted_iota(jnp.int32, (SSD_STATE, 2 * SSD_HEAD_DIM), 1) < SSD_HEAD_DIM
    pairs_per_group = n_pairs // SSD_GROUPS

    for g in range(SSD_GROUPS):
        bg = xc[:, d_ssd + g * SSD_STATE:d_ssd + (g + 1) * SSD_STATE]
        cg = xc[:, d_ssd + gn + g * SSD_STATE:d_ssd + gn + (g + 1) * SSD_STATE]
        bg_t = bg.T
        cb = _dot(cg.astype(BF16), bg_t.astype(BF16))
        ys = []
        for q in range(pairs_per_group):
            pair = g * pairs_per_group + q
            h0 = 2 * pair
            lo = pair * 2 * SSD_HEAD_DIM
            xs = xc[:, lo:lo + 2 * SSD_HEAD_DIM]
            dt2 = jnp.where(first, dt[:, h0:h0 + 1], dt[:, h0 + 1:h0 + 2])
            xdt = (xs * dt2).astype(BF16)
            s_prev = state_ref[pair]
            s_prev_b = s_prev.astype(BF16)
            y2, snew2, dec2 = [], [], []
            for r in range(2):
                h = h0 + r
                col = acs[:, h:h + 1]
                row = acs_t[h:h + 1, :]
                last = acs_t[h:h + 1, ch - 1:ch]
                decay = jnp.exp(jnp.where(causal, col - row, -jnp.inf))
                y = _dot((cb * decay).astype(BF16), xdt)
                y = y + _dot((cg * jnp.exp(col)).astype(BF16), s_prev_b)
                snew = _dot((bg_t * jnp.exp(last - row)).astype(BF16), xdt)
                y2.append(y)
                snew2.append(snew)
                dec2.append(jnp.exp(last))
            y = jnp.where(first, y2[0], y2[1])
            state_ref[pair] = (s_prev * jnp.where(first_n, dec2[0], dec2[1])
                               + jnp.where(first_n, snew2[0], snew2[1]))
            ys.append(y + dskip_ref[:, lo:lo + 2 * SSD_HEAD_DIM] * xs)
        gw = pairs_per_group * 2 * SSD_HEAD_DIM
        yg = jnp.concatenate(ys, axis=-1) * _silu(z_ref[:, g * gw:(g + 1) * gw])
        yg = yg * lax.rsqrt(jnp.mean(yg * yg, axis=-1, keepdims=True) + RMS_EPS)
        yn_ref[:, g * gw:(g + 1) * gw] = (yg * ng_ref[:, g * gw:(g + 1) * gw]).astype(yn_ref.dtype)

    @pl.when(c == pl.num_programs(1) - 1)
    def _():
        st_ref[0] = state_ref[...]


def _ssd_prompt(xbc, z, dt, conv_w, conv_b, dt_bias, a_log, d_skip, norm_g, *, batch, seq):
    t, conv_dim = xbc.shape
    d_ssd = z.shape[1]
    n_pairs = d_ssd // (2 * SSD_HEAD_DIM)
    nc = seq // SSD_CHUNK
    n_taps = conv_w.shape[0]
    tile = lambda n: pl.BlockSpec((SSD_CHUNK, n), lambda b, c: (b * nc + c, 0))
    return pl.pallas_call(
        functools.partial(_ssd_prompt_kernel, d_ssd=d_ssd, n_taps=n_taps),
        grid=(batch, nc),
        in_specs=[tile(conv_dim), tile(d_ssd), tile(V7X_LANES), _resident(conv_w.shape), _resident(conv_b.shape),
                  _resident(dt_bias.shape), _resident(a_log.shape), _resident(d_skip.shape),
                  _resident(norm_g.shape)],
        out_specs=[tile(d_ssd),
                   pl.BlockSpec((1, n_pairs, SSD_STATE, 2 * SSD_HEAD_DIM), lambda b, c: (b, 0, 0, 0))],
        out_shape=[jax.ShapeDtypeStruct((t, d_ssd), BF16),
                   jax.ShapeDtypeStruct((batch, n_pairs, SSD_STATE, 2 * SSD_HEAD_DIM), F32)],
        scratch_shapes=[pltpu.VMEM((SSD_CHUNK + V7X_SUBLANES, conv_dim), F32),
                        pltpu.VMEM((n_pairs, SSD_STATE, 2 * SSD_HEAD_DIM), F32),
                        pltpu.VMEM((SSD_CHUNK, conv_dim), F32)],
        compiler_params=_params("parallel", "arbitrary"),
        name="ssd_prompt",
    )(xbc, z, dt, conv_w, conv_b, dt_bias, a_log, d_skip, norm_g)


def _gmlp_prompt_kernel(ug_ref, vn_ref, ws_ref, bs_ref, out_ref):
    ch = GMLP_CHUNK
    n_groups = ws_ref.shape[0]
    gd = vn_ref.shape[1] // n_groups
    ii = lax.broadcasted_iota(jnp.int32, (ch, ch), 0)
    jj = lax.broadcasted_iota(jnp.int32, (ch, ch), 1)
    for g in range(n_groups):
        ws = jnp.where(ii >= jj, ws_ref[g], 0.0).astype(BF16)
        cols = slice(g * gd, (g + 1) * gd)
        for r0 in range(0, vn_ref.shape[0], ch):
            rows = slice(r0, r0 + ch)
            mixed = _dot(ws, vn_ref[rows, cols].astype(BF16)) + bs_ref[:, g:g + 1]
            out_ref[rows, cols] = (ug_ref[rows, cols].astype(F32) * mixed).astype(out_ref.dtype)


def _gmlp_prompt(ug, vn, w_s, b_s_t, *, seq):
    t, d = vn.shape
    rows = _row_tile(seq, GMLP_TILE)
    assert rows % GMLP_CHUNK == 0
    tile = pl.BlockSpec((rows, d), lambda i: (i, 0))
    return pl.pallas_call(
        _gmlp_prompt_kernel,
        grid=(t // rows,),
        in_specs=[tile, tile, _resident(w_s.shape), _resident(b_s_t.shape)],
        out_specs=tile,
        out_shape=jax.ShapeDtypeStruct((t, d), BF16),
        compiler_params=_params("parallel"),
        name="gmlp_prompt",
    )(ug, vn, w_s, b_s_t)


def _ssd_sample_prep_kernel(xbc_ref, buf_ref, dt_ref, cw_ref, cb_ref, dtb_ref, alog_ref, expand_ref,
                            newbuf_ref, xs_ref, xdt_ref, bc_ref, da_ref, *, d_ssd, n_taps):
    conv_dim = xbc_ref.shape[1]
    x_new = xbc_ref[...]
    acc = cb_ref[...] + cw_ref[n_taps - 1:n_taps, :] * x_new
    for k in range(n_taps - 1):
        acc = acc + cw_ref[k:k + 1, :] * buf_ref[:, k * conv_dim:(k + 1) * conv_dim]
    for k in range(n_taps - 2):
        newbuf_ref[:, k * conv_dim:(k + 1) * conv_dim] = buf_ref[:, (k + 1) * conv_dim:(k + 2) * conv_dim]
    newbuf_ref[:, (n_taps - 2) * conv_dim:(n_taps - 1) * conv_dim] = x_new
    xc = _silu(acc)
    xs = xc[:, :d_ssd]
    dt = _softplus(dt_ref[...] + dtb_ref[...])
    da_ref[...] = jnp.exp(dt * (-jnp.exp(alog_ref[...])))
    dt_wide = jnp.dot(dt, expand_ref[...], precision=HIGHEST, preferred_element_type=F32)
    xs_ref[...] = xs
    xdt_ref[...] = xs * dt_wide
    bc_ref[...] = xc[:, d_ssd:]


def _ssd_sample_prep(xbc, buf, dt, conv_w, conv_b, dt_bias, a_log, expand):
    n, conv_dim = xbc.shape
    d_ssd = expand.shape[1]
    n_taps = conv_w.shape[0]
    args = (xbc, buf, dt, conv_w, conv_b, dt_bias, a_log, expand)
    return pl.pallas_call(
        functools.partial(_ssd_sample_prep_kernel, d_ssd=d_ssd, n_taps=n_taps),
        grid=(1,),
        in_specs=[_resident(a.shape) for a in args],
        out_specs=[_whole(buf.shape), _whole((n, d_ssd)), _whole((n, d_ssd)),
                   _whole((n, conv_dim - d_ssd)), _whole((n, V7X_LANES))],
        out_shape=[jax.ShapeDtypeStruct(buf.shape, F32), jax.ShapeDtypeStruct((n, d_ssd), F32),
                   jax.ShapeDtypeStruct((n, d_ssd), F32), jax.ShapeDtypeStruct((n, conv_dim - d_ssd), F32),
                   jax.ShapeDtypeStruct((n, V7X_LANES), F32)],
        compiler_params=_params("arbitrary"),
        name="ssd_sample_prep",
    )(*args)


def _ssd_sample_state_kernel(h_ref, xdt_ref, da_ref, bc_ref, hnew_ref, y_ref, xcol_ref, ccol_ref):
    bs, n_heads, p, s = h_ref.shape
    n_pairs = n_heads // 2
    pairs_per_group = n_pairs // SSD_GROUPS
    gn = SSD_GROUPS * s
    w = 2 * p
    eye = lax.broadcasted_iota(jnp.int32, (w, w), 0) == lax.broadcasted_iota(jnp.int32, (w, w), 1)
    ones = jnp.ones((w, V7X_LANES), BF16)

    def stacked_diag(v):
        return jnp.concatenate(
            [jnp.where(eye, jnp.broadcast_to(v[j:j + 1], (w, w)), 0.0) for j in range(bs)], axis=0).astype(BF16)

    for q in range(n_pairs):
        x = xdt_ref[:, q * w:(q + 1) * w]
        hi = x.astype(BF16).astype(F32)
        mid = (x - hi).astype(BF16).astype(F32)
        lo = (x - hi) - mid
        xcol_ref[q] = _dot(stacked_diag(hi), ones) + (_dot(stacked_diag(mid), ones) + _dot(stacked_diag(lo), ones))
    for g in range(SSD_GROUPS):
        ccol_ref[g] = _dot(stacked_diag(bc_ref[:, gn + g * s:gn + (g + 1) * s]), ones).astype(BF16)

    for j in range(bs):
        for g in range(SSD_GROUPS):
            brow = bc_ref[j:j + 1, g * s:(g + 1) * s]
            h_new = []
            for q in range(g * pairs_per_group, (g + 1) * pairs_per_group):
                for r in range(2):
                    h = 2 * q + r
                    xcol = xcol_ref[q, j * w + r * p:j * w + (r + 1) * p, :]
                    h_new.append(h_ref[j, h] * da_ref[j:j + 1, h:h + 1] + xcol * brow)
                    hnew_ref[j, h] = h_new[-1]
            hc = _dot(jnp.concatenate(h_new, axis=0).astype(BF16), ccol_ref[g, j * s:(j + 1) * s, :])
            for i in range(pairs_per_group):
                y = jnp.sum(jnp.where(eye, hc[i * w:(i + 1) * w], 0.0), axis=0, keepdims=True)
                q = g * pairs_per_group + i
                y_ref[j, q:q + 1, :] = y


def _ssd_sample_state(h, xdt, da, bc):
    n, n_heads, p, s = h.shape
    assert 2 * p == V7X_LANES and s == V7X_LANES
    bs = SAMPLE_STATE_TILE
    rows = lambda w: pl.BlockSpec((bs, w), lambda i: (i, 0))
    return pl.pallas_call(
        _ssd_sample_state_kernel,
        grid=(n // bs,),
        in_specs=[pl.BlockSpec((bs, n_heads, p, s), lambda i: (i, 0, 0, 0)), rows(xdt.shape[1]), rows(da.shape[1]),
                  rows(bc.shape[1])],
        out_specs=[pl.BlockSpec((bs, n_heads, p, s), lambda i: (i, 0, 0, 0)),
                   pl.BlockSpec((bs, n_heads // 2, 2 * p), lambda i: (i, 0, 0))],
        out_shape=[jax.ShapeDtypeStruct(h.shape, F32), jax.ShapeDtypeStruct((n, n_heads // 2, 2 * p), F32)],
        scratch_shapes=[pltpu.VMEM((n_heads // 2, bs * 2 * p, V7X_LANES), F32),
                        pltpu.VMEM((SSD_GROUPS, bs * s, V7X_LANES), BF16)],
        compiler_params=_params("parallel"),
        name="ssd_sample_state",
    )(h, xdt, da, bc)


def _even_sample_post_kernel(y_ref, xs_ref, z_ref, dskip_ref, ng_ref, ug_ref, vn_ref, w0_ref, b0_ref,
                             yn_ref, gout_ref):
    d_ssd = y_ref.shape[1]
    gw = d_ssd // SSD_GROUPS
    y = (y_ref[...] + dskip_ref[...] * xs_ref[...]) * _silu(z_ref[...])
    for g in range(SSD_GROUPS):
        yg = y[:, g * gw:(g + 1) * gw]
        yg = yg * lax.rsqrt(jnp.mean(yg * yg, axis=-1, keepdims=True) + RMS_EPS)
        yn_ref[:, g * gw:(g + 1) * gw] = (yg * ng_ref[:, g * gw:(g + 1) * gw]).astype(yn_ref.dtype)
    mixed = w0_ref[...] * vn_ref[...] + b0_ref[...]
    gout_ref[...] = (ug_ref[...].astype(F32) * mixed).astype(gout_ref.dtype)


def _even_sample_post(y, xs, z, d_skip, norm_g, ug, vn, w0, b0):
    args = (y, xs, z, d_skip, norm_g, ug, vn, w0, b0)
    return pl.pallas_call(
        _even_sample_post_kernel,
        grid=(1,),
        in_specs=[_resident(a.shape) for a in args],
        out_specs=[_whole(y.shape), _whole(vn.shape)],
        out_shape=[jax.ShapeDtypeStruct(y.shape, BF16), jax.ShapeDtypeStruct(vn.shape, BF16)],
        compiler_params=_params("arbitrary"),
        name="even_sample_post",
    )(*args)


def _even_out_kernel(x_ref, yn_ref, gout_ref, wo_ref, g_ref, wg_ref, wu_ref, wd_ref, out_ref):
    d_ssd = yn_ref.shape[1]
    x = x_ref[...] + _dot(yn_ref[...], wo_ref[0:d_ssd, :]) + _dot(gout_ref[...], wo_ref[d_ssd:, :])
    h = _rms(x, g_ref[...]).astype(BF16)
    mid = (_silu(_dot(h, wg_ref[...])) * _dot(h, wu_ref[...])).astype(BF16)
    out_ref[...] = x + _dot(mid, wd_ref[...])


def _even_out(x, yn, gout, w_out, g, wg, wu, wd):
    t, d = x.shape
    tm = _row_tile(t, 512)
    row = lambda n: pl.BlockSpec((tm, n), lambda i: (i, 0))
    return pl.pallas_call(
        _even_out_kernel,
        grid=(t // tm,),
        in_specs=[row(d), row(yn.shape[1]), row(gout.shape[1]), _resident(w_out.shape), _resident(g.shape),
                  _resident(wg.shape), _resident(wu.shape), _resident(wd.shape)],
        out_specs=row(d),
        out_shape=jax.ShapeDtypeStruct((t, d), F32),
        compiler_params=_params("parallel"),
        name="even_out_ffn",
    )(x, yn, gout, w_out, g, wg, wu, wd)


def _conf_in_kernel(x_ref, g_ref, w_ref, b_ref, a_ref):
    d = a_ref.shape[1]
    h = _rms(x_ref[...], g_ref[...]).astype(BF16)
    val = _dot(h, w_ref[:, 0:d]) + b_ref[:, 0:d]
    gate = _dot(h, w_ref[:, d:]) + b_ref[:, d:]
    a_ref[...] = val * _sigmoid(gate)


def _conf_in(x, g, w1, b1):
    t, d = x.shape
    dc = w1.shape[1] // 2
    tm = _row_tile(t, 512)
    return pl.pallas_call(
        _conf_in_kernel,
        grid=(t // tm,),
        in_specs=[pl.BlockSpec((tm, d), lambda i: (i, 0)), _resident(g.shape), _resident(w1.shape),
                  _resident(b1.shape)],
        out_specs=pl.BlockSpec((tm, dc), lambda i: (i, 0)),
        out_shape=jax.ShapeDtypeStruct((t, dc), F32),
        compiler_params=_params("parallel"),
        name="conf_in",
    )(x, g, w1, b1)


def _route(logits, n_experts, route_ref, route_t_ref):
    lane = lax.broadcasted_iota(jnp.int32, logits.shape, 1)
    lane_f = lane.astype(F32)
    big = jnp.float32(V7X_LANES)
    lg = jnp.where(lane < n_experts, logits, -jnp.inf)
    m1 = jnp.max(lg, axis=-1, keepdims=True)
    i1 = jnp.min(jnp.where(lg == m1, lane_f, big), axis=-1, keepdims=True)
    lg2 = jnp.where(lane_f == i1, -jnp.inf, lg)
    m2 = jnp.max(lg2, axis=-1, keepdims=True)
    i2 = jnp.min(jnp.where(lg2 == m2, lane_f, big), axis=-1, keepdims=True)
    e2 = jnp.exp(m2 - m1)
    g1 = 1.0 / (1.0 + e2)
    g2 = e2 / (1.0 + e2)
    route = jnp.where(lane == 0, i1, jnp.where(lane == 1, i2, jnp.where(lane == 2, g1,
                      jnp.where(lane == 3, g2, 0.0))))
    route_ref[...] = route
    route_t_ref[...] = route.T[0:V7X_SUBLANES, :]


def _conf_finish(c, x, lng_ref, lnb_ref, w2_ref, b2_ref, ng_ref, rhi_ref, rlo_ref, out_ref, hn_ref, route_ref,
                 route_t_ref, *, n_experts):
    hmid = _silu(_layernorm(c, lng_ref[...], lnb_ref[...])).astype(BF16)
    x_new = x + _dot(hmid, w2_ref[...]) + b2_ref[...]
    out_ref[...] = x_new
    hn = _rms(x_new, ng_ref[...])
    hn_ref[...] = _pack_bf16_pairs(hn)
    hi = hn.astype(BF16)
    lo = (hn - hi.astype(F32)).astype(BF16)
    logits = _dot(hi, rhi_ref[...]) + (_dot(lo, rhi_ref[...]) + _dot(hi, rlo_ref[...]))
    _route(logits, n_experts, route_ref, route_t_ref)


def _conf_prompt_kernel(a_ref, x_ref, xs_ref, hns_ref, routes_ref, dw_ref, db_ref, lng_ref, lnb_ref, w2_ref, b2_ref,
                        ng_ref, rhi_ref, rlo_ref, out_ref, hn_ref, route_ref, route_t_ref, xp_ref, c_ref, win_ref,
                        *, n_taps, n_experts, tiles_per_seq, n_prompt_tiles):
    tl, d = a_ref.shape
    halo = xp_ref.shape[0] - tl
    base = halo - (n_taps - 1)
    rows = CONF_ACC_ROWS
    step = pl.program_id(0)

    @pl.when(step < n_prompt_tiles)
    def _():
        @pl.when(step % tiles_per_seq == 0)
        def _():
            xp_ref[0:halo, :] = jnp.zeros((halo, d), F32)

        xp_ref[halo:halo + tl, :] = a_ref[...]

        def lane_block(j, carry):
            cols = pl.ds(pl.multiple_of(j * V7X_LANES, V7X_LANES), V7X_LANES)
            for r0 in range(0, tl, rows):
                acc = jnp.broadcast_to(db_ref[:, cols], (rows, V7X_LANES))
                for phase in range(V7X_SUBLANES):
                    taps = [k for k in range(n_taps) if (base + k) % V7X_SUBLANES == phase]
                    if not taps:
                        continue
                    span = max(base + k - phase for k in taps) + rows
                    win_ref[0:span, :] = xp_ref[pl.ds(r0 + phase, span), cols]
                    for k in taps:
                        o = base + k - phase
                        acc = acc + dw_ref[k:k + 1, cols] * win_ref[o:o + rows, :]
                c_ref[r0:r0 + rows, cols] = acc
            return carry

        lax.fori_loop(0, d // V7X_LANES, lane_block, 0)
        xp_ref[0:halo, :] = xp_ref[tl:tl + halo, :]
        _conf_finish(c_ref[...], x_ref[...], lng_ref, lnb_ref, w2_ref, b2_ref, ng_ref, rhi_ref, rlo_ref,
                     out_ref, hn_ref, route_ref, route_t_ref, n_experts=n_experts)

    @pl.when(step == n_prompt_tiles)
    def _():
        ns = xs_ref.shape[0]
        out_ref[0:ns, :] = xs_ref[...]
        hn_ref[0:ns, :] = hns_ref[...]
        route_ref[0:ns, :] = routes_ref[...]


def _conf_prompt(a, x, sample_rows, consts, *, seq, n_experts):
    t, d = a.shape
    n_s = sample_rows[0].shape[0]
    tl = _row_tile(seq, CONF_TILE)
    assert tl % CONF_ACC_ROWS == 0 and n_s <= tl
    n_prompt_tiles = t // tl
    n_taps = consts[0].shape[0]
    halo = -(-(n_taps - 1) // V7X_SUBLANES) * V7X_SUBLANES
    in_tile = lambda n: pl.BlockSpec((tl, n), lambda s: (jnp.minimum(s, n_prompt_tiles - 1), 0))
    out_tile = lambda n: pl.BlockSpec((tl, n), lambda s: (s, 0))
    small = sample_rows + tuple(consts)
    return pl.pallas_call(
        functools.partial(_conf_prompt_kernel, n_taps=n_taps, n_experts=n_experts, tiles_per_seq=seq // tl,
                          n_prompt_tiles=n_prompt_tiles),
        grid=(n_prompt_tiles + 1,),
        in_specs=[in_tile(d), in_tile(d)] + [_resident(c.shape) for c in small],
        out_specs=[out_tile(d), out_tile(d // 2), out_tile(V7X_LANES),
                   pl.BlockSpec((V7X_SUBLANES, tl), lambda s: (0, jnp.minimum(s, n_prompt_tiles - 1)))],
        out_shape=[jax.ShapeDtypeStruct((t + n_s, d), F32), jax.ShapeDtypeStruct((t + n_s, d // 2), jnp.uint32),
                   jax.ShapeDtypeStruct((t + n_s, V7X_LANES), F32), jax.ShapeDtypeStruct((V7X_SUBLANES, t), F32)],
        scratch_shapes=[pltpu.VMEM((tl + halo, d), F32), pltpu.VMEM((tl, d), F32),
                        pltpu.VMEM((CONF_ACC_ROWS + halo, V7X_LANES), F32)],
        compiler_params=_params("arbitrary"),
        name="conf_prompt",
    )(a, x, *small)


def _conf_sample_kernel(a_ref, buf_ref, x_ref, dw_ref, db_ref, lng_ref,
                        lnb_ref, w2_ref, b2_ref, ng_ref, rhi_ref, rlo_ref, newbuf_ref, out_ref, hn_ref, route_ref,
                        route_t_ref, *, n_taps, n_experts):
    d = a_ref.shape[1]
    a_new = a_ref[...]
    acc = db_ref[...] + dw_ref[n_taps - 1:n_taps, :] * a_new
    for k in range(n_taps - 1):
        acc = acc + dw_ref[k:k + 1, :] * buf_ref[:, k * d:(k + 1) * d]
    for k in range(n_taps - 2):
        newbuf_ref[:, k * d:(k + 1) * d] = buf_ref[:, (k + 1) * d:(k + 2) * d]
    newbuf_ref[:, (n_taps - 2) * d:(n_taps - 1) * d] = a_new
    _conf_finish(acc, x_ref[...], lng_ref, lnb_ref, w2_ref, b2_ref, ng_ref, rhi_ref, rlo_ref,
                 out_ref, hn_ref, route_ref, route_t_ref, n_experts=n_experts)


def _conf_sample(a, buf, x, consts, *, n_experts):
    n, d = a.shape
    args = (a, buf, x) + tuple(consts)
    return pl.pallas_call(
        functools.partial(_conf_sample_kernel, n_taps=consts[0].shape[0], n_experts=n_experts),
        grid=(1,),
        in_specs=[_resident(v.shape) for v in args],
        out_specs=[_whole(buf.shape), _whole((n, d)), _whole((n, d // 2)), _whole((n, V7X_LANES)),
                   _whole((V7X_SUBLANES, n))],
        out_shape=[jax.ShapeDtypeStruct(buf.shape, F32), jax.ShapeDtypeStruct((n, d), F32),
                   jax.ShapeDtypeStruct((n, d // 2), jnp.uint32), jax.ShapeDtypeStruct((n, V7X_LANES), F32),
                   jax.ShapeDtypeStruct((V7X_SUBLANES, n), F32)],
        compiler_params=_params("arbitrary"),
        name="conf_sample",
    )(*args)


def _sc_window_rows(x):
    assert x.dtype.itemsize == 4, "the SparseCore indirect copies move 32-bit elements"
    return SC_WINDOW_BYTES // (x.shape[1] * x.dtype.itemsize)


def _sc_gather_rows(x, idx):
    n, d = idx.shape[0], x.shape[1]
    w = _sc_window_rows(x)
    assert n % w == 0
    mesh = plsc.VectorSubcoreMesh(core_axis_name="core", subcore_axis_name="subcore")

    @functools.partial(pl.kernel, out_type=jax.ShapeDtypeStruct((n, d), x.dtype), mesh=mesh, scratch_types=[])
    def gather(x_hbm, idx_hbm, out_hbm):
        def body(idx_vmem, out_vmem):
            pltpu.sync_copy(x_hbm.at[idx_vmem.at[0]], out_vmem)

        pltpu.emit_pipeline(
            body,
            grid=(n // w,),
            in_specs=[pl.BlockSpec((1, w), lambda i: (i, 0))],
            out_specs=[pl.BlockSpec((w, d), lambda i: (i, 0))],
            core_axis_name=("core", "subcore"),
            dimension_semantics=(pltpu.PARALLEL,),
        )(idx_hbm, out_hbm)

    return gather(x, idx.reshape(n // w, w))


def _sc_dispatch_rows(x, pos, pad_pos, n_out):
    t, d = x.shape
    w = _sc_window_rows(x)
    n_pad = pad_pos.shape[0]
    assert t % w == 0 and n_pad % w == 0 and TOP_K * t + n_pad == n_out
    mesh = plsc.VectorSubcoreMesh(core_axis_name="core", subcore_axis_name="subcore")

    @functools.partial(pl.kernel, out_type=jax.ShapeDtypeStruct((n_out, d), x.dtype), mesh=mesh, scratch_types=[])
    def dispatch(x_hbm, pos_hbm, pad_hbm, zero_hbm, out_hbm):
        def body(rows_vmem, idx_vmem):
            pltpu.sync_copy(rows_vmem, out_hbm.at[idx_vmem.at[0]])

        def scatter(src_hbm, idx_hbm, src_map):
            pltpu.emit_pipeline(
                body,
                grid=(idx_hbm.shape[0],),
                in_specs=[pl.BlockSpec((w, d), src_map), pl.BlockSpec((1, w), lambda i: (i, 0))],
                out_specs=[],
                core_axis_name=("core", "subcore"),
                dimension_semantics=(pltpu.PARALLEL,),
            )(src_hbm, idx_hbm)

        for k in range(TOP_K):
            scatter(x_hbm, pos_hbm.at[k], lambda i: (i, 0))
        scatter(zero_hbm, pad_hbm, lambda i: (0, 0))

    return dispatch(x, pos.reshape(TOP_K, t // w, w), pad_pos.reshape(n_pad // w, w), jnp.zeros((w, d), x.dtype))


def _moe_ffn_kernel(tile_expert_ref, first_ref, n_tiles_ref, x_ref, wg_hbm, wu_hbm, wd_hbm, out_ref,
                    wg_v, wu_v, wd_v, stage_g, stage_u, stage_d, sems):
    i = pl.program_id(0)
    e = tile_expert_ref[i]
    dff = wg_v.shape[1]
    ck = stage_g.shape[2]
    n_chunks = dff // ck
    active = i < n_tiles_ref[0]

    def chunk_copies(c, slot):
        cols = slice(c * ck, (c + 1) * ck)
        return (pltpu.make_async_copy(wg_hbm.at[e, :, cols], stage_g.at[slot], sems.at[0, slot]),
                pltpu.make_async_copy(wu_hbm.at[e, :, cols], stage_u.at[slot], sems.at[1, slot]),
                pltpu.make_async_copy(wd_hbm.at[e, cols, :], stage_d.at[slot], sems.at[2, slot]))

    @pl.when(active & (first_ref[i] == 1))
    def _():
        for cp in chunk_copies(0, 0):
            cp.start()
        for c in range(n_chunks):
            slot = c % 2
            if c + 1 < n_chunks:
                for cp in chunk_copies(c + 1, 1 - slot):
                    cp.start()
            for cp in chunk_copies(c, slot):
                cp.wait()
            cols = slice(c * ck, (c + 1) * ck)
            wg_v[:, cols] = stage_g[slot].astype(BF16)
            wu_v[:, cols] = stage_u[slot].astype(BF16)
            wd_v[cols, :] = stage_d[slot].astype(BF16)

    @pl.when(active)
    def _():
        xb = _unpack_bf16_pairs(x_ref[...]).astype(BF16)
        half = dff // 2
        acc = None
        for j in range(2):
            cols = slice(j * half, (j + 1) * half)
            mid = (_silu(_dot(xb, wg_v[:, cols])) * _dot(xb, wu_v[:, cols])).astype(BF16)
            part = _dot(mid, wd_v[cols, :])
            acc = part if acc is None else acc + part
        out_ref[...] = _pack_bf16_pairs(acc)

    @pl.when(jnp.logical_not(active))
    def _():
        out_ref[...] = jnp.zeros_like(out_ref)


def _moe_ffn(tile_expert, n_tiles, xs, wg, wu, wd, *, tm):
    p, d = xs.shape[0], wg.shape[1]
    dff = wg.shape[2]
    ck = MOE_FF_CHUNK
    assert dff % ck == 0
    nt = p // tm
    first = jnp.concatenate([jnp.ones((1,), jnp.int32), (tile_expert[1:] != tile_expert[:-1]).astype(jnp.int32)])
    any_spec = pl.BlockSpec(memory_space=pl.ANY)
    grid_spec = pltpu.PrefetchScalarGridSpec(
        num_scalar_prefetch=3,
        grid=(nt,),
        in_specs=[pl.BlockSpec((tm, d // 2), lambda i, te, fi, n: (i, 0)), any_spec, any_spec, any_spec],
        out_specs=pl.BlockSpec((tm, d // 2), lambda i, te, fi, n: (i, 0)),
        scratch_shapes=[pltpu.VMEM((d, dff), BF16), pltpu.VMEM((d, dff), BF16), pltpu.VMEM((dff, d), BF16),
                        pltpu.VMEM((2, d, ck), F32), pltpu.VMEM((2, d, ck), F32), pltpu.VMEM((2, ck, d), F32),
                        pltpu.SemaphoreType.DMA((3, 2))],
    )
    return pl.pallas_call(
        _moe_ffn_kernel,
        grid_spec=grid_spec,
        out_shape=jax.ShapeDtypeStruct((p, d // 2), jnp.uint32),
        compiler_params=_params("arbitrary"),
        name="moe_ffn",
    )(tile_expert, first, n_tiles, xs, wg, wu, wd)


def _moe_combine_kernel(x_ref, route_ref, gf_ref, a_ref, b_ref, y_ref):
    route = route_ref[...]
    x = (x_ref[...] + route[:, 2:3] * _unpack_bf16_pairs(a_ref[...])
         + route[:, 3:4] * _unpack_bf16_pairs(b_ref[...]))
    y_ref[...] = _rms(x, gf_ref[...])


def _moe_combine(x, route, gf, picked, *, row0, n_rows, a_row0, b_row0):
    d = x.shape[1]
    tc = _row_tile(n_rows, COMBINE_TILE)
    assert row0 % tc == 0 and a_row0 % tc == 0 and b_row0 % tc == 0
    rows = lambda w, r0: pl.BlockSpec((tc, w), lambda i: (i + r0 // tc, 0))
    return pl.pallas_call(
        _moe_combine_kernel,
        grid=(n_rows // tc,),
        in_specs=[rows(d, row0), rows(V7X_LANES, row0), _resident(gf.shape), rows(d // 2, a_row0),
                  rows(d // 2, b_row0)],
        out_specs=pl.BlockSpec((tc, d), lambda i: (i, 0)),
        out_shape=jax.ShapeDtypeStruct((n_rows, d), F32),
        compiler_params=_params("parallel"),
        name="moe_combine",
    )(x, route, gf, picked, picked)


def _moe_plan(route_t, n_experts, tm):
    t = route_t.shape[1]
    n_slots = t * TOP_K
    i32 = jnp.int32
    expert = route_t[:TOP_K].astype(i32).reshape(1, n_slots)
    onehot = (expert == jnp.arange(n_experts, dtype=i32)[:, None]).astype(i32)
    counts = jnp.sum(onehot, axis=1)
    tiles = (counts + tm - 1) // tm
    tile_end = jnp.cumsum(tiles)
    start = (tile_end - tiles) * tm
    rank = jnp.cumsum(onehot, axis=1) - onehot
    pos = jnp.sum(onehot * (rank + start[:, None]), axis=0)
    nt = -(-n_slots // tm) + n_experts
    n_tiles = tile_end[-1]
    tile_id = jnp.minimum(jnp.arange(nt, dtype=i32), n_tiles - 1)
    tile_expert = jnp.sum((tile_id[:, None] >= tile_end[None, :]).astype(i32), axis=1)
    seg_start = jnp.concatenate([start + counts, (n_tiles * tm)[None]])
    seg_len = jnp.concatenate([tiles * tm - counts, (nt * tm - n_tiles * tm)[None]])
    seg_end = jnp.cumsum(seg_len)
    q = jnp.arange(nt * tm - n_slots, dtype=i32)
    in_seg = (q[:, None] >= (seg_end - seg_len)[None, :]) & (q[:, None] < seg_end[None, :])
    pad_pos = jnp.sum(in_seg.astype(i32) * (seg_start - (seg_end - seg_len))[None, :], axis=1) + q
    return (tile_expert.astype(i32), n_tiles.astype(i32).reshape(1), pos.reshape(TOP_K, t).astype(i32),
            pad_pos.astype(i32), nt * tm)


def _row(v):
    return v.reshape(1, -1).astype(F32)


def _pad_lanes(v):
    return jnp.pad(_row(v), ((0, 0), (0, V7X_LANES - v.shape[-1])))


def kernel(x_prompt, x_sample, state_ssm, state_ssd_conv, state_conf_conv, norm_mix_even, w_in_even, ssd_conv_w, ssd_conv_b, ssd_dt_bias, ssd_a_log, ssd_d, ssd_norm, gmlp_ln_g, gmlp_ln_b, gmlp_w_s, gmlp_b_s, w_out_even, norm_ffn_even, ffn_w_gate, ffn_w_up, ffn_w_down, norm_mix_odd, conf_w1, conf_b1, conf_dw_w, conf_dw_b, conf_ln_g, conf_ln_b, conf_w2, conf_b2, norm_ffn_odd, moe_router, moe_w_gate, moe_w_up, moe_w_down, final_norm):
    batch, seq, d_model = x_prompt.shape
    n_dec, dec_seq, _ = x_sample.shape
    assert dec_seq == 1, "the sample group advances one token per sequence"
    assert seq % SSD_CHUNK == 0 and n_dec % V7X_LANES == 0
    n_even, n_odd = w_in_even.shape[0], conf_w1.shape[0]
    assert (n_even, n_odd) == (1, 1), "the final norm is fused into the last (odd) layer's MoE combine"
    n_heads = ssd_dt_bias.shape[1]
    d_ssd = n_heads * SSD_HEAD_DIM
    conv_dim = ssd_conv_w.shape[2]
    d_gmlp = gmlp_ln_g.shape[1]
    n_groups_gmlp = gmlp_w_s.shape[1]
    n_experts = moe_router.shape[2]
    tp = batch * seq

    xp = x_prompt.reshape(tp, d_model)
    xs = x_sample.reshape(n_dec, d_model)
    outs = dict(ssm_p=[], conv_p=[], conf_p=[], ssm_s=[], conv_s=[], conf_s=[], v_s=[])

    for layer in range(n_even + n_odd):
        i = layer // 2
        if layer % 2 == 0:
            w = w_in_even[i]
            o1, o2, o3 = d_ssd, d_ssd + conv_dim, d_ssd + conv_dim + n_heads
            w_r = jnp.concatenate([w[:, :o2], w[:, o3:], jnp.pad(w[:, o2:o3], ((0, 0), (0, V7X_LANES - n_heads)))],
                                  axis=1).astype(BF16)
            in_args = (_row(norm_mix_even[i]), w_r, _row(gmlp_ln_g[i]), _row(gmlp_ln_b[i]))
            dims = dict(d_ssd=d_ssd, conv_dim=conv_dim, d_gmlp=d_gmlp)
            dt_bias, a_log = _pad_lanes(ssd_dt_bias[i]), _pad_lanes(ssd_a_log[i])
            d_skip = _row(jnp.repeat(ssd_d[i], SSD_HEAD_DIM))
            norm_g = _row(ssd_norm[i])
            conv_w, conv_b = ssd_conv_w[i], _row(ssd_conv_b[i])
            z, xbc, ug, vn, dt = _even_in(xp, *in_args, **dims)
            yn, st = _ssd_prompt(xbc, z, dt, conv_w, conv_b, dt_bias, a_log, d_skip, norm_g, batch=batch, seq=seq)
            gout = _gmlp_prompt(ug, vn, gmlp_w_s[i], gmlp_b_s[i].T, seq=seq)
            n_pairs = n_heads // 2
            st = st.reshape(batch, n_pairs, SSD_STATE, 2, SSD_HEAD_DIM).transpose(0, 1, 3, 4, 2)
            outs['ssm_p'].append(st.reshape(batch, n_heads, SSD_HEAD_DIM, SSD_STATE))
            outs['conv_p'].append(xbc.reshape(batch, seq, conv_dim)[:, seq - (conv_w.shape[0] - 1):])
            ffn = (w_out_even[i].astype(BF16), _row(norm_ffn_even[i]), ffn_w_gate[i].astype(BF16),
                   ffn_w_up[i].astype(BF16), ffn_w_down[i].astype(BF16))
            xp = _even_out(xp, yn, gout, *ffn)
            z, xbc, ug, vn, dt = _even_in(xs, *in_args, **dims)
            expand = (jnp.arange(d_ssd)[None, :] // SSD_HEAD_DIM == jnp.arange(V7X_LANES)[:, None]).astype(F32)
            buf = state_ssd_conv[i].reshape(n_dec, -1)
            newbuf, xc, xdt, bc, da = _ssd_sample_prep(xbc, buf, dt, conv_w, conv_b, dt_bias, a_log, expand)
            h_new, y4 = _ssd_sample_state(state_ssm[i], xdt, da, bc)
            gd = d_gmlp // n_groups_gmlp
            w0 = _row(jnp.repeat(gmlp_w_s[i][:, 0, 0], gd))
            b0 = _row(jnp.repeat(gmlp_b_s[i][:, 0], gd))
            yn, gout = _even_sample_post(y4.reshape(n_dec, d_ssd), xc, z, d_skip, norm_g, ug, vn, w0, b0)
            outs['ssm_s'].append(h_new)
            outs['conv_s'].append(newbuf.reshape(state_ssd_conv[i].shape))
            outs['v_s'].append(vn.reshape(n_dec, 1, d_gmlp))
            xs = _even_out(xs, yn, gout, *ffn)
        else:
            w1, b1 = conf_w1[i].astype(BF16), _row(conf_b1[i])
            router = jnp.pad(moe_router[i], ((0, 0), (0, V7X_LANES - n_experts)))
            router_hi = router.astype(BF16)
            router_lo = (router - router_hi.astype(F32)).astype(BF16)
            tail = (conf_dw_w[i], _row(conf_dw_b[i]), _row(conf_ln_g[i]), _row(conf_ln_b[i]),
                    conf_w2[i].astype(BF16), _row(conf_b2[i]), _row(norm_ffn_odd[i]), router_hi, router_lo)
            n_keep = conf_dw_w.shape[1] - 1
            a = _conf_in(xp, _row(norm_mix_odd[i]), w1, b1)
            outs['conf_p'].append(a.reshape(batch, seq, -1)[:, seq - n_keep:])
            a_s = _conf_in(xs, _row(norm_mix_odd[i]), w1, b1)
            buf = state_conf_conv[i].reshape(n_dec, -1)
            newbuf, *sample_rows, route_t_s = _conf_sample(a_s, buf, xs, tail, n_experts=n_experts)
            outs['conf_s'].append(newbuf.reshape(state_conf_conv[i].shape))
            x_all, hn, route, route_t_p = _conf_prompt(a, xp, tuple(sample_rows), tail, seq=seq,
                                                       n_experts=n_experts)
            route_t = jnp.concatenate([route_t_p, route_t_s], axis=1)
            tile_expert, n_tiles, pos, pad_pos, n_rows = _moe_plan(route_t, n_experts, MOE_TILE)
            xs_sorted = _sc_dispatch_rows(hn, pos, pad_pos, n_rows)
            ys = _moe_ffn(tile_expert, n_tiles, xs_sorted, moe_w_gate[i], moe_w_up[i], moe_w_down[i], tm=MOE_TILE)
            picked = _sc_gather_rows(ys, jnp.concatenate([pos[:, :tp].reshape(-1), pos[:, tp:].reshape(-1)]))
            gf = _row(final_norm)
            xp = _moe_combine(x_all, route, gf, picked, row0=0, n_rows=tp, a_row0=0, b_row0=tp)
            xs = _moe_combine(x_all, route, gf, picked, row0=tp, n_rows=n_dec, a_row0=TOP_K * tp,
                              b_row0=TOP_K * tp + n_dec)

    y_prompt = xp.reshape(batch, seq, d_model)
    y_sample = xs.reshape(n_dec, 1, d_model)
    return (y_prompt, y_sample, jnp.stack(outs['ssm_p']), jnp.stack(outs['conv_p']), jnp.stack(outs['conf_p']),
            jnp.stack(outs['ssm_s']), jnp.stack(outs['conv_s']), jnp.stack(outs['conf_s']), jnp.stack(outs['v_s']))
```

```python
import functools

import jax
import jax.numpy as jnp
from jax import lax
from jax.experimental import pallas as pl
from jax.experimental.pallas import tpu as pltpu
from jax.experimental.pallas import tpu_sc as plsc

F32 = jnp.float32
BF16 = jnp.bfloat16
HIGHEST = lax.Precision.HIGHEST

SSD_HEAD_DIM = 64
SSD_GROUPS = 4
SSD_STATE = 128
SSD_CHUNK = 128
GMLP_CHUNK = 128
TOP_K = 2
RMS_EPS = 1e-6
LN_EPS = 1e-5

V7X_LANES = 128
V7X_SUBLANES = 8
V7X_VMEM_BYTES = 64 * 1024 * 1024
VMEM_LIMIT = (V7X_VMEM_BYTES * 7) // 8

MOE_TILE = 256
COMBINE_TILE = 512
SSD_SEQS_PER_STEP = 2
GMLP_TILE = 512
MOE_FF_CHUNK = 256
CONF_TILE = 256
CONF_ACC_ROWS = 64
SC_WINDOW_BYTES = 128 * 1024
SAMPLE_STATE_TILE = 8


def _params(*semantics):
    return pltpu.CompilerParams(dimension_semantics=semantics, vmem_limit_bytes=VMEM_LIMIT)


def _resident(shape):
    zeros = (0,) * len(shape)
    return pl.BlockSpec(shape, lambda *_: zeros, pipeline_mode=pl.Buffered(1))


def _whole(shape):
    zeros = (0,) * len(shape)
    return pl.BlockSpec(shape, lambda *_: zeros)


def _row_tile(n_rows, preferred):
    for t in (preferred, 512, 256, 128):
        if t <= preferred and n_rows % t == 0:
            return t
    raise ValueError(f"row count {n_rows} is not a multiple of {V7X_LANES}")


def _dot(a, b):
    return jnp.dot(a, b, preferred_element_type=F32)


def _pack_bf16_pairs(x):
    w = x.shape[1] // 2
    lo = pltpu.bitcast(x[:, :w].astype(BF16).astype(F32), jnp.uint32) >> 16
    hi = pltpu.bitcast(x[:, w:].astype(BF16).astype(F32), jnp.uint32) & jnp.uint32(0xFFFF0000)
    return hi | lo


def _unpack_bf16_pairs(u):
    lo = pltpu.bitcast(u << 16, F32)
    hi = pltpu.bitcast(u & jnp.uint32(0xFFFF0000), F32)
    return jnp.concatenate([lo, hi], axis=1)


def _rms(x, g):
    return x * lax.rsqrt(jnp.mean(x * x, axis=-1, keepdims=True) + RMS_EPS) * g


def _layernorm(x, g, b):
    mu = jnp.mean(x, axis=-1, keepdims=True)
    xc = x - mu
    var = jnp.mean(xc * xc, axis=-1, keepdims=True)
    return xc * lax.rsqrt(var + LN_EPS) * g + b


def _sigmoid(x):
    return 0.5 * jnp.tanh(0.5 * x) + 0.5


def _silu(x):
    return x * _sigmoid(x)


def _softplus(x):
    return jnp.maximum(x, 0.0) + jnp.log1p(jnp.exp(-jnp.abs(x)))


def _even_in_kernel(x_ref, g_ref, w_ref, lng_ref, lnb_ref, z_ref, xbc_ref, ug_ref, vn_ref, dt_ref,
                    *, d_ssd, conv_dim, d_gmlp):
    h = _rms(x_ref[...], g_ref[...]).astype(BF16)
    o1 = d_ssd
    o2 = o1 + conv_dim
    o3 = o2 + d_gmlp
    o4 = o3 + d_gmlp
    z_ref[...] = _dot(h, w_ref[:, 0:o1])
    xbc_ref[...] = _dot(h, w_ref[:, o1:o2])
    ug_ref[...] = jax.nn.gelu(_dot(h, w_ref[:, o2:o3])).astype(ug_ref.dtype)
    v = jax.nn.gelu(_dot(h, w_ref[:, o3:o4]))
    vn_ref[...] = _layernorm(v, lng_ref[...], lnb_ref[...])
    dt_ref[...] = _dot(h, w_ref[:, o4:o4 + V7X_LANES])


def _even_in(x, g, w, ln_g, ln_b, *, d_ssd, conv_dim, d_gmlp):
    t, d = x.shape
    tm = _row_tile(t, 512)
    row = lambda n: pl.BlockSpec((tm, n), lambda i: (i, 0))
    return pl.pallas_call(
        functools.partial(_even_in_kernel, d_ssd=d_ssd, conv_dim=conv_dim, d_gmlp=d_gmlp),
        grid=(t // tm,),
        in_specs=[row(d), _resident(g.shape), _resident(w.shape), _resident(ln_g.shape), _resident(ln_b.shape)],
        out_specs=[row(d_ssd), row(conv_dim), row(d_gmlp), row(d_gmlp), row(V7X_LANES)],
        out_shape=[jax.ShapeDtypeStruct((t, d_ssd), F32), jax.ShapeDtypeStruct((t, conv_dim), F32),
                   jax.ShapeDtypeStruct((t, d_gmlp), BF16), jax.ShapeDtypeStruct((t, d_gmlp), F32),
                   jax.ShapeDtypeStruct((t, V7X_LANES), F32)],
        compiler_params=_params("parallel"),
        name="even_in",
    )(x, g, w, ln_g, ln_b)


def _ssd_prompt_kernel(xbc_ref, z_ref, dt_ref, cw_ref, cb_ref, dtb_ref, alog_ref, dskip_ref, ng_ref,
                       yn_ref, st_ref, xp_ref, state_ref, xc_ref, *, d_ssd, n_taps):
    c = pl.program_id(1)

    @pl.when(c == 0)
    def _():
        state_ref[...] = jnp.zeros_like(state_ref)
        xp_ref[:, 0:V7X_SUBLANES, :] = jnp.zeros((xp_ref.shape[0], V7X_SUBLANES, xp_ref.shape[2]), F32)

    for s in range(xbc_ref.shape[1]):
        _ssd_chunk(xbc_ref.at[0, s], z_ref.at[0, s], dt_ref.at[0, s], cw_ref, cb_ref, dtb_ref, alog_ref, dskip_ref,
                   ng_ref, yn_ref.at[0, s], xp_ref.at[s], state_ref.at[s], xc_ref.at[s], d_ssd=d_ssd, n_taps=n_taps)

    @pl.when(c == pl.num_programs(1) - 1)
    def _():
        st_ref[...] = state_ref[...]


def _ssd_chunk(xbc_ref, z_ref, dt_ref, cw_ref, cb_ref, dtb_ref, alog_ref, dskip_ref, ng_ref,
               yn_ref, xp_ref, state_ref, xc_ref, *, d_ssd, n_taps):
    n_pairs = state_ref.shape[0]
    ch = SSD_CHUNK
    gn = SSD_GROUPS * SSD_STATE

    xp_ref[V7X_SUBLANES:V7X_SUBLANES + ch, :] = xbc_ref[...]
    base = V7X_SUBLANES - (n_taps - 1)
    acc = cb_ref[...] + cw_ref[n_taps - 1:n_taps, :] * xbc_ref[...]
    for k in range(n_taps - 1):
        acc = acc + cw_ref[k:k + 1, :] * xp_ref[base + k:base + k + ch, :]
    xp_ref[0:V7X_SUBLANES, :] = xp_ref[ch:ch + V7X_SUBLANES, :]
    xc_ref[...] = _silu(acc)
    xc = xc_ref

    dt = _softplus(dt_ref[...] + dtb_ref[...])
    a = dt * (-jnp.exp(alog_ref[...]))
    li = lax.broadcasted_iota(jnp.int32, (ch, ch), 0)
    si = lax.broadcasted_iota(jnp.int32, (ch, ch), 1)
    causal = li >= si
    tril = jnp.where(causal, 1.0, 0.0).astype(F32)
    acs = jnp.dot(tril, a, precision=HIGHEST, preferred_element_type=F32)
    acs_t = acs.T
    lane = lax.broadcasted_iota(jnp.int32, (ch, 2 * SSD_HEAD_DIM), 1)
    first = lane < SSD_HEAD_DIM
    first_n = lax.broadcasted_iota(jnp.int32, (SSD_STATE, 2 * SSD_HEAD_DIM), 1) < SSD_HEAD_DIM
    pairs_per_group = n_pairs // SSD_GROUPS

    for g in range(SSD_GROUPS):
        bg = xc[:, d_ssd + g * SSD_STATE:d_ssd + (g + 1) * SSD_STATE]
        cg = xc[:, d_ssd + gn + g * SSD_STATE:d_ssd + gn + (g + 1) * SSD_STATE]
        bg_t = bg.T
        cb = _dot(cg.astype(BF16), bg_t.astype(BF16))
        ys = []
        for q in range(pairs_per_group):
            pair = g * pairs_per_group + q
            h0 = 2 * pair
            lo = pair * 2 * SSD_HEAD_DIM
            xs = xc[:, lo:lo + 2 * SSD_HEAD_DIM]
            dt2 = jnp.where(first, dt[:, h0:h0 + 1], dt[:, h0 + 1:h0 + 2])
            xdt = (xs * dt2).astype(BF16)
            s_prev = state_ref[pair]
            s_prev_b = s_prev.astype(BF16)
            y2, snew2, dec2 = [], [], []
            for r in range(2):
                h = h0 + r
                col = acs[:, h:h + 1]
                row = acs_t[h:h + 1, :]
                last = acs_t[h:h + 1, ch - 1:ch]
                decay = jnp.exp(jnp.where(causal, col - row, -jnp.inf))
                y = _dot((cb * decay).astype(BF16), xdt)
                y = y + _dot((cg * jnp.exp(col)).astype(BF16), s_prev_b)
                snew = _dot((bg_t * jnp.exp(last - row)).astype(BF16), xdt)
                y2.append(y)
                snew2.append(snew)
                dec2.append(jnp.exp(last))
            y = jnp.where(first, y2[0], y2[1])
            state_ref[pair] = (s_prev * jnp.where(first_n, dec2[0], dec2[1])
                               + jnp.where(first_n, snew2[0], snew2[1]))
            ys.append(y + dskip_ref[:, lo:lo + 2 * SSD_HEAD_DIM] * xs)
        gw = pairs_per_group * 2 * SSD_HEAD_DIM
        yg = jnp.concatenate(ys, axis=-1) * _silu(z_ref[:, g * gw:(g + 1) * gw])
        yg = yg * lax.rsqrt(jnp.mean(yg * yg, axis=-1, keepdims=True) + RMS_EPS)
        yn_ref[:, g * gw:(g + 1) * gw] = (yg * ng_ref[:, g * gw:(g + 1) * gw]).astype(yn_ref.dtype)


def _ssd_prompt(xbc, z, dt, conv_w, conv_b, dt_bias, a_log, d_skip, norm_g, *, batch, seq):
    t, conv_dim = xbc.shape
    d_ssd = z.shape[1]
    n_pairs = d_ssd // (2 * SSD_HEAD_DIM)
    nc = seq // SSD_CHUNK
    n_taps = conv_w.shape[0]
    g = SSD_SEQS_PER_STEP if batch % SSD_SEQS_PER_STEP == 0 else 1
    tile = lambda n: pl.BlockSpec((1, g, SSD_CHUNK, n), lambda b, c: (b, 0, c, 0))
    by_seq = lambda a: a.reshape(batch // g, g, seq, a.shape[1])
    yn, st = pl.pallas_call(
        functools.partial(_ssd_prompt_kernel, d_ssd=d_ssd, n_taps=n_taps),
        grid=(batch // g, nc),
        in_specs=[tile(conv_dim), tile(d_ssd), tile(V7X_LANES), _resident(conv_w.shape), _resident(conv_b.shape),
                  _resident(dt_bias.shape), _resident(a_log.shape), _resident(d_skip.shape),
                  _resident(norm_g.shape)],
        out_specs=[tile(d_ssd),
                   pl.BlockSpec((g, n_pairs, SSD_STATE, 2 * SSD_HEAD_DIM), lambda b, c: (b, 0, 0, 0))],
        out_shape=[jax.ShapeDtypeStruct((batch // g, g, seq, d_ssd), BF16),
                   jax.ShapeDtypeStruct((batch, n_pairs, SSD_STATE, 2 * SSD_HEAD_DIM), F32)],
        scratch_shapes=[pltpu.VMEM((g, SSD_CHUNK + V7X_SUBLANES, conv_dim), F32),
                        pltpu.VMEM((g, n_pairs, SSD_STATE, 2 * SSD_HEAD_DIM), F32),
                        pltpu.VMEM((g, SSD_CHUNK, conv_dim), F32)],
        compiler_params=_params("parallel", "arbitrary"),
        name="ssd_prompt",
    )(by_seq(xbc), by_seq(z), by_seq(dt), conv_w, conv_b, dt_bias, a_log, d_skip, norm_g)
    return yn.reshape(t, d_ssd), st


def _gmlp_prompt_kernel(ug_ref, vn_ref, ws_ref, bs_ref, out_ref):
    ch = GMLP_CHUNK
    n_groups = ws_ref.shape[0]
    gd = vn_ref.shape[1] // n_groups
    ii = lax.broadcasted_iota(jnp.int32, (ch, ch), 0)
    jj = lax.broadcasted_iota(jnp.int32, (ch, ch), 1)
    for g in range(n_groups):
        ws = jnp.where(ii >= jj, ws_ref[g], 0.0).astype(BF16)
        cols = slice(g * gd, (g + 1) * gd)
        for r0 in range(0, vn_ref.shape[0], ch):
            rows = slice(r0, r0 + ch)
            mixed = _dot(ws, vn_ref[rows, cols].astype(BF16)) + bs_ref[:, g:g + 1]
            out_ref[rows, cols] = (ug_ref[rows, cols].astype(F32) * mixed).astype(out_ref.dtype)


def _gmlp_prompt(ug, vn, w_s, b_s_t, *, seq):
    t, d = vn.shape
    rows = _row_tile(seq, GMLP_TILE)
    assert rows % GMLP_CHUNK == 0
    tile = pl.BlockSpec((rows, d), lambda i: (i, 0))
    return pl.pallas_call(
        _gmlp_prompt_kernel,
        grid=(t // rows,),
        in_specs=[tile, tile, _resident(w_s.shape), _resident(b_s_t.shape)],
        out_specs=tile,
        out_shape=jax.ShapeDtypeStruct((t, d), BF16),
        compiler_params=_params("parallel"),
        name="gmlp_prompt",
    )(ug, vn, w_s, b_s_t)


def _ssd_sample_prep_kernel(xbc_ref, buf_ref, dt_ref, cw_ref, cb_ref, dtb_ref, alog_ref, expand_ref,
                            newbuf_ref, xs_ref, xdt_ref, bc_ref, da_ref, *, d_ssd, n_taps):
    conv_dim = xbc_ref.shape[1]
    x_new = xbc_ref[...]
    acc = cb_ref[...] + cw_ref[n_taps - 1:n_taps, :] * x_new
    for k in range(n_taps - 1):
        acc = acc + cw_ref[k:k + 1, :] * buf_ref[:, k * conv_dim:(k + 1) * conv_dim]
    for k in range(n_taps - 2):
        newbuf_ref[:, k * conv_dim:(k + 1) * conv_dim] = buf_ref[:, (k + 1) * conv_dim:(k + 2) * conv_dim]
    newbuf_ref[:, (n_taps - 2) * conv_dim:(n_taps - 1) * conv_dim] = x_new
    xc = _silu(acc)
    xs = xc[:, :d_ssd]
    dt = _softplus(dt_ref[...] + dtb_ref[...])
    da_ref[...] = jnp.exp(dt * (-jnp.exp(alog_ref[...])))
    dt_wide = jnp.dot(dt, expand_ref[...], precision=HIGHEST, preferred_element_type=F32)
    xs_ref[...] = xs
    xdt_ref[...] = xs * dt_wide
    bc_ref[...] = xc[:, d_ssd:]


def _ssd_sample_prep(xbc, buf, dt, conv_w, conv_b, dt_bias, a_log, expand):
    n, conv_dim = xbc.shape
    d_ssd = expand.shape[1]
    n_taps = conv_w.shape[0]
    args = (xbc, buf, dt, conv_w, conv_b, dt_bias, a_log, expand)
    return pl.pallas_call(
        functools.partial(_ssd_sample_prep_kernel, d_ssd=d_ssd, n_taps=n_taps),
        grid=(1,),
        in_specs=[_resident(a.shape) for a in args],
        out_specs=[_whole(buf.shape), _whole((n, d_ssd)), _whole((n, d_ssd)),
                   _whole((n, conv_dim - d_ssd)), _whole((n, V7X_LANES))],
        out_shape=[jax.ShapeDtypeStruct(buf.shape, F32), jax.ShapeDtypeStruct((n, d_ssd), F32),
                   jax.ShapeDtypeStruct((n, d_ssd), F32), jax.ShapeDtypeStruct((n, conv_dim - d_ssd), F32),
                   jax.ShapeDtypeStruct((n, V7X_LANES), F32)],
        compiler_params=_params("arbitrary"),
        name="ssd_sample_prep",
    )(*args)


def _ssd_sample_state_kernel(h_ref, xdt_ref, da_ref, bc_ref, hnew_ref, y_ref, xcol_ref, ccol_ref):
    bs, n_heads, p, s = h_ref.shape
    n_pairs = n_heads // 2
    pairs_per_group = n_pairs // SSD_GROUPS
    gn = SSD_GROUPS * s
    w = 2 * p
    eye = lax.broadcasted_iota(jnp.int32, (w, w), 0) == lax.broadcasted_iota(jnp.int32, (w, w), 1)
    ones = jnp.ones((w, V7X_LANES), BF16)

    def stacked_diag(v):
        return jnp.concatenate(
            [jnp.where(eye, jnp.broadcast_to(v[j:j + 1], (w, w)), 0.0) for j in range(bs)], axis=0).astype(BF16)

    for q in range(n_pairs):
        x = xdt_ref[:, q * w:(q + 1) * w]
        hi = x.astype(BF16).astype(F32)
        mid = (x - hi).astype(BF16).astype(F32)
        lo = (x - hi) - mid
        xcol_ref[q] = _dot(stacked_diag(hi), ones) + (_dot(stacked_diag(mid), ones) + _dot(stacked_diag(lo), ones))
    for g in range(SSD_GROUPS):
        ccol_ref[g] = _dot(stacked_diag(bc_ref[:, gn + g * s:gn + (g + 1) * s]), ones).astype(BF16)

    for j in range(bs):
        for g in range(SSD_GROUPS):
            brow = bc_ref[j:j + 1, g * s:(g + 1) * s]
            h_new = []
            for q in range(g * pairs_per_group, (g + 1) * pairs_per_group):
                for r in range(2):
                    h = 2 * q + r
                    xcol = xcol_ref[q, j * w + r * p:j * w + (r + 1) * p, :]
                    h_new.append(h_ref[j, h] * da_ref[j:j + 1, h:h + 1] + xcol * brow)
                    hnew_ref[j, h] = h_new[-1]
            hc = _dot(jnp.concatenate(h_new, axis=0).astype(BF16), ccol_ref[g, j * s:(j + 1) * s, :])
            for i in range(pairs_per_group):
                y = jnp.sum(jnp.where(eye, hc[i * w:(i + 1) * w], 0.0), axis=0, keepdims=True)
                q = g * pairs_per_group + i
                y_ref[j, q:q + 1, :] = y


def _ssd_sample_state(h, xdt, da, bc):
    n, n_heads, p, s = h.shape
    assert 2 * p == V7X_LANES and s == V7X_LANES
    bs = SAMPLE_STATE_TILE
    rows = lambda w: pl.BlockSpec((bs, w), lambda i: (i, 0))
    return pl.pallas_call(
        _ssd_sample_state_kernel,
        grid=(n // bs,),
        in_specs=[pl.BlockSpec((bs, n_heads, p, s), lambda i: (i, 0, 0, 0)), rows(xdt.shape[1]), rows(da.shape[1]),
                  rows(bc.shape[1])],
        out_specs=[pl.BlockSpec((bs, n_heads, p, s), lambda i: (i, 0, 0, 0)),
                   pl.BlockSpec((bs, n_heads // 2, 2 * p), lambda i: (i, 0, 0))],
        out_shape=[jax.ShapeDtypeStruct(h.shape, F32), jax.ShapeDtypeStruct((n, n_heads // 2, 2 * p), F32)],
        scratch_shapes=[pltpu.VMEM((n_heads // 2, bs * 2 * p, V7X_LANES), F32),
                        pltpu.VMEM((SSD_GROUPS, bs * s, V7X_LANES), BF16)],
        compiler_params=_params("parallel"),
        name="ssd_sample_state",
    )(h, xdt, da, bc)


def _even_sample_post_kernel(y_ref, xs_ref, z_ref, dskip_ref, ng_ref, ug_ref, vn_ref, w0_ref, b0_ref,
                             yn_ref, gout_ref):
    d_ssd = y_ref.shape[1]
    gw = d_ssd // SSD_GROUPS
    y = (y_ref[...] + dskip_ref[...] * xs_ref[...]) * _silu(z_ref[...])
    for g in range(SSD_GROUPS):
        yg = y[:, g * gw:(g + 1) * gw]
        yg = yg * lax.rsqrt(jnp.mean(yg * yg, axis=-1, keepdims=True) + RMS_EPS)
        yn_ref[:, g * gw:(g + 1) * gw] = (yg * ng_ref[:, g * gw:(g + 1) * gw]).astype(yn_ref.dtype)
    mixed = w0_ref[...] * vn_ref[...] + b0_ref[...]
    gout_ref[...] = (ug_ref[...].astype(F32) * mixed).astype(gout_ref.dtype)


def _even_sample_post(y, xs, z, d_skip, norm_g, ug, vn, w0, b0):
    args = (y, xs, z, d_skip, norm_g, ug, vn, w0, b0)
    return pl.pallas_call(
        _even_sample_post_kernel,
        grid=(1,),
        in_specs=[_resident(a.shape) for a in args],
        out_specs=[_whole(y.shape), _whole(vn.shape)],
        out_shape=[jax.ShapeDtypeStruct(y.shape, BF16), jax.ShapeDtypeStruct(vn.shape, BF16)],
        compiler_params=_params("arbitrary"),
        name="even_sample_post",
    )(*args)


def _even_out_kernel(x_ref, yn_ref, gout_ref, wo_ref, g_ref, wg_ref, wu_ref, wd_ref, out_ref):
    d_ssd = yn_ref.shape[1]
    x = x_ref[...] + _dot(yn_ref[...], wo_ref[0:d_ssd, :]) + _dot(gout_ref[...], wo_ref[d_ssd:, :])
    h = _rms(x, g_ref[...]).astype(BF16)
    mid = (_silu(_dot(h, wg_ref[...])) * _dot(h, wu_ref[...])).astype(BF16)
    out_ref[...] = x + _dot(mid, wd_ref[...])


def _even_out(x, yn, gout, w_out, g, wg, wu, wd):
    t, d = x.shape
    tm = _row_tile(t, 512)
    row = lambda n: pl.BlockSpec((tm, n), lambda i: (i, 0))
    return pl.pallas_call(
        _even_out_kernel,
        grid=(t // tm,),
        in_specs=[row(d), row(yn.shape[1]), row(gout.shape[1]), _resident(w_out.shape), _resident(g.shape),
                  _resident(wg.shape), _resident(wu.shape), _resident(wd.shape)],
        out_specs=row(d),
        out_shape=jax.ShapeDtypeStruct((t, d), F32),
        compiler_params=_params("parallel"),
        name="even_out_ffn",
    )(x, yn, gout, w_out, g, wg, wu, wd)


def _conf_in_kernel(x_ref, g_ref, w_ref, b_ref, a_ref):
    d = a_ref.shape[1]
    h = _rms(x_ref[...], g_ref[...]).astype(BF16)
    val = _dot(h, w_ref[:, 0:d]) + b_ref[:, 0:d]
    gate = _dot(h, w_ref[:, d:]) + b_ref[:, d:]
    a_ref[...] = val * _sigmoid(gate)


def _conf_in(x, g, w1, b1):
    t, d = x.shape
    dc = w1.shape[1] // 2
    tm = _row_tile(t, 512)
    return pl.pallas_call(
        _conf_in_kernel,
        grid=(t // tm,),
        in_specs=[pl.BlockSpec((tm, d), lambda i: (i, 0)), _resident(g.shape), _resident(w1.shape),
                  _resident(b1.shape)],
        out_specs=pl.BlockSpec((tm, dc), lambda i: (i, 0)),
        out_shape=jax.ShapeDtypeStruct((t, dc), F32),
        compiler_params=_params("parallel"),
        name="conf_in",
    )(x, g, w1, b1)


def _route(logits, n_experts, route_ref, route_t_ref):
    lane = lax.broadcasted_iota(jnp.int32, logits.shape, 1)
    lane_f = lane.astype(F32)
    big = jnp.float32(V7X_LANES)
    lg = jnp.where(lane < n_experts, logits, -jnp.inf)
    m1 = jnp.max(lg, axis=-1, keepdims=True)
    i1 = jnp.min(jnp.where(lg == m1, lane_f, big), axis=-1, keepdims=True)
    lg2 = jnp.where(lane_f == i1, -jnp.inf, lg)
    m2 = jnp.max(lg2, axis=-1, keepdims=True)
    i2 = jnp.min(jnp.where(lg2 == m2, lane_f, big), axis=-1, keepdims=True)
    e2 = jnp.exp(m2 - m1)
    g1 = 1.0 / (1.0 + e2)
    g2 = e2 / (1.0 + e2)
    route = jnp.where(lane == 0, i1, jnp.where(lane == 1, i2, jnp.where(lane == 2, g1,
                      jnp.where(lane == 3, g2, 0.0))))
    route_ref[...] = route
    route_t_ref[...] = route.T[0:V7X_SUBLANES, :]


def _conf_finish(c, x, lng_ref, lnb_ref, w2_ref, b2_ref, ng_ref, rhi_ref, rlo_ref, out_ref, hn_ref, route_ref,
                 route_t_ref, *, n_experts):
    hmid = _silu(_layernorm(c, lng_ref[...], lnb_ref[...])).astype(BF16)
    x_new = x + _dot(hmid, w2_ref[...]) + b2_ref[...]
    out_ref[...] = x_new
    hn = _rms(x_new, ng_ref[...])
    hn_ref[...] = _pack_bf16_pairs(hn)
    hi = hn.astype(BF16)
    lo = (hn - hi.astype(F32)).astype(BF16)
    logits = _dot(hi, rhi_ref[...]) + (_dot(lo, rhi_ref[...]) + _dot(hi, rlo_ref[...]))
    _route(logits, n_experts, route_ref, route_t_ref)


def _conf_prompt_kernel(a_ref, x_ref, xs_ref, hns_ref, routes_ref, dw_ref, db_ref, lng_ref, lnb_ref, w2_ref, b2_ref,
                        ng_ref, rhi_ref, rlo_ref, out_ref, hn_ref, route_ref, route_t_ref, xp_ref, c_ref, win_ref,
                        *, n_taps, n_experts, tiles_per_seq, n_prompt_tiles):
    tl, d = a_ref.shape
    halo = xp_ref.shape[0] - tl
    base = halo - (n_taps - 1)
    rows = CONF_ACC_ROWS
    step = pl.program_id(0)

    @pl.when(step < n_prompt_tiles)
    def _():
        @pl.when(step % tiles_per_seq == 0)
        def _():
            xp_ref[0:halo, :] = jnp.zeros((halo, d), F32)

        xp_ref[halo:halo + tl, :] = a_ref[...]

        def lane_block(j, carry):
            cols = pl.ds(pl.multiple_of(j * V7X_LANES, V7X_LANES), V7X_LANES)
            for r0 in range(0, tl, rows):
                acc = jnp.broadcast_to(db_ref[:, cols], (rows, V7X_LANES))
                for phase in range(V7X_SUBLANES):
                    taps = [k for k in range(n_taps) if (base + k) % V7X_SUBLANES == phase]
                    if not taps:
                        continue
                    span = max(base + k - phase for k in taps) + rows
                    win_ref[0:span, :] = xp_ref[pl.ds(r0 + phase, span), cols]
                    for k in taps:
                        o = base + k - phase
                        acc = acc + dw_ref[k:k + 1, cols] * win_ref[o:o + rows, :]
                c_ref[r0:r0 + rows, cols] = acc
            return carry

        lax.fori_loop(0, d // V7X_LANES, lane_block, 0)
        xp_ref[0:halo, :] = xp_ref[tl:tl + halo, :]
        _conf_finish(c_ref[...], x_ref[...], lng_ref, lnb_ref, w2_ref, b2_ref, ng_ref, rhi_ref, rlo_ref,
                     out_ref, hn_ref, route_ref, route_t_ref, n_experts=n_experts)

    @pl.when(step == n_prompt_tiles)
    def _():
        ns = xs_ref.shape[0]
        out_ref[0:ns, :] = xs_ref[...]
        hn_ref[0:ns, :] = hns_ref[...]
        route_ref[0:ns, :] = routes_ref[...]


def _conf_prompt(a, x, sample_rows, consts, *, seq, n_experts):
    t, d = a.shape
    n_s = sample_rows[0].shape[0]
    tl = _row_tile(seq, CONF_TILE)
    assert tl % CONF_ACC_ROWS == 0 and n_s <= tl
    n_prompt_tiles = t // tl
    n_taps = consts[0].shape[0]
    halo = -(-(n_taps - 1) // V7X_SUBLANES) * V7X_SUBLANES
    in_tile = lambda n: pl.BlockSpec((tl, n), lambda s: (jnp.minimum(s, n_prompt_tiles - 1), 0))
    out_tile = lambda n: pl.BlockSpec((tl, n), lambda s: (s, 0))
    small = sample_rows + tuple(consts)
    return pl.pallas_call(
        functools.partial(_conf_prompt_kernel, n_taps=n_taps, n_experts=n_experts, tiles_per_seq=seq // tl,
                          n_prompt_tiles=n_prompt_tiles),
        grid=(n_prompt_tiles + 1,),
        in_specs=[in_tile(d), in_tile(d)] + [_resident(c.shape) for c in small],
        out_specs=[out_tile(d), out_tile(d // 2), out_tile(V7X_LANES),
                   pl.BlockSpec((V7X_SUBLANES, tl), lambda s: (0, jnp.minimum(s, n_prompt_tiles - 1)))],
        out_shape=[jax.ShapeDtypeStruct((t + n_s, d), F32), jax.ShapeDtypeStruct((t + n_s, d // 2), jnp.uint32),
                   jax.ShapeDtypeStruct((t + n_s, V7X_LANES), F32), jax.ShapeDtypeStruct((V7X_SUBLANES, t), F32)],
        scratch_shapes=[pltpu.VMEM((tl + halo, d), F32), pltpu.VMEM((tl, d), F32),
                        pltpu.VMEM((CONF_ACC_ROWS + halo, V7X_LANES), F32)],
        compiler_params=_params("arbitrary"),
        name="conf_prompt",
    )(a, x, *small)


def _conf_sample_kernel(a_ref, buf_ref, x_ref, dw_ref, db_ref, lng_ref,
                        lnb_ref, w2_ref, b2_ref, ng_ref, rhi_ref, rlo_ref, newbuf_ref, out_ref, hn_ref, route_ref,
                        route_t_ref, *, n_taps, n_experts):
    d = a_ref.shape[1]
    a_new = a_ref[...]
    acc = db_ref[...] + dw_ref[n_taps - 1:n_taps, :] * a_new
    for k in range(n_taps - 1):
        acc = acc + dw_ref[k:k + 1, :] * buf_ref[:, k * d:(k + 1) * d]
    for k in range(n_taps - 2):
        newbuf_ref[:, k * d:(k + 1) * d] = buf_ref[:, (k + 1) * d:(k + 2) * d]
    newbuf_ref[:, (n_taps - 2) * d:(n_taps - 1) * d] = a_new
    _conf_finish(acc, x_ref[...], lng_ref, lnb_ref, w2_ref, b2_ref, ng_ref, rhi_ref, rlo_ref,
                 out_ref, hn_ref, route_ref, route_t_ref, n_experts=n_experts)


def _conf_sample(a, buf, x, consts, *, n_experts):
    n, d = a.shape
    args = (a, buf, x) + tuple(consts)
    return pl.pallas_call(
        functools.partial(_conf_sample_kernel, n_taps=consts[0].shape[0], n_experts=n_experts),
        grid=(1,),
        in_specs=[_resident(v.shape) for v in args],
        out_specs=[_whole(buf.shape), _whole((n, d)), _whole((n, d // 2)), _whole((n, V7X_LANES)),
                   _whole((V7X_SUBLANES, n))],
        out_shape=[jax.ShapeDtypeStruct(buf.shape, F32), jax.ShapeDtypeStruct((n, d), F32),
                   jax.ShapeDtypeStruct((n, d // 2), jnp.uint32), jax.ShapeDtypeStruct((n, V7X_LANES), F32),
                   jax.ShapeDtypeStruct((V7X_SUBLANES, n), F32)],
        compiler_params=_params("arbitrary"),
        name="conf_sample",
    )(*args)


def _sc_window_rows(x):
    assert x.dtype.itemsize == 4, "the SparseCore indirect copies move 32-bit elements"
    return SC_WINDOW_BYTES // (x.shape[1] * x.dtype.itemsize)


def _sc_gather_rows(x, idx):
    n, d = idx.shape[0], x.shape[1]
    w = _sc_window_rows(x)
    assert n % w == 0
    mesh = plsc.VectorSubcoreMesh(core_axis_name="core", subcore_axis_name="subcore")

    @functools.partial(pl.kernel, out_type=jax.ShapeDtypeStruct((n, d), x.dtype), mesh=mesh, scratch_types=[])
    def gather(x_hbm, idx_hbm, out_hbm):
        def body(idx_vmem, out_vmem):
            pltpu.sync_copy(x_hbm.at[idx_vmem.at[0]], out_vmem)

        pltpu.emit_pipeline(
            body,
            grid=(n // w,),
            in_specs=[pl.BlockSpec((1, w), lambda i: (i, 0))],
            out_specs=[pl.BlockSpec((w, d), lambda i: (i, 0))],
            core_axis_name=("core", "subcore"),
            dimension_semantics=(pltpu.PARALLEL,),
        )(idx_hbm, out_hbm)

    return gather(x, idx.reshape(n // w, w))


def _sc_dispatch_rows(x, pos, pad_pos, n_out):
    t, d = x.shape
    w = _sc_window_rows(x)
    n_pad = pad_pos.shape[0]
    assert t % w == 0 and n_pad % w == 0 and TOP_K * t + n_pad == n_out
    mesh = plsc.VectorSubcoreMesh(core_axis_name="core", subcore_axis_name="subcore")

    @functools.partial(pl.kernel, out_type=jax.ShapeDtypeStruct((n_out, d), x.dtype), mesh=mesh, scratch_types=[])
    def dispatch(x_hbm, pos_hbm, pad_hbm, zero_hbm, out_hbm):
        def body(rows_vmem, idx_vmem):
            pltpu.sync_copy(rows_vmem, out_hbm.at[idx_vmem.at[0]])

        def scatter(src_hbm, idx_hbm, src_map):
            pltpu.emit_pipeline(
                body,
                grid=(idx_hbm.shape[0],),
                in_specs=[pl.BlockSpec((w, d), src_map), pl.BlockSpec((1, w), lambda i: (i, 0))],
                out_specs=[],
                core_axis_name=("core", "subcore"),
                dimension_semantics=(pltpu.PARALLEL,),
            )(src_hbm, idx_hbm)

        for k in range(TOP_K):
            scatter(x_hbm, pos_hbm.at[k], lambda i: (i, 0))
        scatter(zero_hbm, pad_hbm, lambda i: (0, 0))

    return dispatch(x, pos.reshape(TOP_K, t // w, w), pad_pos.reshape(n_pad // w, w), jnp.zeros((w, d), x.dtype))


def _moe_ffn_kernel(tile_expert_ref, first_ref, bank_ref, next_expert_ref, n_tiles_ref, x_ref, wg_hbm, wu_hbm,
                    wd_hbm, out_ref, wg_v, wu_v, wd_v, stage_g, stage_u, stage_d, acc_ref, sems):
    i = pl.program_id(0)
    dff = wg_v.shape[2]
    ck = stage_g.shape[2]
    n_chunks = dff // ck
    active = i < n_tiles_ref[0]
    starts_run = active & (first_ref[i] == 1)
    bank = bank_ref[i]
    nxt = next_expert_ref[i]

    def chunk_copies(expert, c, slot):
        cols = slice(c * ck, (c + 1) * ck)
        return (pltpu.make_async_copy(wg_hbm.at[expert, :, cols], stage_g.at[slot], sems.at[0, slot]),
                pltpu.make_async_copy(wu_hbm.at[expert, :, cols], stage_u.at[slot], sems.at[1, slot]),
                pltpu.make_async_copy(wd_hbm.at[expert, cols, :], stage_d.at[slot], sems.at[2, slot]))

    def stream_weights(expert, dst_bank, between_chunks):
        for cp in chunk_copies(expert, 0, 0):
            cp.start()
        for c in range(n_chunks):
            slot = c % 2
            if c + 1 < n_chunks:
                for cp in chunk_copies(expert, c + 1, 1 - slot):
                    cp.start()
            for cp in chunk_copies(expert, c, slot):
                cp.wait()
            cols = slice(c * ck, (c + 1) * ck)
            wg_v[dst_bank, :, cols] = stage_g[slot].astype(BF16)
            wu_v[dst_bank, :, cols] = stage_u[slot].astype(BF16)
            wd_v[dst_bank, cols, :] = stage_d[slot].astype(BF16)
            between_chunks(c)

    def ffn_part(xb, cols):
        mid = (_silu(_dot(xb, wg_v[bank, :, cols])) * _dot(xb, wu_v[bank, :, cols])).astype(BF16)
        return _dot(mid, wd_v[bank, cols, :])

    @pl.when(starts_run & (i == 0))
    def _():
        stream_weights(tile_expert_ref[i], bank, lambda c: None)

    overlapped = starts_run & (nxt >= 0)

    @pl.when(overlapped)
    def _():
        xb = _unpack_bf16_pairs(x_ref[...]).astype(BF16)
        acc_ref[...] = jnp.zeros_like(acc_ref)

        def piece(c):
            acc_ref[...] += ffn_part(xb, slice(c * ck, (c + 1) * ck))

        stream_weights(nxt, 1 - bank, piece)
        out_ref[...] = _pack_bf16_pairs(acc_ref[...])

    @pl.when(active & jnp.logical_not(overlapped))
    def _():
        xb = _unpack_bf16_pairs(x_ref[...]).astype(BF16)
        half = dff // 2
        acc = ffn_part(xb, slice(0, half)) + ffn_part(xb, slice(half, dff))
        out_ref[...] = _pack_bf16_pairs(acc)

    @pl.when(jnp.logical_not(active))
    def _():
        out_ref[...] = jnp.zeros_like(out_ref)


def _moe_ffn(tile_expert, n_tiles, xs, wg, wu, wd, *, tm):
    p, d = xs.shape[0], wg.shape[1]
    dff = wg.shape[2]
    ck = MOE_FF_CHUNK
    assert dff % ck == 0
    nt = p // tm
    i32 = jnp.int32
    tile = jnp.arange(nt, dtype=i32)
    first = jnp.concatenate([jnp.ones((1,), i32), (tile_expert[1:] != tile_expert[:-1]).astype(i32)])
    bank = (jnp.cumsum(first) - 1) % 2
    run_start = jnp.where((first == 1) & (tile < n_tiles[0]), tile, nt)
    next_start = jnp.concatenate([lax.cummin(run_start, reverse=True)[1:], jnp.full((1,), nt, i32)])
    next_expert = jnp.sum((next_start[:, None] == tile[None, :]).astype(i32) * tile_expert[None, :], axis=1)
    next_expert = jnp.where(next_start < nt, next_expert, -1)
    any_spec = pl.BlockSpec(memory_space=pl.ANY)
    row_tile = pl.BlockSpec((tm, d // 2), lambda i, *_: (i, 0))
    grid_spec = pltpu.PrefetchScalarGridSpec(
        num_scalar_prefetch=5,
        grid=(nt,),
        in_specs=[row_tile, any_spec, any_spec, any_spec],
        out_specs=row_tile,
        scratch_shapes=[pltpu.VMEM((2, d, dff), BF16), pltpu.VMEM((2, d, dff), BF16), pltpu.VMEM((2, dff, d), BF16),
                        pltpu.VMEM((2, d, ck), F32), pltpu.VMEM((2, d, ck), F32), pltpu.VMEM((2, ck, d), F32),
                        pltpu.VMEM((tm, d), F32), pltpu.SemaphoreType.DMA((3, 2))],
    )
    return pl.pallas_call(
        _moe_ffn_kernel,
        grid_spec=grid_spec,
        out_shape=jax.ShapeDtypeStruct((p, d // 2), jnp.uint32),
        compiler_params=_params("arbitrary"),
        name="moe_ffn",
    )(tile_expert, first, bank.astype(i32), next_expert.astype(i32), n_tiles, xs, wg, wu, wd)


def _moe_combine_kernel(x_ref, route_ref, gf_ref, a_ref, b_ref, y_ref):
    route = route_ref[...]
    x = (x_ref[...] + route[:, 2:3] * _unpack_bf16_pairs(a_ref[...])
         + route[:, 3:4] * _unpack_bf16_pairs(b_ref[...]))
    y_ref[...] = _rms(x, gf_ref[...])


def _moe_combine(x, route, gf, picked, *, row0, n_rows, a_row0, b_row0):
    d = x.shape[1]
    tc = _row_tile(n_rows, COMBINE_TILE)
    assert row0 % tc == 0 and a_row0 % tc == 0 and b_row0 % tc == 0
    rows = lambda w, r0: pl.BlockSpec((tc, w), lambda i: (i + r0 // tc, 0))
    return pl.pallas_call(
        _moe_combine_kernel,
        grid=(n_rows // tc,),
        in_specs=[rows(d, row0), rows(V7X_LANES, row0), _resident(gf.shape), rows(d // 2, a_row0),
                  rows(d // 2, b_row0)],
        out_specs=pl.BlockSpec((tc, d), lambda i: (i, 0)),
        out_shape=jax.ShapeDtypeStruct((n_rows, d), F32),
        compiler_params=_params("parallel"),
        name="moe_combine",
    )(x, route, gf, picked, picked)


def _moe_plan(route_t, n_experts, tm):
    t = route_t.shape[1]
    n_slots = t * TOP_K
    i32 = jnp.int32
    expert = route_t[:TOP_K].astype(i32).reshape(1, n_slots)
    onehot = (expert == jnp.arange(n_experts, dtype=i32)[:, None]).astype(i32)
    counts = jnp.sum(onehot, axis=1)
    tiles = (counts + tm - 1) // tm
    tile_end = jnp.cumsum(tiles)
    start = (tile_end - tiles) * tm
    rank = jnp.cumsum(onehot, axis=1) - onehot
    pos = jnp.sum(onehot * (rank + start[:, None]), axis=0)
    nt = -(-n_slots // tm) + n_experts
    n_tiles = tile_end[-1]
    tile_id = jnp.minimum(jnp.arange(nt, dtype=i32), n_tiles - 1)
    tile_expert = jnp.sum((tile_id[:, None] >= tile_end[None, :]).astype(i32), axis=1)
    seg_start = jnp.concatenate([start + counts, (n_tiles * tm)[None]])
    seg_len = jnp.concatenate([tiles * tm - counts, (nt * tm - n_tiles * tm)[None]])
    seg_end = jnp.cumsum(seg_len)
    q = jnp.arange(nt * tm - n_slots, dtype=i32)
    in_seg = (q[:, None] >= (seg_end - seg_len)[None, :]) & (q[:, None] < seg_end[None, :])
    pad_pos = jnp.sum(in_seg.astype(i32) * (seg_start - (seg_end - seg_len))[None, :], axis=1) + q
    return (tile_expert.astype(i32), n_tiles.astype(i32).reshape(1), pos.reshape(TOP_K, t).astype(i32),
            pad_pos.astype(i32), nt * tm)


def _row(v):
    return v.reshape(1, -1).astype(F32)


def _pad_lanes(v):
    return jnp.pad(_row(v), ((0, 0), (0, V7X_LANES - v.shape[-1])))


def kernel(x_prompt, x_sample, state_ssm, state_ssd_conv, state_conf_conv, norm_mix_even, w_in_even, ssd_conv_w, ssd_conv_b, ssd_dt_bias, ssd_a_log, ssd_d, ssd_norm, gmlp_ln_g, gmlp_ln_b, gmlp_w_s, gmlp_b_s, w_out_even, norm_ffn_even, ffn_w_gate, ffn_w_up, ffn_w_down, norm_mix_odd, conf_w1, conf_b1, conf_dw_w, conf_dw_b, conf_ln_g, conf_ln_b, conf_w2, conf_b2, norm_ffn_odd, moe_router, moe_w_gate, moe_w_up, moe_w_down, final_norm):
    batch, seq, d_model = x_prompt.shape
    n_dec, dec_seq, _ = x_sample.shape
    assert dec_seq == 1, "the sample group advances one token per sequence"
    assert seq % SSD_CHUNK == 0 and n_dec % V7X_LANES == 0
    n_even, n_odd = w_in_even.shape[0], conf_w1.shape[0]
    assert (n_even, n_odd) == (1, 1), "the final norm is fused into the last (odd) layer's MoE combine"
    n_heads = ssd_dt_bias.shape[1]
    d_ssd = n_heads * SSD_HEAD_DIM
    conv_dim = ssd_conv_w.shape[2]
    d_gmlp = gmlp_ln_g.shape[1]
    n_groups_gmlp = gmlp_w_s.shape[1]
    n_experts = moe_router.shape[2]
    tp = batch * seq

    xp = x_prompt.reshape(tp, d_model)
    xs = x_sample.reshape(n_dec, d_model)
    outs = dict(ssm_p=[], conv_p=[], conf_p=[], ssm_s=[], conv_s=[], conf_s=[], v_s=[])

    for layer in range(n_even + n_odd):
        i = layer // 2
        if layer % 2 == 0:
            w = w_in_even[i]
            o1, o2, o3 = d_ssd, d_ssd + conv_dim, d_ssd + conv_dim + n_heads
            w_r = jnp.concatenate([w[:, :o2], w[:, o3:], jnp.pad(w[:, o2:o3], ((0, 0), (0, V7X_LANES - n_heads)))],
                                  axis=1).astype(BF16)
            in_args = (_row(norm_mix_even[i]), w_r, _row(gmlp_ln_g[i]), _row(gmlp_ln_b[i]))
            dims = dict(d_ssd=d_ssd, conv_dim=conv_dim, d_gmlp=d_gmlp)
            dt_bias, a_log = _pad_lanes(ssd_dt_bias[i]), _pad_lanes(ssd_a_log[i])
            d_skip = _row(jnp.repeat(ssd_d[i], SSD_HEAD_DIM))
            norm_g = _row(ssd_norm[i])
            conv_w, conv_b = ssd_conv_w[i], _row(ssd_conv_b[i])
            z, xbc, ug, vn, dt = _even_in(xp, *in_args, **dims)
            yn, st = _ssd_prompt(xbc, z, dt, conv_w, conv_b, dt_bias, a_log, d_skip, norm_g, batch=batch, seq=seq)
            gout = _gmlp_prompt(ug, vn, gmlp_w_s[i], gmlp_b_s[i].T, seq=seq)
            n_pairs = n_heads // 2
            st = st.reshape(batch, n_pairs, SSD_STATE, 2, SSD_HEAD_DIM).transpose(0, 1, 3, 4, 2)
            outs['ssm_p'].append(st.reshape(batch, n_heads, SSD_HEAD_DIM, SSD_STATE))
            outs['conv_p'].append(xbc.reshape(batch, seq, conv_dim)[:, seq - (conv_w.shape[0] - 1):])
            ffn = (w_out_even[i].astype(BF16), _row(norm_ffn_even[i]), ffn_w_gate[i].astype(BF16),
                   ffn_w_up[i].astype(BF16), ffn_w_down[i].astype(BF16))
            xp = _even_out(xp, yn, gout, *ffn)
            z, xbc, ug, vn, dt = _even_in(xs, *in_args, **dims)
            expand = (jnp.arange(d_ssd)[None, :] // SSD_HEAD_DIM == jnp.arange(V7X_LANES)[:, None]).astype(F32)
            buf = state_ssd_conv[i].reshape(n_dec, -1)
            newbuf, xc, xdt, bc, da = _ssd_sample_prep(xbc, buf, dt, conv_w, conv_b, dt_bias, a_log, expand)
            h_new, y4 = _ssd_sample_state(state_ssm[i], xdt, da, bc)
            gd = d_gmlp // n_groups_gmlp
            w0 = _row(jnp.repeat(gmlp_w_s[i][:, 0, 0], gd))
            b0 = _row(jnp.repeat(gmlp_b_s[i][:, 0], gd))
            yn, gout = _even_sample_post(y4.reshape(n_dec, d_ssd), xc, z, d_skip, norm_g, ug, vn, w0, b0)
            outs['ssm_s'].append(h_new)
            outs['conv_s'].append(newbuf.reshape(state_ssd_conv[i].shape))
            outs['v_s'].append(vn.reshape(n_dec, 1, d_gmlp))
            xs = _even_out(xs, yn, gout, *ffn)
        else:
            w1, b1 = conf_w1[i].astype(BF16), _row(conf_b1[i])
            router = jnp.pad(moe_router[i], ((0, 0), (0, V7X_LANES - n_experts)))
            router_hi = router.astype(BF16)
            router_lo = (router - router_hi.astype(F32)).astype(BF16)
            tail = (conf_dw_w[i], _row(conf_dw_b[i]), _row(conf_ln_g[i]), _row(conf_ln_b[i]),
                    conf_w2[i].astype(BF16), _row(conf_b2[i]), _row(norm_ffn_odd[i]), router_hi, router_lo)
            n_keep = conf_dw_w.shape[1] - 1
            a = _conf_in(xp, _row(norm_mix_odd[i]), w1, b1)
            outs['conf_p'].append(a.reshape(batch, seq, -1)[:, seq - n_keep:])
            a_s = _conf_in(xs, _row(norm_mix_odd[i]), w1, b1)
            buf = state_conf_conv[i].reshape(n_dec, -1)
            newbuf, *sample_rows, route_t_s = _conf_sample(a_s, buf, xs, tail, n_experts=n_experts)
            outs['conf_s'].append(newbuf.reshape(state_conf_conv[i].shape))
            x_all, hn, route, route_t_p = _conf_prompt(a, xp, tuple(sample_rows), tail, seq=seq,
                                                       n_experts=n_experts)
            route_t = jnp.concatenate([route_t_p, route_t_s], axis=1)
            tile_expert, n_tiles, pos, pad_pos, n_rows = _moe_plan(route_t, n_experts, MOE_TILE)
            xs_sorted = _sc_dispatch_rows(hn, pos, pad_pos, n_rows)
            ys = _moe_ffn(tile_expert, n_tiles, xs_sorted, moe_w_gate[i], moe_w_up[i], moe_w_down[i], tm=MOE_TILE)
            picked = _sc_gather_rows(ys, jnp.concatenate([pos[:, :tp].reshape(-1), pos[:, tp:].reshape(-1)]))
            gf = _row(final_norm)
            xp = _moe_combine(x_all, route, gf, picked, row0=0, n_rows=tp, a_row0=0, b_row0=tp)
            xs = _moe_combine(x_all, route, gf, picked, row0=tp, n_rows=n_dec, a_row0=TOP_K * tp,
                              b_row0=TOP_K * tp + n_dec)

    y_prompt = xp.reshape(batch, seq, d_model)
    y_sample = xs.reshape(n_dec, 1, d_model)
    return (y_prompt, y_sample, jnp.stack(outs['ssm_p']), jnp.stack(outs['conv_p']), jnp.stack(outs['conf_p']),
            jnp.stack(outs['ssm_s']), jnp.stack(outs['conv_s']), jnp.stack(outs['conf_s']), jnp.stack(outs['v_s']))
```

```python
import functools

import jax
import jax.numpy as jnp
from jax import lax
from jax.experimental import pallas as pl
from jax.experimental.pallas import tpu as pltpu
from jax.experimental.pallas import tpu_sc as plsc

F32 = jnp.float32
BF16 = jnp.bfloat16
HIGHEST = lax.Precision.HIGHEST

SSD_HEAD_DIM = 64
SSD_GROUPS = 4
SSD_STATE = 128
SSD_CHUNK = 128
GMLP_CHUNK = 128
TOP_K = 2
RMS_EPS = 1e-6
LN_EPS = 1e-5

V7X_LANES = 128
V7X_SUBLANES = 8
V7X_VMEM_BYTES = 64 * 1024 * 1024
VMEM_LIMIT = (V7X_VMEM_BYTES * 7) // 8

MOE_TILE = 256
COMBINE_TILE = 512
SSD_SEQS_PER_STEP = 2
MOE_FF_CHUNK = 256
CONF_TILE = 256
CONF_ACC_ROWS = 64
SC_WINDOW_BYTES = 128 * 1024
SAMPLE_STATE_TILE = 8


def _params(*semantics):
    return pltpu.CompilerParams(dimension_semantics=semantics, vmem_limit_bytes=VMEM_LIMIT)


def _resident(shape):
    zeros = (0,) * len(shape)
    return pl.BlockSpec(shape, lambda *_: zeros, pipeline_mode=pl.Buffered(1))


def _whole(shape):
    zeros = (0,) * len(shape)
    return pl.BlockSpec(shape, lambda *_: zeros)


def _row_tile(n_rows, preferred):
    for t in (preferred, 512, 256, 128):
        if t <= preferred and n_rows % t == 0:
            return t
    raise ValueError(f"row count {n_rows} is not a multiple of {V7X_LANES}")


def _dot(a, b):
    return jnp.dot(a, b, preferred_element_type=F32)


def _pack_bf16_pairs(x):
    w = x.shape[1] // 2
    lo = pltpu.bitcast(x[:, :w].astype(BF16).astype(F32), jnp.uint32) >> 16
    hi = pltpu.bitcast(x[:, w:].astype(BF16).astype(F32), jnp.uint32) & jnp.uint32(0xFFFF0000)
    return hi | lo


def _unpack_bf16_pairs(u):
    lo = pltpu.bitcast(u << 16, F32)
    hi = pltpu.bitcast(u & jnp.uint32(0xFFFF0000), F32)
    return jnp.concatenate([lo, hi], axis=1)


def _rms(x, g):
    return x * lax.rsqrt(jnp.mean(x * x, axis=-1, keepdims=True) + RMS_EPS) * g


def _layernorm(x, g, b):
    mu = jnp.mean(x, axis=-1, keepdims=True)
    xc = x - mu
    var = jnp.mean(xc * xc, axis=-1, keepdims=True)
    return xc * lax.rsqrt(var + LN_EPS) * g + b


def _sigmoid(x):
    return 0.5 * jnp.tanh(0.5 * x) + 0.5


def _silu(x):
    return x * _sigmoid(x)


def _softplus(x):
    return jnp.maximum(x, 0.0) + jnp.log1p(jnp.exp(-jnp.abs(x)))


def _even_in_kernel(x_ref, g_ref, w_ref, lng_ref, lnb_ref, z_ref, xbc_ref, ug_ref, vn_ref, dt_ref,
                    *, d_ssd, conv_dim, d_gmlp):
    h = _rms(x_ref[...], g_ref[...]).astype(BF16)
    o1 = d_ssd
    o2 = o1 + conv_dim
    o3 = o2 + d_gmlp
    o4 = o3 + d_gmlp
    z_ref[...] = _dot(h, w_ref[:, 0:o1])
    xbc_ref[...] = _dot(h, w_ref[:, o1:o2])
    ug_ref[...] = jax.nn.gelu(_dot(h, w_ref[:, o2:o3])).astype(ug_ref.dtype)
    v = jax.nn.gelu(_dot(h, w_ref[:, o3:o4]))
    vn_ref[...] = _layernorm(v, lng_ref[...], lnb_ref[...])
    dt_ref[...] = _dot(h, w_ref[:, o4:o4 + V7X_LANES])


def _even_in(x, g, w, ln_g, ln_b, *, d_ssd, conv_dim, d_gmlp):
    t, d = x.shape
    tm = _row_tile(t, 512)
    row = lambda n: pl.BlockSpec((tm, n), lambda i: (i, 0))
    return pl.pallas_call(
        functools.partial(_even_in_kernel, d_ssd=d_ssd, conv_dim=conv_dim, d_gmlp=d_gmlp),
        grid=(t // tm,),
        in_specs=[row(d), _resident(g.shape), _resident(w.shape), _resident(ln_g.shape), _resident(ln_b.shape)],
        out_specs=[row(d_ssd), row(conv_dim), row(d_gmlp), row(d_gmlp), row(V7X_LANES)],
        out_shape=[jax.ShapeDtypeStruct((t, d_ssd), F32), jax.ShapeDtypeStruct((t, conv_dim), F32),
                   jax.ShapeDtypeStruct((t, d_gmlp), BF16), jax.ShapeDtypeStruct((t, d_gmlp), F32),
                   jax.ShapeDtypeStruct((t, V7X_LANES), F32)],
        compiler_params=_params("parallel"),
        name="even_in",
    )(x, g, w, ln_g, ln_b)


def _ssd_prompt_kernel(xbc_ref, z_ref, dt_ref, cw_ref, cb_ref, dtb_ref, alog_ref, dskip_ref, ng_ref,
                       yn_ref, st_ref, xp_ref, state_ref, xc_ref, *, d_ssd, n_taps):
    c = pl.program_id(1)

    @pl.when(c == 0)
    def _():
        state_ref[...] = jnp.zeros_like(state_ref)
        xp_ref[:, 0:V7X_SUBLANES, :] = jnp.zeros((xp_ref.shape[0], V7X_SUBLANES, xp_ref.shape[2]), F32)

    for s in range(xbc_ref.shape[1]):
        _ssd_chunk(xbc_ref.at[0, s], z_ref.at[0, s], dt_ref.at[0, s], cw_ref, cb_ref, dtb_ref, alog_ref, dskip_ref,
                   ng_ref, yn_ref.at[0, s], xp_ref.at[s], state_ref.at[s], xc_ref.at[s], d_ssd=d_ssd, n_taps=n_taps)

    @pl.when(c == pl.num_programs(1) - 1)
    def _():
        for s in range(state_ref.shape[0]):
            for pair in range(state_ref.shape[1]):
                st_ref[s, pair] = state_ref[s, pair].T


def _ssd_chunk(xbc_ref, z_ref, dt_ref, cw_ref, cb_ref, dtb_ref, alog_ref, dskip_ref, ng_ref,
               yn_ref, xp_ref, state_ref, xc_ref, *, d_ssd, n_taps):
    n_pairs = state_ref.shape[0]
    ch = SSD_CHUNK
    gn = SSD_GROUPS * SSD_STATE

    xp_ref[V7X_SUBLANES:V7X_SUBLANES + ch, :] = xbc_ref[...]
    base = V7X_SUBLANES - (n_taps - 1)
    acc = cb_ref[...] + cw_ref[n_taps - 1:n_taps, :] * xbc_ref[...]
    for k in range(n_taps - 1):
        acc = acc + cw_ref[k:k + 1, :] * xp_ref[base + k:base + k + ch, :]
    xp_ref[0:V7X_SUBLANES, :] = xp_ref[ch:ch + V7X_SUBLANES, :]
    xc_ref[...] = _silu(acc)
    xc = xc_ref

    dt = _softplus(dt_ref[...] + dtb_ref[...])
    a = dt * (-jnp.exp(alog_ref[...]))
    li = lax.broadcasted_iota(jnp.int32, (ch, ch), 0)
    si = lax.broadcasted_iota(jnp.int32, (ch, ch), 1)
    causal = li >= si
    tril = jnp.where(causal, 1.0, 0.0).astype(F32)
    acs = jnp.dot(tril, a, precision=HIGHEST, preferred_element_type=F32)
    acs_t = acs.T
    lane = lax.broadcasted_iota(jnp.int32, (ch, 2 * SSD_HEAD_DIM), 1)
    first = lane < SSD_HEAD_DIM
    first_n = lax.broadcasted_iota(jnp.int32, (SSD_STATE, 2 * SSD_HEAD_DIM), 1) < SSD_HEAD_DIM
    pairs_per_group = n_pairs // SSD_GROUPS

    for g in range(SSD_GROUPS):
        bg = xc[:, d_ssd + g * SSD_STATE:d_ssd + (g + 1) * SSD_STATE]
        cg = xc[:, d_ssd + gn + g * SSD_STATE:d_ssd + gn + (g + 1) * SSD_STATE]
        bg_t = bg.T
        cb = _dot(cg.astype(BF16), bg_t.astype(BF16))
        ys = []
        for q in range(pairs_per_group):
            pair = g * pairs_per_group + q
            h0 = 2 * pair
            lo = pair * 2 * SSD_HEAD_DIM
            xs = xc[:, lo:lo + 2 * SSD_HEAD_DIM]
            dt2 = jnp.where(first, dt[:, h0:h0 + 1], dt[:, h0 + 1:h0 + 2])
            xdt = (xs * dt2).astype(BF16)
            s_prev = state_ref[pair]
            s_prev_b = s_prev.astype(BF16)
            y2, snew2, dec2 = [], [], []
            for r in range(2):
                h = h0 + r
                col = acs[:, h:h + 1]
                row = acs_t[h:h + 1, :]
                last = acs_t[h:h + 1, ch - 1:ch]
                decay = jnp.exp(jnp.where(causal, col - row, -jnp.inf))
                y = _dot((cb * decay).astype(BF16), xdt)
                y = y + _dot((cg * jnp.exp(col)).astype(BF16), s_prev_b)
                snew = _dot((bg_t * jnp.exp(last - row)).astype(BF16), xdt)
                y2.append(y)
                snew2.append(snew)
                dec2.append(jnp.exp(last))
            y = jnp.where(first, y2[0], y2[1])
            state_ref[pair] = (s_prev * jnp.where(first_n, dec2[0], dec2[1])
                               + jnp.where(first_n, snew2[0], snew2[1]))
            ys.append(y + dskip_ref[:, lo:lo + 2 * SSD_HEAD_DIM] * xs)
        gw = pairs_per_group * 2 * SSD_HEAD_DIM
        yg = jnp.concatenate(ys, axis=-1) * _silu(z_ref[:, g * gw:(g + 1) * gw])
        yg = yg * lax.rsqrt(jnp.mean(yg * yg, axis=-1, keepdims=True) + RMS_EPS)
        yn_ref[:, g * gw:(g + 1) * gw] = (yg * ng_ref[:, g * gw:(g + 1) * gw]).astype(yn_ref.dtype)


def _ssd_prompt(xbc, z, dt, conv_w, conv_b, dt_bias, a_log, d_skip, norm_g, *, batch, seq):
    t, conv_dim = xbc.shape
    d_ssd = z.shape[1]
    n_pairs = d_ssd // (2 * SSD_HEAD_DIM)
    nc = seq // SSD_CHUNK
    n_taps = conv_w.shape[0]
    g = SSD_SEQS_PER_STEP if batch % SSD_SEQS_PER_STEP == 0 else 1
    tile = lambda n: pl.BlockSpec((1, g, SSD_CHUNK, n), lambda b, c: (b, 0, c, 0))
    by_seq = lambda a: a.reshape(batch // g, g, seq, a.shape[1])
    yn, st = pl.pallas_call(
        functools.partial(_ssd_prompt_kernel, d_ssd=d_ssd, n_taps=n_taps),
        grid=(batch // g, nc),
        in_specs=[tile(conv_dim), tile(d_ssd), tile(V7X_LANES), _resident(conv_w.shape), _resident(conv_b.shape),
                  _resident(dt_bias.shape), _resident(a_log.shape), _resident(d_skip.shape),
                  _resident(norm_g.shape)],
        out_specs=[tile(d_ssd),
                   pl.BlockSpec((g, n_pairs, 2 * SSD_HEAD_DIM, SSD_STATE), lambda b, c: (b, 0, 0, 0))],
        out_shape=[jax.ShapeDtypeStruct((batch // g, g, seq, d_ssd), BF16),
                   jax.ShapeDtypeStruct((batch, n_pairs, 2 * SSD_HEAD_DIM, SSD_STATE), F32)],
        scratch_shapes=[pltpu.VMEM((g, SSD_CHUNK + V7X_SUBLANES, conv_dim), F32),
                        pltpu.VMEM((g, n_pairs, SSD_STATE, 2 * SSD_HEAD_DIM), F32),
                        pltpu.VMEM((g, SSD_CHUNK, conv_dim), F32)],
        compiler_params=_params("parallel", "arbitrary"),
        name="ssd_prompt",
    )(by_seq(xbc), by_seq(z), by_seq(dt), conv_w, conv_b, dt_bias, a_log, d_skip, norm_g)
    return yn.reshape(t, d_ssd), st


def _gmlp_gate(ug_ref, vn_ref, ws_ref, bs_ref, out_ref):
    ch = GMLP_CHUNK
    n_groups = ws_ref.shape[0]
    gd = vn_ref.shape[1] // n_groups
    ii = lax.broadcasted_iota(jnp.int32, (ch, ch), 0)
    jj = lax.broadcasted_iota(jnp.int32, (ch, ch), 1)
    for g in range(n_groups):
        ws = jnp.where(ii >= jj, ws_ref[g], 0.0).astype(BF16)
        cols = slice(g * gd, (g + 1) * gd)
        for r0 in range(0, vn_ref.shape[0], ch):
            rows = slice(r0, r0 + ch)
            mixed = _dot(ws, vn_ref[rows, cols].astype(BF16)) + bs_ref[:, g:g + 1]
            out_ref[rows, cols] = (ug_ref[rows, cols].astype(F32) * mixed).astype(out_ref.dtype)


def _ssd_sample_prep_kernel(xbc_ref, buf_ref, dt_ref, cw_ref, cb_ref, dtb_ref, alog_ref, expand_ref,
                            newbuf_ref, xs_ref, xdt_ref, bc_ref, da_ref, *, d_ssd, n_taps):
    conv_dim = xbc_ref.shape[1]
    x_new = xbc_ref[...]
    acc = cb_ref[...] + cw_ref[n_taps - 1:n_taps, :] * x_new
    for k in range(n_taps - 1):
        acc = acc + cw_ref[k:k + 1, :] * buf_ref[:, k * conv_dim:(k + 1) * conv_dim]
    for k in range(n_taps - 2):
        newbuf_ref[:, k * conv_dim:(k + 1) * conv_dim] = buf_ref[:, (k + 1) * conv_dim:(k + 2) * conv_dim]
    newbuf_ref[:, (n_taps - 2) * conv_dim:(n_taps - 1) * conv_dim] = x_new
    xc = _silu(acc)
    xs = xc[:, :d_ssd]
    dt = _softplus(dt_ref[...] + dtb_ref[...])
    da_ref[...] = jnp.exp(dt * (-jnp.exp(alog_ref[...])))
    dt_wide = jnp.dot(dt, expand_ref[...], precision=HIGHEST, preferred_element_type=F32)
    xs_ref[...] = xs
    xdt_ref[...] = xs * dt_wide
    bc_ref[...] = xc[:, d_ssd:]


def _ssd_sample_prep(xbc, buf, dt, conv_w, conv_b, dt_bias, a_log, expand):
    n, conv_dim = xbc.shape
    d_ssd = expand.shape[1]
    n_taps = conv_w.shape[0]
    args = (xbc, buf, dt, conv_w, conv_b, dt_bias, a_log, expand)
    return pl.pallas_call(
        functools.partial(_ssd_sample_prep_kernel, d_ssd=d_ssd, n_taps=n_taps),
        grid=(1,),
        in_specs=[_resident(a.shape) for a in args],
        out_specs=[_whole(buf.shape), _whole((n, d_ssd)), _whole((n, d_ssd)),
                   _whole((n, conv_dim - d_ssd)), _whole((n, V7X_LANES))],
        out_shape=[jax.ShapeDtypeStruct(buf.shape, F32), jax.ShapeDtypeStruct((n, d_ssd), F32),
                   jax.ShapeDtypeStruct((n, d_ssd), F32), jax.ShapeDtypeStruct((n, conv_dim - d_ssd), F32),
                   jax.ShapeDtypeStruct((n, V7X_LANES), F32)],
        compiler_params=_params("arbitrary"),
        name="ssd_sample_prep",
    )(*args)


def _ssd_sample_state_kernel(h_ref, xdt_ref, da_ref, bc_ref, hnew_ref, y_ref, xcol_ref, ccol_ref):
    bs, n_heads, p, s = h_ref.shape
    n_pairs = n_heads // 2
    pairs_per_group = n_pairs // SSD_GROUPS
    gn = SSD_GROUPS * s
    w = 2 * p
    eye = lax.broadcasted_iota(jnp.int32, (w, w), 0) == lax.broadcasted_iota(jnp.int32, (w, w), 1)
    ones = jnp.ones((w, V7X_LANES), BF16)

    def stacked_diag(v):
        return jnp.concatenate(
            [jnp.where(eye, jnp.broadcast_to(v[j:j + 1], (w, w)), 0.0) for j in range(bs)], axis=0).astype(BF16)

    for q in range(n_pairs):
        x = xdt_ref[:, q * w:(q + 1) * w]
        hi = x.astype(BF16).astype(F32)
        mid = (x - hi).astype(BF16).astype(F32)
        lo = (x - hi) - mid
        xcol_ref[q] = _dot(stacked_diag(hi), ones) + (_dot(stacked_diag(mid), ones) + _dot(stacked_diag(lo), ones))
    for g in range(SSD_GROUPS):
        ccol_ref[g] = _dot(stacked_diag(bc_ref[:, gn + g * s:gn + (g + 1) * s]), ones).astype(BF16)

    for j in range(bs):
        for g in range(SSD_GROUPS):
            brow = bc_ref[j:j + 1, g * s:(g + 1) * s]
            h_new = []
            for q in range(g * pairs_per_group, (g + 1) * pairs_per_group):
                for r in range(2):
                    h = 2 * q + r
                    xcol = xcol_ref[q, j * w + r * p:j * w + (r + 1) * p, :]
                    h_new.append(h_ref[j, h] * da_ref[j:j + 1, h:h + 1] + xcol * brow)
                    hnew_ref[j, h] = h_new[-1]
            hc = _dot(jnp.concatenate(h_new, axis=0).astype(BF16), ccol_ref[g, j * s:(j + 1) * s, :])
            for i in range(pairs_per_group):
                y = jnp.sum(jnp.where(eye, hc[i * w:(i + 1) * w], 0.0), axis=0, keepdims=True)
                q = g * pairs_per_group + i
                y_ref[j, q:q + 1, :] = y


def _ssd_sample_state(h, xdt, da, bc):
    n, n_heads, p, s = h.shape
    assert 2 * p == V7X_LANES and s == V7X_LANES
    bs = SAMPLE_STATE_TILE
    rows = lambda w: pl.BlockSpec((bs, w), lambda i: (i, 0))
    return pl.pallas_call(
        _ssd_sample_state_kernel,
        grid=(n // bs,),
        in_specs=[pl.BlockSpec((bs, n_heads, p, s), lambda i: (i, 0, 0, 0)), rows(xdt.shape[1]), rows(da.shape[1]),
                  rows(bc.shape[1])],
        out_specs=[pl.BlockSpec((bs, n_heads, p, s), lambda i: (i, 0, 0, 0)),
                   pl.BlockSpec((bs, n_heads // 2, 2 * p), lambda i: (i, 0, 0))],
        out_shape=[jax.ShapeDtypeStruct(h.shape, F32), jax.ShapeDtypeStruct((n, n_heads // 2, 2 * p), F32)],
        scratch_shapes=[pltpu.VMEM((n_heads // 2, bs * 2 * p, V7X_LANES), F32),
                        pltpu.VMEM((SSD_GROUPS, bs * s, V7X_LANES), BF16)],
        compiler_params=_params("parallel"),
        name="ssd_sample_state",
    )(h, xdt, da, bc)


def _even_sample_post_kernel(y_ref, xs_ref, z_ref, dskip_ref, ng_ref, ug_ref, vn_ref, w0_ref, b0_ref,
                             yn_ref, gout_ref):
    d_ssd = y_ref.shape[1]
    gw = d_ssd // SSD_GROUPS
    y = (y_ref[...] + dskip_ref[...] * xs_ref[...]) * _silu(z_ref[...])
    for g in range(SSD_GROUPS):
        yg = y[:, g * gw:(g + 1) * gw]
        yg = yg * lax.rsqrt(jnp.mean(yg * yg, axis=-1, keepdims=True) + RMS_EPS)
        yn_ref[:, g * gw:(g + 1) * gw] = (yg * ng_ref[:, g * gw:(g + 1) * gw]).astype(yn_ref.dtype)
    mixed = w0_ref[...] * vn_ref[...] + b0_ref[...]
    gout_ref[...] = (ug_ref[...].astype(F32) * mixed).astype(gout_ref.dtype)


def _even_sample_post(y, xs, z, d_skip, norm_g, ug, vn, w0, b0):
    args = (y, xs, z, d_skip, norm_g, ug, vn, w0, b0)
    return pl.pallas_call(
        _even_sample_post_kernel,
        grid=(1,),
        in_specs=[_resident(a.shape) for a in args],
        out_specs=[_whole(y.shape), _whole(vn.shape)],
        out_shape=[jax.ShapeDtypeStruct(y.shape, BF16), jax.ShapeDtypeStruct(vn.shape, BF16)],
        compiler_params=_params("arbitrary"),
        name="even_sample_post",
    )(*args)


def _even_out_kernel(*refs, fuse_gmlp):
    if fuse_gmlp:
        (x_ref, yn_ref, ug_ref, vn_ref, ws_ref, bs_ref, wo_ref, g_ref, wg_ref, wu_ref, wd_ref, gc_ref, w1_ref,
         b1_ref, out_ref, a_ref, gout_ref) = refs
        _gmlp_gate(ug_ref, vn_ref, ws_ref, bs_ref, gout_ref)
    else:
        (x_ref, yn_ref, gout_ref, wo_ref, g_ref, wg_ref, wu_ref, wd_ref, gc_ref, w1_ref, b1_ref,
         out_ref, a_ref) = refs
    d_ssd = yn_ref.shape[1]
    x = x_ref[...] + _dot(yn_ref[...], wo_ref[0:d_ssd, :]) + _dot(gout_ref[...], wo_ref[d_ssd:, :])
    h = _rms(x, g_ref[...]).astype(BF16)
    mid = (_silu(_dot(h, wg_ref[...])) * _dot(h, wu_ref[...])).astype(BF16)
    x = x + _dot(mid, wd_ref[...])
    out_ref[...] = x
    dc = a_ref.shape[1]
    h = _rms(x, gc_ref[...]).astype(BF16)
    val = _dot(h, w1_ref[:, 0:dc]) + b1_ref[:, 0:dc]
    gate = _dot(h, w1_ref[:, dc:]) + b1_ref[:, dc:]
    a_ref[...] = val * _sigmoid(gate)


def _even_out(x, yn, gate_in, weights, *, seq=None):
    t, d = x.shape
    dc = weights[-2].shape[1] // 2
    fuse_gmlp = isinstance(gate_in, tuple)
    tm = _row_tile(t if seq is None else seq, 256)
    row = lambda n: pl.BlockSpec((tm, n), lambda i: (i, 0))
    if fuse_gmlp:
        ug, vn, w_s, b_s_t = gate_in
        assert tm % GMLP_CHUNK == 0
        gate_args = (ug, vn, w_s, b_s_t)
        gate_specs = [row(ug.shape[1]), row(vn.shape[1]), _resident(w_s.shape), _resident(b_s_t.shape)]
        scratch = [pltpu.VMEM((tm, ug.shape[1]), BF16)]
    else:
        gate_args, gate_specs, scratch = (gate_in,), [row(gate_in.shape[1])], []
    return pl.pallas_call(
        functools.partial(_even_out_kernel, fuse_gmlp=fuse_gmlp),
        grid=(t // tm,),
        in_specs=[row(d), row(yn.shape[1])] + gate_specs + [_resident(w.shape) for w in weights],
        out_specs=[row(d), row(dc)],
        out_shape=[jax.ShapeDtypeStruct((t, d), F32), jax.ShapeDtypeStruct((t, dc), F32)],
        scratch_shapes=scratch,
        compiler_params=_params("parallel"),
        name="even_out_ffn",
    )(x, yn, *gate_args, *weights)


def _route(logits, n_experts, route_ref, route_t_ref):
    lane = lax.broadcasted_iota(jnp.int32, logits.shape, 1)
    lane_f = lane.astype(F32)
    big = jnp.float32(V7X_LANES)
    lg = jnp.where(lane < n_experts, logits, -jnp.inf)
    m1 = jnp.max(lg, axis=-1, keepdims=True)
    i1 = jnp.min(jnp.where(lg == m1, lane_f, big), axis=-1, keepdims=True)
    lg2 = jnp.where(lane_f == i1, -jnp.inf, lg)
    m2 = jnp.max(lg2, axis=-1, keepdims=True)
    i2 = jnp.min(jnp.where(lg2 == m2, lane_f, big), axis=-1, keepdims=True)
    e2 = jnp.exp(m2 - m1)
    g1 = 1.0 / (1.0 + e2)
    g2 = e2 / (1.0 + e2)
    route = jnp.where(lane == 0, i1, jnp.where(lane == 1, i2, jnp.where(lane == 2, g1,
                      jnp.where(lane == 3, g2, 0.0))))
    route_ref[...] = route
    route_t_ref[...] = route.T[0:V7X_SUBLANES, :]


def _conf_finish(c, x, lng_ref, lnb_ref, w2_ref, b2_ref, ng_ref, rhi_ref, rlo_ref, out_ref, hn_ref, route_ref,
                 route_t_ref, *, n_experts):
    hmid = _silu(_layernorm(c, lng_ref[...], lnb_ref[...])).astype(BF16)
    x_new = x + _dot(hmid, w2_ref[...]) + b2_ref[...]
    out_ref[...] = x_new
    hn = _rms(x_new, ng_ref[...])
    hn_ref[...] = _pack_bf16_pairs(hn)
    hi = hn.astype(BF16)
    lo = (hn - hi.astype(F32)).astype(BF16)
    logits = _dot(hi, rhi_ref[...]) + (_dot(lo, rhi_ref[...]) + _dot(hi, rlo_ref[...]))
    _route(logits, n_experts, route_ref, route_t_ref)


def _conf_prompt_kernel(a_ref, x_ref, xs_ref, hns_ref, routes_ref, dw_ref, db_ref, lng_ref, lnb_ref, w2_ref, b2_ref,
                        ng_ref, rhi_ref, rlo_ref, out_ref, hn_ref, route_ref, route_t_ref, xp_ref, c_ref, win_ref,
                        *, n_taps, n_experts, tiles_per_seq, n_prompt_tiles):
    tl, d = a_ref.shape
    halo = xp_ref.shape[0] - tl
    base = halo - (n_taps - 1)
    rows = CONF_ACC_ROWS
    step = pl.program_id(0)

    @pl.when(step < n_prompt_tiles)
    def _():
        @pl.when(step % tiles_per_seq == 0)
        def _():
            xp_ref[0:halo, :] = jnp.zeros((halo, d), F32)

        xp_ref[halo:halo + tl, :] = a_ref[...]

        def lane_block(j, carry):
            cols = pl.ds(pl.multiple_of(j * V7X_LANES, V7X_LANES), V7X_LANES)
            for r0 in range(0, tl, rows):
                acc = jnp.broadcast_to(db_ref[:, cols], (rows, V7X_LANES))
                for phase in range(V7X_SUBLANES):
                    taps = [k for k in range(n_taps) if (base + k) % V7X_SUBLANES == phase]
                    if not taps:
                        continue
                    span = max(base + k - phase for k in taps) + rows
                    win_ref[0:span, :] = xp_ref[pl.ds(r0 + phase, span), cols]
                    for k in taps:
                        o = base + k - phase
                        acc = acc + dw_ref[k:k + 1, cols] * win_ref[o:o + rows, :]
                c_ref[r0:r0 + rows, cols] = acc
            return carry

        lax.fori_loop(0, d // V7X_LANES, lane_block, 0)
        xp_ref[0:halo, :] = xp_ref[tl:tl + halo, :]
        _conf_finish(c_ref[...], x_ref[...], lng_ref, lnb_ref, w2_ref, b2_ref, ng_ref, rhi_ref, rlo_ref,
                     out_ref, hn_ref, route_ref, route_t_ref, n_experts=n_experts)

    @pl.when(step == n_prompt_tiles)
    def _():
        ns = xs_ref.shape[0]
        out_ref[0:ns, :] = xs_ref[...]
        hn_ref[0:ns, :] = hns_ref[...]
        route_ref[0:ns, :] = routes_ref[...]


def _conf_prompt(a, x, sample_rows, consts, *, seq, n_experts):
    t, d = a.shape
    n_s = sample_rows[0].shape[0]
    tl = _row_tile(seq, CONF_TILE)
    assert tl % CONF_ACC_ROWS == 0 and n_s <= tl
    n_prompt_tiles = t // tl
    n_taps = consts[0].shape[0]
    halo = -(-(n_taps - 1) // V7X_SUBLANES) * V7X_SUBLANES
    in_tile = lambda n: pl.BlockSpec((tl, n), lambda s: (jnp.minimum(s, n_prompt_tiles - 1), 0))
    out_tile = lambda n: pl.BlockSpec((tl, n), lambda s: (s, 0))
    small = sample_rows + tuple(consts)
    return pl.pallas_call(
        functools.partial(_conf_prompt_kernel, n_taps=n_taps, n_experts=n_experts, tiles_per_seq=seq // tl,
                          n_prompt_tiles=n_prompt_tiles),
        grid=(n_prompt_tiles + 1,),
        in_specs=[in_tile(d), in_tile(d)] + [_resident(c.shape) for c in small],
        out_specs=[out_tile(d), out_tile(d // 2), out_tile(V7X_LANES),
                   pl.BlockSpec((V7X_SUBLANES, tl), lambda s: (0, jnp.minimum(s, n_prompt_tiles - 1)))],
        out_shape=[jax.ShapeDtypeStruct((t + n_s, d), F32), jax.ShapeDtypeStruct((t + n_s, d // 2), jnp.uint32),
                   jax.ShapeDtypeStruct((t + n_s, V7X_LANES), F32), jax.ShapeDtypeStruct((V7X_SUBLANES, t), F32)],
        scratch_shapes=[pltpu.VMEM((tl + halo, d), F32), pltpu.VMEM((tl, d), F32),
                        pltpu.VMEM((CONF_ACC_ROWS + halo, V7X_LANES), F32)],
        compiler_params=_params("arbitrary"),
        name="conf_prompt",
    )(a, x, *small)


def _conf_sample_kernel(a_ref, buf_ref, x_ref, dw_ref, db_ref, lng_ref,
                        lnb_ref, w2_ref, b2_ref, ng_ref, rhi_ref, rlo_ref, newbuf_ref, out_ref, hn_ref, route_ref,
                        route_t_ref, *, n_taps, n_experts):
    d = a_ref.shape[1]
    a_new = a_ref[...]
    acc = db_ref[...] + dw_ref[n_taps - 1:n_taps, :] * a_new
    for k in range(n_taps - 1):
        acc = acc + dw_ref[k:k + 1, :] * buf_ref[:, k * d:(k + 1) * d]
    for k in range(n_taps - 2):
        newbuf_ref[:, k * d:(k + 1) * d] = buf_ref[:, (k + 1) * d:(k + 2) * d]
    newbuf_ref[:, (n_taps - 2) * d:(n_taps - 1) * d] = a_new
    _conf_finish(acc, x_ref[...], lng_ref, lnb_ref, w2_ref, b2_ref, ng_ref, rhi_ref, rlo_ref,
                 out_ref, hn_ref, route_ref, route_t_ref, n_experts=n_experts)


def _conf_sample(a, buf, x, consts, *, n_experts):
    n, d = a.shape
    args = (a, buf, x) + tuple(consts)
    return pl.pallas_call(
        functools.partial(_conf_sample_kernel, n_taps=consts[0].shape[0], n_experts=n_experts),
        grid=(1,),
        in_specs=[_resident(v.shape) for v in args],
        out_specs=[_whole(buf.shape), _whole((n, d)), _whole((n, d // 2)), _whole((n, V7X_LANES)),
                   _whole((V7X_SUBLANES, n))],
        out_shape=[jax.ShapeDtypeStruct(buf.shape, F32), jax.ShapeDtypeStruct((n, d), F32),
                   jax.ShapeDtypeStruct((n, d // 2), jnp.uint32), jax.ShapeDtypeStruct((n, V7X_LANES), F32),
                   jax.ShapeDtypeStruct((V7X_SUBLANES, n), F32)],
        compiler_params=_params("arbitrary"),
        name="conf_sample",
    )(*args)


def _sc_window_rows(x):
    assert x.dtype.itemsize == 4, "the SparseCore indirect copies move 32-bit elements"
    return SC_WINDOW_BYTES // (x.shape[1] * x.dtype.itemsize)


def _sc_gather_rows(x, idx):
    n, d = idx.shape[0], x.shape[1]
    w = _sc_window_rows(x)
    assert n % w == 0
    mesh = plsc.VectorSubcoreMesh(core_axis_name="core", subcore_axis_name="subcore")

    @functools.partial(pl.kernel, out_type=jax.ShapeDtypeStruct((n, d), x.dtype), mesh=mesh, scratch_types=[])
    def gather(x_hbm, idx_hbm, out_hbm):
        def body(idx_vmem, out_vmem):
            pltpu.sync_copy(x_hbm.at[idx_vmem.at[0]], out_vmem)

        pltpu.emit_pipeline(
            body,
            grid=(n // w,),
            in_specs=[pl.BlockSpec((1, w), lambda i: (i, 0))],
            out_specs=[pl.BlockSpec((w, d), lambda i: (i, 0))],
            core_axis_name=("core", "subcore"),
            dimension_semantics=(pltpu.PARALLEL,),
        )(idx_hbm, out_hbm)

    return gather(x, idx.reshape(n // w, w))


def _sc_dispatch_rows(x, pos, pad_pos, n_out):
    t, d = x.shape
    w = _sc_window_rows(x)
    n_pad = pad_pos.shape[0]
    assert t % w == 0 and n_pad % w == 0 and TOP_K * t + n_pad == n_out
    mesh = plsc.VectorSubcoreMesh(core_axis_name="core", subcore_axis_name="subcore")

    @functools.partial(pl.kernel, out_type=jax.ShapeDtypeStruct((n_out, d), x.dtype), mesh=mesh, scratch_types=[])
    def dispatch(x_hbm, pos_hbm, pad_hbm, zero_hbm, out_hbm):
        def body(rows_vmem, idx_vmem):
            pltpu.sync_copy(rows_vmem, out_hbm.at[idx_vmem.at[0]])

        def scatter(src_hbm, idx_hbm, src_map):
            pltpu.emit_pipeline(
                body,
                grid=(idx_hbm.shape[0],),
                in_specs=[pl.BlockSpec((w, d), src_map), pl.BlockSpec((1, w), lambda i: (i, 0))],
                out_specs=[],
                core_axis_name=("core", "subcore"),
                dimension_semantics=(pltpu.PARALLEL,),
            )(src_hbm, idx_hbm)

        for k in range(TOP_K):
            scatter(x_hbm, pos_hbm.at[k], lambda i: (i, 0))
        scatter(zero_hbm, pad_hbm, lambda i: (0, 0))

    return dispatch(x, pos.reshape(TOP_K, t // w, w), pad_pos.reshape(n_pad // w, w), jnp.zeros((w, d), x.dtype))


def _moe_ffn_kernel(tile_expert_ref, first_ref, bank_ref, next_expert_ref, n_tiles_ref, x_ref, wg_hbm, wu_hbm,
                    wd_hbm, out_ref, wg_v, wu_v, wd_v, stage_g, stage_u, stage_d, acc_ref, sems):
    i = pl.program_id(0)
    dff = wg_v.shape[2]
    ck = stage_g.shape[2]
    n_chunks = dff // ck
    active = i < n_tiles_ref[0]
    starts_run = active & (first_ref[i] == 1)
    bank = bank_ref[i]
    nxt = next_expert_ref[i]

    def chunk_copies(expert, c, slot):
        cols = slice(c * ck, (c + 1) * ck)
        return (pltpu.make_async_copy(wg_hbm.at[expert, :, cols], stage_g.at[slot], sems.at[0, slot]),
                pltpu.make_async_copy(wu_hbm.at[expert, :, cols], stage_u.at[slot], sems.at[1, slot]),
                pltpu.make_async_copy(wd_hbm.at[expert, cols, :], stage_d.at[slot], sems.at[2, slot]))

    def stream_weights(expert, dst_bank, between_chunks):
        for cp in chunk_copies(expert, 0, 0):
            cp.start()
        for c in range(n_chunks):
            slot = c % 2
            if c + 1 < n_chunks:
                for cp in chunk_copies(expert, c + 1, 1 - slot):
                    cp.start()
            for cp in chunk_copies(expert, c, slot):
                cp.wait()
            cols = slice(c * ck, (c + 1) * ck)
            wg_v[dst_bank, :, cols] = stage_g[slot].astype(BF16)
            wu_v[dst_bank, :, cols] = stage_u[slot].astype(BF16)
            wd_v[dst_bank, cols, :] = stage_d[slot].astype(BF16)
            between_chunks(c)

    def ffn_part(xb, cols):
        mid = (_silu(_dot(xb, wg_v[bank, :, cols])) * _dot(xb, wu_v[bank, :, cols])).astype(BF16)
        return _dot(mid, wd_v[bank, cols, :])

    @pl.when(starts_run & (i == 0))
    def _():
        stream_weights(tile_expert_ref[i], bank, lambda c: None)

    overlapped = starts_run & (nxt >= 0)

    @pl.when(overlapped)
    def _():
        xb = _unpack_bf16_pairs(x_ref[...]).astype(BF16)
        acc_ref[...] = jnp.zeros_like(acc_ref)

        def piece(c):
            acc_ref[...] += ffn_part(xb, slice(c * ck, (c + 1) * ck))

        stream_weights(nxt, 1 - bank, piece)
        out_ref[...] = _pack_bf16_pairs(acc_ref[...])

    @pl.when(active & jnp.logical_not(overlapped))
    def _():
        xb = _unpack_bf16_pairs(x_ref[...]).astype(BF16)
        half = dff // 2
        acc = ffn_part(xb, slice(0, half)) + ffn_part(xb, slice(half, dff))
        out_ref[...] = _pack_bf16_pairs(acc)

    @pl.when(jnp.logical_not(active))
    def _():
        out_ref[...] = jnp.zeros_like(out_ref)


def _moe_ffn(tile_expert, n_tiles, xs, wg, wu, wd, *, tm):
    p, d = xs.shape[0], wg.shape[1]
    dff = wg.shape[2]
    ck = MOE_FF_CHUNK
    assert dff % ck == 0
    nt = p // tm
    i32 = jnp.int32
    tile = jnp.arange(nt, dtype=i32)
    first = jnp.concatenate([jnp.ones((1,), i32), (tile_expert[1:] != tile_expert[:-1]).astype(i32)])
    bank = (jnp.cumsum(first) - 1) % 2
    run_start = jnp.where((first == 1) & (tile < n_tiles[0]), tile, nt)
    next_start = jnp.concatenate([lax.cummin(run_start, reverse=True)[1:], jnp.full((1,), nt, i32)])
    next_expert = jnp.sum((next_start[:, None] == tile[None, :]).astype(i32) * tile_expert[None, :], axis=1)
    next_expert = jnp.where(next_start < nt, next_expert, -1)
    any_spec = pl.BlockSpec(memory_space=pl.ANY)
    row_tile = pl.BlockSpec((tm, d // 2), lambda i, *_: (i, 0))
    grid_spec = pltpu.PrefetchScalarGridSpec(
        num_scalar_prefetch=5,
        grid=(nt,),
        in_specs=[row_tile, any_spec, any_spec, any_spec],
        out_specs=row_tile,
        scratch_shapes=[pltpu.VMEM((2, d, dff), BF16), pltpu.VMEM((2, d, dff), BF16), pltpu.VMEM((2, dff, d), BF16),
                        pltpu.VMEM((2, d, ck), F32), pltpu.VMEM((2, d, ck), F32), pltpu.VMEM((2, ck, d), F32),
                        pltpu.VMEM((tm, d), F32), pltpu.SemaphoreType.DMA((3, 2))],
    )
    return pl.pallas_call(
        _moe_ffn_kernel,
        grid_spec=grid_spec,
        out_shape=jax.ShapeDtypeStruct((p, d // 2), jnp.uint32),
        compiler_params=_params("arbitrary"),
        name="moe_ffn",
    )(tile_expert, first, bank.astype(i32), next_expert.astype(i32), n_tiles, xs, wg, wu, wd)


def _moe_combine_kernel(x_ref, route_ref, gf_ref, a_ref, b_ref, y_ref):
    route = route_ref[...]
    x = (x_ref[...] + route[:, 2:3] * _unpack_bf16_pairs(a_ref[...])
         + route[:, 3:4] * _unpack_bf16_pairs(b_ref[...]))
    y_ref[...] = _rms(x, gf_ref[...])


def _moe_combine(x, route, gf, picked, *, row0, n_rows, a_row0, b_row0):
    d = x.shape[1]
    tc = _row_tile(n_rows, COMBINE_TILE)
    assert row0 % tc == 0 and a_row0 % tc == 0 and b_row0 % tc == 0
    rows = lambda w, r0: pl.BlockSpec((tc, w), lambda i: (i + r0 // tc, 0))
    return pl.pallas_call(
        _moe_combine_kernel,
        grid=(n_rows // tc,),
        in_specs=[rows(d, row0), rows(V7X_LANES, row0), _resident(gf.shape), rows(d // 2, a_row0),
                  rows(d // 2, b_row0)],
        out_specs=pl.BlockSpec((tc, d), lambda i: (i, 0)),
        out_shape=jax.ShapeDtypeStruct((n_rows, d), F32),
        compiler_params=_params("parallel"),
        name="moe_combine",
    )(x, route, gf, picked, picked)


def _moe_plan(route_t, n_experts, tm):
    t = route_t.shape[1]
    n_slots = t * TOP_K
    i32 = jnp.int32
    expert = route_t[:TOP_K].astype(i32).reshape(1, n_slots)
    onehot = (expert == jnp.arange(n_experts, dtype=i32)[:, None]).astype(i32)
    counts = jnp.sum(onehot, axis=1)
    tiles = (counts + tm - 1) // tm
    tile_end = jnp.cumsum(tiles)
    start = (tile_end - tiles) * tm
    rank = jnp.cumsum(onehot, axis=1) - onehot
    pos = jnp.sum(onehot * (rank + start[:, None]), axis=0)
    nt = -(-n_slots // tm) + n_experts
    n_tiles = tile_end[-1]
    tile_id = jnp.minimum(jnp.arange(nt, dtype=i32), n_tiles - 1)
    tile_expert = jnp.sum((tile_id[:, None] >= tile_end[None, :]).astype(i32), axis=1)
    seg_start = jnp.concatenate([start + counts, (n_tiles * tm)[None]])
    seg_len = jnp.concatenate([tiles * tm - counts, (nt * tm - n_tiles * tm)[None]])
    seg_end = jnp.cumsum(seg_len)
    q = jnp.arange(nt * tm - n_slots, dtype=i32)
    in_seg = (q[:, None] >= (seg_end - seg_len)[None, :]) & (q[:, None] < seg_end[None, :])
    pad_pos = jnp.sum(in_seg.astype(i32) * (seg_start - (seg_end - seg_len))[None, :], axis=1) + q
    return (tile_expert.astype(i32), n_tiles.astype(i32).reshape(1), pos.reshape(TOP_K, t).astype(i32),
            pad_pos.astype(i32), nt * tm)


def _row(v):
    return v.reshape(1, -1).astype(F32)


def _pad_lanes(v):
    return jnp.pad(_row(v), ((0, 0), (0, V7X_LANES - v.shape[-1])))


def kernel(x_prompt, x_sample, state_ssm, state_ssd_conv, state_conf_conv, norm_mix_even, w_in_even, ssd_conv_w, ssd_conv_b, ssd_dt_bias, ssd_a_log, ssd_d, ssd_norm, gmlp_ln_g, gmlp_ln_b, gmlp_w_s, gmlp_b_s, w_out_even, norm_ffn_even, ffn_w_gate, ffn_w_up, ffn_w_down, norm_mix_odd, conf_w1, conf_b1, conf_dw_w, conf_dw_b, conf_ln_g, conf_ln_b, conf_w2, conf_b2, norm_ffn_odd, moe_router, moe_w_gate, moe_w_up, moe_w_down, final_norm):
    batch, seq, d_model = x_prompt.shape
    n_dec, dec_seq, _ = x_sample.shape
    assert dec_seq == 1, "the sample group advances one token per sequence"
    assert seq % SSD_CHUNK == 0 and n_dec % V7X_LANES == 0
    n_even, n_odd = w_in_even.shape[0], conf_w1.shape[0]
    assert (n_even, n_odd) == (1, 1), "the final norm is fused into the last (odd) layer's MoE combine"
    n_heads = ssd_dt_bias.shape[1]
    d_ssd = n_heads * SSD_HEAD_DIM
    conv_dim = ssd_conv_w.shape[2]
    d_gmlp = gmlp_ln_g.shape[1]
    n_groups_gmlp = gmlp_w_s.shape[1]
    n_experts = moe_router.shape[2]
    tp = batch * seq

    xp = x_prompt.reshape(tp, d_model)
    xs = x_sample.reshape(n_dec, d_model)
    outs = dict(ssm_p=[], conv_p=[], conf_p=[], ssm_s=[], conv_s=[], conf_s=[], v_s=[])

    for layer in range(n_even + n_odd):
        i = layer // 2
        if layer % 2 == 0:
            w = w_in_even[i]
            o1, o2, o3 = d_ssd, d_ssd + conv_dim, d_ssd + conv_dim + n_heads
            w_r = jnp.concatenate([w[:, :o2], w[:, o3:], jnp.pad(w[:, o2:o3], ((0, 0), (0, V7X_LANES - n_heads)))],
                                  axis=1).astype(BF16)
            in_args = (_row(norm_mix_even[i]), w_r, _row(gmlp_ln_g[i]), _row(gmlp_ln_b[i]))
            dims = dict(d_ssd=d_ssd, conv_dim=conv_dim, d_gmlp=d_gmlp)
            dt_bias, a_log = _pad_lanes(ssd_dt_bias[i]), _pad_lanes(ssd_a_log[i])
            d_skip = _row(jnp.repeat(ssd_d[i], SSD_HEAD_DIM))
            norm_g = _row(ssd_norm[i])
            conv_w, conv_b = ssd_conv_w[i], _row(ssd_conv_b[i])
            z, xbc, ug, vn, dt = _even_in(xp, *in_args, **dims)
            yn, st = _ssd_prompt(xbc, z, dt, conv_w, conv_b, dt_bias, a_log, d_skip, norm_g, batch=batch, seq=seq)
            n_pairs = n_heads // 2
            outs['ssm_p'].append(st.reshape(batch, n_heads, SSD_HEAD_DIM, SSD_STATE))
            outs['conv_p'].append(xbc.reshape(batch, seq, conv_dim)[:, seq - (conv_w.shape[0] - 1):])
            tail_w = (w_out_even[i].astype(BF16), _row(norm_ffn_even[i]), ffn_w_gate[i].astype(BF16),
                      ffn_w_up[i].astype(BF16), ffn_w_down[i].astype(BF16), _row(norm_mix_odd[i]),
                      conf_w1[i].astype(BF16), _row(conf_b1[i]))
            xp, a_p = _even_out(xp, yn, (ug, vn, gmlp_w_s[i], gmlp_b_s[i].T), tail_w, seq=seq)
            z, xbc, ug, vn, dt = _even_in(xs, *in_args, **dims)
            expand = (jnp.arange(d_ssd)[None, :] // SSD_HEAD_DIM == jnp.arange(V7X_LANES)[:, None]).astype(F32)
            buf = state_ssd_conv[i].reshape(n_dec, -1)
            newbuf, xc, xdt, bc, da = _ssd_sample_prep(xbc, buf, dt, conv_w, conv_b, dt_bias, a_log, expand)
            h_new, y4 = _ssd_sample_state(state_ssm[i], xdt, da, bc)
            gd = d_gmlp // n_groups_gmlp
            w0 = _row(jnp.repeat(gmlp_w_s[i][:, 0, 0], gd))
            b0 = _row(jnp.repeat(gmlp_b_s[i][:, 0], gd))
            yn, gout = _even_sample_post(y4.reshape(n_dec, d_ssd), xc, z, d_skip, norm_g, ug, vn, w0, b0)
            outs['ssm_s'].append(h_new)
            outs['conv_s'].append(newbuf.reshape(state_ssd_conv[i].shape))
            outs['v_s'].append(vn.reshape(n_dec, 1, d_gmlp))
            xs, a_s = _even_out(xs, yn, gout, tail_w)
        else:
            router = jnp.pad(moe_router[i], ((0, 0), (0, V7X_LANES - n_experts)))
            router_hi = router.astype(BF16)
            router_lo = (router - router_hi.astype(F32)).astype(BF16)
            tail = (conf_dw_w[i], _row(conf_dw_b[i]), _row(conf_ln_g[i]), _row(conf_ln_b[i]),
                    conf_w2[i].astype(BF16), _row(conf_b2[i]), _row(norm_ffn_odd[i]), router_hi, router_lo)
            n_keep = conf_dw_w.shape[1] - 1
            outs['conf_p'].append(a_p.reshape(batch, seq, -1)[:, seq - n_keep:])
            buf = state_conf_conv[i].reshape(n_dec, -1)
            newbuf, *sample_rows, route_t_s = _conf_sample(a_s, buf, xs, tail, n_experts=n_experts)
            outs['conf_s'].append(newbuf.reshape(state_conf_conv[i].shape))
            x_all, hn, route, route_t_p = _conf_prompt(a_p, xp, tuple(sample_rows), tail, seq=seq,
                                                       n_experts=n_experts)
            route_t = jnp.concatenate([route_t_p, route_t_s], axis=1)
            tile_expert, n_tiles, pos, pad_pos, n_rows = _moe_plan(route_t, n_experts, MOE_TILE)
            xs_sorted = _sc_dispatch_rows(hn, pos, pad_pos, n_rows)
            ys = _moe_ffn(tile_expert, n_tiles, xs_sorted, moe_w_gate[i], moe_w_up[i], moe_w_down[i], tm=MOE_TILE)
            picked = _sc_gather_rows(ys, jnp.concatenate([pos[:, :tp].reshape(-1), pos[:, tp:].reshape(-1)]))
            gf = _row(final_norm)
            xp = _moe_combine(x_all, route, gf, picked, row0=0, n_rows=tp, a_row0=0, b_row0=tp)
            xs = _moe_combine(x_all, route, gf, picked, row0=tp, n_rows=n_dec, a_row0=TOP_K * tp,
                              b_row0=TOP_K * tp + n_dec)

    y_prompt = xp.reshape(batch, seq, d_model)
    y_sample = xs.reshape(n_dec, 1, d_model)
    return (y_prompt, y_sample, jnp.stack(outs['ssm_p']), jnp.stack(outs['conv_p']), jnp.stack(outs['conf_p']),
            jnp.stack(outs['ssm_s']), jnp.stack(outs['conv_s']), jnp.stack(outs['conf_s']), jnp.stack(outs['v_s']))
```

```python
import functools

import jax
import jax.numpy as jnp
from jax import lax
from jax.experimental import pallas as pl
from jax.experimental.pallas import tpu as pltpu
from jax.experimental.pallas import tpu_sc as plsc

F32 = jnp.float32
BF16 = jnp.bfloat16
HIGHEST = lax.Precision.HIGHEST

SSD_HEAD_DIM = 64
SSD_GROUPS = 4
SSD_STATE = 128
SSD_CHUNK = 128
GMLP_CHUNK = 128
TOP_K = 2
RMS_EPS = 1e-6
LN_EPS = 1e-5

V7X_LANES = 128
V7X_SUBLANES = 8
V7X_VMEM_BYTES = 64 * 1024 * 1024
VMEM_LIMIT = (V7X_VMEM_BYTES * 7) // 8

MOE_TILE = 256
COMBINE_TILE = 512
SSD_SEQS_PER_STEP = 2
MOE_FF_CHUNK = 256
CONF_TILE = 256
CONF_ACC_ROWS = 64
SC_WINDOW_BYTES = 128 * 1024
SAMPLE_STATE_TILE = 8
CONF_SAMPLE_TILE = 32


def _params(*semantics):
    return pltpu.CompilerParams(dimension_semantics=semantics, vmem_limit_bytes=VMEM_LIMIT)


def _resident(shape):
    zeros = (0,) * len(shape)
    return pl.BlockSpec(shape, lambda *_: zeros, pipeline_mode=pl.Buffered(1))


def _whole(shape):
    zeros = (0,) * len(shape)
    return pl.BlockSpec(shape, lambda *_: zeros)


def _row_tile(n_rows, preferred):
    for t in (preferred, 512, 256, 128):
        if t <= preferred and n_rows % t == 0:
            return t
    raise ValueError(f"row count {n_rows} is not a multiple of {V7X_LANES}")


def _dot(a, b):
    return jnp.dot(a, b, preferred_element_type=F32)


def _pack_bf16_pairs(x):
    w = x.shape[1] // 2
    lo = pltpu.bitcast(x[:, :w].astype(BF16).astype(F32), jnp.uint32) >> 16
    hi = pltpu.bitcast(x[:, w:].astype(BF16).astype(F32), jnp.uint32) & jnp.uint32(0xFFFF0000)
    return hi | lo


def _unpack_bf16_pairs(u):
    lo = pltpu.bitcast(u << 16, F32)
    hi = pltpu.bitcast(u & jnp.uint32(0xFFFF0000), F32)
    return jnp.concatenate([lo, hi], axis=1)


def _rms(x, g):
    return x * lax.rsqrt(jnp.mean(x * x, axis=-1, keepdims=True) + RMS_EPS) * g


def _layernorm(x, g, b):
    mu = jnp.mean(x, axis=-1, keepdims=True)
    xc = x - mu
    var = jnp.mean(xc * xc, axis=-1, keepdims=True)
    return xc * lax.rsqrt(var + LN_EPS) * g + b


def _sigmoid(x):
    return 0.5 * jnp.tanh(0.5 * x) + 0.5


def _silu(x):
    return x * _sigmoid(x)


def _softplus(x):
    return jnp.maximum(x, 0.0) + jnp.log1p(jnp.exp(-jnp.abs(x)))


def _even_in_kernel(x_ref, g_ref, wzx_ref, wuv_ref, wdt_ref, lng_ref, lnb_ref, z_ref, xbc_ref, ug_ref, vn_ref,
                    dt_ref, *, d_ssd, conv_dim, d_gmlp):
    h = _rms(x_ref[...], g_ref[...]).astype(BF16)
    z_ref[...] = _dot(h, wzx_ref[:, 0:d_ssd])
    xbc_ref[...] = _dot(h, wzx_ref[:, d_ssd:d_ssd + conv_dim])
    ug_ref[...] = jax.nn.gelu(_dot(h, wuv_ref[:, 0:d_gmlp])).astype(ug_ref.dtype)
    v = jax.nn.gelu(_dot(h, wuv_ref[:, d_gmlp:2 * d_gmlp]))
    vn_ref[...] = _layernorm(v, lng_ref[...], lnb_ref[...])
    dt_ref[...] = _dot(h, wdt_ref[...])


def _even_in(x, g, w_zx, w_uv, w_dt, ln_g, ln_b, *, d_ssd, conv_dim, d_gmlp):
    t, d = x.shape
    tm = _row_tile(t, 512)
    row = lambda n: pl.BlockSpec((tm, n), lambda i: (i, 0))
    consts = (g, w_zx, w_uv, w_dt, ln_g, ln_b)
    return pl.pallas_call(
        functools.partial(_even_in_kernel, d_ssd=d_ssd, conv_dim=conv_dim, d_gmlp=d_gmlp),
        grid=(t // tm,),
        in_specs=[row(d)] + [_resident(c.shape) for c in consts],
        out_specs=[row(d_ssd), row(conv_dim), row(d_gmlp), row(d_gmlp), row(V7X_LANES)],
        out_shape=[jax.ShapeDtypeStruct((t, d_ssd), F32), jax.ShapeDtypeStruct((t, conv_dim), F32),
                   jax.ShapeDtypeStruct((t, d_gmlp), BF16), jax.ShapeDtypeStruct((t, d_gmlp), F32),
                   jax.ShapeDtypeStruct((t, V7X_LANES), F32)],
        compiler_params=_params("parallel"),
        name="even_in",
    )(x, *consts)


def _ssd_prompt_kernel(xbc_ref, z_ref, dt_ref, cw_ref, cb_ref, dtb_ref, alog_ref, dskip_ref, ng_ref,
                       yn_ref, st_ref, xp_ref, state_ref, xc_ref, *, d_ssd, n_taps):
    c = pl.program_id(1)

    @pl.when(c == 0)
    def _():
        state_ref[...] = jnp.zeros_like(state_ref)
        xp_ref[:, 0:V7X_SUBLANES, :] = jnp.zeros((xp_ref.shape[0], V7X_SUBLANES, xp_ref.shape[2]), F32)

    for s in range(xbc_ref.shape[1]):
        _ssd_chunk(xbc_ref.at[0, s], z_ref.at[0, s], dt_ref.at[0, s], cw_ref, cb_ref, dtb_ref, alog_ref, dskip_ref,
                   ng_ref, yn_ref.at[0, s], xp_ref.at[s], state_ref.at[s], xc_ref.at[s], d_ssd=d_ssd, n_taps=n_taps)

    @pl.when(c == pl.num_programs(1) - 1)
    def _():
        for s in range(state_ref.shape[0]):
            for pair in range(state_ref.shape[1]):
                st_ref[s, pair] = state_ref[s, pair].T


def _ssd_chunk(xbc_ref, z_ref, dt_ref, cw_ref, cb_ref, dtb_ref, alog_ref, dskip_ref, ng_ref,
               yn_ref, xp_ref, state_ref, xc_ref, *, d_ssd, n_taps):
    n_pairs = state_ref.shape[0]
    ch = SSD_CHUNK
    gn = SSD_GROUPS * SSD_STATE

    xp_ref[V7X_SUBLANES:V7X_SUBLANES + ch, :] = xbc_ref[...]
    base = V7X_SUBLANES - (n_taps - 1)
    acc = cb_ref[...] + cw_ref[n_taps - 1:n_taps, :] * xbc_ref[...]
    for k in range(n_taps - 1):
        acc = acc + cw_ref[k:k + 1, :] * xp_ref[base + k:base + k + ch, :]
    xp_ref[0:V7X_SUBLANES, :] = xp_ref[ch:ch + V7X_SUBLANES, :]
    xc_ref[...] = _silu(acc)
    xc = xc_ref

    dt = _softplus(dt_ref[...] + dtb_ref[...])
    a = dt * (-jnp.exp(alog_ref[...]))
    li = lax.broadcasted_iota(jnp.int32, (ch, ch), 0)
    si = lax.broadcasted_iota(jnp.int32, (ch, ch), 1)
    causal = li >= si
    tril = jnp.where(causal, 1.0, 0.0).astype(F32)
    acs = jnp.dot(tril, a, precision=HIGHEST, preferred_element_type=F32)
    acs_t = acs.T
    lane = lax.broadcasted_iota(jnp.int32, (ch, 2 * SSD_HEAD_DIM), 1)
    first = lane < SSD_HEAD_DIM
    first_n = lax.broadcasted_iota(jnp.int32, (SSD_STATE, 2 * SSD_HEAD_DIM), 1) < SSD_HEAD_DIM
    pairs_per_group = n_pairs // SSD_GROUPS

    for g in range(SSD_GROUPS):
        bg = xc[:, d_ssd + g * SSD_STATE:d_ssd + (g + 1) * SSD_STATE]
        cg = xc[:, d_ssd + gn + g * SSD_STATE:d_ssd + gn + (g + 1) * SSD_STATE]
        bg_t = bg.T
        cb = _dot(cg.astype(BF16), bg_t.astype(BF16))
        ys = []
        for q in range(pairs_per_group):
            pair = g * pairs_per_group + q
            h0 = 2 * pair
            lo = pair * 2 * SSD_HEAD_DIM
            xs = xc[:, lo:lo + 2 * SSD_HEAD_DIM]
            dt2 = jnp.where(first, dt[:, h0:h0 + 1], dt[:, h0 + 1:h0 + 2])
            xdt = (xs * dt2).astype(BF16)
            s_prev = state_ref[pair]
            s_prev_b = s_prev.astype(BF16)
            y2, snew2, dec2 = [], [], []
            for r in range(2):
                h = h0 + r
                col = acs[:, h:h + 1]
                row = acs_t[h:h + 1, :]
                last = acs_t[h:h + 1, ch - 1:ch]
                decay = jnp.exp(jnp.where(causal, col - row, -jnp.inf))
                y = _dot((cb * decay).astype(BF16), xdt)
                y = y + _dot((cg * jnp.exp(col)).astype(BF16), s_prev_b)
                snew = _dot((bg_t * jnp.exp(last - row)).astype(BF16), xdt)
                y2.append(y)
                snew2.append(snew)
                dec2.append(jnp.exp(last))
            y = jnp.where(first, y2[0], y2[1])
            state_ref[pair] = (s_prev * jnp.where(first_n, dec2[0], dec2[1])
                               + jnp.where(first_n, snew2[0], snew2[1]))
            ys.append(y + dskip_ref[:, lo:lo + 2 * SSD_HEAD_DIM] * xs)
        gw = pairs_per_group * 2 * SSD_HEAD_DIM
        yg = jnp.concatenate(ys, axis=-1) * _silu(z_ref[:, g * gw:(g + 1) * gw])
        yg = yg * lax.rsqrt(jnp.mean(yg * yg, axis=-1, keepdims=True) + RMS_EPS)
        yn_ref[:, g * gw:(g + 1) * gw] = (yg * ng_ref[:, g * gw:(g + 1) * gw]).astype(yn_ref.dtype)


def _ssd_prompt(xbc, z, dt, conv_w, conv_b, dt_bias, a_log, d_skip, norm_g, *, batch, seq):
    t, conv_dim = xbc.shape
    d_ssd = z.shape[1]
    n_pairs = d_ssd // (2 * SSD_HEAD_DIM)
    nc = seq // SSD_CHUNK
    n_taps = conv_w.shape[0]
    g = SSD_SEQS_PER_STEP if batch % SSD_SEQS_PER_STEP == 0 else 1
    tile = lambda n: pl.BlockSpec((1, g, SSD_CHUNK, n), lambda b, c: (b, 0, c, 0))
    by_seq = lambda a: a.reshape(batch // g, g, seq, a.shape[1])
    yn, st = pl.pallas_call(
        functools.partial(_ssd_prompt_kernel, d_ssd=d_ssd, n_taps=n_taps),
        grid=(batch // g, nc),
        in_specs=[tile(conv_dim), tile(d_ssd), tile(V7X_LANES), _resident(conv_w.shape), _resident(conv_b.shape),
                  _resident(dt_bias.shape), _resident(a_log.shape), _resident(d_skip.shape),
                  _resident(norm_g.shape)],
        out_specs=[tile(d_ssd),
                   pl.BlockSpec((g, n_pairs, 2 * SSD_HEAD_DIM, SSD_STATE), lambda b, c: (b, 0, 0, 0))],
        out_shape=[jax.ShapeDtypeStruct((batch // g, g, seq, d_ssd), BF16),
                   jax.ShapeDtypeStruct((batch, n_pairs, 2 * SSD_HEAD_DIM, SSD_STATE), F32)],
        scratch_shapes=[pltpu.VMEM((g, SSD_CHUNK + V7X_SUBLANES, conv_dim), F32),
                        pltpu.VMEM((g, n_pairs, SSD_STATE, 2 * SSD_HEAD_DIM), F32),
                        pltpu.VMEM((g, SSD_CHUNK, conv_dim), F32)],
        compiler_params=_params("parallel", "arbitrary"),
        name="ssd_prompt",
    )(by_seq(xbc), by_seq(z), by_seq(dt), conv_w, conv_b, dt_bias, a_log, d_skip, norm_g)
    return yn.reshape(t, d_ssd), st


def _gmlp_gate(ug_ref, vn_ref, ws_ref, bs_ref, out_ref):
    ch = GMLP_CHUNK
    n_groups = ws_ref.shape[0]
    gd = vn_ref.shape[1] // n_groups
    ii = lax.broadcasted_iota(jnp.int32, (ch, ch), 0)
    jj = lax.broadcasted_iota(jnp.int32, (ch, ch), 1)
    for g in range(n_groups):
        ws = jnp.where(ii >= jj, ws_ref[g], 0.0).astype(BF16)
        cols = slice(g * gd, (g + 1) * gd)
        for r0 in range(0, vn_ref.shape[0], ch):
            rows = slice(r0, r0 + ch)
            mixed = _dot(ws, vn_ref[rows, cols].astype(BF16)) + bs_ref[:, g:g + 1]
            out_ref[rows, cols] = (ug_ref[rows, cols].astype(F32) * mixed).astype(out_ref.dtype)


def _ssd_sample_prep_kernel(xbc_ref, buf_ref, dt_ref, cw_ref, cb_ref, dtb_ref, alog_ref, expand_ref,
                            newbuf_ref, xs_ref, xdt_ref, bc_ref, da_ref, *, d_ssd, n_taps):
    x_new = xbc_ref[...]
    acc = cb_ref[...] + cw_ref[n_taps - 1:n_taps, :] * x_new
    for k in range(n_taps - 1):
        acc = acc + cw_ref[k:k + 1, :] * buf_ref[:, k, :]
    newbuf_ref[:, 0:n_taps - 2, :] = buf_ref[:, 1:n_taps - 1, :]
    newbuf_ref[:, n_taps - 2, :] = x_new
    xc = _silu(acc)
    xs = xc[:, :d_ssd]
    dt = _softplus(dt_ref[...] + dtb_ref[...])
    da_ref[...] = jnp.exp(dt * (-jnp.exp(alog_ref[...])))
    dt_wide = jnp.dot(dt, expand_ref[...], precision=HIGHEST, preferred_element_type=F32)
    xs_ref[...] = xs
    xdt_ref[...] = xs * dt_wide
    bc_ref[...] = xc[:, d_ssd:]


def _ssd_sample_prep(xbc, buf, dt, conv_w, conv_b, dt_bias, a_log, expand):
    n, conv_dim = xbc.shape
    d_ssd = expand.shape[1]
    n_taps = conv_w.shape[0]
    args = (xbc, buf, dt, conv_w, conv_b, dt_bias, a_log, expand)
    return pl.pallas_call(
        functools.partial(_ssd_sample_prep_kernel, d_ssd=d_ssd, n_taps=n_taps),
        grid=(1,),
        in_specs=[_resident(a.shape) for a in args],
        out_specs=[_whole(buf.shape), _whole((n, d_ssd)), _whole((n, d_ssd)),
                   _whole((n, conv_dim - d_ssd)), _whole((n, V7X_LANES))],
        out_shape=[jax.ShapeDtypeStruct(buf.shape, F32), jax.ShapeDtypeStruct((n, d_ssd), F32),
                   jax.ShapeDtypeStruct((n, d_ssd), F32), jax.ShapeDtypeStruct((n, conv_dim - d_ssd), F32),
                   jax.ShapeDtypeStruct((n, V7X_LANES), F32)],
        compiler_params=_params("arbitrary"),
        name="ssd_sample_prep",
    )(*args)


def _ssd_sample_state_kernel(h_ref, xdt_ref, da_ref, bc_ref, hnew_ref, y_ref, xcol_ref, ccol_ref):
    bs, n_heads, p, s = h_ref.shape
    n_pairs = n_heads // 2
    pairs_per_group = n_pairs // SSD_GROUPS
    gn = SSD_GROUPS * s
    w = 2 * p
    eye = lax.broadcasted_iota(jnp.int32, (w, w), 0) == lax.broadcasted_iota(jnp.int32, (w, w), 1)
    ones = jnp.ones((w, V7X_LANES), BF16)

    def stacked_diag(v):
        return jnp.concatenate(
            [jnp.where(eye, jnp.broadcast_to(v[j:j + 1], (w, w)), 0.0) for j in range(bs)], axis=0).astype(BF16)

    for q in range(n_pairs):
        x = xdt_ref[:, q * w:(q + 1) * w]
        hi = x.astype(BF16).astype(F32)
        mid = (x - hi).astype(BF16).astype(F32)
        lo = (x - hi) - mid
        xcol_ref[q] = _dot(stacked_diag(hi), ones) + (_dot(stacked_diag(mid), ones) + _dot(stacked_diag(lo), ones))
    for g in range(SSD_GROUPS):
        ccol_ref[g] = _dot(stacked_diag(bc_ref[:, gn + g * s:gn + (g + 1) * s]), ones).astype(BF16)

    for j in range(bs):
        for g in range(SSD_GROUPS):
            brow = bc_ref[j:j + 1, g * s:(g + 1) * s]
            h_new = []
            for q in range(g * pairs_per_group, (g + 1) * pairs_per_group):
                for r in range(2):
                    h = 2 * q + r
                    xcol = xcol_ref[q, j * w + r * p:j * w + (r + 1) * p, :]
                    h_new.append(h_ref[j, h] * da_ref[j:j + 1, h:h + 1] + xcol * brow)
                    hnew_ref[j, h] = h_new[-1]
            hc = _dot(jnp.concatenate(h_new, axis=0).astype(BF16), ccol_ref[g, j * s:(j + 1) * s, :])
            for i in range(pairs_per_group):
                y = jnp.sum(jnp.where(eye, hc[i * w:(i + 1) * w], 0.0), axis=0, keepdims=True)
                q = g * pairs_per_group + i
                y_ref[j, q:q + 1, :] = y


def _ssd_sample_state(h, xdt, da, bc):
    n, n_heads, p, s = h.shape
    assert 2 * p == V7X_LANES and s == V7X_LANES
    bs = SAMPLE_STATE_TILE
    rows = lambda w: pl.BlockSpec((bs, w), lambda i: (i, 0))
    return pl.pallas_call(
        _ssd_sample_state_kernel,
        grid=(n // bs,),
        in_specs=[pl.BlockSpec((bs, n_heads, p, s), lambda i: (i, 0, 0, 0)), rows(xdt.shape[1]), rows(da.shape[1]),
                  rows(bc.shape[1])],
        out_specs=[pl.BlockSpec((bs, n_heads, p, s), lambda i: (i, 0, 0, 0)),
                   pl.BlockSpec((bs, n_heads // 2, 2 * p), lambda i: (i, 0, 0))],
        out_shape=[jax.ShapeDtypeStruct(h.shape, F32), jax.ShapeDtypeStruct((n, n_heads // 2, 2 * p), F32)],
        scratch_shapes=[pltpu.VMEM((n_heads // 2, bs * 2 * p, V7X_LANES), F32),
                        pltpu.VMEM((SSD_GROUPS, bs * s, V7X_LANES), BF16)],
        compiler_params=_params("parallel"),
        name="ssd_sample_state",
    )(h, xdt, da, bc)


def _even_sample_post_kernel(y_ref, xs_ref, z_ref, dskip_ref, ng_ref, ug_ref, vn_ref, w0_ref, b0_ref,
                             yn_ref, gout_ref):
    d_ssd = y_ref.shape[1]
    gw = d_ssd // SSD_GROUPS
    y = (y_ref[...] + dskip_ref[...] * xs_ref[...]) * _silu(z_ref[...])
    for g in range(SSD_GROUPS):
        yg = y[:, g * gw:(g + 1) * gw]
        yg = yg * lax.rsqrt(jnp.mean(yg * yg, axis=-1, keepdims=True) + RMS_EPS)
        yn_ref[:, g * gw:(g + 1) * gw] = (yg * ng_ref[:, g * gw:(g + 1) * gw]).astype(yn_ref.dtype)
    mixed = w0_ref[...] * vn_ref[...] + b0_ref[...]
    gout_ref[...] = (ug_ref[...].astype(F32) * mixed).astype(gout_ref.dtype)


def _even_sample_post(y, xs, z, d_skip, norm_g, ug, vn, w0, b0):
    args = (y, xs, z, d_skip, norm_g, ug, vn, w0, b0)
    return pl.pallas_call(
        _even_sample_post_kernel,
        grid=(1,),
        in_specs=[_resident(a.shape) for a in args],
        out_specs=[_whole(y.shape), _whole(vn.shape)],
        out_shape=[jax.ShapeDtypeStruct(y.shape, BF16), jax.ShapeDtypeStruct(vn.shape, BF16)],
        compiler_params=_params("arbitrary"),
        name="even_sample_post",
    )(*args)


def _even_out_kernel(*refs, fuse_gmlp):
    if fuse_gmlp:
        (x_ref, yn_ref, ug_ref, vn_ref, ws_ref, bs_ref, wo_ref, g_ref, wg_ref, wu_ref, wd_ref, gc_ref, w1_ref,
         b1_ref, out_ref, a_ref, gout_ref) = refs
        _gmlp_gate(ug_ref, vn_ref, ws_ref, bs_ref, gout_ref)
    else:
        (x_ref, yn_ref, gout_ref, wo_ref, g_ref, wg_ref, wu_ref, wd_ref, gc_ref, w1_ref, b1_ref,
         out_ref, a_ref) = refs
    d_ssd = yn_ref.shape[1]
    x = x_ref[...] + _dot(yn_ref[...], wo_ref[0:d_ssd, :]) + _dot(gout_ref[...], wo_ref[d_ssd:, :])
    h = _rms(x, g_ref[...]).astype(BF16)
    mid = (_silu(_dot(h, wg_ref[...])) * _dot(h, wu_ref[...])).astype(BF16)
    x = x + _dot(mid, wd_ref[...])
    out_ref[...] = x
    dc = a_ref.shape[1]
    h = _rms(x, gc_ref[...]).astype(BF16)
    val = _dot(h, w1_ref[:, 0:dc]) + b1_ref[:, 0:dc]
    gate = _dot(h, w1_ref[:, dc:]) + b1_ref[:, dc:]
    a_ref[...] = val * _sigmoid(gate)


def _even_out(x, yn, gate_in, weights, *, seq=None):
    t, d = x.shape
    dc = weights[-2].shape[1] // 2
    fuse_gmlp = isinstance(gate_in, tuple)
    tm = _row_tile(t if seq is None else seq, 256)
    row = lambda n: pl.BlockSpec((tm, n), lambda i: (i, 0))
    if fuse_gmlp:
        ug, vn, w_s, b_s_t = gate_in
        assert tm % GMLP_CHUNK == 0
        gate_args = (ug, vn, w_s, b_s_t)
        gate_specs = [row(ug.shape[1]), row(vn.shape[1]), _resident(w_s.shape), _resident(b_s_t.shape)]
        scratch = [pltpu.VMEM((tm, ug.shape[1]), BF16)]
    else:
        gate_args, gate_specs, scratch = (gate_in,), [row(gate_in.shape[1])], []
    return pl.pallas_call(
        functools.partial(_even_out_kernel, fuse_gmlp=fuse_gmlp),
        grid=(t // tm,),
        in_specs=[row(d), row(yn.shape[1])] + gate_specs + [_resident(w.shape) for w in weights],
        out_specs=[row(d), row(dc)],
        out_shape=[jax.ShapeDtypeStruct((t, d), F32), jax.ShapeDtypeStruct((t, dc), F32)],
        scratch_shapes=scratch,
        compiler_params=_params("parallel"),
        name="even_out_ffn",
    )(x, yn, *gate_args, *weights)


def _route(logits, n_experts, route_ref, route_t_ref):
    lane = lax.broadcasted_iota(jnp.int32, logits.shape, 1)
    lane_f = lane.astype(F32)
    big = jnp.float32(V7X_LANES)
    lg = jnp.where(lane < n_experts, logits, -jnp.inf)
    m1 = jnp.max(lg, axis=-1, keepdims=True)
    i1 = jnp.min(jnp.where(lg == m1, lane_f, big), axis=-1, keepdims=True)
    lg2 = jnp.where(lane_f == i1, -jnp.inf, lg)
    m2 = jnp.max(lg2, axis=-1, keepdims=True)
    i2 = jnp.min(jnp.where(lg2 == m2, lane_f, big), axis=-1, keepdims=True)
    e2 = jnp.exp(m2 - m1)
    g1 = 1.0 / (1.0 + e2)
    g2 = e2 / (1.0 + e2)
    route = jnp.where(lane == 0, i1, jnp.where(lane == 1, i2, jnp.where(lane == 2, g1,
                      jnp.where(lane == 3, g2, 0.0))))
    route_ref[...] = route
    route_t_ref[...] = route.T[0:V7X_SUBLANES, :]


def _conf_finish(c, x, lng_ref, lnb_ref, w2_ref, b2_ref, ng_ref, rhi_ref, rlo_ref, out_ref, hn_ref, route_ref,
                 route_t_ref, *, n_experts):
    hmid = _silu(_layernorm(c, lng_ref[...], lnb_ref[...])).astype(BF16)
    x_new = x + _dot(hmid, w2_ref[...]) + b2_ref[...]
    out_ref[...] = x_new
    hn = _rms(x_new, ng_ref[...])
    hn_ref[...] = _pack_bf16_pairs(hn)
    hi = hn.astype(BF16)
    lo = (hn - hi.astype(F32)).astype(BF16)
    logits = _dot(hi, rhi_ref[...]) + (_dot(lo, rhi_ref[...]) + _dot(hi, rlo_ref[...]))
    _route(logits, n_experts, route_ref, route_t_ref)


def _conf_prompt_kernel(a_ref, x_ref, xs_ref, hns_ref, routes_ref, dw_ref, db_ref, lng_ref, lnb_ref, w2_ref, b2_ref,
                        ng_ref, rhi_ref, rlo_ref, out_ref, hn_ref, route_ref, route_t_ref, xp_ref, c_ref, win_ref,
                        *, n_taps, n_experts, tiles_per_seq, n_prompt_tiles):
    tl, d = a_ref.shape
    halo = xp_ref.shape[0] - tl
    base = halo - (n_taps - 1)
    rows = CONF_ACC_ROWS
    step = pl.program_id(0)

    @pl.when(step < n_prompt_tiles)
    def _():
        @pl.when(step % tiles_per_seq == 0)
        def _():
            xp_ref[0:halo, :] = jnp.zeros((halo, d), F32)

        xp_ref[halo:halo + tl, :] = a_ref[...]

        def lane_block(j, carry):
            cols = pl.ds(pl.multiple_of(j * V7X_LANES, V7X_LANES), V7X_LANES)
            for r0 in range(0, tl, rows):
                acc = jnp.broadcast_to(db_ref[:, cols], (rows, V7X_LANES))
                for phase in range(V7X_SUBLANES):
                    taps = [k for k in range(n_taps) if (base + k) % V7X_SUBLANES == phase]
                    if not taps:
                        continue
                    span = max(base + k - phase for k in taps) + rows
                    win_ref[0:span, :] = xp_ref[pl.ds(r0 + phase, span), cols]
                    for k in taps:
                        o = base + k - phase
                        acc = acc + dw_ref[k:k + 1, cols] * win_ref[o:o + rows, :]
                c_ref[r0:r0 + rows, cols] = acc
            return carry

        lax.fori_loop(0, d // V7X_LANES, lane_block, 0)
        xp_ref[0:halo, :] = xp_ref[tl:tl + halo, :]
        _conf_finish(c_ref[...], x_ref[...], lng_ref, lnb_ref, w2_ref, b2_ref, ng_ref, rhi_ref, rlo_ref,
                     out_ref, hn_ref, route_ref, route_t_ref, n_experts=n_experts)

    @pl.when(step == n_prompt_tiles)
    def _():
        ns = xs_ref.shape[0]
        out_ref[0:ns, :] = xs_ref[...]
        hn_ref[0:ns, :] = hns_ref[...]
        route_ref[0:ns, :] = routes_ref[...]


def _conf_prompt(a, x, sample_rows, consts, *, seq, n_experts):
    t, d = a.shape
    n_s = sample_rows[0].shape[0]
    tl = _row_tile(seq, CONF_TILE)
    assert tl % CONF_ACC_ROWS == 0 and n_s <= tl
    n_prompt_tiles = t // tl
    n_taps = consts[0].shape[0]
    halo = -(-(n_taps - 1) // V7X_SUBLANES) * V7X_SUBLANES
    in_tile = lambda n: pl.BlockSpec((tl, n), lambda s: (jnp.minimum(s, n_prompt_tiles - 1), 0))
    out_tile = lambda n: pl.BlockSpec((tl, n), lambda s: (s, 0))
    small = sample_rows + tuple(consts)
    return pl.pallas_call(
        functools.partial(_conf_prompt_kernel, n_taps=n_taps, n_experts=n_experts, tiles_per_seq=seq // tl,
                          n_prompt_tiles=n_prompt_tiles),
        grid=(n_prompt_tiles + 1,),
        in_specs=[in_tile(d), in_tile(d)] + [_resident(c.shape) for c in small],
        out_specs=[out_tile(d), out_tile(d // 2), out_tile(V7X_LANES),
                   pl.BlockSpec((V7X_SUBLANES, tl), lambda s: (0, jnp.minimum(s, n_prompt_tiles - 1)))],
        out_shape=[jax.ShapeDtypeStruct((t + n_s, d), F32), jax.ShapeDtypeStruct((t + n_s, d // 2), jnp.uint32),
                   jax.ShapeDtypeStruct((t + n_s, V7X_LANES), F32), jax.ShapeDtypeStruct((V7X_SUBLANES, t), F32)],
        scratch_shapes=[pltpu.VMEM((tl + halo, d), F32), pltpu.VMEM((tl, d), F32),
                        pltpu.VMEM((CONF_ACC_ROWS + halo, V7X_LANES), F32)],
        compiler_params=_params("arbitrary"),
        name="conf_prompt",
    )(a, x, *small)


def _conf_sample_conv_kernel(a_ref, buf_ref, dw_ref, db_ref, newbuf_ref, c_ref, *, n_taps):
    a_new = a_ref[...]
    acc = db_ref[...] + dw_ref[n_taps - 1:n_taps, :] * a_new
    for k in range(n_taps - 1):
        acc = acc + dw_ref[k:k + 1, :] * buf_ref[:, k, :]
    c_ref[...] = acc
    newbuf_ref[:, 0:n_taps - 2, :] = buf_ref[:, 1:n_taps - 1, :]
    newbuf_ref[:, n_taps - 2, :] = a_new


def _conf_sample_conv(a, buf, dw_w, dw_b):
    n, d = a.shape
    bs = CONF_SAMPLE_TILE
    hist = pl.BlockSpec((bs,) + buf.shape[1:], lambda i: (i, 0, 0))
    rows = pl.BlockSpec((bs, d), lambda i: (i, 0))
    return pl.pallas_call(
        functools.partial(_conf_sample_conv_kernel, n_taps=dw_w.shape[0]),
        grid=(n // bs,),
        in_specs=[rows, hist, _resident(dw_w.shape), _resident(dw_b.shape)],
        out_specs=[hist, rows],
        out_shape=[jax.ShapeDtypeStruct(buf.shape, F32), jax.ShapeDtypeStruct((n, d), F32)],
        compiler_params=_params("parallel"),
        name="conf_sample_conv",
    )(a, buf, dw_w, dw_b)


def _conf_sample_kernel(c_ref, x_ref, lng_ref, lnb_ref, w2_ref, b2_ref, ng_ref, rhi_ref, rlo_ref,
                        out_ref, hn_ref, route_ref, route_t_ref, *, n_experts):
    _conf_finish(c_ref[...], x_ref[...], lng_ref, lnb_ref, w2_ref, b2_ref, ng_ref, rhi_ref, rlo_ref,
                 out_ref, hn_ref, route_ref, route_t_ref, n_experts=n_experts)


def _conf_sample(c, x, consts, *, n_experts):
    n, d = c.shape
    args = (c, x) + tuple(consts)
    return pl.pallas_call(
        functools.partial(_conf_sample_kernel, n_experts=n_experts),
        grid=(1,),
        in_specs=[_resident(v.shape) for v in args],
        out_specs=[_whole((n, d)), _whole((n, d // 2)), _whole((n, V7X_LANES)), _whole((V7X_SUBLANES, n))],
        out_shape=[jax.ShapeDtypeStruct((n, d), F32), jax.ShapeDtypeStruct((n, d // 2), jnp.uint32),
                   jax.ShapeDtypeStruct((n, V7X_LANES), F32), jax.ShapeDtypeStruct((V7X_SUBLANES, n), F32)],
        compiler_params=_params("arbitrary"),
        name="conf_sample",
    )(*args)


def _sc_window_rows(x):
    assert x.dtype.itemsize == 4, "the SparseCore indirect copies move 32-bit elements"
    return SC_WINDOW_BYTES // (x.shape[1] * x.dtype.itemsize)


def _sc_gather_rows(x, idx):
    n, d = idx.shape[0], x.shape[1]
    w = _sc_window_rows(x)
    assert n % w == 0
    mesh = plsc.VectorSubcoreMesh(core_axis_name="core", subcore_axis_name="subcore")

    @functools.partial(pl.kernel, out_type=jax.ShapeDtypeStruct((n, d), x.dtype), mesh=mesh, scratch_types=[])
    def gather(x_hbm, idx_hbm, out_hbm):
        def body(idx_vmem, out_vmem):
            pltpu.sync_copy(x_hbm.at[idx_vmem.at[0]], out_vmem)

        pltpu.emit_pipeline(
            body,
            grid=(n // w,),
            in_specs=[pl.BlockSpec((1, w), lambda i: (i, 0))],
            out_specs=[pl.BlockSpec((w, d), lambda i: (i, 0))],
            core_axis_name=("core", "subcore"),
            dimension_semantics=(pltpu.PARALLEL,),
        )(idx_hbm, out_hbm)

    return gather(x, idx.reshape(n // w, w))


def _sc_dispatch_rows(x, pos, pad_pos, n_out):
    t, d = x.shape
    w = _sc_window_rows(x)
    n_pad = pad_pos.shape[0]
    assert t % w == 0 and n_pad % w == 0 and TOP_K * t + n_pad == n_out
    mesh = plsc.VectorSubcoreMesh(core_axis_name="core", subcore_axis_name="subcore")

    @functools.partial(pl.kernel, out_type=jax.ShapeDtypeStruct((n_out, d), x.dtype), mesh=mesh, scratch_types=[])
    def dispatch(x_hbm, pos_hbm, pad_hbm, zero_hbm, out_hbm):
        def body(rows_vmem, idx_vmem):
            pltpu.sync_copy(rows_vmem, out_hbm.at[idx_vmem.at[0]])

        def scatter(src_hbm, idx_hbm, src_map):
            pltpu.emit_pipeline(
                body,
                grid=(idx_hbm.shape[0],),
                in_specs=[pl.BlockSpec((w, d), src_map), pl.BlockSpec((1, w), lambda i: (i, 0))],
                out_specs=[],
                core_axis_name=("core", "subcore"),
                dimension_semantics=(pltpu.PARALLEL,),
            )(src_hbm, idx_hbm)

        for k in range(TOP_K):
            scatter(x_hbm, pos_hbm.at[k], lambda i: (i, 0))
        scatter(zero_hbm, pad_hbm, lambda i: (0, 0))

    return dispatch(x, pos.reshape(TOP_K, t // w, w), pad_pos.reshape(n_pad // w, w), jnp.zeros((w, d), x.dtype))


def _moe_ffn_kernel(tile_expert_ref, first_ref, bank_ref, next_expert_ref, n_tiles_ref, x_ref, wg_hbm, wu_hbm,
                    wd_hbm, out_ref, wg_v, wu_v, wd_v, stage_g, stage_u, stage_d, acc_ref, sems):
    i = pl.program_id(0)
    dff = wg_v.shape[2]
    ck = stage_g.shape[2]
    n_chunks = dff // ck
    active = i < n_tiles_ref[0]
    starts_run = active & (first_ref[i] == 1)
    bank = bank_ref[i]
    nxt = next_expert_ref[i]

    def chunk_copies(expert, c, slot):
        cols = slice(c * ck, (c + 1) * ck)
        return (pltpu.make_async_copy(wg_hbm.at[expert, :, cols], stage_g.at[slot], sems.at[0, slot]),
                pltpu.make_async_copy(wu_hbm.at[expert, :, cols], stage_u.at[slot], sems.at[1, slot]),
                pltpu.make_async_copy(wd_hbm.at[expert, cols, :], stage_d.at[slot], sems.at[2, slot]))

    def stream_weights(expert, dst_bank, between_chunks):
        for cp in chunk_copies(expert, 0, 0):
            cp.start()
        for c in range(n_chunks):
            slot = c % 2
            if c + 1 < n_chunks:
                for cp in chunk_copies(expert, c + 1, 1 - slot):
                    cp.start()
            for cp in chunk_copies(expert, c, slot):
                cp.wait()
            cols = slice(c * ck, (c + 1) * ck)
            wg_v[dst_bank, :, cols] = stage_g[slot].astype(BF16)
            wu_v[dst_bank, :, cols] = stage_u[slot].astype(BF16)
            wd_v[dst_bank, cols, :] = stage_d[slot].astype(BF16)
            between_chunks(c)

    def ffn_part(xb, cols):
        mid = (_silu(_dot(xb, wg_v[bank, :, cols])) * _dot(xb, wu_v[bank, :, cols])).astype(BF16)
        return _dot(mid, wd_v[bank, cols, :])

    @pl.when(starts_run & (i == 0))
    def _():
        stream_weights(tile_expert_ref[i], bank, lambda c: None)

    overlapped = starts_run & (nxt >= 0)

    @pl.when(overlapped)
    def _():
        xb = _unpack_bf16_pairs(x_ref[...]).astype(BF16)
        acc_ref[...] = jnp.zeros_like(acc_ref)

        def piece(c):
            acc_ref[...] += ffn_part(xb, slice(c * ck, (c + 1) * ck))

        stream_weights(nxt, 1 - bank, piece)
        out_ref[...] = _pack_bf16_pairs(acc_ref[...])

    @pl.when(active & jnp.logical_not(overlapped))
    def _():
        xb = _unpack_bf16_pairs(x_ref[...]).astype(BF16)
        half = dff // 2
        acc = ffn_part(xb, slice(0, half)) + ffn_part(xb, slice(half, dff))
        out_ref[...] = _pack_bf16_pairs(acc)

    @pl.when(jnp.logical_not(active))
    def _():
        out_ref[...] = jnp.zeros_like(out_ref)


def _moe_ffn(tile_expert, n_tiles, xs, wg, wu, wd, *, tm):
    p, d = xs.shape[0], wg.shape[1]
    dff = wg.shape[2]
    ck = MOE_FF_CHUNK
    assert dff % ck == 0
    nt = p // tm
    i32 = jnp.int32
    tile = jnp.arange(nt, dtype=i32)
    first = jnp.concatenate([jnp.ones((1,), i32), (tile_expert[1:] != tile_expert[:-1]).astype(i32)])
    bank = (jnp.cumsum(first) - 1) % 2
    run_start = jnp.where((first == 1) & (tile < n_tiles[0]), tile, nt)
    next_start = jnp.concatenate([lax.cummin(run_start, reverse=True)[1:], jnp.full((1,), nt, i32)])
    next_expert = jnp.sum((next_start[:, None] == tile[None, :]).astype(i32) * tile_expert[None, :], axis=1)
    next_expert = jnp.where(next_start < nt, next_expert, -1)
    any_spec = pl.BlockSpec(memory_space=pl.ANY)
    row_tile = pl.BlockSpec((tm, d // 2), lambda i, *_: (i, 0))
    grid_spec = pltpu.PrefetchScalarGridSpec(
        num_scalar_prefetch=5,
        grid=(nt,),
        in_specs=[row_tile, any_spec, any_spec, any_spec],
        out_specs=row_tile,
        scratch_shapes=[pltpu.VMEM((2, d, dff), BF16), pltpu.VMEM((2, d, dff), BF16), pltpu.VMEM((2, dff, d), BF16),
                        pltpu.VMEM((2, d, ck), F32), pltpu.VMEM((2, d, ck), F32), pltpu.VMEM((2, ck, d), F32),
                        pltpu.VMEM((tm, d), F32), pltpu.SemaphoreType.DMA((3, 2))],
    )
    return pl.pallas_call(
        _moe_ffn_kernel,
        grid_spec=grid_spec,
        out_shape=jax.ShapeDtypeStruct((p, d // 2), jnp.uint32),
        compiler_params=_params("arbitrary"),
        name="moe_ffn",
    )(tile_expert, first, bank.astype(i32), next_expert.astype(i32), n_tiles, xs, wg, wu, wd)


def _moe_combine_kernel(x_ref, route_ref, gf_ref, a_ref, b_ref, y_ref):
    route = route_ref[...]
    x = (x_ref[...] + route[:, 2:3] * _unpack_bf16_pairs(a_ref[...])
         + route[:, 3:4] * _unpack_bf16_pairs(b_ref[...]))
    y_ref[...] = _rms(x, gf_ref[...])


def _moe_combine(x, route, gf, picked, *, row0, n_rows, a_row0, b_row0):
    d = x.shape[1]
    tc = _row_tile(n_rows, COMBINE_TILE)
    assert row0 % tc == 0 and a_row0 % tc == 0 and b_row0 % tc == 0
    rows = lambda w, r0: pl.BlockSpec((tc, w), lambda i: (i + r0 // tc, 0))
    return pl.pallas_call(
        _moe_combine_kernel,
        grid=(n_rows // tc,),
        in_specs=[rows(d, row0), rows(V7X_LANES, row0), _resident(gf.shape), rows(d // 2, a_row0),
                  rows(d // 2, b_row0)],
        out_specs=pl.BlockSpec((tc, d), lambda i: (i, 0)),
        out_shape=jax.ShapeDtypeStruct((n_rows, d), F32),
        compiler_params=_params("parallel"),
        name="moe_combine",
    )(x, route, gf, picked, picked)


def _moe_plan(route_t, n_experts, tm):
    t = route_t.shape[1]
    n_slots = t * TOP_K
    i32 = jnp.int32
    expert = route_t[:TOP_K].astype(i32).reshape(1, n_slots)
    onehot = (expert == jnp.arange(n_experts, dtype=i32)[:, None]).astype(i32)
    counts = jnp.sum(onehot, axis=1)
    tiles = (counts + tm - 1) // tm
    tile_end = jnp.cumsum(tiles)
    start = (tile_end - tiles) * tm
    rank = jnp.cumsum(onehot, axis=1) - onehot
    pos = jnp.sum(onehot * (rank + start[:, None]), axis=0)
    nt = -(-n_slots // tm) + n_experts
    n_tiles = tile_end[-1]
    tile_id = jnp.minimum(jnp.arange(nt, dtype=i32), n_tiles - 1)
    tile_expert = jnp.sum((tile_id[:, None] >= tile_end[None, :]).astype(i32), axis=1)
    seg_start = jnp.concatenate([start + counts, (n_tiles * tm)[None]])
    seg_len = jnp.concatenate([tiles * tm - counts, (nt * tm - n_tiles * tm)[None]])
    seg_end = jnp.cumsum(seg_len)
    q = jnp.arange(nt * tm - n_slots, dtype=i32)
    in_seg = (q[:, None] >= (seg_end - seg_len)[None, :]) & (q[:, None] < seg_end[None, :])
    pad_pos = jnp.sum(in_seg.astype(i32) * (seg_start - (seg_end - seg_len))[None, :], axis=1) + q
    return (tile_expert.astype(i32), n_tiles.astype(i32).reshape(1), pos.reshape(TOP_K, t).astype(i32),
            pad_pos.astype(i32), nt * tm)


def _row(v):
    return v.reshape(1, -1).astype(F32)


def _pad_lanes(v):
    return jnp.pad(_row(v), ((0, 0), (0, V7X_LANES - v.shape[-1])))


def kernel(x_prompt, x_sample, state_ssm, state_ssd_conv, state_conf_conv, norm_mix_even, w_in_even, ssd_conv_w, ssd_conv_b, ssd_dt_bias, ssd_a_log, ssd_d, ssd_norm, gmlp_ln_g, gmlp_ln_b, gmlp_w_s, gmlp_b_s, w_out_even, norm_ffn_even, ffn_w_gate, ffn_w_up, ffn_w_down, norm_mix_odd, conf_w1, conf_b1, conf_dw_w, conf_dw_b, conf_ln_g, conf_ln_b, conf_w2, conf_b2, norm_ffn_odd, moe_router, moe_w_gate, moe_w_up, moe_w_down, final_norm):
    batch, seq, d_model = x_prompt.shape
    n_dec, dec_seq, _ = x_sample.shape
    assert dec_seq == 1, "the sample group advances one token per sequence"
    assert seq % SSD_CHUNK == 0 and n_dec % V7X_LANES == 0
    n_even, n_odd = w_in_even.shape[0], conf_w1.shape[0]
    assert (n_even, n_odd) == (1, 1), "the final norm is fused into the last (odd) layer's MoE combine"
    n_heads = ssd_dt_bias.shape[1]
    d_ssd = n_heads * SSD_HEAD_DIM
    conv_dim = ssd_conv_w.shape[2]
    d_gmlp = gmlp_ln_g.shape[1]
    n_groups_gmlp = gmlp_w_s.shape[1]
    n_experts = moe_router.shape[2]
    tp = batch * seq

    xp = x_prompt.reshape(tp, d_model)
    xs = x_sample.reshape(n_dec, d_model)
    outs = dict(ssm_p=[], conv_p=[], conf_p=[], ssm_s=[], conv_s=[], conf_s=[], v_s=[])

    for layer in range(n_even + n_odd):
        i = layer // 2
        if layer % 2 == 0:
            w = w_in_even[i]
            o1, o2, o3 = d_ssd, d_ssd + conv_dim, d_ssd + conv_dim + n_heads
            w_parts = (w[:, :o2].astype(BF16), w[:, o3:].astype(BF16),
                       jnp.pad(w[:, o2:o3], ((0, 0), (0, V7X_LANES - n_heads))).astype(BF16))
            in_args = (_row(norm_mix_even[i]), *w_parts, _row(gmlp_ln_g[i]), _row(gmlp_ln_b[i]))
            dims = dict(d_ssd=d_ssd, conv_dim=conv_dim, d_gmlp=d_gmlp)
            dt_bias, a_log = _pad_lanes(ssd_dt_bias[i]), _pad_lanes(ssd_a_log[i])
            d_skip = _row(jnp.repeat(ssd_d[i], SSD_HEAD_DIM))
            norm_g = _row(ssd_norm[i])
            conv_w, conv_b = ssd_conv_w[i], _row(ssd_conv_b[i])
            z, xbc, ug, vn, dt = _even_in(xp, *in_args, **dims)
            yn, st = _ssd_prompt(xbc, z, dt, conv_w, conv_b, dt_bias, a_log, d_skip, norm_g, batch=batch, seq=seq)
            n_pairs = n_heads // 2
            outs['ssm_p'].append(st.reshape(batch, n_heads, SSD_HEAD_DIM, SSD_STATE))
            outs['conv_p'].append(xbc.reshape(batch, seq, conv_dim)[:, seq - (conv_w.shape[0] - 1):])
            tail_w = (w_out_even[i].astype(BF16), _row(norm_ffn_even[i]), ffn_w_gate[i].astype(BF16),
                      ffn_w_up[i].astype(BF16), ffn_w_down[i].astype(BF16), _row(norm_mix_odd[i]),
                      conf_w1[i].astype(BF16), _row(conf_b1[i]))
            xp, a_p = _even_out(xp, yn, (ug, vn, gmlp_w_s[i], gmlp_b_s[i].T), tail_w, seq=seq)
            z, xbc, ug, vn, dt = _even_in(xs, *in_args, **dims)
            expand = (jnp.arange(d_ssd)[None, :] // SSD_HEAD_DIM == jnp.arange(V7X_LANES)[:, None]).astype(F32)
            newbuf, xc, xdt, bc, da = _ssd_sample_prep(xbc, state_ssd_conv[i], dt, conv_w, conv_b, dt_bias, a_log,
                                                        expand)
            h_new, y4 = _ssd_sample_state(state_ssm[i], xdt, da, bc)
            gd = d_gmlp // n_groups_gmlp
            w0 = _row(jnp.repeat(gmlp_w_s[i][:, 0, 0], gd))
            b0 = _row(jnp.repeat(gmlp_b_s[i][:, 0], gd))
            yn, gout = _even_sample_post(y4.reshape(n_dec, d_ssd), xc, z, d_skip, norm_g, ug, vn, w0, b0)
            outs['ssm_s'].append(h_new)
            outs['conv_s'].append(newbuf)
            outs['v_s'].append(vn.reshape(n_dec, 1, d_gmlp))
            xs, a_s = _even_out(xs, yn, gout, tail_w)
        else:
            router = jnp.pad(moe_router[i], ((0, 0), (0, V7X_LANES - n_experts)))
            router_hi = router.astype(BF16)
            router_lo = (router - router_hi.astype(F32)).astype(BF16)
            tail = (conf_dw_w[i], _row(conf_dw_b[i]), _row(conf_ln_g[i]), _row(conf_ln_b[i]),
                    conf_w2[i].astype(BF16), _row(conf_b2[i]), _row(norm_ffn_odd[i]), router_hi, router_lo)
            n_keep = conf_dw_w.shape[1] - 1
            outs['conf_p'].append(a_p.reshape(batch, seq, -1)[:, seq - n_keep:])
            newbuf, c_s = _conf_sample_conv(a_s, state_conf_conv[i], tail[0], tail[1])
            outs['conf_s'].append(newbuf)
            *sample_rows, route_t_s = _conf_sample(c_s, xs, tail[2:], n_experts=n_experts)
            x_all, hn, route, route_t_p = _conf_prompt(a_p, xp, tuple(sample_rows), tail, seq=seq,
                                                       n_experts=n_experts)
            route_t = jnp.concatenate([route_t_p, route_t_s], axis=1)
            tile_expert, n_tiles, pos, pad_pos, n_rows = _moe_plan(route_t, n_experts, MOE_TILE)
            xs_sorted = _sc_dispatch_rows(hn, pos, pad_pos, n_rows)
            ys = _moe_ffn(tile_expert, n_tiles, xs_sorted, moe_w_gate[i], moe_w_up[i], moe_w_down[i], tm=MOE_TILE)
            picked = _sc_gather_rows(ys, jnp.concatenate([pos[:, :tp].reshape(-1), pos[:, tp:].reshape(-1)]))
            gf = _row(final_norm)
            xp = _moe_combine(x_all, route, gf, picked, row0=0, n_rows=tp, a_row0=0, b_row0=tp)
            xs = _moe_combine(x_all, route, gf, picked, row0=tp, n_rows=n_dec, a_row0=TOP_K * tp,
                              b_row0=TOP_K * tp + n_dec)

    y_prompt = xp.reshape(batch, seq, d_model)
    y_sample = xs.reshape(n_dec, 1, d_model)
    return (y_prompt, y_sample, jnp.stack(outs['ssm_p']), jnp.stack(outs['conv_p']), jnp.stack(outs['conf_p']),
            jnp.stack(outs['ssm_s']), jnp.stack(outs['conv_s']), jnp.stack(outs['conf_s']), jnp.stack(outs['v_s']))
```

```python
import functools

import jax
import jax.numpy as jnp
from jax import lax
from jax.experimental import pallas as pl
from jax.experimental.pallas import tpu as pltpu
from jax.experimental.pallas import tpu_sc as plsc

F32 = jnp.float32
BF16 = jnp.bfloat16
HIGHEST = lax.Precision.HIGHEST

SSD_HEAD_DIM = 64
SSD_GROUPS = 4
SSD_STATE = 128
SSD_CHUNK = 128
GMLP_CHUNK = 128
TOP_K = 2
RMS_EPS = 1e-6
LN_EPS = 1e-5

V7X_LANES = 128
V7X_SUBLANES = 8
V7X_VMEM_BYTES = 64 * 1024 * 1024
VMEM_LIMIT = (V7X_VMEM_BYTES * 7) // 8

MOE_TILE = 256
COMBINE_TILE = 512
SSD_SEQS_PER_STEP = 2
MOE_FF_CHUNK = 256
CONF_TILE = 256
CONF_ACC_ROWS = 64
SC_WINDOW_BYTES = 128 * 1024
SAMPLE_STATE_TILE = 8
CONF_SAMPLE_TILE = 32


def _params(*semantics):
    return pltpu.CompilerParams(dimension_semantics=semantics, vmem_limit_bytes=VMEM_LIMIT)


def _resident(shape):
    zeros = (0,) * len(shape)
    return pl.BlockSpec(shape, lambda *_: zeros, pipeline_mode=pl.Buffered(1))


def _whole(shape):
    zeros = (0,) * len(shape)
    return pl.BlockSpec(shape, lambda *_: zeros)


def _row_tile(n_rows, preferred):
    for t in (preferred, 512, 256, 128):
        if t <= preferred and n_rows % t == 0:
            return t
    raise ValueError(f"row count {n_rows} is not a multiple of {V7X_LANES}")


def _dot(a, b):
    return jnp.dot(a, b, preferred_element_type=F32)


def _pack_bf16_pairs(x):
    w = x.shape[1] // 2
    lo = pltpu.bitcast(x[:, :w].astype(BF16).astype(F32), jnp.uint32) >> 16
    hi = pltpu.bitcast(x[:, w:].astype(BF16).astype(F32), jnp.uint32) & jnp.uint32(0xFFFF0000)
    return hi | lo


def _unpack_bf16_pairs(u):
    lo = pltpu.bitcast(u << 16, F32)
    hi = pltpu.bitcast(u & jnp.uint32(0xFFFF0000), F32)
    return jnp.concatenate([lo, hi], axis=1)


def _rms(x, g):
    return x * lax.rsqrt(jnp.mean(x * x, axis=-1, keepdims=True) + RMS_EPS) * g


def _layernorm(x, g, b):
    mu = jnp.mean(x, axis=-1, keepdims=True)
    xc = x - mu
    var = jnp.mean(xc * xc, axis=-1, keepdims=True)
    return xc * lax.rsqrt(var + LN_EPS) * g + b


def _sigmoid(x):
    return 0.5 * jnp.tanh(0.5 * x) + 0.5


def _silu(x):
    return x * _sigmoid(x)


def _softplus(x):
    return jnp.maximum(x, 0.0) + jnp.log1p(jnp.exp(-jnp.abs(x)))


def _even_in_kernel(x_ref, g_ref, wzx_ref, wuv_ref, wdt_ref, lng_ref, lnb_ref, z_ref, xbc_ref, ug_ref, vn_ref,
                    dt_ref, *, d_ssd, conv_dim, d_gmlp):
    h = _rms(x_ref[...], g_ref[...]).astype(BF16)
    z_ref[...] = _dot(h, wzx_ref[:, 0:d_ssd])
    xbc_ref[...] = _dot(h, wzx_ref[:, d_ssd:d_ssd + conv_dim])
    ug_ref[...] = jax.nn.gelu(_dot(h, wuv_ref[:, 0:d_gmlp])).astype(ug_ref.dtype)
    v = jax.nn.gelu(_dot(h, wuv_ref[:, d_gmlp:2 * d_gmlp]))
    vn_ref[...] = _layernorm(v, lng_ref[...], lnb_ref[...])
    dt_ref[...] = _dot(h, wdt_ref[...])


def _even_in(x, g, w_zx, w_uv, w_dt, ln_g, ln_b, *, d_ssd, conv_dim, d_gmlp):
    t, d = x.shape
    tm = _row_tile(t, 512)
    row = lambda n: pl.BlockSpec((tm, n), lambda i: (i, 0))
    consts = (g, w_zx, w_uv, w_dt, ln_g, ln_b)
    return pl.pallas_call(
        functools.partial(_even_in_kernel, d_ssd=d_ssd, conv_dim=conv_dim, d_gmlp=d_gmlp),
        grid=(t // tm,),
        in_specs=[row(d)] + [_resident(c.shape) for c in consts],
        out_specs=[row(d_ssd), row(conv_dim), row(d_gmlp), row(d_gmlp), row(V7X_LANES)],
        out_shape=[jax.ShapeDtypeStruct((t, d_ssd), F32), jax.ShapeDtypeStruct((t, conv_dim), F32),
                   jax.ShapeDtypeStruct((t, d_gmlp), BF16), jax.ShapeDtypeStruct((t, d_gmlp), F32),
                   jax.ShapeDtypeStruct((t, V7X_LANES), F32)],
        compiler_params=_params("parallel"),
        name="even_in",
    )(x, *consts)


def _ssd_prompt_kernel(xbc_ref, z_ref, dt_ref, cw_ref, cb_ref, dtb_ref, alog_ref, dskip_ref, ng_ref,
                       yn_ref, st_ref, xp_ref, state_ref, xc_ref, *, d_ssd, n_taps):
    c = pl.program_id(1)

    @pl.when(c == 0)
    def _():
        state_ref[...] = jnp.zeros_like(state_ref)
        xp_ref[:, 0:V7X_SUBLANES, :] = jnp.zeros((xp_ref.shape[0], V7X_SUBLANES, xp_ref.shape[2]), F32)

    for s in range(xbc_ref.shape[1]):
        _ssd_chunk(xbc_ref.at[0, s], z_ref.at[0, s], dt_ref.at[0, s], cw_ref, cb_ref, dtb_ref, alog_ref, dskip_ref,
                   ng_ref, yn_ref.at[0, s], xp_ref.at[s], state_ref.at[s], xc_ref.at[s], d_ssd=d_ssd, n_taps=n_taps)

    @pl.when(c == pl.num_programs(1) - 1)
    def _():
        for s in range(state_ref.shape[0]):
            for pair in range(state_ref.shape[1]):
                st_ref[s, pair] = state_ref[s, pair].T


def _ssd_chunk(xbc_ref, z_ref, dt_ref, cw_ref, cb_ref, dtb_ref, alog_ref, dskip_ref, ng_ref,
               yn_ref, xp_ref, state_ref, xc_ref, *, d_ssd, n_taps):
    n_pairs = state_ref.shape[0]
    ch = SSD_CHUNK
    gn = SSD_GROUPS * SSD_STATE

    xp_ref[V7X_SUBLANES:V7X_SUBLANES + ch, :] = xbc_ref[...]
    base = V7X_SUBLANES - (n_taps - 1)
    acc = cb_ref[...] + cw_ref[n_taps - 1:n_taps, :] * xbc_ref[...]
    for k in range(n_taps - 1):
        acc = acc + cw_ref[k:k + 1, :] * xp_ref[base + k:base + k + ch, :]
    xp_ref[0:V7X_SUBLANES, :] = xp_ref[ch:ch + V7X_SUBLANES, :]
    xc_ref[...] = _silu(acc)
    xc = xc_ref

    dt = _softplus(dt_ref[...] + dtb_ref[...])
    a = dt * (-jnp.exp(alog_ref[...]))
    li = lax.broadcasted_iota(jnp.int32, (ch, ch), 0)
    si = lax.broadcasted_iota(jnp.int32, (ch, ch), 1)
    causal = li >= si
    tril = jnp.where(causal, 1.0, 0.0).astype(F32)
    acs = jnp.dot(tril, a, precision=HIGHEST, preferred_element_type=F32)
    acs_t = acs.T
    lane = lax.broadcasted_iota(jnp.int32, (ch, 2 * SSD_HEAD_DIM), 1)
    first = lane < SSD_HEAD_DIM
    first_n = lax.broadcasted_iota(jnp.int32, (SSD_STATE, 2 * SSD_HEAD_DIM), 1) < SSD_HEAD_DIM
    pairs_per_group = n_pairs // SSD_GROUPS

    for g in range(SSD_GROUPS):
        bg = xc[:, d_ssd + g * SSD_STATE:d_ssd + (g + 1) * SSD_STATE]
        cg = xc[:, d_ssd + gn + g * SSD_STATE:d_ssd + gn + (g + 1) * SSD_STATE]
        bg_t = bg.T
        cb = _dot(cg.astype(BF16), bg_t.astype(BF16))
        ys = []
        for q in range(pairs_per_group):
            pair = g * pairs_per_group + q
            h0 = 2 * pair
            lo = pair * 2 * SSD_HEAD_DIM
            xs = xc[:, lo:lo + 2 * SSD_HEAD_DIM]
            dt2 = jnp.where(first, dt[:, h0:h0 + 1], dt[:, h0 + 1:h0 + 2])
            xdt = (xs * dt2).astype(BF16)
            s_prev = state_ref[pair]
            s_prev_b = s_prev.astype(BF16)
            y2, snew2, dec2 = [], [], []
            for r in range(2):
                h = h0 + r
                col = acs[:, h:h + 1]
                row = acs_t[h:h + 1, :]
                last = acs_t[h:h + 1, ch - 1:ch]
                decay = jnp.exp(jnp.where(causal, col - row, -jnp.inf))
                y = _dot((cb * decay).astype(BF16), xdt)
                y = y + _dot((cg * jnp.exp(col)).astype(BF16), s_prev_b)
                snew = _dot((bg_t * jnp.exp(last - row)).astype(BF16), xdt)
                y2.append(y)
                snew2.append(snew)
                dec2.append(jnp.exp(last))
            y = jnp.where(first, y2[0], y2[1])
            state_ref[pair] = (s_prev * jnp.where(first_n, dec2[0], dec2[1])
                               + jnp.where(first_n, snew2[0], snew2[1]))
            ys.append(y + dskip_ref[:, lo:lo + 2 * SSD_HEAD_DIM] * xs)
        gw = pairs_per_group * 2 * SSD_HEAD_DIM
        yg = jnp.concatenate(ys, axis=-1) * _silu(z_ref[:, g * gw:(g + 1) * gw])
        yg = yg * lax.rsqrt(jnp.mean(yg * yg, axis=-1, keepdims=True) + RMS_EPS)
        yn_ref[:, g * gw:(g + 1) * gw] = (yg * ng_ref[:, g * gw:(g + 1) * gw]).astype(yn_ref.dtype)


def _ssd_prompt(xbc, z, dt, conv_w, conv_b, dt_bias, a_log, d_skip, norm_g, *, batch, seq):
    t, conv_dim = xbc.shape
    d_ssd = z.shape[1]
    n_pairs = d_ssd // (2 * SSD_HEAD_DIM)
    nc = seq // SSD_CHUNK
    n_taps = conv_w.shape[0]
    g = SSD_SEQS_PER_STEP if batch % SSD_SEQS_PER_STEP == 0 else 1
    tile = lambda n: pl.BlockSpec((1, g, SSD_CHUNK, n), lambda b, c: (b, 0, c, 0))
    by_seq = lambda a: a.reshape(batch // g, g, seq, a.shape[1])
    yn, st = pl.pallas_call(
        functools.partial(_ssd_prompt_kernel, d_ssd=d_ssd, n_taps=n_taps),
        grid=(batch // g, nc),
        in_specs=[tile(conv_dim), tile(d_ssd), tile(V7X_LANES), _resident(conv_w.shape), _resident(conv_b.shape),
                  _resident(dt_bias.shape), _resident(a_log.shape), _resident(d_skip.shape),
                  _resident(norm_g.shape)],
        out_specs=[tile(d_ssd),
                   pl.BlockSpec((g, n_pairs, 2 * SSD_HEAD_DIM, SSD_STATE), lambda b, c: (b, 0, 0, 0))],
        out_shape=[jax.ShapeDtypeStruct((batch // g, g, seq, d_ssd), BF16),
                   jax.ShapeDtypeStruct((batch, n_pairs, 2 * SSD_HEAD_DIM, SSD_STATE), F32)],
        scratch_shapes=[pltpu.VMEM((g, SSD_CHUNK + V7X_SUBLANES, conv_dim), F32),
                        pltpu.VMEM((g, n_pairs, SSD_STATE, 2 * SSD_HEAD_DIM), F32),
                        pltpu.VMEM((g, SSD_CHUNK, conv_dim), F32)],
        compiler_params=_params("parallel", "arbitrary"),
        name="ssd_prompt",
    )(by_seq(xbc), by_seq(z), by_seq(dt), conv_w, conv_b, dt_bias, a_log, d_skip, norm_g)
    return yn.reshape(t, d_ssd), st


def _gmlp_gate(ug_ref, vn_ref, ws_ref, bs_ref, out_ref):
    ch = GMLP_CHUNK
    n_groups = ws_ref.shape[0]
    gd = vn_ref.shape[1] // n_groups
    ii = lax.broadcasted_iota(jnp.int32, (ch, ch), 0)
    jj = lax.broadcasted_iota(jnp.int32, (ch, ch), 1)
    for g in range(n_groups):
        ws = jnp.where(ii >= jj, ws_ref[g], 0.0).astype(BF16)
        cols = slice(g * gd, (g + 1) * gd)
        for r0 in range(0, vn_ref.shape[0], ch):
            rows = slice(r0, r0 + ch)
            mixed = _dot(ws, vn_ref[rows, cols].astype(BF16)) + bs_ref[:, g:g + 1]
            out_ref[rows, cols] = (ug_ref[rows, cols].astype(F32) * mixed).astype(out_ref.dtype)


def _ssd_sample_prep_kernel(xbc_ref, buf_ref, dt_ref, cw_ref, cb_ref, dtb_ref, alog_ref, expand_ref,
                            newbuf_ref, xs_ref, xdt_ref, bc_ref, da_ref, *, d_ssd, n_taps):
    x_new = xbc_ref[...]
    acc = cb_ref[...] + cw_ref[n_taps - 1:n_taps, :] * x_new
    for k in range(n_taps - 1):
        acc = acc + cw_ref[k:k + 1, :] * buf_ref[k]
    newbuf_ref[0:n_taps - 2] = buf_ref[1:n_taps - 1]
    newbuf_ref[n_taps - 2] = x_new
    xc = _silu(acc)
    xs = xc[:, :d_ssd]
    dt = _softplus(dt_ref[...] + dtb_ref[...])
    da_ref[...] = jnp.exp(dt * (-jnp.exp(alog_ref[...])))
    dt_wide = jnp.dot(dt, expand_ref[...], precision=HIGHEST, preferred_element_type=F32)
    xs_ref[...] = xs
    xdt_ref[...] = xs * dt_wide
    bc_ref[...] = xc[:, d_ssd:]


def _ssd_sample_prep(xbc, buf, dt, conv_w, conv_b, dt_bias, a_log, expand):
    n, conv_dim = xbc.shape
    d_ssd = expand.shape[1]
    n_taps = conv_w.shape[0]
    args = (xbc, buf, dt, conv_w, conv_b, dt_bias, a_log, expand)
    return pl.pallas_call(
        functools.partial(_ssd_sample_prep_kernel, d_ssd=d_ssd, n_taps=n_taps),
        grid=(1,),
        in_specs=[_resident(a.shape) for a in args],
        out_specs=[_whole(buf.shape), _whole((n, d_ssd)), _whole((n, d_ssd)),
                   _whole((n, conv_dim - d_ssd)), _whole((n, V7X_LANES))],
        out_shape=[jax.ShapeDtypeStruct(buf.shape, F32), jax.ShapeDtypeStruct((n, d_ssd), F32),
                   jax.ShapeDtypeStruct((n, d_ssd), F32), jax.ShapeDtypeStruct((n, conv_dim - d_ssd), F32),
                   jax.ShapeDtypeStruct((n, V7X_LANES), F32)],
        compiler_params=_params("arbitrary"),
        name="ssd_sample_prep",
    )(*args)


def _ssd_sample_state_kernel(h_ref, xdt_ref, da_ref, bc_ref, hnew_ref, y_ref, xcol_ref, ccol_ref):
    bs, n_heads, p, s = h_ref.shape
    n_pairs = n_heads // 2
    pairs_per_group = n_pairs // SSD_GROUPS
    gn = SSD_GROUPS * s
    w = 2 * p
    eye = lax.broadcasted_iota(jnp.int32, (w, w), 0) == lax.broadcasted_iota(jnp.int32, (w, w), 1)
    ones = jnp.ones((w, V7X_LANES), BF16)

    def stacked_diag(v):
        return jnp.concatenate(
            [jnp.where(eye, jnp.broadcast_to(v[j:j + 1], (w, w)), 0.0) for j in range(bs)], axis=0).astype(BF16)

    for q in range(n_pairs):
        x = xdt_ref[:, q * w:(q + 1) * w]
        hi = x.astype(BF16).astype(F32)
        mid = (x - hi).astype(BF16).astype(F32)
        lo = (x - hi) - mid
        xcol_ref[q] = _dot(stacked_diag(hi), ones) + (_dot(stacked_diag(mid), ones) + _dot(stacked_diag(lo), ones))
    for g in range(SSD_GROUPS):
        ccol_ref[g] = _dot(stacked_diag(bc_ref[:, gn + g * s:gn + (g + 1) * s]), ones).astype(BF16)

    for j in range(bs):
        for g in range(SSD_GROUPS):
            brow = bc_ref[j:j + 1, g * s:(g + 1) * s]
            h_new = []
            for q in range(g * pairs_per_group, (g + 1) * pairs_per_group):
                for r in range(2):
                    h = 2 * q + r
                    xcol = xcol_ref[q, j * w + r * p:j * w + (r + 1) * p, :]
                    h_new.append(h_ref[j, h] * da_ref[j:j + 1, h:h + 1] + xcol * brow)
                    hnew_ref[j, h] = h_new[-1]
            hc = _dot(jnp.concatenate(h_new, axis=0).astype(BF16), ccol_ref[g, j * s:(j + 1) * s, :])
            for i in range(pairs_per_group):
                y = jnp.sum(jnp.where(eye, hc[i * w:(i + 1) * w], 0.0), axis=0, keepdims=True)
                q = g * pairs_per_group + i
                y_ref[j, q:q + 1, :] = y


def _ssd_sample_state(h, xdt, da, bc):
    n, n_heads, p, s = h.shape
    assert 2 * p == V7X_LANES and s == V7X_LANES
    bs = SAMPLE_STATE_TILE
    rows = lambda w: pl.BlockSpec((bs, w), lambda i: (i, 0))
    return pl.pallas_call(
        _ssd_sample_state_kernel,
        grid=(n // bs,),
        in_specs=[pl.BlockSpec((bs, n_heads, p, s), lambda i: (i, 0, 0, 0)), rows(xdt.shape[1]), rows(da.shape[1]),
                  rows(bc.shape[1])],
        out_specs=[pl.BlockSpec((bs, n_heads, p, s), lambda i: (i, 0, 0, 0)),
                   pl.BlockSpec((bs, n_heads // 2, 2 * p), lambda i: (i, 0, 0))],
        out_shape=[jax.ShapeDtypeStruct(h.shape, F32), jax.ShapeDtypeStruct((n, n_heads // 2, 2 * p), F32)],
        scratch_shapes=[pltpu.VMEM((n_heads // 2, bs * 2 * p, V7X_LANES), F32),
                        pltpu.VMEM((SSD_GROUPS, bs * s, V7X_LANES), BF16)],
        compiler_params=_params("parallel"),
        name="ssd_sample_state",
    )(h, xdt, da, bc)


def _even_sample_post_kernel(y_ref, xs_ref, z_ref, dskip_ref, ng_ref, ug_ref, vn_ref, w0_ref, b0_ref,
                             yn_ref, gout_ref):
    d_ssd = y_ref.shape[1]
    gw = d_ssd // SSD_GROUPS
    y = (y_ref[...] + dskip_ref[...] * xs_ref[...]) * _silu(z_ref[...])
    for g in range(SSD_GROUPS):
        yg = y[:, g * gw:(g + 1) * gw]
        yg = yg * lax.rsqrt(jnp.mean(yg * yg, axis=-1, keepdims=True) + RMS_EPS)
        yn_ref[:, g * gw:(g + 1) * gw] = (yg * ng_ref[:, g * gw:(g + 1) * gw]).astype(yn_ref.dtype)
    mixed = w0_ref[...] * vn_ref[...] + b0_ref[...]
    gout_ref[...] = (ug_ref[...].astype(F32) * mixed).astype(gout_ref.dtype)


def _even_sample_post(y, xs, z, d_skip, norm_g, ug, vn, w0, b0):
    args = (y, xs, z, d_skip, norm_g, ug, vn, w0, b0)
    return pl.pallas_call(
        _even_sample_post_kernel,
        grid=(1,),
        in_specs=[_resident(a.shape) for a in args],
        out_specs=[_whole(y.shape), _whole(vn.shape)],
        out_shape=[jax.ShapeDtypeStruct(y.shape, BF16), jax.ShapeDtypeStruct(vn.shape, BF16)],
        compiler_params=_params("arbitrary"),
        name="even_sample_post",
    )(*args)


def _even_out_kernel(*refs, fuse_gmlp):
    if fuse_gmlp:
        (x_ref, yn_ref, ug_ref, vn_ref, ws_ref, bs_ref, wo_ref, g_ref, wg_ref, wu_ref, wd_ref, gc_ref, w1_ref,
         b1_ref, out_ref, a_ref, gout_ref) = refs
        _gmlp_gate(ug_ref, vn_ref, ws_ref, bs_ref, gout_ref)
    else:
        (x_ref, yn_ref, gout_ref, wo_ref, g_ref, wg_ref, wu_ref, wd_ref, gc_ref, w1_ref, b1_ref,
         out_ref, a_ref) = refs
    d_ssd = yn_ref.shape[1]
    x = x_ref[...] + _dot(yn_ref[...], wo_ref[0:d_ssd, :]) + _dot(gout_ref[...], wo_ref[d_ssd:, :])
    h = _rms(x, g_ref[...]).astype(BF16)
    mid = (_silu(_dot(h, wg_ref[...])) * _dot(h, wu_ref[...])).astype(BF16)
    x = x + _dot(mid, wd_ref[...])
    out_ref[...] = x
    dc = a_ref.shape[1]
    h = _rms(x, gc_ref[...]).astype(BF16)
    val = _dot(h, w1_ref[:, 0:dc]) + b1_ref[:, 0:dc]
    gate = _dot(h, w1_ref[:, dc:]) + b1_ref[:, dc:]
    a_ref[...] = val * _sigmoid(gate)


def _even_out(x, yn, gate_in, weights, *, seq=None):
    t, d = x.shape
    dc = weights[-2].shape[1] // 2
    fuse_gmlp = isinstance(gate_in, tuple)
    tm = _row_tile(t if seq is None else seq, 256)
    row = lambda n: pl.BlockSpec((tm, n), lambda i: (i, 0))
    if fuse_gmlp:
        ug, vn, w_s, b_s_t = gate_in
        assert tm % GMLP_CHUNK == 0
        gate_args = (ug, vn, w_s, b_s_t)
        gate_specs = [row(ug.shape[1]), row(vn.shape[1]), _resident(w_s.shape), _resident(b_s_t.shape)]
        scratch = [pltpu.VMEM((tm, ug.shape[1]), BF16)]
    else:
        gate_args, gate_specs, scratch = (gate_in,), [row(gate_in.shape[1])], []
    return pl.pallas_call(
        functools.partial(_even_out_kernel, fuse_gmlp=fuse_gmlp),
        grid=(t // tm,),
        in_specs=[row(d), row(yn.shape[1])] + gate_specs + [_resident(w.shape) for w in weights],
        out_specs=[row(d), row(dc)],
        out_shape=[jax.ShapeDtypeStruct((t, d), F32), jax.ShapeDtypeStruct((t, dc), F32)],
        scratch_shapes=scratch,
        compiler_params=_params("parallel"),
        name="even_out_ffn",
    )(x, yn, *gate_args, *weights)


def _route(logits, n_experts, route_ref, route_t_ref):
    lane = lax.broadcasted_iota(jnp.int32, logits.shape, 1)
    lane_f = lane.astype(F32)
    big = jnp.float32(V7X_LANES)
    lg = jnp.where(lane < n_experts, logits, -jnp.inf)
    m1 = jnp.max(lg, axis=-1, keepdims=True)
    i1 = jnp.min(jnp.where(lg == m1, lane_f, big), axis=-1, keepdims=True)
    lg2 = jnp.where(lane_f == i1, -jnp.inf, lg)
    m2 = jnp.max(lg2, axis=-1, keepdims=True)
    i2 = jnp.min(jnp.where(lg2 == m2, lane_f, big), axis=-1, keepdims=True)
    e2 = jnp.exp(m2 - m1)
    g1 = 1.0 / (1.0 + e2)
    g2 = e2 / (1.0 + e2)
    route = jnp.where(lane == 0, i1, jnp.where(lane == 1, i2, jnp.where(lane == 2, g1,
                      jnp.where(lane == 3, g2, 0.0))))
    route_ref[...] = route
    route_t_ref[...] = route.T[0:V7X_SUBLANES, :]


def _conf_finish(c, x, lng_ref, lnb_ref, w2_ref, b2_ref, ng_ref, rhi_ref, rlo_ref, out_ref, hn_ref, route_ref,
                 route_t_ref, *, n_experts):
    hmid = _silu(_layernorm(c, lng_ref[...], lnb_ref[...])).astype(BF16)
    x_new = x + _dot(hmid, w2_ref[...]) + b2_ref[...]
    out_ref[...] = x_new
    hn = _rms(x_new, ng_ref[...])
    hn_ref[...] = _pack_bf16_pairs(hn)
    hi = hn.astype(BF16)
    lo = (hn - hi.astype(F32)).astype(BF16)
    logits = _dot(hi, rhi_ref[...]) + (_dot(lo, rhi_ref[...]) + _dot(hi, rlo_ref[...]))
    _route(logits, n_experts, route_ref, route_t_ref)


def _conf_prompt_kernel(a_ref, x_ref, xs_ref, hns_ref, routes_ref, dw_ref, db_ref, lng_ref, lnb_ref, w2_ref, b2_ref,
                        ng_ref, rhi_ref, rlo_ref, out_ref, hn_ref, route_ref, route_t_ref, xp_ref, c_ref, win_ref,
                        *, n_taps, n_experts, tiles_per_seq, n_prompt_tiles):
    tl, d = a_ref.shape
    halo = xp_ref.shape[0] - tl
    base = halo - (n_taps - 1)
    rows = CONF_ACC_ROWS
    step = pl.program_id(0)

    @pl.when(step < n_prompt_tiles)
    def _():
        @pl.when(step % tiles_per_seq == 0)
        def _():
            xp_ref[0:halo, :] = jnp.zeros((halo, d), F32)

        xp_ref[halo:halo + tl, :] = a_ref[...]

        def lane_block(j, carry):
            cols = pl.ds(pl.multiple_of(j * V7X_LANES, V7X_LANES), V7X_LANES)
            for r0 in range(0, tl, rows):
                acc = jnp.broadcast_to(db_ref[:, cols], (rows, V7X_LANES))
                for phase in range(V7X_SUBLANES):
                    taps = [k for k in range(n_taps) if (base + k) % V7X_SUBLANES == phase]
                    if not taps:
                        continue
                    span = max(base + k - phase for k in taps) + rows
                    if phase:
                        win_ref[0:span, :] = xp_ref[pl.ds(r0 + phase, span), cols]
                    for k in taps:
                        o = base + k - phase
                        win = win_ref[o:o + rows, :] if phase else xp_ref[pl.ds(r0 + o, rows), cols]
                        acc = acc + dw_ref[k:k + 1, cols] * win
                c_ref[r0:r0 + rows, cols] = acc
            return carry

        lax.fori_loop(0, d // V7X_LANES, lane_block, 0)
        xp_ref[0:halo, :] = xp_ref[tl:tl + halo, :]
        _conf_finish(c_ref[...], x_ref[...], lng_ref, lnb_ref, w2_ref, b2_ref, ng_ref, rhi_ref, rlo_ref,
                     out_ref, hn_ref, route_ref, route_t_ref, n_experts=n_experts)

    @pl.when(step == n_prompt_tiles)
    def _():
        ns = xs_ref.shape[0]
        out_ref[0:ns, :] = xs_ref[...]
        hn_ref[0:ns, :] = hns_ref[...]
        route_ref[0:ns, :] = routes_ref[...]


def _conf_prompt(a, x, sample_rows, consts, *, seq, n_experts):
    t, d = a.shape
    n_s = sample_rows[0].shape[0]
    tl = _row_tile(seq, CONF_TILE)
    assert tl % CONF_ACC_ROWS == 0 and n_s <= tl
    n_prompt_tiles = t // tl
    n_taps = consts[0].shape[0]
    halo = -(-(n_taps - 1) // V7X_SUBLANES) * V7X_SUBLANES
    in_tile = lambda n: pl.BlockSpec((tl, n), lambda s: (jnp.minimum(s, n_prompt_tiles - 1), 0))
    out_tile = lambda n: pl.BlockSpec((tl, n), lambda s: (s, 0))
    small = sample_rows + tuple(consts)
    return pl.pallas_call(
        functools.partial(_conf_prompt_kernel, n_taps=n_taps, n_experts=n_experts, tiles_per_seq=seq // tl,
                          n_prompt_tiles=n_prompt_tiles),
        grid=(n_prompt_tiles + 1,),
        in_specs=[in_tile(d), in_tile(d)] + [_resident(c.shape) for c in small],
        out_specs=[out_tile(d), out_tile(d // 2), out_tile(V7X_LANES),
                   pl.BlockSpec((V7X_SUBLANES, tl), lambda s: (0, jnp.minimum(s, n_prompt_tiles - 1)))],
        out_shape=[jax.ShapeDtypeStruct((t + n_s, d), F32), jax.ShapeDtypeStruct((t + n_s, d // 2), jnp.uint32),
                   jax.ShapeDtypeStruct((t + n_s, V7X_LANES), F32), jax.ShapeDtypeStruct((V7X_SUBLANES, t), F32)],
        scratch_shapes=[pltpu.VMEM((tl + halo, d), F32), pltpu.VMEM((tl, d), F32),
                        pltpu.VMEM((CONF_ACC_ROWS + halo, V7X_LANES), F32)],
        compiler_params=_params("arbitrary"),
        name="conf_prompt",
    )(a, x, *small)


def _conf_sample_conv_kernel(a_ref, buf_ref, dw_ref, db_ref, newbuf_ref, c_ref, *, n_taps):
    a_new = a_ref[...]
    acc = db_ref[...] + dw_ref[n_taps - 1:n_taps, :] * a_new
    for k in range(n_taps - 1):
        acc = acc + dw_ref[k:k + 1, :] * buf_ref[k]
    c_ref[...] = acc
    newbuf_ref[0:n_taps - 2] = buf_ref[1:n_taps - 1]
    newbuf_ref[n_taps - 2] = a_new


def _conf_sample_conv(a, buf, dw_w, dw_b):
    n, d = a.shape
    bs = CONF_SAMPLE_TILE
    hist = pl.BlockSpec((buf.shape[0], bs, d), lambda i: (0, i, 0))
    rows = pl.BlockSpec((bs, d), lambda i: (i, 0))
    return pl.pallas_call(
        functools.partial(_conf_sample_conv_kernel, n_taps=dw_w.shape[0]),
        grid=(n // bs,),
        in_specs=[rows, hist, _resident(dw_w.shape), _resident(dw_b.shape)],
        out_specs=[hist, rows],
        out_shape=[jax.ShapeDtypeStruct(buf.shape, F32), jax.ShapeDtypeStruct((n, d), F32)],
        compiler_params=_params("parallel"),
        name="conf_sample_conv",
    )(a, buf, dw_w, dw_b)


def _conf_sample_kernel(c_ref, x_ref, lng_ref, lnb_ref, w2_ref, b2_ref, ng_ref, rhi_ref, rlo_ref,
                        out_ref, hn_ref, route_ref, route_t_ref, *, n_experts):
    _conf_finish(c_ref[...], x_ref[...], lng_ref, lnb_ref, w2_ref, b2_ref, ng_ref, rhi_ref, rlo_ref,
                 out_ref, hn_ref, route_ref, route_t_ref, n_experts=n_experts)


def _conf_sample(c, x, consts, *, n_experts):
    n, d = c.shape
    args = (c, x) + tuple(consts)
    return pl.pallas_call(
        functools.partial(_conf_sample_kernel, n_experts=n_experts),
        grid=(1,),
        in_specs=[_resident(v.shape) for v in args],
        out_specs=[_whole((n, d)), _whole((n, d // 2)), _whole((n, V7X_LANES)), _whole((V7X_SUBLANES, n))],
        out_shape=[jax.ShapeDtypeStruct((n, d), F32), jax.ShapeDtypeStruct((n, d // 2), jnp.uint32),
                   jax.ShapeDtypeStruct((n, V7X_LANES), F32), jax.ShapeDtypeStruct((V7X_SUBLANES, n), F32)],
        compiler_params=_params("arbitrary"),
        name="conf_sample",
    )(*args)


def _sc_window_rows(x):
    assert x.dtype.itemsize == 4, "the SparseCore indirect copies move 32-bit elements"
    return SC_WINDOW_BYTES // (x.shape[1] * x.dtype.itemsize)


def _sc_gather_rows(x, idx):
    n, d = idx.shape[0], x.shape[1]
    w = _sc_window_rows(x)
    assert n % w == 0
    mesh = plsc.VectorSubcoreMesh(core_axis_name="core", subcore_axis_name="subcore")

    @functools.partial(pl.kernel, out_type=jax.ShapeDtypeStruct((n, d), x.dtype), mesh=mesh, scratch_types=[])
    def gather(x_hbm, idx_hbm, out_hbm):
        def body(idx_vmem, out_vmem):
            pltpu.sync_copy(x_hbm.at[idx_vmem.at[0]], out_vmem)

        pltpu.emit_pipeline(
            body,
            grid=(n // w,),
            in_specs=[pl.BlockSpec((1, w), lambda i: (i, 0))],
            out_specs=[pl.BlockSpec((w, d), lambda i: (i, 0))],
            core_axis_name=("core", "subcore"),
            dimension_semantics=(pltpu.PARALLEL,),
        )(idx_hbm, out_hbm)

    return gather(x, idx.reshape(n // w, w))


def _sc_dispatch_rows(x, pos, pad_pos, n_out):
    t, d = x.shape
    w = _sc_window_rows(x)
    n_pad = pad_pos.shape[0]
    assert t % w == 0 and n_pad % w == 0 and TOP_K * t + n_pad == n_out
    mesh = plsc.VectorSubcoreMesh(core_axis_name="core", subcore_axis_name="subcore")

    @functools.partial(pl.kernel, out_type=jax.ShapeDtypeStruct((n_out, d), x.dtype), mesh=mesh, scratch_types=[])
    def dispatch(x_hbm, pos_hbm, pad_hbm, zero_hbm, out_hbm):
        def body(rows_vmem, idx_vmem):
            pltpu.sync_copy(rows_vmem, out_hbm.at[idx_vmem.at[0]])

        def scatter(src_hbm, idx_hbm, src_map):
            pltpu.emit_pipeline(
                body,
                grid=(idx_hbm.shape[0],),
                in_specs=[pl.BlockSpec((w, d), src_map), pl.BlockSpec((1, w), lambda i: (i, 0))],
                out_specs=[],
                core_axis_name=("core", "subcore"),
                dimension_semantics=(pltpu.PARALLEL,),
            )(src_hbm, idx_hbm)

        for k in range(TOP_K):
            scatter(x_hbm, pos_hbm.at[k], lambda i: (i, 0))
        scatter(zero_hbm, pad_hbm, lambda i: (0, 0))

    return dispatch(x, pos.reshape(TOP_K, t // w, w), pad_pos.reshape(n_pad // w, w), jnp.zeros((w, d), x.dtype))


def _moe_ffn_kernel(tile_expert_ref, first_ref, bank_ref, next_expert_ref, n_tiles_ref, x_ref, wg_hbm, wu_hbm,
                    wd_hbm, out_ref, wg_v, wu_v, wd_v, stage_g, stage_u, stage_d, acc_ref, sems):
    i = pl.program_id(0)
    dff = wg_v.shape[2]
    ck = stage_g.shape[2]
    n_chunks = dff // ck
    active = i < n_tiles_ref[0]
    starts_run = active & (first_ref[i] == 1)
    bank = bank_ref[i]
    nxt = next_expert_ref[i]

    def chunk_copies(expert, c, slot):
        cols = slice(c * ck, (c + 1) * ck)
        return (pltpu.make_async_copy(wg_hbm.at[expert, :, cols], stage_g.at[slot], sems.at[0, slot]),
                pltpu.make_async_copy(wu_hbm.at[expert, :, cols], stage_u.at[slot], sems.at[1, slot]),
                pltpu.make_async_copy(wd_hbm.at[expert, cols, :], stage_d.at[slot], sems.at[2, slot]))

    def stream_weights(expert, dst_bank, between_chunks):
        for cp in chunk_copies(expert, 0, 0):
            cp.start()
        for c in range(n_chunks):
            slot = c % 2
            if c + 1 < n_chunks:
                for cp in chunk_copies(expert, c + 1, 1 - slot):
                    cp.start()
            for cp in chunk_copies(expert, c, slot):
                cp.wait()
            cols = slice(c * ck, (c + 1) * ck)
            wg_v[dst_bank, :, cols] = stage_g[slot].astype(BF16)
            wu_v[dst_bank, :, cols] = stage_u[slot].astype(BF16)
            wd_v[dst_bank, cols, :] = stage_d[slot].astype(BF16)
            between_chunks(c)

    def ffn_part(xb, cols):
        mid = (_silu(_dot(xb, wg_v[bank, :, cols])) * _dot(xb, wu_v[bank, :, cols])).astype(BF16)
        return _dot(mid, wd_v[bank, cols, :])

    @pl.when(starts_run & (i == 0))
    def _():
        stream_weights(tile_expert_ref[i], bank, lambda c: None)

    overlapped = starts_run & (nxt >= 0)

    @pl.when(overlapped)
    def _():
        xb = _unpack_bf16_pairs(x_ref[...]).astype(BF16)
        acc_ref[...] = jnp.zeros_like(acc_ref)

        def piece(c):
            acc_ref[...] += ffn_part(xb, slice(c * ck, (c + 1) * ck))

        stream_weights(nxt, 1 - bank, piece)
        out_ref[...] = _pack_bf16_pairs(acc_ref[...])

    @pl.when(active & jnp.logical_not(overlapped))
    def _():
        xb = _unpack_bf16_pairs(x_ref[...]).astype(BF16)
        half = dff // 2
        acc = ffn_part(xb, slice(0, half)) + ffn_part(xb, slice(half, dff))
        out_ref[...] = _pack_bf16_pairs(acc)

    @pl.when(jnp.logical_not(active))
    def _():
        out_ref[...] = jnp.zeros_like(out_ref)


def _moe_ffn(tile_expert, n_tiles, xs, wg, wu, wd, *, tm):
    p, d = xs.shape[0], wg.shape[1]
    dff = wg.shape[2]
    ck = MOE_FF_CHUNK
    assert dff % ck == 0
    nt = p // tm
    i32 = jnp.int32
    tile = jnp.arange(nt, dtype=i32)
    first = jnp.concatenate([jnp.ones((1,), i32), (tile_expert[1:] != tile_expert[:-1]).astype(i32)])
    bank = (jnp.cumsum(first) - 1) % 2
    run_start = jnp.where((first == 1) & (tile < n_tiles[0]), tile, nt)
    next_start = jnp.concatenate([lax.cummin(run_start, reverse=True)[1:], jnp.full((1,), nt, i32)])
    next_expert = jnp.sum((next_start[:, None] == tile[None, :]).astype(i32) * tile_expert[None, :], axis=1)
    next_expert = jnp.where(next_start < nt, next_expert, -1)
    any_spec = pl.BlockSpec(memory_space=pl.ANY)
    row_tile = pl.BlockSpec((tm, d // 2), lambda i, *_: (i, 0))
    grid_spec = pltpu.PrefetchScalarGridSpec(
        num_scalar_prefetch=5,
        grid=(nt,),
        in_specs=[row_tile, any_spec, any_spec, any_spec],
        out_specs=row_tile,
        scratch_shapes=[pltpu.VMEM((2, d, dff), BF16), pltpu.VMEM((2, d, dff), BF16), pltpu.VMEM((2, dff, d), BF16),
                        pltpu.VMEM((2, d, ck), F32), pltpu.VMEM((2, d, ck), F32), pltpu.VMEM((2, ck, d), F32),
                        pltpu.VMEM((tm, d), F32), pltpu.SemaphoreType.DMA((3, 2))],
    )
    return pl.pallas_call(
        _moe_ffn_kernel,
        grid_spec=grid_spec,
        out_shape=jax.ShapeDtypeStruct((p, d // 2), jnp.uint32),
        compiler_params=_params("arbitrary"),
        name="moe_ffn",
    )(tile_expert, first, bank.astype(i32), next_expert.astype(i32), n_tiles, xs, wg, wu, wd)


def _moe_combine_kernel(x_ref, route_ref, gf_ref, a_ref, b_ref, y_ref):
    route = route_ref[...]
    x = (x_ref[...] + route[:, 2:3] * _unpack_bf16_pairs(a_ref[...])
         + route[:, 3:4] * _unpack_bf16_pairs(b_ref[...]))
    y_ref[...] = _rms(x, gf_ref[...])


def _moe_combine(x, route, gf, picked, *, row0, n_rows, a_row0, b_row0):
    d = x.shape[1]
    tc = _row_tile(n_rows, COMBINE_TILE)
    assert row0 % tc == 0 and a_row0 % tc == 0 and b_row0 % tc == 0
    rows = lambda w, r0: pl.BlockSpec((tc, w), lambda i: (i + r0 // tc, 0))
    return pl.pallas_call(
        _moe_combine_kernel,
        grid=(n_rows // tc,),
        in_specs=[rows(d, row0), rows(V7X_LANES, row0), _resident(gf.shape), rows(d // 2, a_row0),
                  rows(d // 2, b_row0)],
        out_specs=pl.BlockSpec((tc, d), lambda i: (i, 0)),
        out_shape=jax.ShapeDtypeStruct((n_rows, d), F32),
        compiler_params=_params("parallel"),
        name="moe_combine",
    )(x, route, gf, picked, picked)


def _moe_plan(route_t, n_experts, tm):
    t = route_t.shape[1]
    n_slots = t * TOP_K
    i32 = jnp.int32
    expert = route_t[:TOP_K].astype(i32).reshape(1, n_slots)
    onehot = (expert == jnp.arange(n_experts, dtype=i32)[:, None]).astype(i32)
    counts = jnp.sum(onehot, axis=1)
    tiles = (counts + tm - 1) // tm
    tile_end = jnp.cumsum(tiles)
    start = (tile_end - tiles) * tm
    rank = jnp.cumsum(onehot, axis=1) - onehot
    pos = jnp.sum(onehot * (rank + start[:, None]), axis=0)
    nt = -(-n_slots // tm) + n_experts
    n_tiles = tile_end[-1]
    tile_id = jnp.minimum(jnp.arange(nt, dtype=i32), n_tiles - 1)
    tile_expert = jnp.sum((tile_id[:, None] >= tile_end[None, :]).astype(i32), axis=1)
    seg_start = jnp.concatenate([start + counts, (n_tiles * tm)[None]])
    seg_len = jnp.concatenate([tiles * tm - counts, (nt * tm - n_tiles * tm)[None]])
    seg_end = jnp.cumsum(seg_len)
    q = jnp.arange(nt * tm - n_slots, dtype=i32)
    in_seg = (q[:, None] >= (seg_end - seg_len)[None, :]) & (q[:, None] < seg_end[None, :])
    pad_pos = jnp.sum(in_seg.astype(i32) * (seg_start - (seg_end - seg_len))[None, :], axis=1) + q
    return (tile_expert.astype(i32), n_tiles.astype(i32).reshape(1), pos.reshape(TOP_K, t).astype(i32),
            pad_pos.astype(i32), nt * tm)


def _row(v):
    return v.reshape(1, -1).astype(F32)


def _pad_lanes(v):
    return jnp.pad(_row(v), ((0, 0), (0, V7X_LANES - v.shape[-1])))


def kernel(x_prompt, x_sample, state_ssm, state_ssd_conv, state_conf_conv, norm_mix_even, w_in_even, ssd_conv_w, ssd_conv_b, ssd_dt_bias, ssd_a_log, ssd_d, ssd_norm, gmlp_ln_g, gmlp_ln_b, gmlp_w_s, gmlp_b_s, w_out_even, norm_ffn_even, ffn_w_gate, ffn_w_up, ffn_w_down, norm_mix_odd, conf_w1, conf_b1, conf_dw_w, conf_dw_b, conf_ln_g, conf_ln_b, conf_w2, conf_b2, norm_ffn_odd, moe_router, moe_w_gate, moe_w_up, moe_w_down, final_norm):
    batch, seq, d_model = x_prompt.shape
    n_dec, dec_seq, _ = x_sample.shape
    assert dec_seq == 1, "the sample group advances one token per sequence"
    assert seq % SSD_CHUNK == 0 and n_dec % V7X_LANES == 0
    n_even, n_odd = w_in_even.shape[0], conf_w1.shape[0]
    assert (n_even, n_odd) == (1, 1), "the final norm is fused into the last (odd) layer's MoE combine"
    n_heads = ssd_dt_bias.shape[1]
    d_ssd = n_heads * SSD_HEAD_DIM
    conv_dim = ssd_conv_w.shape[2]
    d_gmlp = gmlp_ln_g.shape[1]
    n_groups_gmlp = gmlp_w_s.shape[1]
    n_experts = moe_router.shape[2]
    tp = batch * seq

    xp = x_prompt.reshape(tp, d_model)
    xs = x_sample.reshape(n_dec, d_model)
    outs = dict(ssm_p=[], conv_p=[], conf_p=[], ssm_s=[], conv_s=[], conf_s=[], v_s=[])

    for layer in range(n_even + n_odd):
        i = layer // 2
        if layer % 2 == 0:
            w = w_in_even[i]
            o1, o2, o3 = d_ssd, d_ssd + conv_dim, d_ssd + conv_dim + n_heads
            w_parts = (w[:, :o2].astype(BF16), w[:, o3:].astype(BF16),
                       jnp.pad(w[:, o2:o3], ((0, 0), (0, V7X_LANES - n_heads))).astype(BF16))
            in_args = (_row(norm_mix_even[i]), *w_parts, _row(gmlp_ln_g[i]), _row(gmlp_ln_b[i]))
            dims = dict(d_ssd=d_ssd, conv_dim=conv_dim, d_gmlp=d_gmlp)
            dt_bias, a_log = _pad_lanes(ssd_dt_bias[i]), _pad_lanes(ssd_a_log[i])
            d_skip = _row(jnp.repeat(ssd_d[i], SSD_HEAD_DIM))
            norm_g = _row(ssd_norm[i])
            conv_w, conv_b = ssd_conv_w[i], _row(ssd_conv_b[i])
            z, xbc, ug, vn, dt = _even_in(xp, *in_args, **dims)
            yn, st = _ssd_prompt(xbc, z, dt, conv_w, conv_b, dt_bias, a_log, d_skip, norm_g, batch=batch, seq=seq)
            n_pairs = n_heads // 2
            outs['ssm_p'].append(st.reshape(batch, n_heads, SSD_HEAD_DIM, SSD_STATE))
            outs['conv_p'].append(xbc.reshape(batch, seq, conv_dim)[:, seq - (conv_w.shape[0] - 1):])
            tail_w = (w_out_even[i].astype(BF16), _row(norm_ffn_even[i]), ffn_w_gate[i].astype(BF16),
                      ffn_w_up[i].astype(BF16), ffn_w_down[i].astype(BF16), _row(norm_mix_odd[i]),
                      conf_w1[i].astype(BF16), _row(conf_b1[i]))
            xp, a_p = _even_out(xp, yn, (ug, vn, gmlp_w_s[i], gmlp_b_s[i].T), tail_w, seq=seq)
            z, xbc, ug, vn, dt = _even_in(xs, *in_args, **dims)
            expand = (jnp.arange(d_ssd)[None, :] // SSD_HEAD_DIM == jnp.arange(V7X_LANES)[:, None]).astype(F32)
            newbuf, xc, xdt, bc, da = _ssd_sample_prep(xbc, state_ssd_conv[i].transpose(1, 0, 2), dt, conv_w, conv_b,
                                                        dt_bias, a_log, expand)
            h_new, y4 = _ssd_sample_state(state_ssm[i], xdt, da, bc)
            gd = d_gmlp // n_groups_gmlp
            w0 = _row(jnp.repeat(gmlp_w_s[i][:, 0, 0], gd))
            b0 = _row(jnp.repeat(gmlp_b_s[i][:, 0], gd))
            yn, gout = _even_sample_post(y4.reshape(n_dec, d_ssd), xc, z, d_skip, norm_g, ug, vn, w0, b0)
            outs['ssm_s'].append(h_new)
            outs['conv_s'].append(newbuf.transpose(1, 0, 2))
            outs['v_s'].append(vn.reshape(n_dec, 1, d_gmlp))
            xs, a_s = _even_out(xs, yn, gout, tail_w)
        else:
            router = jnp.pad(moe_router[i], ((0, 0), (0, V7X_LANES - n_experts)))
            router_hi = router.astype(BF16)
            router_lo = (router - router_hi.astype(F32)).astype(BF16)
            tail = (conf_dw_w[i], _row(conf_dw_b[i]), _row(conf_ln_g[i]), _row(conf_ln_b[i]),
                    conf_w2[i].astype(BF16), _row(conf_b2[i]), _row(norm_ffn_odd[i]), router_hi, router_lo)
            n_keep = conf_dw_w.shape[1] - 1
            outs['conf_p'].append(a_p.reshape(batch, seq, -1)[:, seq - n_keep:])
            newbuf, c_s = _conf_sample_conv(a_s, state_conf_conv[i].transpose(1, 0, 2), tail[0], tail[1])
            outs['conf_s'].append(newbuf.transpose(1, 0, 2))
            *sample_rows, route_t_s = _conf_sample(c_s, xs, tail[2:], n_experts=n_experts)
            x_all, hn, route, route_t_p = _conf_prompt(a_p, xp, tuple(sample_rows), tail, seq=seq,
                                                       n_experts=n_experts)
            route_t = jnp.concatenate([route_t_p, route_t_s], axis=1)
            tile_expert, n_tiles, pos, pad_pos, n_rows = _moe_plan(route_t, n_experts, MOE_TILE)
            xs_sorted = _sc_dispatch_rows(hn, pos, pad_pos, n_rows)
            ys = _moe_ffn(tile_expert, n_tiles, xs_sorted, moe_w_gate[i], moe_w_up[i], moe_w_down[i], tm=MOE_TILE)
            picked = _sc_gather_rows(ys, jnp.concatenate([pos[:, :tp].reshape(-1), pos[:, tp:].reshape(-1)]))
            gf = _row(final_norm)
            xp = _moe_combine(x_all, route, gf, picked, row0=0, n_rows=tp, a_row0=0, b_row0=tp)
            xs = _moe_combine(x_all, route, gf, picked, row0=tp, n_rows=n_dec, a_row0=TOP_K * tp,
                              b_row0=TOP_K * tp + n_dec)

    y_prompt = xp.reshape(batch, seq, d_model)
    y_sample = xs.reshape(n_dec, 1, d_model)
    return (y_prompt, y_sample, jnp.stack(outs['ssm_p']), jnp.stack(outs['conv_p']), jnp.stack(outs['conf_p']),
            jnp.stack(outs['ssm_s']), jnp.stack(outs['conv_s']), jnp.stack(outs['conf_s']), jnp.stack(outs['v_s']))
```

```python
import functools

import jax
import jax.numpy as jnp
from jax import lax
from jax.experimental import pallas as pl
from jax.experimental.pallas import tpu as pltpu
from jax.experimental.pallas import tpu_sc as plsc

F32 = jnp.float32
BF16 = jnp.bfloat16
HIGHEST = lax.Precision.HIGHEST

SSD_HEAD_DIM = 64
SSD_GROUPS = 4
SSD_STATE = 128
SSD_CHUNK = 128
GMLP_CHUNK = 128
TOP_K = 2
RMS_EPS = 1e-6
LN_EPS = 1e-5

V7X_LANES = 128
V7X_SUBLANES = 8
V7X_VMEM_BYTES = 64 * 1024 * 1024
VMEM_LIMIT = (V7X_VMEM_BYTES * 7) // 8

MOE_TILE = 256
COMBINE_TILE = 512
SSD_SEQS_PER_STEP = 2
MOE_FF_CHUNK = 256
CONF_TILE = 256
CONF_ACC_ROWS = 64
SC_WINDOW_BYTES = 128 * 1024
SAMPLE_STATE_TILE = 8
CONF_SAMPLE_TILE = 32


def _params(*semantics):
    return pltpu.CompilerParams(dimension_semantics=semantics, vmem_limit_bytes=VMEM_LIMIT)


def _resident(shape):
    zeros = (0,) * len(shape)
    return pl.BlockSpec(shape, lambda *_: zeros, pipeline_mode=pl.Buffered(1))


def _whole(shape):
    zeros = (0,) * len(shape)
    return pl.BlockSpec(shape, lambda *_: zeros)


def _row_tile(n_rows, preferred):
    for t in (preferred, 512, 256, 128):
        if t <= preferred and n_rows % t == 0:
            return t
    raise ValueError(f"row count {n_rows} is not a multiple of {V7X_LANES}")


def _dot(a, b):
    return jnp.dot(a, b, preferred_element_type=F32)


def _pack_bf16_pairs(x):
    w = x.shape[1] // 2
    lo = pltpu.bitcast(x[:, :w].astype(BF16).astype(F32), jnp.uint32) >> 16
    hi = pltpu.bitcast(x[:, w:].astype(BF16).astype(F32), jnp.uint32) & jnp.uint32(0xFFFF0000)
    return hi | lo


def _unpack_bf16_pairs(u):
    lo = pltpu.bitcast(u << 16, F32)
    hi = pltpu.bitcast(u & jnp.uint32(0xFFFF0000), F32)
    return jnp.concatenate([lo, hi], axis=1)


def _rms(x, g):
    return x * lax.rsqrt(jnp.mean(x * x, axis=-1, keepdims=True) + RMS_EPS) * g


def _layernorm(x, g, b):
    mu = jnp.mean(x, axis=-1, keepdims=True)
    xc = x - mu
    var = jnp.mean(xc * xc, axis=-1, keepdims=True)
    return xc * lax.rsqrt(var + LN_EPS) * g + b


def _sigmoid(x):
    return 0.5 * jnp.tanh(0.5 * x) + 0.5


def _silu(x):
    return x * _sigmoid(x)


def _softplus(x):
    return jnp.maximum(x, 0.0) + jnp.log1p(jnp.exp(-jnp.abs(x)))


def _even_in_kernel(x_ref, g_ref, wzx_ref, wuv_ref, wdt_ref, lng_ref, lnb_ref, z_ref, xbc_ref, ug_ref, vn_ref,
                    dt_ref, *, d_ssd, conv_dim, d_gmlp):
    h = _rms(x_ref[...], g_ref[...]).astype(BF16)
    z_ref[...] = _dot(h, wzx_ref[:, 0:d_ssd])
    xbc_ref[...] = _dot(h, wzx_ref[:, d_ssd:d_ssd + conv_dim])
    ug_ref[...] = jax.nn.gelu(_dot(h, wuv_ref[:, 0:d_gmlp])).astype(ug_ref.dtype)
    v = jax.nn.gelu(_dot(h, wuv_ref[:, d_gmlp:2 * d_gmlp]))
    vn_ref[...] = _layernorm(v, lng_ref[...], lnb_ref[...])
    dt_ref[...] = _dot(h, wdt_ref[...])


def _even_in(x, g, w_zx, w_uv, w_dt, ln_g, ln_b, *, d_ssd, conv_dim, d_gmlp):
    t, d = x.shape
    tm = _row_tile(t, 512)
    row = lambda n: pl.BlockSpec((tm, n), lambda i: (i, 0))
    consts = (g, w_zx, w_uv, w_dt, ln_g, ln_b)
    return pl.pallas_call(
        functools.partial(_even_in_kernel, d_ssd=d_ssd, conv_dim=conv_dim, d_gmlp=d_gmlp),
        grid=(t // tm,),
        in_specs=[row(d)] + [_resident(c.shape) for c in consts],
        out_specs=[row(d_ssd), row(conv_dim), row(d_gmlp), row(d_gmlp), row(V7X_LANES)],
        out_shape=[jax.ShapeDtypeStruct((t, d_ssd), F32), jax.ShapeDtypeStruct((t, conv_dim), F32),
                   jax.ShapeDtypeStruct((t, d_gmlp), BF16), jax.ShapeDtypeStruct((t, d_gmlp), F32),
                   jax.ShapeDtypeStruct((t, V7X_LANES), F32)],
        compiler_params=_params("parallel"),
        name="even_in",
    )(x, *consts)


def _ssd_prompt_kernel(xbc_ref, z_ref, dt_ref, cw_ref, cb_ref, dtb_ref, alog_ref, dskip_ref, ng_ref, expand_ref,
                       yn_ref, st_ref, xp_ref, state_ref, xc_ref, *, d_ssd, n_taps):
    c = pl.program_id(1)

    @pl.when(c == 0)
    def _():
        state_ref[...] = jnp.zeros_like(state_ref)
        xp_ref[:, 0:V7X_SUBLANES, :] = jnp.zeros((xp_ref.shape[0], V7X_SUBLANES, xp_ref.shape[2]), F32)

    for s in range(xbc_ref.shape[1]):
        _ssd_chunk(xbc_ref.at[0, s], z_ref.at[0, s], dt_ref.at[0, s], cw_ref, cb_ref, dtb_ref, alog_ref, dskip_ref,
                   ng_ref, expand_ref, yn_ref.at[0, s], xp_ref.at[s], state_ref.at[s], xc_ref.at[s],
                   d_ssd=d_ssd, n_taps=n_taps)

    @pl.when(c == pl.num_programs(1) - 1)
    def _():
        for s in range(state_ref.shape[0]):
            for pair in range(state_ref.shape[1]):
                st_ref[s, pair] = state_ref[s, pair].T


def _ssd_chunk(xbc_ref, z_ref, dt_ref, cw_ref, cb_ref, dtb_ref, alog_ref, dskip_ref, ng_ref, expand_ref,
               yn_ref, xp_ref, state_ref, xc_ref, *, d_ssd, n_taps):
    n_pairs = state_ref.shape[0]
    ch = SSD_CHUNK
    gn = SSD_GROUPS * SSD_STATE

    xp_ref[V7X_SUBLANES:V7X_SUBLANES + ch, :] = xbc_ref[...]
    base = V7X_SUBLANES - (n_taps - 1)
    acc = cb_ref[...] + cw_ref[n_taps - 1:n_taps, :] * xbc_ref[...]
    for k in range(n_taps - 1):
        acc = acc + cw_ref[k:k + 1, :] * xp_ref[base + k:base + k + ch, :]
    xp_ref[0:V7X_SUBLANES, :] = xp_ref[ch:ch + V7X_SUBLANES, :]
    xc_ref[...] = _silu(acc)
    xc = xc_ref

    dt = _softplus(dt_ref[...] + dtb_ref[...])
    a = dt * (-jnp.exp(alog_ref[...]))
    def spread(v, ones_ref):
        hi = v.astype(BF16)
        rest = v - hi.astype(F32)
        mid = rest.astype(BF16)
        lo = (rest - mid.astype(F32)).astype(BF16)
        return _dot(hi, ones_ref[...]) + (_dot(mid, ones_ref[...]) + _dot(lo, ones_ref[...]))

    dt_wide = spread(dt, expand_ref)
    li = lax.broadcasted_iota(jnp.int32, (ch, ch), 0)
    si = lax.broadcasted_iota(jnp.int32, (ch, ch), 1)
    causal = li >= si
    tril = jnp.where(causal, 1.0, 0.0).astype(F32)
    acs = jnp.dot(tril, a, precision=HIGHEST, preferred_element_type=F32)
    acs_t = acs.T
    lane = lax.broadcasted_iota(jnp.int32, (ch, 2 * SSD_HEAD_DIM), 1)
    first = lane < SSD_HEAD_DIM
    first_n = lax.broadcasted_iota(jnp.int32, (SSD_STATE, 2 * SSD_HEAD_DIM), 1) < SSD_HEAD_DIM
    pairs_per_group = n_pairs // SSD_GROUPS

    for g in range(SSD_GROUPS):
        bg = xc[:, d_ssd + g * SSD_STATE:d_ssd + (g + 1) * SSD_STATE]
        cg = xc[:, d_ssd + gn + g * SSD_STATE:d_ssd + gn + (g + 1) * SSD_STATE]
        bg_t = bg.T
        cb = _dot(cg.astype(BF16), bg_t.astype(BF16))
        ys = []
        for q in range(pairs_per_group):
            pair = g * pairs_per_group + q
            h0 = 2 * pair
            lo = pair * 2 * SSD_HEAD_DIM
            xs = xc[:, lo:lo + 2 * SSD_HEAD_DIM]
            xdt = (xs * dt_wide[:, lo:lo + 2 * SSD_HEAD_DIM]).astype(BF16)
            s_prev = state_ref[pair]
            s_prev_b = s_prev.astype(BF16)
            y2, snew2, dec2 = [], [], []
            for r in range(2):
                h = h0 + r
                col = acs[:, h:h + 1]
                row = acs_t[h:h + 1, :]
                last = acs_t[h:h + 1, ch - 1:ch]
                decay = jnp.exp(jnp.where(causal, col - row, -jnp.inf))
                y = _dot((cb * decay).astype(BF16), xdt)
                y = y + _dot((cg * jnp.exp(col)).astype(BF16), s_prev_b)
                snew = _dot((bg_t * jnp.exp(last - row)).astype(BF16), xdt)
                y2.append(y)
                snew2.append(snew)
                dec2.append(jnp.exp(last))
            y = jnp.where(first, y2[0], y2[1])
            state_ref[pair] = (s_prev * jnp.where(first_n, dec2[0], dec2[1])
                               + jnp.where(first_n, snew2[0], snew2[1]))
            ys.append(y + dskip_ref[:, lo:lo + 2 * SSD_HEAD_DIM] * xs)
        gw = pairs_per_group * 2 * SSD_HEAD_DIM
        yg = jnp.concatenate(ys, axis=-1) * _silu(z_ref[:, g * gw:(g + 1) * gw])
        yg = yg * lax.rsqrt(jnp.mean(yg * yg, axis=-1, keepdims=True) + RMS_EPS)
        yn_ref[:, g * gw:(g + 1) * gw] = (yg * ng_ref[:, g * gw:(g + 1) * gw]).astype(yn_ref.dtype)


def _ssd_prompt(xbc, z, dt, conv_w, conv_b, dt_bias, a_log, d_skip, norm_g, expand, *, batch, seq):
    t, conv_dim = xbc.shape
    d_ssd = z.shape[1]
    n_pairs = d_ssd // (2 * SSD_HEAD_DIM)
    nc = seq // SSD_CHUNK
    n_taps = conv_w.shape[0]
    g = SSD_SEQS_PER_STEP if batch % SSD_SEQS_PER_STEP == 0 else 1
    tile = lambda n: pl.BlockSpec((1, g, SSD_CHUNK, n), lambda b, c: (b, 0, c, 0))
    by_seq = lambda a: a.reshape(batch // g, g, seq, a.shape[1])
    yn, st = pl.pallas_call(
        functools.partial(_ssd_prompt_kernel, d_ssd=d_ssd, n_taps=n_taps),
        grid=(batch // g, nc),
        in_specs=[tile(conv_dim), tile(d_ssd), tile(V7X_LANES), _resident(conv_w.shape), _resident(conv_b.shape),
                  _resident(dt_bias.shape), _resident(a_log.shape), _resident(d_skip.shape),
                  _resident(norm_g.shape), _resident(expand.shape)],
        out_specs=[tile(d_ssd),
                   pl.BlockSpec((g, n_pairs, 2 * SSD_HEAD_DIM, SSD_STATE), lambda b, c: (b, 0, 0, 0))],
        out_shape=[jax.ShapeDtypeStruct((batch // g, g, seq, d_ssd), BF16),
                   jax.ShapeDtypeStruct((batch, n_pairs, 2 * SSD_HEAD_DIM, SSD_STATE), F32)],
        scratch_shapes=[pltpu.VMEM((g, SSD_CHUNK + V7X_SUBLANES, conv_dim), F32),
                        pltpu.VMEM((g, n_pairs, SSD_STATE, 2 * SSD_HEAD_DIM), F32),
                        pltpu.VMEM((g, SSD_CHUNK, conv_dim), F32)],
        compiler_params=_params("parallel", "arbitrary"),
        name="ssd_prompt",
    )(by_seq(xbc), by_seq(z), by_seq(dt), conv_w, conv_b, dt_bias, a_log, d_skip, norm_g, expand)
    return yn.reshape(t, d_ssd), st


def _gmlp_gate(ug_ref, vn_ref, ws_ref, bs_ref, out_ref):
    ch = GMLP_CHUNK
    n_groups = ws_ref.shape[0]
    gd = vn_ref.shape[1] // n_groups
    ii = lax.broadcasted_iota(jnp.int32, (ch, ch), 0)
    jj = lax.broadcasted_iota(jnp.int32, (ch, ch), 1)
    for g in range(n_groups):
        ws = jnp.where(ii >= jj, ws_ref[g], 0.0).astype(BF16)
        cols = slice(g * gd, (g + 1) * gd)
        for r0 in range(0, vn_ref.shape[0], ch):
            rows = slice(r0, r0 + ch)
            mixed = _dot(ws, vn_ref[rows, cols].astype(BF16)) + bs_ref[:, g:g + 1]
            out_ref[rows, cols] = (ug_ref[rows, cols].astype(F32) * mixed).astype(out_ref.dtype)


def _ssd_sample_prep_kernel(xbc_ref, buf_ref, dt_ref, cw_ref, cb_ref, dtb_ref, alog_ref, expand_ref,
                            newbuf_ref, xs_ref, xdt_ref, bc_ref, da_ref, *, d_ssd, n_taps):
    x_new = xbc_ref[...]
    acc = cb_ref[...] + cw_ref[n_taps - 1:n_taps, :] * x_new
    for k in range(n_taps - 1):
        acc = acc + cw_ref[k:k + 1, :] * buf_ref[k]
    newbuf_ref[0:n_taps - 2] = buf_ref[1:n_taps - 1]
    newbuf_ref[n_taps - 2] = x_new
    xc = _silu(acc)
    xs = xc[:, :d_ssd]
    dt = _softplus(dt_ref[...] + dtb_ref[...])
    da_ref[...] = jnp.exp(dt * (-jnp.exp(alog_ref[...])))
    dt_wide = jnp.dot(dt, expand_ref[...], precision=HIGHEST, preferred_element_type=F32)
    xs_ref[...] = xs
    xdt_ref[...] = xs * dt_wide
    bc_ref[...] = xc[:, d_ssd:]


def _ssd_sample_prep(xbc, buf, dt, conv_w, conv_b, dt_bias, a_log, expand):
    n, conv_dim = xbc.shape
    d_ssd = expand.shape[1]
    n_taps = conv_w.shape[0]
    args = (xbc, buf, dt, conv_w, conv_b, dt_bias, a_log, expand)
    return pl.pallas_call(
        functools.partial(_ssd_sample_prep_kernel, d_ssd=d_ssd, n_taps=n_taps),
        grid=(1,),
        in_specs=[_resident(a.shape) for a in args],
        out_specs=[_whole(buf.shape), _whole((n, d_ssd)), _whole((n, d_ssd)),
                   _whole((n, conv_dim - d_ssd)), _whole((n, V7X_LANES))],
        out_shape=[jax.ShapeDtypeStruct(buf.shape, F32), jax.ShapeDtypeStruct((n, d_ssd), F32),
                   jax.ShapeDtypeStruct((n, d_ssd), F32), jax.ShapeDtypeStruct((n, conv_dim - d_ssd), F32),
                   jax.ShapeDtypeStruct((n, V7X_LANES), F32)],
        compiler_params=_params("arbitrary"),
        name="ssd_sample_prep",
    )(*args)


def _ssd_sample_state_kernel(h_ref, xdt_ref, da_ref, bc_ref, hnew_ref, y_ref, xcol_ref, ccol_ref):
    bs, n_heads, p, s = h_ref.shape
    n_pairs = n_heads // 2
    pairs_per_group = n_pairs // SSD_GROUPS
    gn = SSD_GROUPS * s
    w = 2 * p
    eye = lax.broadcasted_iota(jnp.int32, (w, w), 0) == lax.broadcasted_iota(jnp.int32, (w, w), 1)
    ones = jnp.ones((w, V7X_LANES), BF16)

    def stacked_diag(v):
        return jnp.concatenate(
            [jnp.where(eye, jnp.broadcast_to(v[j:j + 1], (w, w)), 0.0) for j in range(bs)], axis=0).astype(BF16)

    for q in range(n_pairs):
        x = xdt_ref[:, q * w:(q + 1) * w]
        hi = x.astype(BF16).astype(F32)
        mid = (x - hi).astype(BF16).astype(F32)
        lo = (x - hi) - mid
        xcol_ref[q] = _dot(stacked_diag(hi), ones) + (_dot(stacked_diag(mid), ones) + _dot(stacked_diag(lo), ones))
    for g in range(SSD_GROUPS):
        ccol_ref[g] = _dot(stacked_diag(bc_ref[:, gn + g * s:gn + (g + 1) * s]), ones).astype(BF16)

    for j in range(bs):
        for g in range(SSD_GROUPS):
            brow = bc_ref[j:j + 1, g * s:(g + 1) * s]
            h_new = []
            for q in range(g * pairs_per_group, (g + 1) * pairs_per_group):
                for r in range(2):
                    h = 2 * q + r
                    xcol = xcol_ref[q, j * w + r * p:j * w + (r + 1) * p, :]
                    h_new.append(h_ref[j, h] * da_ref[j:j + 1, h:h + 1] + xcol * brow)
                    hnew_ref[j, h] = h_new[-1]
            hc = _dot(jnp.concatenate(h_new, axis=0).astype(BF16), ccol_ref[g, j * s:(j + 1) * s, :])
            for i in range(pairs_per_group):
                y = jnp.sum(jnp.where(eye, hc[i * w:(i + 1) * w], 0.0), axis=0, keepdims=True)
                q = g * pairs_per_group + i
                y_ref[j, q:q + 1, :] = y


def _ssd_sample_state(h, xdt, da, bc):
    n, n_heads, p, s = h.shape
    assert 2 * p == V7X_LANES and s == V7X_LANES
    bs = SAMPLE_STATE_TILE
    rows = lambda w: pl.BlockSpec((bs, w), lambda i: (i, 0))
    return pl.pallas_call(
        _ssd_sample_state_kernel,
        grid=(n // bs,),
        in_specs=[pl.BlockSpec((bs, n_heads, p, s), lambda i: (i, 0, 0, 0)), rows(xdt.shape[1]), rows(da.shape[1]),
                  rows(bc.shape[1])],
        out_specs=[pl.BlockSpec((bs, n_heads, p, s), lambda i: (i, 0, 0, 0)),
                   pl.BlockSpec((bs, n_heads // 2, 2 * p), lambda i: (i, 0, 0))],
        out_shape=[jax.ShapeDtypeStruct(h.shape, F32), jax.ShapeDtypeStruct((n, n_heads // 2, 2 * p), F32)],
        scratch_shapes=[pltpu.VMEM((n_heads // 2, bs * 2 * p, V7X_LANES), F32),
                        pltpu.VMEM((SSD_GROUPS, bs * s, V7X_LANES), BF16)],
        compiler_params=_params("parallel"),
        name="ssd_sample_state",
    )(h, xdt, da, bc)


def _even_sample_post_kernel(y_ref, xs_ref, z_ref, dskip_ref, ng_ref, ug_ref, vn_ref, w0_ref, b0_ref,
                             yn_ref, gout_ref):
    d_ssd = y_ref.shape[1]
    gw = d_ssd // SSD_GROUPS
    y = (y_ref[...] + dskip_ref[...] * xs_ref[...]) * _silu(z_ref[...])
    for g in range(SSD_GROUPS):
        yg = y[:, g * gw:(g + 1) * gw]
        yg = yg * lax.rsqrt(jnp.mean(yg * yg, axis=-1, keepdims=True) + RMS_EPS)
        yn_ref[:, g * gw:(g + 1) * gw] = (yg * ng_ref[:, g * gw:(g + 1) * gw]).astype(yn_ref.dtype)
    mixed = w0_ref[...] * vn_ref[...] + b0_ref[...]
    gout_ref[...] = (ug_ref[...].astype(F32) * mixed).astype(gout_ref.dtype)


def _even_sample_post(y, xs, z, d_skip, norm_g, ug, vn, w0, b0):
    args = (y, xs, z, d_skip, norm_g, ug, vn, w0, b0)
    return pl.pallas_call(
        _even_sample_post_kernel,
        grid=(1,),
        in_specs=[_resident(a.shape) for a in args],
        out_specs=[_whole(y.shape), _whole(vn.shape)],
        out_shape=[jax.ShapeDtypeStruct(y.shape, BF16), jax.ShapeDtypeStruct(vn.shape, BF16)],
        compiler_params=_params("arbitrary"),
        name="even_sample_post",
    )(*args)


def _even_out_kernel(*refs, fuse_gmlp):
    if fuse_gmlp:
        (x_ref, yn_ref, ug_ref, vn_ref, ws_ref, bs_ref, wo_ref, g_ref, wg_ref, wu_ref, wd_ref, gc_ref, w1_ref,
         b1_ref, out_ref, a_ref, gout_ref) = refs
        _gmlp_gate(ug_ref, vn_ref, ws_ref, bs_ref, gout_ref)
    else:
        (x_ref, yn_ref, gout_ref, wo_ref, g_ref, wg_ref, wu_ref, wd_ref, gc_ref, w1_ref, b1_ref,
         out_ref, a_ref) = refs
    d_ssd = yn_ref.shape[1]
    x = x_ref[...] + _dot(yn_ref[...], wo_ref[0:d_ssd, :]) + _dot(gout_ref[...], wo_ref[d_ssd:, :])
    h = _rms(x, g_ref[...]).astype(BF16)
    mid = (_silu(_dot(h, wg_ref[...])) * _dot(h, wu_ref[...])).astype(BF16)
    x = x + _dot(mid, wd_ref[...])
    out_ref[...] = x
    dc = a_ref.shape[1]
    h = _rms(x, gc_ref[...]).astype(BF16)
    val = _dot(h, w1_ref[:, 0:dc]) + b1_ref[:, 0:dc]
    gate = _dot(h, w1_ref[:, dc:]) + b1_ref[:, dc:]
    a_ref[...] = val * _sigmoid(gate)


def _even_out(x, yn, gate_in, weights, *, seq=None):
    t, d = x.shape
    dc = weights[-2].shape[1] // 2
    fuse_gmlp = isinstance(gate_in, tuple)
    tm = _row_tile(t if seq is None else seq, 256)
    row = lambda n: pl.BlockSpec((tm, n), lambda i: (i, 0))
    if fuse_gmlp:
        ug, vn, w_s, b_s_t = gate_in
        assert tm % GMLP_CHUNK == 0
        gate_args = (ug, vn, w_s, b_s_t)
        gate_specs = [row(ug.shape[1]), row(vn.shape[1]), _resident(w_s.shape), _resident(b_s_t.shape)]
        scratch = [pltpu.VMEM((tm, ug.shape[1]), BF16)]
    else:
        gate_args, gate_specs, scratch = (gate_in,), [row(gate_in.shape[1])], []
    return pl.pallas_call(
        functools.partial(_even_out_kernel, fuse_gmlp=fuse_gmlp),
        grid=(t // tm,),
        in_specs=[row(d), row(yn.shape[1])] + gate_specs + [_resident(w.shape) for w in weights],
        out_specs=[row(d), row(dc)],
        out_shape=[jax.ShapeDtypeStruct((t, d), F32), jax.ShapeDtypeStruct((t, dc), F32)],
        scratch_shapes=scratch,
        compiler_params=_params("parallel"),
        name="even_out_ffn",
    )(x, yn, *gate_args, *weights)


def _route(logits, n_experts, route_ref, route_t_ref):
    lane = lax.broadcasted_iota(jnp.int32, logits.shape, 1)
    lane_f = lane.astype(F32)
    big = jnp.float32(V7X_LANES)
    lg = jnp.where(lane < n_experts, logits, -jnp.inf)
    m1 = jnp.max(lg, axis=-1, keepdims=True)
    i1 = jnp.min(jnp.where(lg == m1, lane_f, big), axis=-1, keepdims=True)
    lg2 = jnp.where(lane_f == i1, -jnp.inf, lg)
    m2 = jnp.max(lg2, axis=-1, keepdims=True)
    i2 = jnp.min(jnp.where(lg2 == m2, lane_f, big), axis=-1, keepdims=True)
    e2 = jnp.exp(m2 - m1)
    g1 = 1.0 / (1.0 + e2)
    g2 = e2 / (1.0 + e2)
    route = jnp.where(lane == 0, i1, jnp.where(lane == 1, i2, jnp.where(lane == 2, g1,
                      jnp.where(lane == 3, g2, 0.0))))
    route_ref[...] = route
    route_t_ref[...] = route.T[0:V7X_SUBLANES, :]


def _conf_finish(c, x, lng_ref, lnb_ref, w2_ref, b2_ref, ng_ref, rhi_ref, rlo_ref, out_ref, hn_ref, route_ref,
                 route_t_ref, *, n_experts):
    hmid = _silu(_layernorm(c, lng_ref[...], lnb_ref[...])).astype(BF16)
    x_new = x + _dot(hmid, w2_ref[...]) + b2_ref[...]
    out_ref[...] = x_new
    hn = _rms(x_new, ng_ref[...])
    hn_ref[...] = _pack_bf16_pairs(hn)
    hi = hn.astype(BF16)
    lo = (hn - hi.astype(F32)).astype(BF16)
    logits = _dot(hi, rhi_ref[...]) + (_dot(lo, rhi_ref[...]) + _dot(hi, rlo_ref[...]))
    _route(logits, n_experts, route_ref, route_t_ref)


def _conf_prompt_kernel(a_ref, x_ref, xs_ref, hns_ref, routes_ref, dw_ref, db_ref, lng_ref, lnb_ref, w2_ref, b2_ref,
                        ng_ref, rhi_ref, rlo_ref, out_ref, hn_ref, route_ref, route_t_ref, xp_ref, c_ref, win_ref,
                        *, n_taps, n_experts, tiles_per_seq, n_prompt_tiles):
    tl, d = a_ref.shape
    halo = xp_ref.shape[0] - tl
    base = halo - (n_taps - 1)
    rows = CONF_ACC_ROWS
    step = pl.program_id(0)

    @pl.when(step < n_prompt_tiles)
    def _():
        @pl.when(step % tiles_per_seq == 0)
        def _():
            xp_ref[0:halo, :] = jnp.zeros((halo, d), F32)

        xp_ref[halo:halo + tl, :] = a_ref[...]

        def lane_block(j, carry):
            cols = pl.ds(pl.multiple_of(j * V7X_LANES, V7X_LANES), V7X_LANES)
            for r0 in range(0, tl, rows):
                acc = jnp.broadcast_to(db_ref[:, cols], (rows, V7X_LANES))
                for phase in range(V7X_SUBLANES):
                    taps = [k for k in range(n_taps) if (base + k) % V7X_SUBLANES == phase]
                    if not taps:
                        continue
                    span = max(base + k - phase for k in taps) + rows
                    if phase:
                        win_ref[0:span, :] = xp_ref[pl.ds(r0 + phase, span), cols]
                    for k in taps:
                        o = base + k - phase
                        win = win_ref[o:o + rows, :] if phase else xp_ref[pl.ds(r0 + o, rows), cols]
                        acc = acc + dw_ref[k:k + 1, cols] * win
                c_ref[r0:r0 + rows, cols] = acc
            return carry

        lax.fori_loop(0, d // V7X_LANES, lane_block, 0)
        xp_ref[0:halo, :] = xp_ref[tl:tl + halo, :]
        _conf_finish(c_ref[...], x_ref[...], lng_ref, lnb_ref, w2_ref, b2_ref, ng_ref, rhi_ref, rlo_ref,
                     out_ref, hn_ref, route_ref, route_t_ref, n_experts=n_experts)

    @pl.when(step == n_prompt_tiles)
    def _():
        ns = xs_ref.shape[0]
        out_ref[0:ns, :] = xs_ref[...]
        hn_ref[0:ns, :] = hns_ref[...]
        route_ref[0:ns, :] = routes_ref[...]


def _conf_prompt(a, x, sample_rows, consts, *, seq, n_experts):
    t, d = a.shape
    n_s = sample_rows[0].shape[0]
    tl = _row_tile(seq, CONF_TILE)
    assert tl % CONF_ACC_ROWS == 0 and n_s <= tl
    n_prompt_tiles = t // tl
    n_taps = consts[0].shape[0]
    halo = -(-(n_taps - 1) // V7X_SUBLANES) * V7X_SUBLANES
    in_tile = lambda n: pl.BlockSpec((tl, n), lambda s: (jnp.minimum(s, n_prompt_tiles - 1), 0))
    out_tile = lambda n: pl.BlockSpec((tl, n), lambda s: (s, 0))
    small = sample_rows + tuple(consts)
    return pl.pallas_call(
        functools.partial(_conf_prompt_kernel, n_taps=n_taps, n_experts=n_experts, tiles_per_seq=seq // tl,
                          n_prompt_tiles=n_prompt_tiles),
        grid=(n_prompt_tiles + 1,),
        in_specs=[in_tile(d), in_tile(d)] + [_resident(c.shape) for c in small],
        out_specs=[out_tile(d), out_tile(d // 2), out_tile(V7X_LANES),
                   pl.BlockSpec((V7X_SUBLANES, tl), lambda s: (0, jnp.minimum(s, n_prompt_tiles - 1)))],
        out_shape=[jax.ShapeDtypeStruct((t + n_s, d), F32), jax.ShapeDtypeStruct((t + n_s, d // 2), jnp.uint32),
                   jax.ShapeDtypeStruct((t + n_s, V7X_LANES), F32), jax.ShapeDtypeStruct((V7X_SUBLANES, t), F32)],
        scratch_shapes=[pltpu.VMEM((tl + halo, d), F32), pltpu.VMEM((tl, d), F32),
                        pltpu.VMEM((CONF_ACC_ROWS + halo, V7X_LANES), F32)],
        compiler_params=_params("arbitrary"),
        name="conf_prompt",
    )(a, x, *small)


def _conf_sample_conv_kernel(a_ref, buf_ref, dw_ref, db_ref, newbuf_ref, c_ref, *, n_taps):
    a_new = a_ref[...]
    acc = db_ref[...] + dw_ref[n_taps - 1:n_taps, :] * a_new
    for k in range(n_taps - 1):
        acc = acc + dw_ref[k:k + 1, :] * buf_ref[k]
    c_ref[...] = acc
    newbuf_ref[0:n_taps - 2] = buf_ref[1:n_taps - 1]
    newbuf_ref[n_taps - 2] = a_new


def _conf_sample_conv(a, buf, dw_w, dw_b):
    n, d = a.shape
    bs = CONF_SAMPLE_TILE
    hist = pl.BlockSpec((buf.shape[0], bs, d), lambda i: (0, i, 0))
    rows = pl.BlockSpec((bs, d), lambda i: (i, 0))
    return pl.pallas_call(
        functools.partial(_conf_sample_conv_kernel, n_taps=dw_w.shape[0]),
        grid=(n // bs,),
        in_specs=[rows, hist, _resident(dw_w.shape), _resident(dw_b.shape)],
        out_specs=[hist, rows],
        out_shape=[jax.ShapeDtypeStruct(buf.shape, F32), jax.ShapeDtypeStruct((n, d), F32)],
        compiler_params=_params("parallel"),
        name="conf_sample_conv",
    )(a, buf, dw_w, dw_b)


def _conf_sample_kernel(c_ref, x_ref, lng_ref, lnb_ref, w2_ref, b2_ref, ng_ref, rhi_ref, rlo_ref,
                        out_ref, hn_ref, route_ref, route_t_ref, *, n_experts):
    _conf_finish(c_ref[...], x_ref[...], lng_ref, lnb_ref, w2_ref, b2_ref, ng_ref, rhi_ref, rlo_ref,
                 out_ref, hn_ref, route_ref, route_t_ref, n_experts=n_experts)


def _conf_sample(c, x, consts, *, n_experts):
    n, d = c.shape
    args = (c, x) + tuple(consts)
    return pl.pallas_call(
        functools.partial(_conf_sample_kernel, n_experts=n_experts),
        grid=(1,),
        in_specs=[_resident(v.shape) for v in args],
        out_specs=[_whole((n, d)), _whole((n, d // 2)), _whole((n, V7X_LANES)), _whole((V7X_SUBLANES, n))],
        out_shape=[jax.ShapeDtypeStruct((n, d), F32), jax.ShapeDtypeStruct((n, d // 2), jnp.uint32),
                   jax.ShapeDtypeStruct((n, V7X_LANES), F32), jax.ShapeDtypeStruct((V7X_SUBLANES, n), F32)],
        compiler_params=_params("arbitrary"),
        name="conf_sample",
    )(*args)


def _sc_window_rows(x):
    assert x.dtype.itemsize == 4, "the SparseCore indirect copies move 32-bit elements"
    return SC_WINDOW_BYTES // (x.shape[1] * x.dtype.itemsize)


def _sc_gather_rows(x, idx):
    n, d = idx.shape[0], x.shape[1]
    w = _sc_window_rows(x)
    assert n % w == 0
    mesh = plsc.VectorSubcoreMesh(core_axis_name="core", subcore_axis_name="subcore")

    @functools.partial(pl.kernel, out_type=jax.ShapeDtypeStruct((n, d), x.dtype), mesh=mesh, scratch_types=[])
    def gather(x_hbm, idx_hbm, out_hbm):
        def body(idx_vmem, out_vmem):
            pltpu.sync_copy(x_hbm.at[idx_vmem.at[0]], out_vmem)

        pltpu.emit_pipeline(
            body,
            grid=(n // w,),
            in_specs=[pl.BlockSpec((1, w), lambda i: (i, 0))],
            out_specs=[pl.BlockSpec((w, d), lambda i: (i, 0))],
            core_axis_name=("core", "subcore"),
            dimension_semantics=(pltpu.PARALLEL,),
        )(idx_hbm, out_hbm)

    return gather(x, idx.reshape(n // w, w))


def _sc_dispatch_rows(x, pos, pad_pos, n_out):
    t, d = x.shape
    w = _sc_window_rows(x)
    n_pad = pad_pos.shape[0]
    assert t % w == 0 and n_pad % w == 0 and TOP_K * t + n_pad == n_out
    mesh = plsc.VectorSubcoreMesh(core_axis_name="core", subcore_axis_name="subcore")

    @functools.partial(pl.kernel, out_type=jax.ShapeDtypeStruct((n_out, d), x.dtype), mesh=mesh, scratch_types=[])
    def dispatch(x_hbm, pos_hbm, pad_hbm, zero_hbm, out_hbm):
        def body(rows_vmem, idx_vmem):
            pltpu.sync_copy(rows_vmem, out_hbm.at[idx_vmem.at[0]])

        def scatter(src_hbm, idx_hbm, src_map):
            pltpu.emit_pipeline(
                body,
                grid=(idx_hbm.shape[0],),
                in_specs=[pl.BlockSpec((w, d), src_map), pl.BlockSpec((1, w), lambda i: (i, 0))],
                out_specs=[],
                core_axis_name=("core", "subcore"),
                dimension_semantics=(pltpu.PARALLEL,),
            )(src_hbm, idx_hbm)

        for k in range(TOP_K):
            scatter(x_hbm, pos_hbm.at[k], lambda i: (i, 0))
        scatter(zero_hbm, pad_hbm, lambda i: (0, 0))

    return dispatch(x, pos.reshape(TOP_K, t // w, w), pad_pos.reshape(n_pad // w, w), jnp.zeros((w, d), x.dtype))


def _moe_ffn_kernel(tile_expert_ref, first_ref, bank_ref, next_expert_ref, n_tiles_ref, x_ref, wg_hbm, wu_hbm,
                    wd_hbm, out_ref, wg_v, wu_v, wd_v, stage_g, stage_u, stage_d, acc_ref, sems):
    i = pl.program_id(0)
    dff = wg_v.shape[2]
    ck = stage_g.shape[2]
    n_chunks = dff // ck
    active = i < n_tiles_ref[0]
    starts_run = active & (first_ref[i] == 1)
    bank = bank_ref[i]
    nxt = next_expert_ref[i]

    def chunk_copies(expert, c, slot):
        cols = slice(c * ck, (c + 1) * ck)
        return (pltpu.make_async_copy(wg_hbm.at[expert, :, cols], stage_g.at[slot], sems.at[0, slot]),
                pltpu.make_async_copy(wu_hbm.at[expert, :, cols], stage_u.at[slot], sems.at[1, slot]),
                pltpu.make_async_copy(wd_hbm.at[expert, cols, :], stage_d.at[slot], sems.at[2, slot]))

    def stream_weights(expert, dst_bank, between_chunks):
        for cp in chunk_copies(expert, 0, 0):
            cp.start()
        for c in range(n_chunks):
            slot = c % 2
            if c + 1 < n_chunks:
                for cp in chunk_copies(expert, c + 1, 1 - slot):
                    cp.start()
            for cp in chunk_copies(expert, c, slot):
                cp.wait()
            cols = slice(c * ck, (c + 1) * ck)
            wg_v[dst_bank, :, cols] = stage_g[slot].astype(BF16)
            wu_v[dst_bank, :, cols] = stage_u[slot].astype(BF16)
            wd_v[dst_bank, cols, :] = stage_d[slot].astype(BF16)
            between_chunks(c)

    def ffn_part(xb, cols):
        mid = (_silu(_dot(xb, wg_v[bank, :, cols])) * _dot(xb, wu_v[bank, :, cols])).astype(BF16)
        return _dot(mid, wd_v[bank, cols, :])

    @pl.when(starts_run & (i == 0))
    def _():
        stream_weights(tile_expert_ref[i], bank, lambda c: None)

    overlapped = starts_run & (nxt >= 0)

    @pl.when(overlapped)
    def _():
        xb = _unpack_bf16_pairs(x_ref[...]).astype(BF16)
        acc_ref[...] = jnp.zeros_like(acc_ref)

        def piece(c):
            acc_ref[...] += ffn_part(xb, slice(c * ck, (c + 1) * ck))

        stream_weights(nxt, 1 - bank, piece)
        out_ref[...] = _pack_bf16_pairs(acc_ref[...])

    @pl.when(active & jnp.logical_not(overlapped))
    def _():
        xb = _unpack_bf16_pairs(x_ref[...]).astype(BF16)
        half = dff // 2
        acc = ffn_part(xb, slice(0, half)) + ffn_part(xb, slice(half, dff))
        out_ref[...] = _pack_bf16_pairs(acc)

    @pl.when(jnp.logical_not(active))
    def _():
        out_ref[...] = jnp.zeros_like(out_ref)


def _moe_ffn(tile_expert, n_tiles, xs, wg, wu, wd, *, tm):
    p, d = xs.shape[0], wg.shape[1]
    dff = wg.shape[2]
    ck = MOE_FF_CHUNK
    assert dff % ck == 0
    nt = p // tm
    i32 = jnp.int32
    tile = jnp.arange(nt, dtype=i32)
    first = jnp.concatenate([jnp.ones((1,), i32), (tile_expert[1:] != tile_expert[:-1]).astype(i32)])
    bank = (jnp.cumsum(first) - 1) % 2
    run_start = jnp.where((first == 1) & (tile < n_tiles[0]), tile, nt)
    next_start = jnp.concatenate([lax.cummin(run_start, reverse=True)[1:], jnp.full((1,), nt, i32)])
    next_expert = jnp.sum((next_start[:, None] == tile[None, :]).astype(i32) * tile_expert[None, :], axis=1)
    next_expert = jnp.where(next_start < nt, next_expert, -1)
    any_spec = pl.BlockSpec(memory_space=pl.ANY)
    row_tile = pl.BlockSpec((tm, d // 2), lambda i, *_: (i, 0))
    grid_spec = pltpu.PrefetchScalarGridSpec(
        num_scalar_prefetch=5,
        grid=(nt,),
        in_specs=[row_tile, any_spec, any_spec, any_spec],
        out_specs=row_tile,
        scratch_shapes=[pltpu.VMEM((2, d, dff), BF16), pltpu.VMEM((2, d, dff), BF16), pltpu.VMEM((2, dff, d), BF16),
                        pltpu.VMEM((2, d, ck), F32), pltpu.VMEM((2, d, ck), F32), pltpu.VMEM((2, ck, d), F32),
                        pltpu.VMEM((tm, d), F32), pltpu.SemaphoreType.DMA((3, 2))],
    )
    return pl.pallas_call(
        _moe_ffn_kernel,
        grid_spec=grid_spec,
        out_shape=jax.ShapeDtypeStruct((p, d // 2), jnp.uint32),
        compiler_params=_params("arbitrary"),
        name="moe_ffn",
    )(tile_expert, first, bank.astype(i32), next_expert.astype(i32), n_tiles, xs, wg, wu, wd)


def _moe_combine_kernel(x_ref, route_ref, gf_ref, a_ref, b_ref, y_ref):
    route = route_ref[...]
    x = (x_ref[...] + route[:, 2:3] * _unpack_bf16_pairs(a_ref[...])
         + route[:, 3:4] * _unpack_bf16_pairs(b_ref[...]))
    y_ref[...] = _rms(x, gf_ref[...])


def _moe_combine(x, route, gf, picked, *, row0, n_rows, a_row0, b_row0):
    d = x.shape[1]
    tc = _row_tile(n_rows, COMBINE_TILE)
    assert row0 % tc == 0 and a_row0 % tc == 0 and b_row0 % tc == 0
    rows = lambda w, r0: pl.BlockSpec((tc, w), lambda i: (i + r0 // tc, 0))
    return pl.pallas_call(
        _moe_combine_kernel,
        grid=(n_rows // tc,),
        in_specs=[rows(d, row0), rows(V7X_LANES, row0), _resident(gf.shape), rows(d // 2, a_row0),
                  rows(d // 2, b_row0)],
        out_specs=pl.BlockSpec((tc, d), lambda i: (i, 0)),
        out_shape=jax.ShapeDtypeStruct((n_rows, d), F32),
        compiler_params=_params("parallel"),
        name="moe_combine",
    )(x, route, gf, picked, picked)


def _moe_plan(route_t, n_experts, tm):
    t = route_t.shape[1]
    n_slots = t * TOP_K
    i32 = jnp.int32
    expert = route_t[:TOP_K].astype(i32).reshape(1, n_slots)
    onehot = (expert == jnp.arange(n_experts, dtype=i32)[:, None]).astype(i32)
    counts = jnp.sum(onehot, axis=1)
    tiles = (counts + tm - 1) // tm
    tile_end = jnp.cumsum(tiles)
    start = (tile_end - tiles) * tm
    rank = jnp.cumsum(onehot, axis=1) - onehot
    pos = jnp.sum(onehot * (rank + start[:, None]), axis=0)
    nt = -(-n_slots // tm) + n_experts
    n_tiles = tile_end[-1]
    tile_id = jnp.minimum(jnp.arange(nt, dtype=i32), n_tiles - 1)
    tile_expert = jnp.sum((tile_id[:, None] >= tile_end[None, :]).astype(i32), axis=1)
    seg_start = jnp.concatenate([start + counts, (n_tiles * tm)[None]])
    seg_len = jnp.concatenate([tiles * tm - counts, (nt * tm - n_tiles * tm)[None]])
    seg_end = jnp.cumsum(seg_len)
    q = jnp.arange(nt * tm - n_slots, dtype=i32)
    in_seg = (q[:, None] >= (seg_end - seg_len)[None, :]) & (q[:, None] < seg_end[None, :])
    pad_pos = jnp.sum(in_seg.astype(i32) * (seg_start - (seg_end - seg_len))[None, :], axis=1) + q
    return (tile_expert.astype(i32), n_tiles.astype(i32).reshape(1), pos.reshape(TOP_K, t).astype(i32),
            pad_pos.astype(i32), nt * tm)


def _row(v):
    return v.reshape(1, -1).astype(F32)


def _pad_lanes(v):
    return jnp.pad(_row(v), ((0, 0), (0, V7X_LANES - v.shape[-1])))


def kernel(x_prompt, x_sample, state_ssm, state_ssd_conv, state_conf_conv, norm_mix_even, w_in_even, ssd_conv_w, ssd_conv_b, ssd_dt_bias, ssd_a_log, ssd_d, ssd_norm, gmlp_ln_g, gmlp_ln_b, gmlp_w_s, gmlp_b_s, w_out_even, norm_ffn_even, ffn_w_gate, ffn_w_up, ffn_w_down, norm_mix_odd, conf_w1, conf_b1, conf_dw_w, conf_dw_b, conf_ln_g, conf_ln_b, conf_w2, conf_b2, norm_ffn_odd, moe_router, moe_w_gate, moe_w_up, moe_w_down, final_norm):
    batch, seq, d_model = x_prompt.shape
    n_dec, dec_seq, _ = x_sample.shape
    assert dec_seq == 1, "the sample group advances one token per sequence"
    assert seq % SSD_CHUNK == 0 and n_dec % V7X_LANES == 0
    n_even, n_odd = w_in_even.shape[0], conf_w1.shape[0]
    assert (n_even, n_odd) == (1, 1), "the final norm is fused into the last (odd) layer's MoE combine"
    n_heads = ssd_dt_bias.shape[1]
    d_ssd = n_heads * SSD_HEAD_DIM
    conv_dim = ssd_conv_w.shape[2]
    d_gmlp = gmlp_ln_g.shape[1]
    n_groups_gmlp = gmlp_w_s.shape[1]
    n_experts = moe_router.shape[2]
    tp = batch * seq

    xp = x_prompt.reshape(tp, d_model)
    xs = x_sample.reshape(n_dec, d_model)
    outs = dict(ssm_p=[], conv_p=[], conf_p=[], ssm_s=[], conv_s=[], conf_s=[], v_s=[])

    for layer in range(n_even + n_odd):
        i = layer // 2
        if layer % 2 == 0:
            w = w_in_even[i]
            o1, o2, o3 = d_ssd, d_ssd + conv_dim, d_ssd + conv_dim + n_heads
            w_parts = (w[:, :o2].astype(BF16), w[:, o3:].astype(BF16),
                       jnp.pad(w[:, o2:o3], ((0, 0), (0, V7X_LANES - n_heads))).astype(BF16))
            in_args = (_row(norm_mix_even[i]), *w_parts, _row(gmlp_ln_g[i]), _row(gmlp_ln_b[i]))
            dims = dict(d_ssd=d_ssd, conv_dim=conv_dim, d_gmlp=d_gmlp)
            dt_bias, a_log = _pad_lanes(ssd_dt_bias[i]), _pad_lanes(ssd_a_log[i])
            d_skip = _row(jnp.repeat(ssd_d[i], SSD_HEAD_DIM))
            norm_g = _row(ssd_norm[i])
            conv_w, conv_b = ssd_conv_w[i], _row(ssd_conv_b[i])
            z, xbc, ug, vn, dt = _even_in(xp, *in_args, **dims)
            expand = (jnp.arange(d_ssd)[None, :] // SSD_HEAD_DIM == jnp.arange(V7X_LANES)[:, None]).astype(F32)
            yn, st = _ssd_prompt(xbc, z, dt, conv_w, conv_b, dt_bias, a_log, d_skip, norm_g, expand.astype(BF16),
                                 batch=batch, seq=seq)
            n_pairs = n_heads // 2
            outs['ssm_p'].append(st.reshape(batch, n_heads, SSD_HEAD_DIM, SSD_STATE))
            outs['conv_p'].append(xbc.reshape(batch, seq, conv_dim)[:, seq - (conv_w.shape[0] - 1):])
            tail_w = (w_out_even[i].astype(BF16), _row(norm_ffn_even[i]), ffn_w_gate[i].astype(BF16),
                      ffn_w_up[i].astype(BF16), ffn_w_down[i].astype(BF16), _row(norm_mix_odd[i]),
                      conf_w1[i].astype(BF16), _row(conf_b1[i]))
            xp, a_p = _even_out(xp, yn, (ug, vn, gmlp_w_s[i], gmlp_b_s[i].T), tail_w, seq=seq)
            z, xbc, ug, vn, dt = _even_in(xs, *in_args, **dims)
            newbuf, xc, xdt, bc, da = _ssd_sample_prep(xbc, state_ssd_conv[i].transpose(1, 0, 2), dt, conv_w, conv_b,
                                                        dt_bias, a_log, expand)
            h_new, y4 = _ssd_sample_state(state_ssm[i], xdt, da, bc)
            gd = d_gmlp // n_groups_gmlp
            w0 = _row(jnp.repeat(gmlp_w_s[i][:, 0, 0], gd))
            b0 = _row(jnp.repeat(gmlp_b_s[i][:, 0], gd))
            yn, gout = _even_sample_post(y4.reshape(n_dec, d_ssd), xc, z, d_skip, norm_g, ug, vn, w0, b0)
            outs['ssm_s'].append(h_new)
            outs['conv_s'].append(newbuf.transpose(1, 0, 2))
            outs['v_s'].append(vn.reshape(n_dec, 1, d_gmlp))
            xs, a_s = _even_out(xs, yn, gout, tail_w)
        else:
            router = jnp.pad(moe_router[i], ((0, 0), (0, V7X_LANES - n_experts)))
            router_hi = router.astype(BF16)
            router_lo = (router - router_hi.astype(F32)).astype(BF16)
            tail = (conf_dw_w[i], _row(conf_dw_b[i]), _row(conf_ln_g[i]), _row(conf_ln_b[i]),
                    conf_w2[i].astype(BF16), _row(conf_b2[i]), _row(norm_ffn_odd[i]), router_hi, router_lo)
            n_keep = conf_dw_w.shape[1] - 1
            outs['conf_p'].append(a_p.reshape(batch, seq, -1)[:, seq - n_keep:])
            newbuf, c_s = _conf_sample_conv(a_s, state_conf_conv[i].transpose(1, 0, 2), tail[0], tail[1])
            outs['conf_s'].append(newbuf.transpose(1, 0, 2))
            *sample_rows, route_t_s = _conf_sample(c_s, xs, tail[2:], n_experts=n_experts)
            x_all, hn, route, route_t_p = _conf_prompt(a_p, xp, tuple(sample_rows), tail, seq=seq,
                                                       n_experts=n_experts)
            route_t = jnp.concatenate([route_t_p, route_t_s], axis=1)
            tile_expert, n_tiles, pos, pad_pos, n_rows = _moe_plan(route_t, n_experts, MOE_TILE)
            xs_sorted = _sc_dispatch_rows(hn, pos, pad_pos, n_rows)
            ys = _moe_ffn(tile_expert, n_tiles, xs_sorted, moe_w_gate[i], moe_w_up[i], moe_w_down[i], tm=MOE_TILE)
            picked = _sc_gather_rows(ys, jnp.concatenate([pos[:, :tp].reshape(-1), pos[:, tp:].reshape(-1)]))
            gf = _row(final_norm)
            xp = _moe_combine(x_all, route, gf, picked, row0=0, n_rows=tp, a_row0=0, b_row0=tp)
            xs = _moe_combine(x_all, route, gf, picked, row0=tp, n_rows=n_dec, a_row0=TOP_K * tp,
                              b_row0=TOP_K * tp + n_dec)

    y_prompt = xp.reshape(batch, seq, d_model)
    y_sample = xs.reshape(n_dec, 1, d_model)
    return (y_prompt, y_sample, jnp.stack(outs['ssm_p']), jnp.stack(outs['conv_p']), jnp.stack(outs['conf_p']),
            jnp.stack(outs['ssm_s']), jnp.stack(outs['conv_s']), jnp.stack(outs['conf_s']), jnp.stack(outs['v_s']))
```

```python
import functools

import jax
import jax.numpy as jnp
from jax import lax
from jax.experimental import pallas as pl
from jax.experimental.pallas import tpu as pltpu
from jax.experimental.pallas import tpu_sc as plsc

F32 = jnp.float32
BF16 = jnp.bfloat16
HIGHEST = lax.Precision.HIGHEST

SSD_HEAD_DIM = 64
SSD_GROUPS = 4
SSD_STATE = 128
SSD_CHUNK = 128
GMLP_CHUNK = 128
TOP_K = 2
RMS_EPS = 1e-6
LN_EPS = 1e-5

V7X_LANES = 128
V7X_SUBLANES = 8
V7X_VMEM_BYTES = 64 * 1024 * 1024
VMEM_LIMIT = (V7X_VMEM_BYTES * 7) // 8

MOE_TILE = 256
COMBINE_TILE = 512
SSD_SEQS_PER_STEP = 4
MOE_FF_CHUNK = 256
CONF_TILE = 256
CONF_ACC_ROWS = 64
SC_WINDOW_BYTES = 128 * 1024
SAMPLE_STATE_TILE = 8
CONF_SAMPLE_TILE = 32


def _params(*semantics):
    return pltpu.CompilerParams(dimension_semantics=semantics, vmem_limit_bytes=VMEM_LIMIT)


def _resident(shape):
    zeros = (0,) * len(shape)
    return pl.BlockSpec(shape, lambda *_: zeros, pipeline_mode=pl.Buffered(1))


def _whole(shape):
    zeros = (0,) * len(shape)
    return pl.BlockSpec(shape, lambda *_: zeros)


def _row_tile(n_rows, preferred):
    for t in (preferred, 512, 256, 128):
        if t <= preferred and n_rows % t == 0:
            return t
    raise ValueError(f"row count {n_rows} is not a multiple of {V7X_LANES}")


def _dot(a, b):
    return jnp.dot(a, b, preferred_element_type=F32)


def _pack_bf16_pairs(x):
    w = x.shape[1] // 2
    lo = pltpu.bitcast(x[:, :w].astype(BF16).astype(F32), jnp.uint32) >> 16
    hi = pltpu.bitcast(x[:, w:].astype(BF16).astype(F32), jnp.uint32) & jnp.uint32(0xFFFF0000)
    return hi | lo


def _unpack_bf16_pairs(u):
    lo = pltpu.bitcast(u << 16, F32)
    hi = pltpu.bitcast(u & jnp.uint32(0xFFFF0000), F32)
    return jnp.concatenate([lo, hi], axis=1)


def _rms(x, g):
    return x * lax.rsqrt(jnp.mean(x * x, axis=-1, keepdims=True) + RMS_EPS) * g


def _layernorm(x, g, b):
    mu = jnp.mean(x, axis=-1, keepdims=True)
    xc = x - mu
    var = jnp.mean(xc * xc, axis=-1, keepdims=True)
    return xc * lax.rsqrt(var + LN_EPS) * g + b


def _sigmoid(x):
    return 0.5 * jnp.tanh(0.5 * x) + 0.5


def _silu(x):
    return x * _sigmoid(x)


def _softplus(x):
    return jnp.maximum(x, 0.0) + jnp.log1p(jnp.exp(-jnp.abs(x)))


def _even_in_kernel(x_ref, g_ref, wzx_ref, wuv_ref, wdt_ref, lng_ref, lnb_ref, z_ref, xbc_ref, ug_ref, vn_ref,
                    dt_ref, *, d_ssd, conv_dim, d_gmlp):
    h = _rms(x_ref[...], g_ref[...]).astype(BF16)
    z_ref[...] = _dot(h, wzx_ref[:, 0:d_ssd])
    xbc_ref[...] = _dot(h, wzx_ref[:, d_ssd:d_ssd + conv_dim])
    ug_ref[...] = jax.nn.gelu(_dot(h, wuv_ref[:, 0:d_gmlp])).astype(ug_ref.dtype)
    v = jax.nn.gelu(_dot(h, wuv_ref[:, d_gmlp:2 * d_gmlp]))
    vn_ref[...] = _layernorm(v, lng_ref[...], lnb_ref[...])
    dt_ref[...] = _dot(h, wdt_ref[...])


def _even_in(x, g, w_zx, w_uv, w_dt, ln_g, ln_b, *, d_ssd, conv_dim, d_gmlp):
    t, d = x.shape
    tm = _row_tile(t, 512)
    row = lambda n: pl.BlockSpec((tm, n), lambda i: (i, 0))
    consts = (g, w_zx, w_uv, w_dt, ln_g, ln_b)
    return pl.pallas_call(
        functools.partial(_even_in_kernel, d_ssd=d_ssd, conv_dim=conv_dim, d_gmlp=d_gmlp),
        grid=(t // tm,),
        in_specs=[row(d)] + [_resident(c.shape) for c in consts],
        out_specs=[row(d_ssd), row(conv_dim), row(d_gmlp), row(d_gmlp), row(V7X_LANES)],
        out_shape=[jax.ShapeDtypeStruct((t, d_ssd), F32), jax.ShapeDtypeStruct((t, conv_dim), F32),
                   jax.ShapeDtypeStruct((t, d_gmlp), BF16), jax.ShapeDtypeStruct((t, d_gmlp), F32),
                   jax.ShapeDtypeStruct((t, V7X_LANES), F32)],
        compiler_params=_params("parallel"),
        name="even_in",
    )(x, *consts)


def _ssd_prompt_kernel(xbc_ref, z_ref, dt_ref, cw_ref, cb_ref, dtb_ref, alog_ref, dskip_ref, ng_ref, expand_ref,
                       yn_ref, st_ref, xp_ref, state_ref, xc_ref, *, d_ssd, n_taps):
    c = pl.program_id(1)

    @pl.when(c == 0)
    def _():
        state_ref[...] = jnp.zeros_like(state_ref)
        xp_ref[:, 0:V7X_SUBLANES, :] = jnp.zeros((xp_ref.shape[0], V7X_SUBLANES, xp_ref.shape[2]), F32)

    for s in range(xbc_ref.shape[1]):
        _ssd_chunk(xbc_ref.at[0, s], z_ref.at[0, s], dt_ref.at[0, s], cw_ref, cb_ref, dtb_ref, alog_ref, dskip_ref,
                   ng_ref, expand_ref, yn_ref.at[0, s], xp_ref.at[s], state_ref.at[s], xc_ref.at[s],
                   d_ssd=d_ssd, n_taps=n_taps)

    @pl.when(c == pl.num_programs(1) - 1)
    def _():
        for s in range(state_ref.shape[0]):
            for pair in range(state_ref.shape[1]):
                st_ref[s, pair] = state_ref[s, pair].T


def _ssd_chunk(xbc_ref, z_ref, dt_ref, cw_ref, cb_ref, dtb_ref, alog_ref, dskip_ref, ng_ref, expand_ref,
               yn_ref, xp_ref, state_ref, xc_ref, *, d_ssd, n_taps):
    n_pairs = state_ref.shape[0]
    ch = SSD_CHUNK
    gn = SSD_GROUPS * SSD_STATE

    xp_ref[V7X_SUBLANES:V7X_SUBLANES + ch, :] = xbc_ref[...]
    base = V7X_SUBLANES - (n_taps - 1)
    acc = cb_ref[...] + cw_ref[n_taps - 1:n_taps, :] * xbc_ref[...]
    for k in range(n_taps - 1):
        acc = acc + cw_ref[k:k + 1, :] * xp_ref[base + k:base + k + ch, :]
    xp_ref[0:V7X_SUBLANES, :] = xp_ref[ch:ch + V7X_SUBLANES, :]
    xc_ref[...] = _silu(acc)
    xc = xc_ref

    dt = _softplus(dt_ref[...] + dtb_ref[...])
    a = dt * (-jnp.exp(alog_ref[...]))
    def spread(v, ones_ref):
        hi = v.astype(BF16)
        rest = v - hi.astype(F32)
        mid = rest.astype(BF16)
        lo = (rest - mid.astype(F32)).astype(BF16)
        return _dot(hi, ones_ref[...]) + (_dot(mid, ones_ref[...]) + _dot(lo, ones_ref[...]))

    dt_wide = spread(dt, expand_ref)
    li = lax.broadcasted_iota(jnp.int32, (ch, ch), 0)
    si = lax.broadcasted_iota(jnp.int32, (ch, ch), 1)
    causal = li >= si
    tril = jnp.where(causal, 1.0, 0.0).astype(F32)
    acs = jnp.dot(tril, a, precision=HIGHEST, preferred_element_type=F32)
    acs_t = acs.T
    lane = lax.broadcasted_iota(jnp.int32, (ch, 2 * SSD_HEAD_DIM), 1)
    first = lane < SSD_HEAD_DIM
    first_n = lax.broadcasted_iota(jnp.int32, (SSD_STATE, 2 * SSD_HEAD_DIM), 1) < SSD_HEAD_DIM
    pairs_per_group = n_pairs // SSD_GROUPS

    for g in range(SSD_GROUPS):
        bg = xc[:, d_ssd + g * SSD_STATE:d_ssd + (g + 1) * SSD_STATE]
        cg = xc[:, d_ssd + gn + g * SSD_STATE:d_ssd + gn + (g + 1) * SSD_STATE]
        bg_t = bg.T
        cb = _dot(cg.astype(BF16), bg_t.astype(BF16))
        ys = []
        for q in range(pairs_per_group):
            pair = g * pairs_per_group + q
            h0 = 2 * pair
            lo = pair * 2 * SSD_HEAD_DIM
            xs = xc[:, lo:lo + 2 * SSD_HEAD_DIM]
            xdt = (xs * dt_wide[:, lo:lo + 2 * SSD_HEAD_DIM]).astype(BF16)
            s_prev = state_ref[pair]
            s_prev_b = s_prev.astype(BF16)
            y2, snew2, dec2 = [], [], []
            for r in range(2):
                h = h0 + r
                col = acs[:, h:h + 1]
                row = acs_t[h:h + 1, :]
                last = acs_t[h:h + 1, ch - 1:ch]
                decay = jnp.exp(jnp.where(causal, col - row, -jnp.inf))
                y = _dot((cb * decay).astype(BF16), xdt)
                y = y + _dot((cg * jnp.exp(col)).astype(BF16), s_prev_b)
                snew = _dot((bg_t * jnp.exp(last - row)).astype(BF16), xdt)
                y2.append(y)
                snew2.append(snew)
                dec2.append(jnp.exp(last))
            y = jnp.where(first, y2[0], y2[1])
            state_ref[pair] = (s_prev * jnp.where(first_n, dec2[0], dec2[1])
                               + jnp.where(first_n, snew2[0], snew2[1]))
            ys.append(y + dskip_ref[:, lo:lo + 2 * SSD_HEAD_DIM] * xs)
        gw = pairs_per_group * 2 * SSD_HEAD_DIM
        yg = jnp.concatenate(ys, axis=-1) * _silu(z_ref[:, g * gw:(g + 1) * gw])
        yg = yg * lax.rsqrt(jnp.mean(yg * yg, axis=-1, keepdims=True) + RMS_EPS)
        yn_ref[:, g * gw:(g + 1) * gw] = (yg * ng_ref[:, g * gw:(g + 1) * gw]).astype(yn_ref.dtype)


def _ssd_prompt(xbc, z, dt, conv_w, conv_b, dt_bias, a_log, d_skip, norm_g, expand, *, batch, seq):
    t, conv_dim = xbc.shape
    d_ssd = z.shape[1]
    n_pairs = d_ssd // (2 * SSD_HEAD_DIM)
    nc = seq // SSD_CHUNK
    n_taps = conv_w.shape[0]
    g = SSD_SEQS_PER_STEP if batch % SSD_SEQS_PER_STEP == 0 else 1
    tile = lambda n: pl.BlockSpec((1, g, SSD_CHUNK, n), lambda b, c: (b, 0, c, 0))
    by_seq = lambda a: a.reshape(batch // g, g, seq, a.shape[1])
    yn, st = pl.pallas_call(
        functools.partial(_ssd_prompt_kernel, d_ssd=d_ssd, n_taps=n_taps),
        grid=(batch // g, nc),
        in_specs=[tile(conv_dim), tile(d_ssd), tile(V7X_LANES), _resident(conv_w.shape), _resident(conv_b.shape),
                  _resident(dt_bias.shape), _resident(a_log.shape), _resident(d_skip.shape),
                  _resident(norm_g.shape), _resident(expand.shape)],
        out_specs=[tile(d_ssd),
                   pl.BlockSpec((g, n_pairs, 2 * SSD_HEAD_DIM, SSD_STATE), lambda b, c: (b, 0, 0, 0))],
        out_shape=[jax.ShapeDtypeStruct((batch // g, g, seq, d_ssd), BF16),
                   jax.ShapeDtypeStruct((batch, n_pairs, 2 * SSD_HEAD_DIM, SSD_STATE), F32)],
        scratch_shapes=[pltpu.VMEM((g, SSD_CHUNK + V7X_SUBLANES, conv_dim), F32),
                        pltpu.VMEM((g, n_pairs, SSD_STATE, 2 * SSD_HEAD_DIM), F32),
                        pltpu.VMEM((g, SSD_CHUNK, conv_dim), F32)],
        compiler_params=_params("parallel", "arbitrary"),
        name="ssd_prompt",
    )(by_seq(xbc), by_seq(z), by_seq(dt), conv_w, conv_b, dt_bias, a_log, d_skip, norm_g, expand)
    return yn.reshape(t, d_ssd), st


def _gmlp_gate(ug_ref, vn_ref, ws_ref, bs_ref, out_ref):
    ch = GMLP_CHUNK
    n_groups = ws_ref.shape[0]
    gd = vn_ref.shape[1] // n_groups
    ii = lax.broadcasted_iota(jnp.int32, (ch, ch), 0)
    jj = lax.broadcasted_iota(jnp.int32, (ch, ch), 1)
    for g in range(n_groups):
        ws = jnp.where(ii >= jj, ws_ref[g], 0.0).astype(BF16)
        cols = slice(g * gd, (g + 1) * gd)
        for r0 in range(0, vn_ref.shape[0], ch):
            rows = slice(r0, r0 + ch)
            mixed = _dot(ws, vn_ref[rows, cols].astype(BF16)) + bs_ref[:, g:g + 1]
            out_ref[rows, cols] = (ug_ref[rows, cols].astype(F32) * mixed).astype(out_ref.dtype)


def _ssd_sample_prep_kernel(xbc_ref, buf_ref, dt_ref, cw_ref, cb_ref, dtb_ref, alog_ref, expand_ref,
                            newbuf_ref, xs_ref, xdt_ref, bc_ref, da_ref, *, d_ssd, n_taps):
    x_new = xbc_ref[...]
    acc = cb_ref[...] + cw_ref[n_taps - 1:n_taps, :] * x_new
    for k in range(n_taps - 1):
        acc = acc + cw_ref[k:k + 1, :] * buf_ref[k]
    newbuf_ref[0:n_taps - 2] = buf_ref[1:n_taps - 1]
    newbuf_ref[n_taps - 2] = x_new
    xc = _silu(acc)
    xs = xc[:, :d_ssd]
    dt = _softplus(dt_ref[...] + dtb_ref[...])
    da_ref[...] = jnp.exp(dt * (-jnp.exp(alog_ref[...])))
    dt_wide = jnp.dot(dt, expand_ref[...], precision=HIGHEST, preferred_element_type=F32)
    xs_ref[...] = xs
    xdt_ref[...] = xs * dt_wide
    bc_ref[...] = xc[:, d_ssd:]


def _ssd_sample_prep(xbc, buf, dt, conv_w, conv_b, dt_bias, a_log, expand):
    n, conv_dim = xbc.shape
    d_ssd = expand.shape[1]
    n_taps = conv_w.shape[0]
    args = (xbc, buf, dt, conv_w, conv_b, dt_bias, a_log, expand)
    return pl.pallas_call(
        functools.partial(_ssd_sample_prep_kernel, d_ssd=d_ssd, n_taps=n_taps),
        grid=(1,),
        in_specs=[_resident(a.shape) for a in args],
        out_specs=[_whole(buf.shape), _whole((n, d_ssd)), _whole((n, d_ssd)),
                   _whole((n, conv_dim - d_ssd)), _whole((n, V7X_LANES))],
        out_shape=[jax.ShapeDtypeStruct(buf.shape, F32), jax.ShapeDtypeStruct((n, d_ssd), F32),
                   jax.ShapeDtypeStruct((n, d_ssd), F32), jax.ShapeDtypeStruct((n, conv_dim - d_ssd), F32),
                   jax.ShapeDtypeStruct((n, V7X_LANES), F32)],
        compiler_params=_params("arbitrary"),
        name="ssd_sample_prep",
    )(*args)


def _ssd_sample_state_kernel(h_ref, xdt_ref, da_ref, bc_ref, hnew_ref, y_ref, xcol_ref, ccol_ref):
    bs, n_heads, p, s = h_ref.shape
    n_pairs = n_heads // 2
    pairs_per_group = n_pairs // SSD_GROUPS
    gn = SSD_GROUPS * s
    w = 2 * p
    eye = lax.broadcasted_iota(jnp.int32, (w, w), 0) == lax.broadcasted_iota(jnp.int32, (w, w), 1)
    ones = jnp.ones((w, V7X_LANES), BF16)

    def stacked_diag(v):
        return jnp.concatenate(
            [jnp.where(eye, jnp.broadcast_to(v[j:j + 1], (w, w)), 0.0) for j in range(bs)], axis=0).astype(BF16)

    for q in range(n_pairs):
        x = xdt_ref[:, q * w:(q + 1) * w]
        hi = x.astype(BF16).astype(F32)
        mid = (x - hi).astype(BF16).astype(F32)
        lo = (x - hi) - mid
        xcol_ref[q] = _dot(stacked_diag(hi), ones) + (_dot(stacked_diag(mid), ones) + _dot(stacked_diag(lo), ones))
    for g in range(SSD_GROUPS):
        ccol_ref[g] = _dot(stacked_diag(bc_ref[:, gn + g * s:gn + (g + 1) * s]), ones).astype(BF16)

    for j in range(bs):
        for g in range(SSD_GROUPS):
            brow = bc_ref[j:j + 1, g * s:(g + 1) * s]
            h_new = []
            for q in range(g * pairs_per_group, (g + 1) * pairs_per_group):
                for r in range(2):
                    h = 2 * q + r
                    xcol = xcol_ref[q, j * w + r * p:j * w + (r + 1) * p, :]
                    h_new.append(h_ref[j, h] * da_ref[j:j + 1, h:h + 1] + xcol * brow)
                    hnew_ref[j, h] = h_new[-1]
            hc = _dot(jnp.concatenate(h_new, axis=0).astype(BF16), ccol_ref[g, j * s:(j + 1) * s, :])
            for i in range(pairs_per_group):
                y = jnp.sum(jnp.where(eye, hc[i * w:(i + 1) * w], 0.0), axis=0, keepdims=True)
                q = g * pairs_per_group + i
                y_ref[j, q:q + 1, :] = y


def _ssd_sample_state(h, xdt, da, bc):
    n, n_heads, p, s = h.shape
    assert 2 * p == V7X_LANES and s == V7X_LANES
    bs = SAMPLE_STATE_TILE
    rows = lambda w: pl.BlockSpec((bs, w), lambda i: (i, 0))
    return pl.pallas_call(
        _ssd_sample_state_kernel,
        grid=(n // bs,),
        in_specs=[pl.BlockSpec((bs, n_heads, p, s), lambda i: (i, 0, 0, 0)), rows(xdt.shape[1]), rows(da.shape[1]),
                  rows(bc.shape[1])],
        out_specs=[pl.BlockSpec((bs, n_heads, p, s), lambda i: (i, 0, 0, 0)),
                   pl.BlockSpec((bs, n_heads // 2, 2 * p), lambda i: (i, 0, 0))],
        out_shape=[jax.ShapeDtypeStruct(h.shape, F32), jax.ShapeDtypeStruct((n, n_heads // 2, 2 * p), F32)],
        scratch_shapes=[pltpu.VMEM((n_heads // 2, bs * 2 * p, V7X_LANES), F32),
                        pltpu.VMEM((SSD_GROUPS, bs * s, V7X_LANES), BF16)],
        compiler_params=_params("parallel"),
        name="ssd_sample_state",
    )(h, xdt, da, bc)


def _even_sample_post_kernel(y_ref, xs_ref, z_ref, dskip_ref, ng_ref, ug_ref, vn_ref, w0_ref, b0_ref,
                             yn_ref, gout_ref):
    d_ssd = y_ref.shape[1]
    gw = d_ssd // SSD_GROUPS
    y = (y_ref[...] + dskip_ref[...] * xs_ref[...]) * _silu(z_ref[...])
    for g in range(SSD_GROUPS):
        yg = y[:, g * gw:(g + 1) * gw]
        yg = yg * lax.rsqrt(jnp.mean(yg * yg, axis=-1, keepdims=True) + RMS_EPS)
        yn_ref[:, g * gw:(g + 1) * gw] = (yg * ng_ref[:, g * gw:(g + 1) * gw]).astype(yn_ref.dtype)
    mixed = w0_ref[...] * vn_ref[...] + b0_ref[...]
    gout_ref[...] = (ug_ref[...].astype(F32) * mixed).astype(gout_ref.dtype)


def _even_sample_post(y, xs, z, d_skip, norm_g, ug, vn, w0, b0):
    args = (y, xs, z, d_skip, norm_g, ug, vn, w0, b0)
    return pl.pallas_call(
        _even_sample_post_kernel,
        grid=(1,),
        in_specs=[_resident(a.shape) for a in args],
        out_specs=[_whole(y.shape), _whole(vn.shape)],
        out_shape=[jax.ShapeDtypeStruct(y.shape, BF16), jax.ShapeDtypeStruct(vn.shape, BF16)],
        compiler_params=_params("arbitrary"),
        name="even_sample_post",
    )(*args)


def _even_out_kernel(*refs, fuse_gmlp):
    if fuse_gmlp:
        (x_ref, yn_ref, ug_ref, vn_ref, ws_ref, bs_ref, wo_ref, g_ref, wg_ref, wu_ref, wd_ref, gc_ref, w1_ref,
         b1_ref, out_ref, a_ref, gout_ref) = refs
        _gmlp_gate(ug_ref, vn_ref, ws_ref, bs_ref, gout_ref)
    else:
        (x_ref, yn_ref, gout_ref, wo_ref, g_ref, wg_ref, wu_ref, wd_ref, gc_ref, w1_ref, b1_ref,
         out_ref, a_ref) = refs
    d_ssd = yn_ref.shape[1]
    x = x_ref[...] + _dot(yn_ref[...], wo_ref[0:d_ssd, :]) + _dot(gout_ref[...], wo_ref[d_ssd:, :])
    h = _rms(x, g_ref[...]).astype(BF16)
    mid = (_silu(_dot(h, wg_ref[...])) * _dot(h, wu_ref[...])).astype(BF16)
    x = x + _dot(mid, wd_ref[...])
    out_ref[...] = x
    dc = a_ref.shape[1]
    h = _rms(x, gc_ref[...]).astype(BF16)
    val = _dot(h, w1_ref[:, 0:dc]) + b1_ref[:, 0:dc]
    gate = _dot(h, w1_ref[:, dc:]) + b1_ref[:, dc:]
    a_ref[...] = val * _sigmoid(gate)


def _even_out(x, yn, gate_in, weights, *, seq=None):
    t, d = x.shape
    dc = weights[-2].shape[1] // 2
    fuse_gmlp = isinstance(gate_in, tuple)
    tm = _row_tile(t if seq is None else seq, 256)
    row = lambda n: pl.BlockSpec((tm, n), lambda i: (i, 0))
    if fuse_gmlp:
        ug, vn, w_s, b_s_t = gate_in
        assert tm % GMLP_CHUNK == 0
        gate_args = (ug, vn, w_s, b_s_t)
        gate_specs = [row(ug.shape[1]), row(vn.shape[1]), _resident(w_s.shape), _resident(b_s_t.shape)]
        scratch = [pltpu.VMEM((tm, ug.shape[1]), BF16)]
    else:
        gate_args, gate_specs, scratch = (gate_in,), [row(gate_in.shape[1])], []
    return pl.pallas_call(
        functools.partial(_even_out_kernel, fuse_gmlp=fuse_gmlp),
        grid=(t // tm,),
        in_specs=[row(d), row(yn.shape[1])] + gate_specs + [_resident(w.shape) for w in weights],
        out_specs=[row(d), row(dc)],
        out_shape=[jax.ShapeDtypeStruct((t, d), F32), jax.ShapeDtypeStruct((t, dc), F32)],
        scratch_shapes=scratch,
        compiler_params=_params("parallel"),
        name="even_out_ffn",
    )(x, yn, *gate_args, *weights)


def _route(logits, n_experts, route_ref, route_t_ref):
    lane = lax.broadcasted_iota(jnp.int32, logits.shape, 1)
    lane_f = lane.astype(F32)
    big = jnp.float32(V7X_LANES)
    lg = jnp.where(lane < n_experts, logits, -jnp.inf)
    m1 = jnp.max(lg, axis=-1, keepdims=True)
    i1 = jnp.min(jnp.where(lg == m1, lane_f, big), axis=-1, keepdims=True)
    lg2 = jnp.where(lane_f == i1, -jnp.inf, lg)
    m2 = jnp.max(lg2, axis=-1, keepdims=True)
    i2 = jnp.min(jnp.where(lg2 == m2, lane_f, big), axis=-1, keepdims=True)
    e2 = jnp.exp(m2 - m1)
    g1 = 1.0 / (1.0 + e2)
    g2 = e2 / (1.0 + e2)
    route = jnp.where(lane == 0, i1, jnp.where(lane == 1, i2, jnp.where(lane == 2, g1,
                      jnp.where(lane == 3, g2, 0.0))))
    route_ref[...] = route
    route_t_ref[...] = route.T[0:V7X_SUBLANES, :]


def _conf_finish(c, x, lng_ref, lnb_ref, w2_ref, b2_ref, ng_ref, rhi_ref, rlo_ref, out_ref, hn_ref, route_ref,
                 route_t_ref, *, n_experts):
    hmid = _silu(_layernorm(c, lng_ref[...], lnb_ref[...])).astype(BF16)
    x_new = x + _dot(hmid, w2_ref[...]) + b2_ref[...]
    out_ref[...] = x_new
    hn = _rms(x_new, ng_ref[...])
    hn_ref[...] = _pack_bf16_pairs(hn)
    hi = hn.astype(BF16)
    lo = (hn - hi.astype(F32)).astype(BF16)
    logits = _dot(hi, rhi_ref[...]) + (_dot(lo, rhi_ref[...]) + _dot(hi, rlo_ref[...]))
    _route(logits, n_experts, route_ref, route_t_ref)


def _conf_prompt_kernel(a_ref, x_ref, xs_ref, hns_ref, routes_ref, dw_ref, db_ref, lng_ref, lnb_ref, w2_ref, b2_ref,
                        ng_ref, rhi_ref, rlo_ref, out_ref, hn_ref, route_ref, route_t_ref, xp_ref, c_ref, win_ref,
                        *, n_taps, n_experts, tiles_per_seq, n_prompt_tiles):
    tl, d = a_ref.shape
    halo = xp_ref.shape[0] - tl
    base = halo - (n_taps - 1)
    rows = CONF_ACC_ROWS
    step = pl.program_id(0)

    @pl.when(step < n_prompt_tiles)
    def _():
        @pl.when(step % tiles_per_seq == 0)
        def _():
            xp_ref[0:halo, :] = jnp.zeros((halo, d), F32)

        xp_ref[halo:halo + tl, :] = a_ref[...]

        def lane_block(j, carry):
            cols = pl.ds(pl.multiple_of(j * V7X_LANES, V7X_LANES), V7X_LANES)
            for r0 in range(0, tl, rows):
                acc = jnp.broadcast_to(db_ref[:, cols], (rows, V7X_LANES))
                for phase in range(V7X_SUBLANES):
                    taps = [k for k in range(n_taps) if (base + k) % V7X_SUBLANES == phase]
                    if not taps:
                        continue
                    span = max(base + k - phase for k in taps) + rows
                    if phase:
                        win_ref[0:span, :] = xp_ref[pl.ds(r0 + phase, span), cols]
                    for k in taps:
                        o = base + k - phase
                        win = win_ref[o:o + rows, :] if phase else xp_ref[pl.ds(r0 + o, rows), cols]
                        acc = acc + dw_ref[k:k + 1, cols] * win
                c_ref[r0:r0 + rows, cols] = acc
            return carry

        lax.fori_loop(0, d // V7X_LANES, lane_block, 0)
        xp_ref[0:halo, :] = xp_ref[tl:tl + halo, :]
        _conf_finish(c_ref[...], x_ref[...], lng_ref, lnb_ref, w2_ref, b2_ref, ng_ref, rhi_ref, rlo_ref,
                     out_ref, hn_ref, route_ref, route_t_ref, n_experts=n_experts)

    @pl.when(step == n_prompt_tiles)
    def _():
        ns = xs_ref.shape[0]
        out_ref[0:ns, :] = xs_ref[...]
        hn_ref[0:ns, :] = hns_ref[...]
        route_ref[0:ns, :] = routes_ref[...]


def _conf_prompt(a, x, sample_rows, consts, *, seq, n_experts):
    t, d = a.shape
    n_s = sample_rows[0].shape[0]
    tl = _row_tile(seq, CONF_TILE)
    assert tl % CONF_ACC_ROWS == 0 and n_s <= tl
    n_prompt_tiles = t // tl
    n_taps = consts[0].shape[0]
    halo = -(-(n_taps - 1) // V7X_SUBLANES) * V7X_SUBLANES
    in_tile = lambda n: pl.BlockSpec((tl, n), lambda s: (jnp.minimum(s, n_prompt_tiles - 1), 0))
    out_tile = lambda n: pl.BlockSpec((tl, n), lambda s: (s, 0))
    small = sample_rows + tuple(consts)
    return pl.pallas_call(
        functools.partial(_conf_prompt_kernel, n_taps=n_taps, n_experts=n_experts, tiles_per_seq=seq // tl,
                          n_prompt_tiles=n_prompt_tiles),
        grid=(n_prompt_tiles + 1,),
        in_specs=[in_tile(d), in_tile(d)] + [_resident(c.shape) for c in small],
        out_specs=[out_tile(d), out_tile(d // 2), out_tile(V7X_LANES),
                   pl.BlockSpec((V7X_SUBLANES, tl), lambda s: (0, jnp.minimum(s, n_prompt_tiles - 1)))],
        out_shape=[jax.ShapeDtypeStruct((t + n_s, d), F32), jax.ShapeDtypeStruct((t + n_s, d // 2), jnp.uint32),
                   jax.ShapeDtypeStruct((t + n_s, V7X_LANES), F32), jax.ShapeDtypeStruct((V7X_SUBLANES, t), F32)],
        scratch_shapes=[pltpu.VMEM((tl + halo, d), F32), pltpu.VMEM((tl, d), F32),
                        pltpu.VMEM((CONF_ACC_ROWS + halo, V7X_LANES), F32)],
        compiler_params=_params("arbitrary"),
        name="conf_prompt",
    )(a, x, *small)


def _conf_sample_conv_kernel(a_ref, buf_ref, dw_ref, db_ref, newbuf_ref, c_ref, *, n_taps):
    a_new = a_ref[...]
    acc = db_ref[...] + dw_ref[n_taps - 1:n_taps, :] * a_new
    for k in range(n_taps - 1):
        acc = acc + dw_ref[k:k + 1, :] * buf_ref[k]
    c_ref[...] = acc
    newbuf_ref[0:n_taps - 2] = buf_ref[1:n_taps - 1]
    newbuf_ref[n_taps - 2] = a_new


def _conf_sample_conv(a, buf, dw_w, dw_b):
    n, d = a.shape
    bs = CONF_SAMPLE_TILE
    hist = pl.BlockSpec((buf.shape[0], bs, d), lambda i: (0, i, 0))
    rows = pl.BlockSpec((bs, d), lambda i: (i, 0))
    return pl.pallas_call(
        functools.partial(_conf_sample_conv_kernel, n_taps=dw_w.shape[0]),
        grid=(n // bs,),
        in_specs=[rows, hist, _resident(dw_w.shape), _resident(dw_b.shape)],
        out_specs=[hist, rows],
        out_shape=[jax.ShapeDtypeStruct(buf.shape, F32), jax.ShapeDtypeStruct((n, d), F32)],
        compiler_params=_params("parallel"),
        name="conf_sample_conv",
    )(a, buf, dw_w, dw_b)


def _conf_sample_kernel(c_ref, x_ref, lng_ref, lnb_ref, w2_ref, b2_ref, ng_ref, rhi_ref, rlo_ref,
                        out_ref, hn_ref, route_ref, route_t_ref, *, n_experts):
    _conf_finish(c_ref[...], x_ref[...], lng_ref, lnb_ref, w2_ref, b2_ref, ng_ref, rhi_ref, rlo_ref,
                 out_ref, hn_ref, route_ref, route_t_ref, n_experts=n_experts)


def _conf_sample(c, x, consts, *, n_experts):
    n, d = c.shape
    args = (c, x) + tuple(consts)
    return pl.pallas_call(
        functools.partial(_conf_sample_kernel, n_experts=n_experts),
        grid=(1,),
        in_specs=[_resident(v.shape) for v in args],
        out_specs=[_whole((n, d)), _whole((n, d // 2)), _whole((n, V7X_LANES)), _whole((V7X_SUBLANES, n))],
        out_shape=[jax.ShapeDtypeStruct((n, d), F32), jax.ShapeDtypeStruct((n, d // 2), jnp.uint32),
                   jax.ShapeDtypeStruct((n, V7X_LANES), F32), jax.ShapeDtypeStruct((V7X_SUBLANES, n), F32)],
        compiler_params=_params("arbitrary"),
        name="conf_sample",
    )(*args)


def _sc_window_rows(x):
    assert x.dtype.itemsize == 4, "the SparseCore indirect copies move 32-bit elements"
    return SC_WINDOW_BYTES // (x.shape[1] * x.dtype.itemsize)


def _sc_gather_rows(x, idx):
    n, d = idx.shape[0], x.shape[1]
    w = _sc_window_rows(x)
    assert n % w == 0
    mesh = plsc.VectorSubcoreMesh(core_axis_name="core", subcore_axis_name="subcore")

    @functools.partial(pl.kernel, out_type=jax.ShapeDtypeStruct((n, d), x.dtype), mesh=mesh, scratch_types=[])
    def gather(x_hbm, idx_hbm, out_hbm):
        def body(idx_vmem, out_vmem):
            pltpu.sync_copy(x_hbm.at[idx_vmem.at[0]], out_vmem)

        pltpu.emit_pipeline(
            body,
            grid=(n // w,),
            in_specs=[pl.BlockSpec((1, w), lambda i: (i, 0))],
            out_specs=[pl.BlockSpec((w, d), lambda i: (i, 0))],
            core_axis_name=("core", "subcore"),
            dimension_semantics=(pltpu.PARALLEL,),
        )(idx_hbm, out_hbm)

    return gather(x, idx.reshape(n // w, w))


def _sc_dispatch_rows(x, pos, pad_pos, n_out):
    t, d = x.shape
    w = _sc_window_rows(x)
    n_pad = pad_pos.shape[0]
    assert t % w == 0 and n_pad % w == 0 and TOP_K * t + n_pad == n_out
    mesh = plsc.VectorSubcoreMesh(core_axis_name="core", subcore_axis_name="subcore")

    @functools.partial(pl.kernel, out_type=jax.ShapeDtypeStruct((n_out, d), x.dtype), mesh=mesh, scratch_types=[])
    def dispatch(x_hbm, pos_hbm, pad_hbm, zero_hbm, out_hbm):
        def body(rows_vmem, idx_vmem):
            pltpu.sync_copy(rows_vmem, out_hbm.at[idx_vmem.at[0]])

        def scatter(src_hbm, idx_hbm, src_map):
            pltpu.emit_pipeline(
                body,
                grid=(idx_hbm.shape[0],),
                in_specs=[pl.BlockSpec((w, d), src_map), pl.BlockSpec((1, w), lambda i: (i, 0))],
                out_specs=[],
                core_axis_name=("core", "subcore"),
                dimension_semantics=(pltpu.PARALLEL,),
            )(src_hbm, idx_hbm)

        for k in range(TOP_K):
            scatter(x_hbm, pos_hbm.at[k], lambda i: (i, 0))
        scatter(zero_hbm, pad_hbm, lambda i: (0, 0))

    return dispatch(x, pos.reshape(TOP_K, t // w, w), pad_pos.reshape(n_pad // w, w), jnp.zeros((w, d), x.dtype))


def _moe_ffn_kernel(tile_expert_ref, first_ref, bank_ref, next_expert_ref, n_tiles_ref, x_ref, wg_hbm, wu_hbm,
                    wd_hbm, out_ref, wg_v, wu_v, wd_v, stage_g, stage_u, stage_d, acc_ref, sems):
    i = pl.program_id(0)
    dff = wg_v.shape[2]
    ck = stage_g.shape[2]
    n_chunks = dff // ck
    active = i < n_tiles_ref[0]
    starts_run = active & (first_ref[i] == 1)
    bank = bank_ref[i]
    nxt = next_expert_ref[i]

    def chunk_copies(expert, c, slot):
        cols = slice(c * ck, (c + 1) * ck)
        return (pltpu.make_async_copy(wg_hbm.at[expert, :, cols], stage_g.at[slot], sems.at[0, slot]),
                pltpu.make_async_copy(wu_hbm.at[expert, :, cols], stage_u.at[slot], sems.at[1, slot]),
                pltpu.make_async_copy(wd_hbm.at[expert, cols, :], stage_d.at[slot], sems.at[2, slot]))

    def stream_weights(expert, dst_bank, between_chunks):
        for cp in chunk_copies(expert, 0, 0):
            cp.start()
        for c in range(n_chunks):
            slot = c % 2
            if c + 1 < n_chunks:
                for cp in chunk_copies(expert, c + 1, 1 - slot):
                    cp.start()
            for cp in chunk_copies(expert, c, slot):
                cp.wait()
            cols = slice(c * ck, (c + 1) * ck)
            wg_v[dst_bank, :, cols] = stage_g[slot].astype(BF16)
            wu_v[dst_bank, :, cols] = stage_u[slot].astype(BF16)
            wd_v[dst_bank, cols, :] = stage_d[slot].astype(BF16)
            between_chunks(c)

    def ffn_part(xb, cols):
        mid = (_silu(_dot(xb, wg_v[bank, :, cols])) * _dot(xb, wu_v[bank, :, cols])).astype(BF16)
        return _dot(mid, wd_v[bank, cols, :])

    @pl.when(starts_run & (i == 0))
    def _():
        stream_weights(tile_expert_ref[i], bank, lambda c: None)

    overlapped = starts_run & (nxt >= 0)

    @pl.when(overlapped)
    def _():
        xb = _unpack_bf16_pairs(x_ref[...]).astype(BF16)
        acc_ref[...] = jnp.zeros_like(acc_ref)

        def piece(c):
            acc_ref[...] += ffn_part(xb, slice(c * ck, (c + 1) * ck))

        stream_weights(nxt, 1 - bank, piece)
        out_ref[...] = _pack_bf16_pairs(acc_ref[...])

    @pl.when(active & jnp.logical_not(overlapped))
    def _():
        xb = _unpack_bf16_pairs(x_ref[...]).astype(BF16)
        half = dff // 2
        acc = ffn_part(xb, slice(0, half)) + ffn_part(xb, slice(half, dff))
        out_ref[...] = _pack_bf16_pairs(acc)

    @pl.when(jnp.logical_not(active))
    def _():
        out_ref[...] = jnp.zeros_like(out_ref)


def _moe_ffn(tile_expert, n_tiles, xs, wg, wu, wd, *, tm):
    p, d = xs.shape[0], wg.shape[1]
    dff = wg.shape[2]
    ck = MOE_FF_CHUNK
    assert dff % ck == 0
    nt = p // tm
    i32 = jnp.int32
    tile = jnp.arange(nt, dtype=i32)
    first = jnp.concatenate([jnp.ones((1,), i32), (tile_expert[1:] != tile_expert[:-1]).astype(i32)])
    bank = (jnp.cumsum(first) - 1) % 2
    run_start = jnp.where((first == 1) & (tile < n_tiles[0]), tile, nt)
    next_start = jnp.concatenate([lax.cummin(run_start, reverse=True)[1:], jnp.full((1,), nt, i32)])
    next_expert = jnp.sum((next_start[:, None] == tile[None, :]).astype(i32) * tile_expert[None, :], axis=1)
    next_expert = jnp.where(next_start < nt, next_expert, -1)
    any_spec = pl.BlockSpec(memory_space=pl.ANY)
    row_tile = pl.BlockSpec((tm, d // 2), lambda i, *_: (i, 0))
    grid_spec = pltpu.PrefetchScalarGridSpec(
        num_scalar_prefetch=5,
        grid=(nt,),
        in_specs=[row_tile, any_spec, any_spec, any_spec],
        out_specs=row_tile,
        scratch_shapes=[pltpu.VMEM((2, d, dff), BF16), pltpu.VMEM((2, d, dff), BF16), pltpu.VMEM((2, dff, d), BF16),
                        pltpu.VMEM((2, d, ck), F32), pltpu.VMEM((2, d, ck), F32), pltpu.VMEM((2, ck, d), F32),
                        pltpu.VMEM((tm, d), F32), pltpu.SemaphoreType.DMA((3, 2))],
    )
    return pl.pallas_call(
        _moe_ffn_kernel,
        grid_spec=grid_spec,
        out_shape=jax.ShapeDtypeStruct((p, d // 2), jnp.uint32),
        compiler_params=_params("arbitrary"),
        name="moe_ffn",
    )(tile_expert, first, bank.astype(i32), next_expert.astype(i32), n_tiles, xs, wg, wu, wd)


def _moe_combine_kernel(x_ref, route_ref, gf_ref, a_ref, b_ref, y_ref):
    route = route_ref[...]
    x = (x_ref[...] + route[:, 2:3] * _unpack_bf16_pairs(a_ref[...])
         + route[:, 3:4] * _unpack_bf16_pairs(b_ref[...]))
    y_ref[...] = _rms(x, gf_ref[...])


def _moe_combine(x, route, gf, picked, *, row0, n_rows, a_row0, b_row0):
    d = x.shape[1]
    tc = _row_tile(n_rows, COMBINE_TILE)
    assert row0 % tc == 0 and a_row0 % tc == 0 and b_row0 % tc == 0
    rows = lambda w, r0: pl.BlockSpec((tc, w), lambda i: (i + r0 // tc, 0))
    return pl.pallas_call(
        _moe_combine_kernel,
        grid=(n_rows // tc,),
        in_specs=[rows(d, row0), rows(V7X_LANES, row0), _resident(gf.shape), rows(d // 2, a_row0),
                  rows(d // 2, b_row0)],
        out_specs=pl.BlockSpec((tc, d), lambda i: (i, 0)),
        out_shape=jax.ShapeDtypeStruct((n_rows, d), F32),
        compiler_params=_params("parallel"),
        name="moe_combine",
    )(x, route, gf, picked, picked)


def _moe_plan(route_t, n_experts, tm):
    t = route_t.shape[1]
    n_slots = t * TOP_K
    i32 = jnp.int32
    expert = route_t[:TOP_K].astype(i32).reshape(1, n_slots)
    onehot = (expert == jnp.arange(n_experts, dtype=i32)[:, None]).astype(i32)
    counts = jnp.sum(onehot, axis=1)
    tiles = (counts + tm - 1) // tm
    tile_end = jnp.cumsum(tiles)
    start = (tile_end - tiles) * tm
    rank = jnp.cumsum(onehot, axis=1) - onehot
    pos = jnp.sum(onehot * (rank + start[:, None]), axis=0)
    nt = -(-n_slots // tm) + n_experts
    n_tiles = tile_end[-1]
    tile_id = jnp.minimum(jnp.arange(nt, dtype=i32), n_tiles - 1)
    tile_expert = jnp.sum((tile_id[:, None] >= tile_end[None, :]).astype(i32), axis=1)
    seg_start = jnp.concatenate([start + counts, (n_tiles * tm)[None]])
    seg_len = jnp.concatenate([tiles * tm - counts, (nt * tm - n_tiles * tm)[None]])
    seg_end = jnp.cumsum(seg_len)
    q = jnp.arange(nt * tm - n_slots, dtype=i32)
    in_seg = (q[:, None] >= (seg_end - seg_len)[None, :]) & (q[:, None] < seg_end[None, :])
    pad_pos = jnp.sum(in_seg.astype(i32) * (seg_start - (seg_end - seg_len))[None, :], axis=1) + q
    return (tile_expert.astype(i32), n_tiles.astype(i32).reshape(1), pos.reshape(TOP_K, t).astype(i32),
            pad_pos.astype(i32), nt * tm)


def _row(v):
    return v.reshape(1, -1).astype(F32)


def _pad_lanes(v):
    return jnp.pad(_row(v), ((0, 0), (0, V7X_LANES - v.shape[-1])))


def kernel(x_prompt, x_sample, state_ssm, state_ssd_conv, state_conf_conv, norm_mix_even, w_in_even, ssd_conv_w, ssd_conv_b, ssd_dt_bias, ssd_a_log, ssd_d, ssd_norm, gmlp_ln_g, gmlp_ln_b, gmlp_w_s, gmlp_b_s, w_out_even, norm_ffn_even, ffn_w_gate, ffn_w_up, ffn_w_down, norm_mix_odd, conf_w1, conf_b1, conf_dw_w, conf_dw_b, conf_ln_g, conf_ln_b, conf_w2, conf_b2, norm_ffn_odd, moe_router, moe_w_gate, moe_w_up, moe_w_down, final_norm):
    batch, seq, d_model = x_prompt.shape
    n_dec, dec_seq, _ = x_sample.shape
    assert dec_seq == 1, "the sample group advances one token per sequence"
    assert seq % SSD_CHUNK == 0 and n_dec % V7X_LANES == 0
    n_even, n_odd = w_in_even.shape[0], conf_w1.shape[0]
    assert (n_even, n_odd) == (1, 1), "the final norm is fused into the last (odd) layer's MoE combine"
    n_heads = ssd_dt_bias.shape[1]
    d_ssd = n_heads * SSD_HEAD_DIM
    conv_dim = ssd_conv_w.shape[2]
    d_gmlp = gmlp_ln_g.shape[1]
    n_groups_gmlp = gmlp_w_s.shape[1]
    n_experts = moe_router.shape[2]
    tp = batch * seq

    xp = x_prompt.reshape(tp, d_model)
    xs = x_sample.reshape(n_dec, d_model)
    outs = dict(ssm_p=[], conv_p=[], conf_p=[], ssm_s=[], conv_s=[], conf_s=[], v_s=[])

    for layer in range(n_even + n_odd):
        i = layer // 2
        if layer % 2 == 0:
            w = w_in_even[i]
            o2, o3 = d_ssd + conv_dim, d_ssd + conv_dim + n_heads
            w_parts = (w[:, :o2].astype(BF16), w[:, o3:].astype(BF16),
                       jnp.pad(w[:, o2:o3], ((0, 0), (0, V7X_LANES - n_heads))).astype(BF16))
            in_args = (_row(norm_mix_even[i]), *w_parts, _row(gmlp_ln_g[i]), _row(gmlp_ln_b[i]))
            dims = dict(d_ssd=d_ssd, conv_dim=conv_dim, d_gmlp=d_gmlp)
            dt_bias, a_log = _pad_lanes(ssd_dt_bias[i]), _pad_lanes(ssd_a_log[i])
            d_skip = _row(jnp.repeat(ssd_d[i], SSD_HEAD_DIM))
            norm_g = _row(ssd_norm[i])
            conv_w, conv_b = ssd_conv_w[i], _row(ssd_conv_b[i])
            z, xbc, ug, vn, dt = _even_in(xp, *in_args, **dims)
            expand = (jnp.arange(d_ssd)[None, :] // SSD_HEAD_DIM == jnp.arange(V7X_LANES)[:, None]).astype(F32)
            yn, st = _ssd_prompt(xbc, z, dt, conv_w, conv_b, dt_bias, a_log, d_skip, norm_g, expand.astype(BF16),
                                 batch=batch, seq=seq)
            outs['ssm_p'].append(st.reshape(batch, n_heads, SSD_HEAD_DIM, SSD_STATE))
            outs['conv_p'].append(xbc.reshape(batch, seq, conv_dim)[:, seq - (conv_w.shape[0] - 1):])
            tail_w = (w_out_even[i].astype(BF16), _row(norm_ffn_even[i]), ffn_w_gate[i].astype(BF16),
                      ffn_w_up[i].astype(BF16), ffn_w_down[i].astype(BF16), _row(norm_mix_odd[i]),
                      conf_w1[i].astype(BF16), _row(conf_b1[i]))
            xp, a_p = _even_out(xp, yn, (ug, vn, gmlp_w_s[i], gmlp_b_s[i].T), tail_w, seq=seq)
            z, xbc, ug, vn, dt = _even_in(xs, *in_args, **dims)
            newbuf, xc, xdt, bc, da = _ssd_sample_prep(xbc, state_ssd_conv[i].transpose(1, 0, 2), dt, conv_w, conv_b,
                                                        dt_bias, a_log, expand)
            h_new, y4 = _ssd_sample_state(state_ssm[i], xdt, da, bc)
            gd = d_gmlp // n_groups_gmlp
            w0 = _row(jnp.repeat(gmlp_w_s[i][:, 0, 0], gd))
            b0 = _row(jnp.repeat(gmlp_b_s[i][:, 0], gd))
            yn, gout = _even_sample_post(y4.reshape(n_dec, d_ssd), xc, z, d_skip, norm_g, ug, vn, w0, b0)
            outs['ssm_s'].append(h_new)
            outs['conv_s'].append(newbuf.transpose(1, 0, 2))
            outs['v_s'].append(vn.reshape(n_dec, 1, d_gmlp))
            xs, a_s = _even_out(xs, yn, gout, tail_w)
        else:
            router = jnp.pad(moe_router[i], ((0, 0), (0, V7X_LANES - n_experts)))
            router_hi = router.astype(BF16)
            router_lo = (router - router_hi.astype(F32)).astype(BF16)
            tail = (conf_dw_w[i], _row(conf_dw_b[i]), _row(conf_ln_g[i]), _row(conf_ln_b[i]),
                    conf_w2[i].astype(BF16), _row(conf_b2[i]), _row(norm_ffn_odd[i]), router_hi, router_lo)
            n_keep = conf_dw_w.shape[1] - 1
            outs['conf_p'].append(a_p.reshape(batch, seq, -1)[:, seq - n_keep:])
            newbuf, c_s = _conf_sample_conv(a_s, state_conf_conv[i].transpose(1, 0, 2), tail[0], tail[1])
            outs['conf_s'].append(newbuf.transpose(1, 0, 2))
            *sample_rows, route_t_s = _conf_sample(c_s, xs, tail[2:], n_experts=n_experts)
            x_all, hn, route, route_t_p = _conf_prompt(a_p, xp, tuple(sample_rows), tail, seq=seq,
                                                       n_experts=n_experts)
            route_t = jnp.concatenate([route_t_p, route_t_s], axis=1)
            tile_expert, n_tiles, pos, pad_pos, n_rows = _moe_plan(route_t, n_experts, MOE_TILE)
            xs_sorted = _sc_dispatch_rows(hn, pos, pad_pos, n_rows)
            ys = _moe_ffn(tile_expert, n_tiles, xs_sorted, moe_w_gate[i], moe_w_up[i], moe_w_down[i], tm=MOE_TILE)
            picked = _sc_gather_rows(ys, jnp.concatenate([pos[:, :tp].reshape(-1), pos[:, tp:].reshape(-1)]))
            gf = _row(final_norm)
            xp = _moe_combine(x_all, route, gf, picked, row0=0, n_rows=tp, a_row0=0, b_row0=tp)
            xs = _moe_combine(x_all, route, gf, picked, row0=tp, n_rows=n_dec, a_row0=TOP_K * tp,
                              b_row0=TOP_K * tp + n_dec)

    y_prompt = xp.reshape(batch, seq, d_model)
    y_sample = xs.reshape(n_dec, 1, d_model)
    return (y_prompt, y_sample, jnp.stack(outs['ssm_p']), jnp.stack(outs['conv_p']), jnp.stack(outs['conf_p']),
            jnp.stack(outs['ssm_s']), jnp.stack(outs['conv_s']), jnp.stack(outs['conf_s']), jnp.stack(outs['v_s']))
```

```python
import functools

import jax
import jax.numpy as jnp
from jax import lax
from jax.experimental import pallas as pl
from jax.experimental.pallas import tpu as pltpu
from jax.experimental.pallas import tpu_sc as plsc

F32 = jnp.float32
BF16 = jnp.bfloat16
HIGHEST = lax.Precision.HIGHEST

SSD_HEAD_DIM = 64
SSD_GROUPS = 4
SSD_STATE = 128
SSD_CHUNK = 128
GMLP_CHUNK = 128
TOP_K = 2
RMS_EPS = 1e-6
LN_EPS = 1e-5

V7X_LANES = 128
V7X_SUBLANES = 8
V7X_VMEM_BYTES = 64 * 1024 * 1024
VMEM_LIMIT = (V7X_VMEM_BYTES * 7) // 8

MOE_TILE = 512
COMBINE_TILE = 512
SSD_SEQS_PER_STEP = 2
MOE_FF_CHUNK = 256
CONF_TILE = 256
CONF_ACC_ROWS = 64
SC_WINDOW_BYTES = 128 * 1024
SAMPLE_STATE_TILE = 8
CONF_SAMPLE_TILE = 32


def _params(*semantics):
    return pltpu.CompilerParams(dimension_semantics=semantics, vmem_limit_bytes=VMEM_LIMIT)


def _resident(shape):
    zeros = (0,) * len(shape)
    return pl.BlockSpec(shape, lambda *_: zeros, pipeline_mode=pl.Buffered(1))


def _whole(shape):
    zeros = (0,) * len(shape)
    return pl.BlockSpec(shape, lambda *_: zeros)


def _row_tile(n_rows, preferred):
    for t in (preferred, 512, 256, 128):
        if t <= preferred and n_rows % t == 0:
            return t
    raise ValueError(f"row count {n_rows} is not a multiple of {V7X_LANES}")


def _dot(a, b):
    return jnp.dot(a, b, preferred_element_type=F32)


def _pack_bf16_pairs(x):
    w = x.shape[1] // 2
    lo = pltpu.bitcast(x[:, :w].astype(BF16).astype(F32), jnp.uint32) >> 16
    hi = pltpu.bitcast(x[:, w:].astype(BF16).astype(F32), jnp.uint32) & jnp.uint32(0xFFFF0000)
    return hi | lo


def _unpack_bf16_pairs(u):
    lo = pltpu.bitcast(u << 16, F32)
    hi = pltpu.bitcast(u & jnp.uint32(0xFFFF0000), F32)
    return jnp.concatenate([lo, hi], axis=1)


def _rms(x, g):
    return x * lax.rsqrt(jnp.mean(x * x, axis=-1, keepdims=True) + RMS_EPS) * g


def _layernorm(x, g, b):
    mu = jnp.mean(x, axis=-1, keepdims=True)
    xc = x - mu
    var = jnp.mean(xc * xc, axis=-1, keepdims=True)
    return xc * lax.rsqrt(var + LN_EPS) * g + b


def _sigmoid(x):
    return 0.5 * jnp.tanh(0.5 * x) + 0.5


def _silu(x):
    return x * _sigmoid(x)


def _softplus(x):
    return jnp.maximum(x, 0.0) + jnp.log1p(jnp.exp(-jnp.abs(x)))


def _even_in_kernel(x_ref, g_ref, wzx_ref, wuv_ref, wdt_ref, lng_ref, lnb_ref, z_ref, xbc_ref, ug_ref, vn_ref,
                    dt_ref, *, d_ssd, conv_dim, d_gmlp):
    h = _rms(x_ref[...], g_ref[...]).astype(BF16)
    z_ref[...] = _dot(h, wzx_ref[:, 0:d_ssd])
    xbc_ref[...] = _dot(h, wzx_ref[:, d_ssd:d_ssd + conv_dim])
    ug_ref[...] = jax.nn.gelu(_dot(h, wuv_ref[:, 0:d_gmlp])).astype(ug_ref.dtype)
    v = jax.nn.gelu(_dot(h, wuv_ref[:, d_gmlp:2 * d_gmlp]))
    vn_ref[...] = _layernorm(v, lng_ref[...], lnb_ref[...])
    dt_ref[...] = _dot(h, wdt_ref[...])


def _even_in(x, g, w_zx, w_uv, w_dt, ln_g, ln_b, *, d_ssd, conv_dim, d_gmlp):
    t, d = x.shape
    tm = _row_tile(t, 512)
    row = lambda n: pl.BlockSpec((tm, n), lambda i: (i, 0))
    consts = (g, w_zx, w_uv, w_dt, ln_g, ln_b)
    return pl.pallas_call(
        functools.partial(_even_in_kernel, d_ssd=d_ssd, conv_dim=conv_dim, d_gmlp=d_gmlp),
        grid=(t // tm,),
        in_specs=[row(d)] + [_resident(c.shape) for c in consts],
        out_specs=[row(d_ssd), row(conv_dim), row(d_gmlp), row(d_gmlp), row(V7X_LANES)],
        out_shape=[jax.ShapeDtypeStruct((t, d_ssd), F32), jax.ShapeDtypeStruct((t, conv_dim), F32),
                   jax.ShapeDtypeStruct((t, d_gmlp), BF16), jax.ShapeDtypeStruct((t, d_gmlp), F32),
                   jax.ShapeDtypeStruct((t, V7X_LANES), F32)],
        compiler_params=_params("parallel"),
        name="even_in",
    )(x, *consts)


def _ssd_prompt_kernel(xbc_ref, z_ref, dt_ref, cw_ref, cb_ref, dtb_ref, alog_ref, dskip_ref, ng_ref, expand_ref,
                       yn_ref, st_ref, xp_ref, state_ref, xc_ref, *, d_ssd, n_taps):
    c = pl.program_id(1)

    @pl.when(c == 0)
    def _():
        state_ref[...] = jnp.zeros_like(state_ref)
        xp_ref[:, 0:V7X_SUBLANES, :] = jnp.zeros((xp_ref.shape[0], V7X_SUBLANES, xp_ref.shape[2]), F32)

    for s in range(xbc_ref.shape[1]):
        _ssd_chunk(xbc_ref.at[0, s], z_ref.at[0, s], dt_ref.at[0, s], cw_ref, cb_ref, dtb_ref, alog_ref, dskip_ref,
                   ng_ref, expand_ref, yn_ref.at[0, s], xp_ref.at[s], state_ref.at[s], xc_ref.at[s],
                   d_ssd=d_ssd, n_taps=n_taps)

    @pl.when(c == pl.num_programs(1) - 1)
    def _():
        for s in range(state_ref.shape[0]):
            for pair in range(state_ref.shape[1]):
                st_ref[s, pair] = state_ref[s, pair].T


def _ssd_chunk(xbc_ref, z_ref, dt_ref, cw_ref, cb_ref, dtb_ref, alog_ref, dskip_ref, ng_ref, expand_ref,
               yn_ref, xp_ref, state_ref, xc_ref, *, d_ssd, n_taps):
    n_pairs = state_ref.shape[0]
    ch = SSD_CHUNK
    gn = SSD_GROUPS * SSD_STATE

    xp_ref[V7X_SUBLANES:V7X_SUBLANES + ch, :] = xbc_ref[...]
    base = V7X_SUBLANES - (n_taps - 1)
    acc = cb_ref[...] + cw_ref[n_taps - 1:n_taps, :] * xbc_ref[...]
    for k in range(n_taps - 1):
        acc = acc + cw_ref[k:k + 1, :] * xp_ref[base + k:base + k + ch, :]
    xp_ref[0:V7X_SUBLANES, :] = xp_ref[ch:ch + V7X_SUBLANES, :]
    xc_ref[...] = _silu(acc)
    xc = xc_ref

    dt = _softplus(dt_ref[...] + dtb_ref[...])
    a = dt * (-jnp.exp(alog_ref[...]))
    def spread(v, ones_ref):
        hi = v.astype(BF16)
        rest = v - hi.astype(F32)
        mid = rest.astype(BF16)
        lo = (rest - mid.astype(F32)).astype(BF16)
        return _dot(hi, ones_ref[...]) + (_dot(mid, ones_ref[...]) + _dot(lo, ones_ref[...]))

    dt_wide = spread(dt, expand_ref)
    li = lax.broadcasted_iota(jnp.int32, (ch, ch), 0)
    si = lax.broadcasted_iota(jnp.int32, (ch, ch), 1)
    causal = li >= si
    tril = jnp.where(causal, 1.0, 0.0).astype(F32)
    acs = jnp.dot(tril, a, precision=HIGHEST, preferred_element_type=F32)
    acs_t = acs.T
    lane = lax.broadcasted_iota(jnp.int32, (ch, 2 * SSD_HEAD_DIM), 1)
    first = lane < SSD_HEAD_DIM
    first_n = lax.broadcasted_iota(jnp.int32, (SSD_STATE, 2 * SSD_HEAD_DIM), 1) < SSD_HEAD_DIM
    pairs_per_group = n_pairs // SSD_GROUPS

    for g in range(SSD_GROUPS):
        bg = xc[:, d_ssd + g * SSD_STATE:d_ssd + (g + 1) * SSD_STATE]
        cg = xc[:, d_ssd + gn + g * SSD_STATE:d_ssd + gn + (g + 1) * SSD_STATE]
        bg_t = bg.T
        cb = _dot(cg.astype(BF16), bg_t.astype(BF16))
        ys = []
        for q in range(pairs_per_group):
            pair = g * pairs_per_group + q
            h0 = 2 * pair
            lo = pair * 2 * SSD_HEAD_DIM
            xs = xc[:, lo:lo + 2 * SSD_HEAD_DIM]
            xdt = (xs * dt_wide[:, lo:lo + 2 * SSD_HEAD_DIM]).astype(BF16)
            s_prev = state_ref[pair]
            s_prev_b = s_prev.astype(BF16)
            y2, snew2, dec2 = [], [], []
            for r in range(2):
                h = h0 + r
                col = acs[:, h:h + 1]
                row = acs_t[h:h + 1, :]
                last = acs_t[h:h + 1, ch - 1:ch]
                decay = jnp.exp(jnp.where(causal, col - row, -jnp.inf))
                y = _dot((cb * decay).astype(BF16), xdt)
                y = y + _dot((cg * jnp.exp(col)).astype(BF16), s_prev_b)
                snew = _dot((bg_t * jnp.exp(last - row)).astype(BF16), xdt)
                y2.append(y)
                snew2.append(snew)
                dec2.append(jnp.exp(last))
            y = jnp.where(first, y2[0], y2[1])
            state_ref[pair] = (s_prev * jnp.where(first_n, dec2[0], dec2[1])
                               + jnp.where(first_n, snew2[0], snew2[1]))
            ys.append(y + dskip_ref[:, lo:lo + 2 * SSD_HEAD_DIM] * xs)
        gw = pairs_per_group * 2 * SSD_HEAD_DIM
        yg = jnp.concatenate(ys, axis=-1) * _silu(z_ref[:, g * gw:(g + 1) * gw])
        yg = yg * lax.rsqrt(jnp.mean(yg * yg, axis=-1, keepdims=True) + RMS_EPS)
        yn_ref[:, g * gw:(g + 1) * gw] = (yg * ng_ref[:, g * gw:(g + 1) * gw]).astype(yn_ref.dtype)


def _ssd_prompt(xbc, z, dt, conv_w, conv_b, dt_bias, a_log, d_skip, norm_g, expand, *, batch, seq):
    t, conv_dim = xbc.shape
    d_ssd = z.shape[1]
    n_pairs = d_ssd // (2 * SSD_HEAD_DIM)
    nc = seq // SSD_CHUNK
    n_taps = conv_w.shape[0]
    g = SSD_SEQS_PER_STEP if batch % SSD_SEQS_PER_STEP == 0 else 1
    tile = lambda n: pl.BlockSpec((1, g, SSD_CHUNK, n), lambda b, c: (b, 0, c, 0))
    by_seq = lambda a: a.reshape(batch // g, g, seq, a.shape[1])
    yn, st = pl.pallas_call(
        functools.partial(_ssd_prompt_kernel, d_ssd=d_ssd, n_taps=n_taps),
        grid=(batch // g, nc),
        in_specs=[tile(conv_dim), tile(d_ssd), tile(V7X_LANES), _resident(conv_w.shape), _resident(conv_b.shape),
                  _resident(dt_bias.shape), _resident(a_log.shape), _resident(d_skip.shape),
                  _resident(norm_g.shape), _resident(expand.shape)],
        out_specs=[tile(d_ssd),
                   pl.BlockSpec((g, n_pairs, 2 * SSD_HEAD_DIM, SSD_STATE), lambda b, c: (b, 0, 0, 0))],
        out_shape=[jax.ShapeDtypeStruct((batch // g, g, seq, d_ssd), BF16),
                   jax.ShapeDtypeStruct((batch, n_pairs, 2 * SSD_HEAD_DIM, SSD_STATE), F32)],
        scratch_shapes=[pltpu.VMEM((g, SSD_CHUNK + V7X_SUBLANES, conv_dim), F32),
                        pltpu.VMEM((g, n_pairs, SSD_STATE, 2 * SSD_HEAD_DIM), F32),
                        pltpu.VMEM((g, SSD_CHUNK, conv_dim), F32)],
        compiler_params=_params("parallel", "arbitrary"),
        name="ssd_prompt",
    )(by_seq(xbc), by_seq(z), by_seq(dt), conv_w, conv_b, dt_bias, a_log, d_skip, norm_g, expand)
    return yn.reshape(t, d_ssd), st


def _gmlp_gate(ug_ref, vn_ref, ws_ref, bs_ref, out_ref):
    ch = GMLP_CHUNK
    n_groups = ws_ref.shape[0]
    gd = vn_ref.shape[1] // n_groups
    ii = lax.broadcasted_iota(jnp.int32, (ch, ch), 0)
    jj = lax.broadcasted_iota(jnp.int32, (ch, ch), 1)
    for g in range(n_groups):
        ws = jnp.where(ii >= jj, ws_ref[g], 0.0).astype(BF16)
        cols = slice(g * gd, (g + 1) * gd)
        for r0 in range(0, vn_ref.shape[0], ch):
            rows = slice(r0, r0 + ch)
            mixed = _dot(ws, vn_ref[rows, cols].astype(BF16)) + bs_ref[:, g:g + 1]
            out_ref[rows, cols] = (ug_ref[rows, cols].astype(F32) * mixed).astype(out_ref.dtype)


def _ssd_sample_prep_kernel(xbc_ref, buf_ref, dt_ref, cw_ref, cb_ref, dtb_ref, alog_ref, expand_ref,
                            newbuf_ref, xs_ref, xdt_ref, bc_ref, da_ref, *, d_ssd, n_taps):
    x_new = xbc_ref[...]
    acc = cb_ref[...] + cw_ref[n_taps - 1:n_taps, :] * x_new
    for k in range(n_taps - 1):
        acc = acc + cw_ref[k:k + 1, :] * buf_ref[k]
    newbuf_ref[0:n_taps - 2] = buf_ref[1:n_taps - 1]
    newbuf_ref[n_taps - 2] = x_new
    xc = _silu(acc)
    xs = xc[:, :d_ssd]
    dt = _softplus(dt_ref[...] + dtb_ref[...])
    da_ref[...] = jnp.exp(dt * (-jnp.exp(alog_ref[...])))
    dt_wide = jnp.dot(dt, expand_ref[...], precision=HIGHEST, preferred_element_type=F32)
    xs_ref[...] = xs
    xdt_ref[...] = xs * dt_wide
    bc_ref[...] = xc[:, d_ssd:]


def _ssd_sample_prep(xbc, buf, dt, conv_w, conv_b, dt_bias, a_log, expand):
    n, conv_dim = xbc.shape
    d_ssd = expand.shape[1]
    n_taps = conv_w.shape[0]
    args = (xbc, buf, dt, conv_w, conv_b, dt_bias, a_log, expand)
    return pl.pallas_call(
        functools.partial(_ssd_sample_prep_kernel, d_ssd=d_ssd, n_taps=n_taps),
        grid=(1,),
        in_specs=[_resident(a.shape) for a in args],
        out_specs=[_whole(buf.shape), _whole((n, d_ssd)), _whole((n, d_ssd)),
                   _whole((n, conv_dim - d_ssd)), _whole((n, V7X_LANES))],
        out_shape=[jax.ShapeDtypeStruct(buf.shape, F32), jax.ShapeDtypeStruct((n, d_ssd), F32),
                   jax.ShapeDtypeStruct((n, d_ssd), F32), jax.ShapeDtypeStruct((n, conv_dim - d_ssd), F32),
                   jax.ShapeDtypeStruct((n, V7X_LANES), F32)],
        compiler_params=_params("arbitrary"),
        name="ssd_sample_prep",
    )(*args)


def _ssd_sample_state_kernel(h_ref, xdt_ref, da_ref, bc_ref, hnew_ref, y_ref, xcol_ref, ccol_ref):
    bs, n_heads, p, s = h_ref.shape
    n_pairs = n_heads // 2
    pairs_per_group = n_pairs // SSD_GROUPS
    gn = SSD_GROUPS * s
    w = 2 * p
    eye = lax.broadcasted_iota(jnp.int32, (w, w), 0) == lax.broadcasted_iota(jnp.int32, (w, w), 1)
    ones = jnp.ones((w, V7X_LANES), BF16)

    def stacked_diag(v):
        return jnp.concatenate(
            [jnp.where(eye, jnp.broadcast_to(v[j:j + 1], (w, w)), 0.0) for j in range(bs)], axis=0).astype(BF16)

    for q in range(n_pairs):
        x = xdt_ref[:, q * w:(q + 1) * w]
        hi = x.astype(BF16).astype(F32)
        mid = (x - hi).astype(BF16).astype(F32)
        lo = (x - hi) - mid
        xcol_ref[q] = _dot(stacked_diag(hi), ones) + (_dot(stacked_diag(mid), ones) + _dot(stacked_diag(lo), ones))
    for g in range(SSD_GROUPS):
        ccol_ref[g] = _dot(stacked_diag(bc_ref[:, gn + g * s:gn + (g + 1) * s]), ones).astype(BF16)

    for j in range(bs):
        for g in range(SSD_GROUPS):
            brow = bc_ref[j:j + 1, g * s:(g + 1) * s]
            h_new = []
            for q in range(g * pairs_per_group, (g + 1) * pairs_per_group):
                for r in range(2):
                    h = 2 * q + r
                    xcol = xcol_ref[q, j * w + r * p:j * w + (r + 1) * p, :]
                    h_new.append(h_ref[j, h] * da_ref[j:j + 1, h:h + 1] + xcol * brow)
                    hnew_ref[j, h] = h_new[-1]
            hc = _dot(jnp.concatenate(h_new, axis=0).astype(BF16), ccol_ref[g, j * s:(j + 1) * s, :])
            for i in range(pairs_per_group):
                y = jnp.sum(jnp.where(eye, hc[i * w:(i + 1) * w], 0.0), axis=0, keepdims=True)
                q = g * pairs_per_group + i
                y_ref[j, q:q + 1, :] = y


def _ssd_sample_state(h, xdt, da, bc):
    n, n_heads, p, s = h.shape
    assert 2 * p == V7X_LANES and s == V7X_LANES
    bs = SAMPLE_STATE_TILE
    rows = lambda w: pl.BlockSpec((bs, w), lambda i: (i, 0))
    return pl.pallas_call(
        _ssd_sample_state_kernel,
        grid=(n // bs,),
        in_specs=[pl.BlockSpec((bs, n_heads, p, s), lambda i: (i, 0, 0, 0)), rows(xdt.shape[1]), rows(da.shape[1]),
                  rows(bc.shape[1])],
        out_specs=[pl.BlockSpec((bs, n_heads, p, s), lambda i: (i, 0, 0, 0)),
                   pl.BlockSpec((bs, n_heads // 2, 2 * p), lambda i: (i, 0, 0))],
        out_shape=[jax.ShapeDtypeStruct(h.shape, F32), jax.ShapeDtypeStruct((n, n_heads // 2, 2 * p), F32)],
        scratch_shapes=[pltpu.VMEM((n_heads // 2, bs * 2 * p, V7X_LANES), F32),
                        pltpu.VMEM((SSD_GROUPS, bs * s, V7X_LANES), BF16)],
        compiler_params=_params("parallel"),
        name="ssd_sample_state",
    )(h, xdt, da, bc)


def _even_sample_post_kernel(y_ref, xs_ref, z_ref, dskip_ref, ng_ref, ug_ref, vn_ref, w0_ref, b0_ref,
                             yn_ref, gout_ref):
    d_ssd = y_ref.shape[1]
    gw = d_ssd // SSD_GROUPS
    y = (y_ref[...] + dskip_ref[...] * xs_ref[...]) * _silu(z_ref[...])
    for g in range(SSD_GROUPS):
        yg = y[:, g * gw:(g + 1) * gw]
        yg = yg * lax.rsqrt(jnp.mean(yg * yg, axis=-1, keepdims=True) + RMS_EPS)
        yn_ref[:, g * gw:(g + 1) * gw] = (yg * ng_ref[:, g * gw:(g + 1) * gw]).astype(yn_ref.dtype)
    mixed = w0_ref[...] * vn_ref[...] + b0_ref[...]
    gout_ref[...] = (ug_ref[...].astype(F32) * mixed).astype(gout_ref.dtype)


def _even_sample_post(y, xs, z, d_skip, norm_g, ug, vn, w0, b0):
    args = (y, xs, z, d_skip, norm_g, ug, vn, w0, b0)
    return pl.pallas_call(
        _even_sample_post_kernel,
        grid=(1,),
        in_specs=[_resident(a.shape) for a in args],
        out_specs=[_whole(y.shape), _whole(vn.shape)],
        out_shape=[jax.ShapeDtypeStruct(y.shape, BF16), jax.ShapeDtypeStruct(vn.shape, BF16)],
        compiler_params=_params("arbitrary"),
        name="even_sample_post",
    )(*args)


def _even_out_kernel(*refs, fuse_gmlp):
    if fuse_gmlp:
        (x_ref, yn_ref, ug_ref, vn_ref, ws_ref, bs_ref, wo_ref, g_ref, wg_ref, wu_ref, wd_ref, gc_ref, w1_ref,
         b1_ref, out_ref, a_ref, gout_ref) = refs
        _gmlp_gate(ug_ref, vn_ref, ws_ref, bs_ref, gout_ref)
    else:
        (x_ref, yn_ref, gout_ref, wo_ref, g_ref, wg_ref, wu_ref, wd_ref, gc_ref, w1_ref, b1_ref,
         out_ref, a_ref) = refs
    d_ssd = yn_ref.shape[1]
    x = x_ref[...] + _dot(yn_ref[...], wo_ref[0:d_ssd, :]) + _dot(gout_ref[...], wo_ref[d_ssd:, :])
    h = _rms(x, g_ref[...]).astype(BF16)
    mid = (_silu(_dot(h, wg_ref[...])) * _dot(h, wu_ref[...])).astype(BF16)
    x = x + _dot(mid, wd_ref[...])
    out_ref[...] = x
    dc = a_ref.shape[1]
    h = _rms(x, gc_ref[...]).astype(BF16)
    val = _dot(h, w1_ref[:, 0:dc]) + b1_ref[:, 0:dc]
    gate = _dot(h, w1_ref[:, dc:]) + b1_ref[:, dc:]
    a_ref[...] = val * _sigmoid(gate)


def _even_out(x, yn, gate_in, weights, *, seq=None):
    t, d = x.shape
    dc = weights[-2].shape[1] // 2
    fuse_gmlp = isinstance(gate_in, tuple)
    tm = _row_tile(t if seq is None else seq, 256)
    row = lambda n: pl.BlockSpec((tm, n), lambda i: (i, 0))
    if fuse_gmlp:
        ug, vn, w_s, b_s_t = gate_in
        assert tm % GMLP_CHUNK == 0
        gate_args = (ug, vn, w_s, b_s_t)
        gate_specs = [row(ug.shape[1]), row(vn.shape[1]), _resident(w_s.shape), _resident(b_s_t.shape)]
        scratch = [pltpu.VMEM((tm, ug.shape[1]), BF16)]
    else:
        gate_args, gate_specs, scratch = (gate_in,), [row(gate_in.shape[1])], []
    return pl.pallas_call(
        functools.partial(_even_out_kernel, fuse_gmlp=fuse_gmlp),
        grid=(t // tm,),
        in_specs=[row(d), row(yn.shape[1])] + gate_specs + [_resident(w.shape) for w in weights],
        out_specs=[row(d), row(dc)],
        out_shape=[jax.ShapeDtypeStruct((t, d), F32), jax.ShapeDtypeStruct((t, dc), F32)],
        scratch_shapes=scratch,
        compiler_params=_params("parallel"),
        name="even_out_ffn",
    )(x, yn, *gate_args, *weights)


def _route(logits, n_experts, route_ref, route_t_ref):
    lane = lax.broadcasted_iota(jnp.int32, logits.shape, 1)
    lane_f = lane.astype(F32)
    big = jnp.float32(V7X_LANES)
    lg = jnp.where(lane < n_experts, logits, -jnp.inf)
    m1 = jnp.max(lg, axis=-1, keepdims=True)
    i1 = jnp.min(jnp.where(lg == m1, lane_f, big), axis=-1, keepdims=True)
    lg2 = jnp.where(lane_f == i1, -jnp.inf, lg)
    m2 = jnp.max(lg2, axis=-1, keepdims=True)
    i2 = jnp.min(jnp.where(lg2 == m2, lane_f, big), axis=-1, keepdims=True)
    e2 = jnp.exp(m2 - m1)
    g1 = 1.0 / (1.0 + e2)
    g2 = e2 / (1.0 + e2)
    route = jnp.where(lane == 0, i1, jnp.where(lane == 1, i2, jnp.where(lane == 2, g1,
                      jnp.where(lane == 3, g2, 0.0))))
    route_ref[...] = route
    route_t_ref[...] = route.T[0:V7X_SUBLANES, :]


def _conf_finish(c, x, lng_ref, lnb_ref, w2_ref, b2_ref, ng_ref, rhi_ref, rlo_ref, out_ref, hn_ref, route_ref,
                 route_t_ref, *, n_experts):
    hmid = _silu(_layernorm(c, lng_ref[...], lnb_ref[...])).astype(BF16)
    x_new = x + _dot(hmid, w2_ref[...]) + b2_ref[...]
    out_ref[...] = x_new
    hn = _rms(x_new, ng_ref[...])
    hn_ref[...] = _pack_bf16_pairs(hn)
    hi = hn.astype(BF16)
    lo = (hn - hi.astype(F32)).astype(BF16)
    logits = _dot(hi, rhi_ref[...]) + (_dot(lo, rhi_ref[...]) + _dot(hi, rlo_ref[...]))
    _route(logits, n_experts, route_ref, route_t_ref)


def _conf_prompt_kernel(a_ref, x_ref, xs_ref, hns_ref, routes_ref, dw_ref, db_ref, lng_ref, lnb_ref, w2_ref, b2_ref,
                        ng_ref, rhi_ref, rlo_ref, out_ref, hn_ref, route_ref, route_t_ref, xp_ref, c_ref, win_ref,
                        *, n_taps, n_experts, tiles_per_seq, n_prompt_tiles):
    tl, d = a_ref.shape
    halo = xp_ref.shape[0] - tl
    base = halo - (n_taps - 1)
    rows = CONF_ACC_ROWS
    step = pl.program_id(0)

    @pl.when(step < n_prompt_tiles)
    def _():
        @pl.when(step % tiles_per_seq == 0)
        def _():
            xp_ref[0:halo, :] = jnp.zeros((halo, d), F32)

        xp_ref[halo:halo + tl, :] = a_ref[...]

        def lane_block(j, carry):
            cols = pl.ds(pl.multiple_of(j * V7X_LANES, V7X_LANES), V7X_LANES)
            for r0 in range(0, tl, rows):
                acc = jnp.broadcast_to(db_ref[:, cols], (rows, V7X_LANES))
                for phase in range(V7X_SUBLANES):
                    taps = [k for k in range(n_taps) if (base + k) % V7X_SUBLANES == phase]
                    if not taps:
                        continue
                    span = max(base + k - phase for k in taps) + rows
                    if phase:
                        win_ref[0:span, :] = xp_ref[pl.ds(r0 + phase, span), cols]
                    for k in taps:
                        o = base + k - phase
                        win = win_ref[o:o + rows, :] if phase else xp_ref[pl.ds(r0 + o, rows), cols]
                        acc = acc + dw_ref[k:k + 1, cols] * win
                c_ref[r0:r0 + rows, cols] = acc
            return carry

        lax.fori_loop(0, d // V7X_LANES, lane_block, 0)
        xp_ref[0:halo, :] = xp_ref[tl:tl + halo, :]
        _conf_finish(c_ref[...], x_ref[...], lng_ref, lnb_ref, w2_ref, b2_ref, ng_ref, rhi_ref, rlo_ref,
                     out_ref, hn_ref, route_ref, route_t_ref, n_experts=n_experts)

    @pl.when(step == n_prompt_tiles)
    def _():
        ns = xs_ref.shape[0]
        out_ref[0:ns, :] = xs_ref[...]
        hn_ref[0:ns, :] = hns_ref[...]
        route_ref[0:ns, :] = routes_ref[...]


def _conf_prompt(a, x, sample_rows, consts, *, seq, n_experts):
    t, d = a.shape
    n_s = sample_rows[0].shape[0]
    tl = _row_tile(seq, CONF_TILE)
    assert tl % CONF_ACC_ROWS == 0 and n_s <= tl
    n_prompt_tiles = t // tl
    n_taps = consts[0].shape[0]
    halo = -(-(n_taps - 1) // V7X_SUBLANES) * V7X_SUBLANES
    in_tile = lambda n: pl.BlockSpec((tl, n), lambda s: (jnp.minimum(s, n_prompt_tiles - 1), 0))
    out_tile = lambda n: pl.BlockSpec((tl, n), lambda s: (s, 0))
    small = sample_rows + tuple(consts)
    return pl.pallas_call(
        functools.partial(_conf_prompt_kernel, n_taps=n_taps, n_experts=n_experts, tiles_per_seq=seq // tl,
                          n_prompt_tiles=n_prompt_tiles),
        grid=(n_prompt_tiles + 1,),
        in_specs=[in_tile(d), in_tile(d)] + [_resident(c.shape) for c in small],
        out_specs=[out_tile(d), out_tile(d // 2), out_tile(V7X_LANES),
                   pl.BlockSpec((V7X_SUBLANES, tl), lambda s: (0, jnp.minimum(s, n_prompt_tiles - 1)))],
        out_shape=[jax.ShapeDtypeStruct((t + n_s, d), F32), jax.ShapeDtypeStruct((t + n_s, d // 2), jnp.uint32),
                   jax.ShapeDtypeStruct((t + n_s, V7X_LANES), F32), jax.ShapeDtypeStruct((V7X_SUBLANES, t), F32)],
        scratch_shapes=[pltpu.VMEM((tl + halo, d), F32), pltpu.VMEM((tl, d), F32),
                        pltpu.VMEM((CONF_ACC_ROWS + halo, V7X_LANES), F32)],
        compiler_params=_params("arbitrary"),
        name="conf_prompt",
    )(a, x, *small)


def _conf_sample_conv_kernel(a_ref, buf_ref, dw_ref, db_ref, newbuf_ref, c_ref, *, n_taps):
    a_new = a_ref[...]
    acc = db_ref[...] + dw_ref[n_taps - 1:n_taps, :] * a_new
    for k in range(n_taps - 1):
        acc = acc + dw_ref[k:k + 1, :] * buf_ref[k]
    c_ref[...] = acc
    newbuf_ref[0:n_taps - 2] = buf_ref[1:n_taps - 1]
    newbuf_ref[n_taps - 2] = a_new


def _conf_sample_conv(a, buf, dw_w, dw_b):
    n, d = a.shape
    bs = CONF_SAMPLE_TILE
    hist = pl.BlockSpec((buf.shape[0], bs, d), lambda i: (0, i, 0))
    rows = pl.BlockSpec((bs, d), lambda i: (i, 0))
    return pl.pallas_call(
        functools.partial(_conf_sample_conv_kernel, n_taps=dw_w.shape[0]),
        grid=(n // bs,),
        in_specs=[rows, hist, _resident(dw_w.shape), _resident(dw_b.shape)],
        out_specs=[hist, rows],
        out_shape=[jax.ShapeDtypeStruct(buf.shape, F32), jax.ShapeDtypeStruct((n, d), F32)],
        compiler_params=_params("parallel"),
        name="conf_sample_conv",
    )(a, buf, dw_w, dw_b)


def _conf_sample_kernel(c_ref, x_ref, lng_ref, lnb_ref, w2_ref, b2_ref, ng_ref, rhi_ref, rlo_ref,
                        out_ref, hn_ref, route_ref, route_t_ref, *, n_experts):
    _conf_finish(c_ref[...], x_ref[...], lng_ref, lnb_ref, w2_ref, b2_ref, ng_ref, rhi_ref, rlo_ref,
                 out_ref, hn_ref, route_ref, route_t_ref, n_experts=n_experts)


def _conf_sample(c, x, consts, *, n_experts):
    n, d = c.shape
    args = (c, x) + tuple(consts)
    return pl.pallas_call(
        functools.partial(_conf_sample_kernel, n_experts=n_experts),
        grid=(1,),
        in_specs=[_resident(v.shape) for v in args],
        out_specs=[_whole((n, d)), _whole((n, d // 2)), _whole((n, V7X_LANES)), _whole((V7X_SUBLANES, n))],
        out_shape=[jax.ShapeDtypeStruct((n, d), F32), jax.ShapeDtypeStruct((n, d // 2), jnp.uint32),
                   jax.ShapeDtypeStruct((n, V7X_LANES), F32), jax.ShapeDtypeStruct((V7X_SUBLANES, n), F32)],
        compiler_params=_params("arbitrary"),
        name="conf_sample",
    )(*args)


def _sc_window_rows(x):
    assert x.dtype.itemsize == 4, "the SparseCore indirect copies move 32-bit elements"
    return SC_WINDOW_BYTES // (x.shape[1] * x.dtype.itemsize)


def _sc_gather_rows(x, idx):
    n, d = idx.shape[0], x.shape[1]
    w = _sc_window_rows(x)
    assert n % w == 0
    mesh = plsc.VectorSubcoreMesh(core_axis_name="core", subcore_axis_name="subcore")

    @functools.partial(pl.kernel, out_type=jax.ShapeDtypeStruct((n, d), x.dtype), mesh=mesh, scratch_types=[])
    def gather(x_hbm, idx_hbm, out_hbm):
        def body(idx_vmem, out_vmem):
            pltpu.sync_copy(x_hbm.at[idx_vmem.at[0]], out_vmem)

        pltpu.emit_pipeline(
            body,
            grid=(n // w,),
            in_specs=[pl.BlockSpec((1, w), lambda i: (i, 0))],
            out_specs=[pl.BlockSpec((w, d), lambda i: (i, 0))],
            core_axis_name=("core", "subcore"),
            dimension_semantics=(pltpu.PARALLEL,),
        )(idx_hbm, out_hbm)

    return gather(x, idx.reshape(n // w, w))


def _sc_dispatch_rows(x, pos, pad_pos, n_out):
    t, d = x.shape
    w = _sc_window_rows(x)
    n_pad = pad_pos.shape[0]
    assert t % w == 0 and n_pad % w == 0 and TOP_K * t + n_pad == n_out
    mesh = plsc.VectorSubcoreMesh(core_axis_name="core", subcore_axis_name="subcore")

    @functools.partial(pl.kernel, out_type=jax.ShapeDtypeStruct((n_out, d), x.dtype), mesh=mesh, scratch_types=[])
    def dispatch(x_hbm, pos_hbm, pad_hbm, zero_hbm, out_hbm):
        def body(rows_vmem, idx_vmem):
            pltpu.sync_copy(rows_vmem, out_hbm.at[idx_vmem.at[0]])

        def scatter(src_hbm, idx_hbm, src_map):
            pltpu.emit_pipeline(
                body,
                grid=(idx_hbm.shape[0],),
                in_specs=[pl.BlockSpec((w, d), src_map), pl.BlockSpec((1, w), lambda i: (i, 0))],
                out_specs=[],
                core_axis_name=("core", "subcore"),
                dimension_semantics=(pltpu.PARALLEL,),
            )(src_hbm, idx_hbm)

        for k in range(TOP_K):
            scatter(x_hbm, pos_hbm.at[k], lambda i: (i, 0))
        scatter(zero_hbm, pad_hbm, lambda i: (0, 0))

    return dispatch(x, pos.reshape(TOP_K, t // w, w), pad_pos.reshape(n_pad // w, w), jnp.zeros((w, d), x.dtype))


def _moe_ffn_kernel(tile_expert_ref, first_ref, bank_ref, next_expert_ref, n_tiles_ref, x_ref, wg_hbm, wu_hbm,
                    wd_hbm, out_ref, wg_v, wu_v, wd_v, stage_g, stage_u, stage_d, acc_ref, sems):
    i = pl.program_id(0)
    dff = wg_v.shape[2]
    ck = stage_g.shape[2]
    n_chunks = dff // ck
    active = i < n_tiles_ref[0]
    starts_run = active & (first_ref[i] == 1)
    bank = bank_ref[i]
    nxt = next_expert_ref[i]

    def chunk_copies(expert, c, slot):
        cols = slice(c * ck, (c + 1) * ck)
        return (pltpu.make_async_copy(wg_hbm.at[expert, :, cols], stage_g.at[slot], sems.at[0, slot]),
                pltpu.make_async_copy(wu_hbm.at[expert, :, cols], stage_u.at[slot], sems.at[1, slot]),
                pltpu.make_async_copy(wd_hbm.at[expert, cols, :], stage_d.at[slot], sems.at[2, slot]))

    def stream_weights(expert, dst_bank, between_chunks):
        for cp in chunk_copies(expert, 0, 0):
            cp.start()
        for c in range(n_chunks):
            slot = c % 2
            if c + 1 < n_chunks:
                for cp in chunk_copies(expert, c + 1, 1 - slot):
                    cp.start()
            for cp in chunk_copies(expert, c, slot):
                cp.wait()
            cols = slice(c * ck, (c + 1) * ck)
            wg_v[dst_bank, :, cols] = stage_g[slot].astype(BF16)
            wu_v[dst_bank, :, cols] = stage_u[slot].astype(BF16)
            wd_v[dst_bank, cols, :] = stage_d[slot].astype(BF16)
            between_chunks(c)

    def ffn_part(xb, cols):
        mid = (_silu(_dot(xb, wg_v[bank, :, cols])) * _dot(xb, wu_v[bank, :, cols])).astype(BF16)
        return _dot(mid, wd_v[bank, cols, :])

    @pl.when(starts_run & (i == 0))
    def _():
        stream_weights(tile_expert_ref[i], bank, lambda c: None)

    overlapped = starts_run & (nxt >= 0)

    @pl.when(overlapped)
    def _():
        xb = _unpack_bf16_pairs(x_ref[...]).astype(BF16)
        acc_ref[...] = jnp.zeros_like(acc_ref)

        def piece(c):
            acc_ref[...] += ffn_part(xb, slice(c * ck, (c + 1) * ck))

        stream_weights(nxt, 1 - bank, piece)
        out_ref[...] = _pack_bf16_pairs(acc_ref[...])

    @pl.when(active & jnp.logical_not(overlapped))
    def _():
        xb = _unpack_bf16_pairs(x_ref[...]).astype(BF16)
        half = dff // 2
        acc = ffn_part(xb, slice(0, half)) + ffn_part(xb, slice(half, dff))
        out_ref[...] = _pack_bf16_pairs(acc)

    @pl.when(jnp.logical_not(active))
    def _():
        out_ref[...] = jnp.zeros_like(out_ref)


def _moe_ffn(tile_expert, n_tiles, xs, wg, wu, wd, *, tm):
    p, d = xs.shape[0], wg.shape[1]
    dff = wg.shape[2]
    ck = MOE_FF_CHUNK
    assert dff % ck == 0
    nt = p // tm
    i32 = jnp.int32
    tile = jnp.arange(nt, dtype=i32)
    first = jnp.concatenate([jnp.ones((1,), i32), (tile_expert[1:] != tile_expert[:-1]).astype(i32)])
    bank = (jnp.cumsum(first) - 1) % 2
    run_start = jnp.where((first == 1) & (tile < n_tiles[0]), tile, nt)
    next_start = jnp.concatenate([lax.cummin(run_start, reverse=True)[1:], jnp.full((1,), nt, i32)])
    next_expert = jnp.sum((next_start[:, None] == tile[None, :]).astype(i32) * tile_expert[None, :], axis=1)
    next_expert = jnp.where(next_start < nt, next_expert, -1)
    any_spec = pl.BlockSpec(memory_space=pl.ANY)
    row_tile = pl.BlockSpec((tm, d // 2), lambda i, *_: (i, 0))
    grid_spec = pltpu.PrefetchScalarGridSpec(
        num_scalar_prefetch=5,
        grid=(nt,),
        in_specs=[row_tile, any_spec, any_spec, any_spec],
        out_specs=row_tile,
        scratch_shapes=[pltpu.VMEM((2, d, dff), BF16), pltpu.VMEM((2, d, dff), BF16), pltpu.VMEM((2, dff, d), BF16),
                        pltpu.VMEM((2, d, ck), F32), pltpu.VMEM((2, d, ck), F32), pltpu.VMEM((2, ck, d), F32),
                        pltpu.VMEM((tm, d), F32), pltpu.SemaphoreType.DMA((3, 2))],
    )
    return pl.pallas_call(
        _moe_ffn_kernel,
        grid_spec=grid_spec,
        out_shape=jax.ShapeDtypeStruct((p, d // 2), jnp.uint32),
        compiler_params=_params("arbitrary"),
        name="moe_ffn",
    )(tile_expert, first, bank.astype(i32), next_expert.astype(i32), n_tiles, xs, wg, wu, wd)


def _moe_combine_kernel(x_ref, route_ref, gf_ref, a_ref, b_ref, y_ref):
    route = route_ref[...]
    x = (x_ref[...] + route[:, 2:3] * _unpack_bf16_pairs(a_ref[...])
         + route[:, 3:4] * _unpack_bf16_pairs(b_ref[...]))
    y_ref[...] = _rms(x, gf_ref[...])


def _moe_combine(x, route, gf, picked, *, row0, n_rows, a_row0, b_row0):
    d = x.shape[1]
    tc = _row_tile(n_rows, COMBINE_TILE)
    assert row0 % tc == 0 and a_row0 % tc == 0 and b_row0 % tc == 0
    rows = lambda w, r0: pl.BlockSpec((tc, w), lambda i: (i + r0 // tc, 0))
    return pl.pallas_call(
        _moe_combine_kernel,
        grid=(n_rows // tc,),
        in_specs=[rows(d, row0), rows(V7X_LANES, row0), _resident(gf.shape), rows(d // 2, a_row0),
                  rows(d // 2, b_row0)],
        out_specs=pl.BlockSpec((tc, d), lambda i: (i, 0)),
        out_shape=jax.ShapeDtypeStruct((n_rows, d), F32),
        compiler_params=_params("parallel"),
        name="moe_combine",
    )(x, route, gf, picked, picked)


def _moe_plan(route_t, n_experts, tm):
    t = route_t.shape[1]
    n_slots = t * TOP_K
    i32 = jnp.int32
    expert = route_t[:TOP_K].astype(i32).reshape(1, n_slots)
    onehot = (expert == jnp.arange(n_experts, dtype=i32)[:, None]).astype(i32)
    counts = jnp.sum(onehot, axis=1)
    tiles = (counts + tm - 1) // tm
    tile_end = jnp.cumsum(tiles)
    start = (tile_end - tiles) * tm
    rank = jnp.cumsum(onehot, axis=1) - onehot
    pos = jnp.sum(onehot * (rank + start[:, None]), axis=0)
    nt = -(-n_slots // tm) + n_experts
    n_tiles = tile_end[-1]
    tile_id = jnp.minimum(jnp.arange(nt, dtype=i32), n_tiles - 1)
    tile_expert = jnp.sum((tile_id[:, None] >= tile_end[None, :]).astype(i32), axis=1)
    seg_start = jnp.concatenate([start + counts, (n_tiles * tm)[None]])
    seg_len = jnp.concatenate([tiles * tm - counts, (nt * tm - n_tiles * tm)[None]])
    seg_end = jnp.cumsum(seg_len)
    q = jnp.arange(nt * tm - n_slots, dtype=i32)
    in_seg = (q[:, None] >= (seg_end - seg_len)[None, :]) & (q[:, None] < seg_end[None, :])
    pad_pos = jnp.sum(in_seg.astype(i32) * (seg_start - (seg_end - seg_len))[None, :], axis=1) + q
    return (tile_expert.astype(i32), n_tiles.astype(i32).reshape(1), pos.reshape(TOP_K, t).astype(i32),
            pad_pos.astype(i32), nt * tm)


def _row(v):
    return v.reshape(1, -1).astype(F32)


def _pad_lanes(v):
    return jnp.pad(_row(v), ((0, 0), (0, V7X_LANES - v.shape[-1])))


def kernel(x_prompt, x_sample, state_ssm, state_ssd_conv, state_conf_conv, norm_mix_even, w_in_even, ssd_conv_w, ssd_conv_b, ssd_dt_bias, ssd_a_log, ssd_d, ssd_norm, gmlp_ln_g, gmlp_ln_b, gmlp_w_s, gmlp_b_s, w_out_even, norm_ffn_even, ffn_w_gate, ffn_w_up, ffn_w_down, norm_mix_odd, conf_w1, conf_b1, conf_dw_w, conf_dw_b, conf_ln_g, conf_ln_b, conf_w2, conf_b2, norm_ffn_odd, moe_router, moe_w_gate, moe_w_up, moe_w_down, final_norm):
    batch, seq, d_model = x_prompt.shape
    n_dec, dec_seq, _ = x_sample.shape
    assert dec_seq == 1, "the sample group advances one token per sequence"
    assert seq % SSD_CHUNK == 0 and n_dec % V7X_LANES == 0
    n_even, n_odd = w_in_even.shape[0], conf_w1.shape[0]
    assert (n_even, n_odd) == (1, 1), "the final norm is fused into the last (odd) layer's MoE combine"
    n_heads = ssd_dt_bias.shape[1]
    d_ssd = n_heads * SSD_HEAD_DIM
    conv_dim = ssd_conv_w.shape[2]
    d_gmlp = gmlp_ln_g.shape[1]
    n_groups_gmlp = gmlp_w_s.shape[1]
    n_experts = moe_router.shape[2]
    tp = batch * seq

    xp = x_prompt.reshape(tp, d_model)
    xs = x_sample.reshape(n_dec, d_model)
    outs = dict(ssm_p=[], conv_p=[], conf_p=[], ssm_s=[], conv_s=[], conf_s=[], v_s=[])

    for layer in range(n_even + n_odd):
        i = layer // 2
        if layer % 2 == 0:
            w = w_in_even[i]
            o1, o2, o3 = d_ssd, d_ssd + conv_dim, d_ssd + conv_dim + n_heads
            w_parts = (w[:, :o2].astype(BF16), w[:, o3:].astype(BF16),
                       jnp.pad(w[:, o2:o3], ((0, 0), (0, V7X_LANES - n_heads))).astype(BF16))
            in_args = (_row(norm_mix_even[i]), *w_parts, _row(gmlp_ln_g[i]), _row(gmlp_ln_b[i]))
            dims = dict(d_ssd=d_ssd, conv_dim=conv_dim, d_gmlp=d_gmlp)
            dt_bias, a_log = _pad_lanes(ssd_dt_bias[i]), _pad_lanes(ssd_a_log[i])
            d_skip = _row(jnp.repeat(ssd_d[i], SSD_HEAD_DIM))
            norm_g = _row(ssd_norm[i])
            conv_w, conv_b = ssd_conv_w[i], _row(ssd_conv_b[i])
            z, xbc, ug, vn, dt = _even_in(xp, *in_args, **dims)
            expand = (jnp.arange(d_ssd)[None, :] // SSD_HEAD_DIM == jnp.arange(V7X_LANES)[:, None]).astype(F32)
            yn, st = _ssd_prompt(xbc, z, dt, conv_w, conv_b, dt_bias, a_log, d_skip, norm_g, expand.astype(BF16),
                                 batch=batch, seq=seq)
            n_pairs = n_heads // 2
            outs['ssm_p'].append(st.reshape(batch, n_heads, SSD_HEAD_DIM, SSD_STATE))
            outs['conv_p'].append(xbc.reshape(batch, seq, conv_dim)[:, seq - (conv_w.shape[0] - 1):])
            tail_w = (w_out_even[i].astype(BF16), _row(norm_ffn_even[i]), ffn_w_gate[i].astype(BF16),
                      ffn_w_up[i].astype(BF16), ffn_w_down[i].astype(BF16), _row(norm_mix_odd[i]),
                      conf_w1[i].astype(BF16), _row(conf_b1[i]))
            xp, a_p = _even_out(xp, yn, (ug, vn, gmlp_w_s[i], gmlp_b_s[i].T), tail_w, seq=seq)
            z, xbc, ug, vn, dt = _even_in(xs, *in_args, **dims)
            newbuf, xc, xdt, bc, da = _ssd_sample_prep(xbc, state_ssd_conv[i].transpose(1, 0, 2), dt, conv_w, conv_b,
                                                        dt_bias, a_log, expand)
            h_new, y4 = _ssd_sample_state(state_ssm[i], xdt, da, bc)
            gd = d_gmlp // n_groups_gmlp
            w0 = _row(jnp.repeat(gmlp_w_s[i][:, 0, 0], gd))
            b0 = _row(jnp.repeat(gmlp_b_s[i][:, 0], gd))
            yn, gout = _even_sample_post(y4.reshape(n_dec, d_ssd), xc, z, d_skip, norm_g, ug, vn, w0, b0)
            outs['ssm_s'].append(h_new)
            outs['conv_s'].append(newbuf.transpose(1, 0, 2))
            outs['v_s'].append(vn.reshape(n_dec, 1, d_gmlp))
            xs, a_s = _even_out(xs, yn, gout, tail_w)
        else:
            router = jnp.pad(moe_router[i], ((0, 0), (0, V7X_LANES - n_experts)))
            router_hi = router.astype(BF16)
            router_lo = (router - router_hi.astype(F32)).astype(BF16)
            tail = (conf_dw_w[i], _row(conf_dw_b[i]), _row(conf_ln_g[i]), _row(conf_ln_b[i]),
                    conf_w2[i].astype(BF16), _row(conf_b2[i]), _row(norm_ffn_odd[i]), router_hi, router_lo)
            n_keep = conf_dw_w.shape[1] - 1
            outs['conf_p'].append(a_p.reshape(batch, seq, -1)[:, seq - n_keep:])
            newbuf, c_s = _conf_sample_conv(a_s, state_conf_conv[i].transpose(1, 0, 2), tail[0], tail[1])
            outs['conf_s'].append(newbuf.transpose(1, 0, 2))
            *sample_rows, route_t_s = _conf_sample(c_s, xs, tail[2:], n_experts=n_experts)
            x_all, hn, route, route_t_p = _conf_prompt(a_p, xp, tuple(sample_rows), tail, seq=seq,
                                                       n_experts=n_experts)
            route_t = jnp.concatenate([route_t_p, route_t_s], axis=1)
            tile_expert, n_tiles, pos, pad_pos, n_rows = _moe_plan(route_t, n_experts, MOE_TILE)
            xs_sorted = _sc_dispatch_rows(hn, pos, pad_pos, n_rows)
            ys = _moe_ffn(tile_expert, n_tiles, xs_sorted, moe_w_gate[i], moe_w_up[i], moe_w_down[i], tm=MOE_TILE)
            picked = _sc_gather_rows(ys, jnp.concatenate([pos[:, :tp].reshape(-1), pos[:, tp:].reshape(-1)]))
            gf = _row(final_norm)
            xp = _moe_combine(x_all, route, gf, picked, row0=0, n_rows=tp, a_row0=0, b_row0=tp)
            xs = _moe_combine(x_all, route, gf, picked, row0=tp, n_rows=n_dec, a_row0=TOP_K * tp,
                              b_row0=TOP_K * tp + n_dec)

    y_prompt = xp.reshape(batch, seq, d_model)
    y_sample = xs.reshape(n_dec, 1, d_model)
    return (y_prompt, y_sample, jnp.stack(outs['ssm_p']), jnp.stack(outs['conv_p']), jnp.stack(outs['conf_p']),
            jnp.stack(outs['ssm_s']), jnp.stack(outs['conv_s']), jnp.stack(outs['conf_s']), jnp.stack(outs['v_s']))
```

```python
import functools

import jax
import jax.numpy as jnp
from jax import lax
from jax.experimental import pallas as pl
from jax.experimental.pallas import tpu as pltpu
from jax.experimental.pallas import tpu_sc as plsc

F32 = jnp.float32
BF16 = jnp.bfloat16
HIGHEST = lax.Precision.HIGHEST

SSD_HEAD_DIM = 64
SSD_GROUPS = 4
SSD_STATE = 128
SSD_CHUNK = 128
GMLP_CHUNK = 128
TOP_K = 2
RMS_EPS = 1e-6
LN_EPS = 1e-5

V7X_LANES = 128
V7X_SUBLANES = 8
V7X_VMEM_BYTES = 64 * 1024 * 1024
VMEM_LIMIT = (V7X_VMEM_BYTES * 7) // 8

MOE_TILE = 512
COMBINE_TILE = 512
SSD_SEQS_PER_STEP = 4
MOE_FF_CHUNK = 256
CONF_TILE = 256
CONF_ACC_ROWS = 64
SC_WINDOW_BYTES = 128 * 1024
SAMPLE_STATE_TILE = 8
CONF_SAMPLE_TILE = 32


def _params(*semantics):
    return pltpu.CompilerParams(dimension_semantics=semantics, vmem_limit_bytes=VMEM_LIMIT)


def _resident(shape):
    zeros = (0,) * len(shape)
    return pl.BlockSpec(shape, lambda *_: zeros, pipeline_mode=pl.Buffered(1))


def _whole(shape):
    zeros = (0,) * len(shape)
    return pl.BlockSpec(shape, lambda *_: zeros)


def _row_tile(n_rows, preferred):
    for t in (preferred, 512, 256, 128):
        if t <= preferred and n_rows % t == 0:
            return t
    raise ValueError(f"row count {n_rows} is not a multiple of {V7X_LANES}")


def _dot(a, b):
    return jnp.dot(a, b, preferred_element_type=F32)


def _pack_bf16_pairs(x):
    w = x.shape[1] // 2
    lo = pltpu.bitcast(x[:, :w].astype(BF16).astype(F32), jnp.uint32) >> 16
    hi = pltpu.bitcast(x[:, w:].astype(BF16).astype(F32), jnp.uint32) & jnp.uint32(0xFFFF0000)
    return hi | lo


def _unpack_bf16_pairs(u):
    lo = pltpu.bitcast(u << 16, F32)
    hi = pltpu.bitcast(u & jnp.uint32(0xFFFF0000), F32)
    return jnp.concatenate([lo, hi], axis=1)


def _rms(x, g):
    return x * lax.rsqrt(jnp.mean(x * x, axis=-1, keepdims=True) + RMS_EPS) * g


def _layernorm(x, g, b):
    mu = jnp.mean(x, axis=-1, keepdims=True)
    xc = x - mu
    var = jnp.mean(xc * xc, axis=-1, keepdims=True)
    return xc * lax.rsqrt(var + LN_EPS) * g + b


def _sigmoid(x):
    return 0.5 * jnp.tanh(0.5 * x) + 0.5


def _silu(x):
    return x * _sigmoid(x)


def _softplus(x):
    return jnp.maximum(x, 0.0) + jnp.log1p(jnp.exp(-jnp.abs(x)))


def _even_in_kernel(x_ref, g_ref, wzx_ref, wuv_ref, wdt_ref, lng_ref, lnb_ref, z_ref, xbc_ref, ug_ref, vn_ref,
                    dt_ref, *, d_ssd, conv_dim, d_gmlp):
    h = _rms(x_ref[...], g_ref[...]).astype(BF16)
    z_ref[...] = _dot(h, wzx_ref[:, 0:d_ssd])
    xbc_ref[...] = _dot(h, wzx_ref[:, d_ssd:d_ssd + conv_dim])
    ug_ref[...] = jax.nn.gelu(_dot(h, wuv_ref[:, 0:d_gmlp])).astype(ug_ref.dtype)
    v = jax.nn.gelu(_dot(h, wuv_ref[:, d_gmlp:2 * d_gmlp]))
    vn_ref[...] = _layernorm(v, lng_ref[...], lnb_ref[...])
    dt_ref[...] = _dot(h, wdt_ref[...])


def _even_in(x, g, w_zx, w_uv, w_dt, ln_g, ln_b, *, d_ssd, conv_dim, d_gmlp):
    t, d = x.shape
    tm = _row_tile(t, 512)
    row = lambda n: pl.BlockSpec((tm, n), lambda i: (i, 0))
    consts = (g, w_zx, w_uv, w_dt, ln_g, ln_b)
    return pl.pallas_call(
        functools.partial(_even_in_kernel, d_ssd=d_ssd, conv_dim=conv_dim, d_gmlp=d_gmlp),
        grid=(t // tm,),
        in_specs=[row(d)] + [_resident(c.shape) for c in consts],
        out_specs=[row(d_ssd), row(conv_dim), row(d_gmlp), row(d_gmlp), row(V7X_LANES)],
        out_shape=[jax.ShapeDtypeStruct((t, d_ssd), F32), jax.ShapeDtypeStruct((t, conv_dim), F32),
                   jax.ShapeDtypeStruct((t, d_gmlp), BF16), jax.ShapeDtypeStruct((t, d_gmlp), F32),
                   jax.ShapeDtypeStruct((t, V7X_LANES), F32)],
        compiler_params=_params("parallel"),
        name="even_in",
    )(x, *consts)


def _ssd_prompt_kernel(xbc_ref, z_ref, dt_ref, cw_ref, cb_ref, dtb_ref, alog_ref, dskip_ref, ng_ref, expand_ref,
                       yn_ref, st_ref, xp_ref, state_ref, xc_ref, *, d_ssd, n_taps):
    c = pl.program_id(1)

    @pl.when(c == 0)
    def _():
        state_ref[...] = jnp.zeros_like(state_ref)
        xp_ref[:, 0:V7X_SUBLANES, :] = jnp.zeros((xp_ref.shape[0], V7X_SUBLANES, xp_ref.shape[2]), F32)

    for s in range(xbc_ref.shape[1]):
        _ssd_chunk(xbc_ref.at[0, s], z_ref.at[0, s], dt_ref.at[0, s], cw_ref, cb_ref, dtb_ref, alog_ref, dskip_ref,
                   ng_ref, expand_ref, yn_ref.at[0, s], xp_ref.at[s], state_ref.at[s], xc_ref.at[s],
                   d_ssd=d_ssd, n_taps=n_taps)

    @pl.when(c == pl.num_programs(1) - 1)
    def _():
        for s in range(state_ref.shape[0]):
            for pair in range(state_ref.shape[1]):
                st_ref[s, pair] = state_ref[s, pair].T


def _ssd_chunk(xbc_ref, z_ref, dt_ref, cw_ref, cb_ref, dtb_ref, alog_ref, dskip_ref, ng_ref, expand_ref,
               yn_ref, xp_ref, state_ref, xc_ref, *, d_ssd, n_taps):
    n_pairs = state_ref.shape[0]
    ch = SSD_CHUNK
    gn = SSD_GROUPS * SSD_STATE

    xp_ref[V7X_SUBLANES:V7X_SUBLANES + ch, :] = xbc_ref[...]
    base = V7X_SUBLANES - (n_taps - 1)
    acc = cb_ref[...] + cw_ref[n_taps - 1:n_taps, :] * xbc_ref[...]
    for k in range(n_taps - 1):
        acc = acc + cw_ref[k:k + 1, :] * xp_ref[base + k:base + k + ch, :]
    xp_ref[0:V7X_SUBLANES, :] = xp_ref[ch:ch + V7X_SUBLANES, :]
    xc_ref[...] = _silu(acc)
    xc = xc_ref

    dt = _softplus(dt_ref[...] + dtb_ref[...])
    a = dt * (-jnp.exp(alog_ref[...]))
    def spread(v, ones_ref):
        hi = v.astype(BF16)
        rest = v - hi.astype(F32)
        mid = rest.astype(BF16)
        lo = (rest - mid.astype(F32)).astype(BF16)
        return _dot(hi, ones_ref[...]) + (_dot(mid, ones_ref[...]) + _dot(lo, ones_ref[...]))

    dt_wide = spread(dt, expand_ref)
    li = lax.broadcasted_iota(jnp.int32, (ch, ch), 0)
    si = lax.broadcasted_iota(jnp.int32, (ch, ch), 1)
    causal = li >= si
    tril = jnp.where(causal, 1.0, 0.0).astype(F32)
    acs = jnp.dot(tril, a, precision=HIGHEST, preferred_element_type=F32)
    acs_t = acs.T
    lane = lax.broadcasted_iota(jnp.int32, (ch, 2 * SSD_HEAD_DIM), 1)
    first = lane < SSD_HEAD_DIM
    first_n = lax.broadcasted_iota(jnp.int32, (SSD_STATE, 2 * SSD_HEAD_DIM), 1) < SSD_HEAD_DIM
    pairs_per_group = n_pairs // SSD_GROUPS

    for g in range(SSD_GROUPS):
        bg = xc[:, d_ssd + g * SSD_STATE:d_ssd + (g + 1) * SSD_STATE]
        cg = xc[:, d_ssd + gn + g * SSD_STATE:d_ssd + gn + (g + 1) * SSD_STATE]
        bg_t = bg.T
        cb = _dot(cg.astype(BF16), bg_t.astype(BF16))
        ys = []
        for q in range(pairs_per_group):
            pair = g * pairs_per_group + q
            h0 = 2 * pair
            lo = pair * 2 * SSD_HEAD_DIM
            xs = xc[:, lo:lo + 2 * SSD_HEAD_DIM]
            xdt = (xs * dt_wide[:, lo:lo + 2 * SSD_HEAD_DIM]).astype(BF16)
            s_prev = state_ref[pair]
            s_prev_b = s_prev.astype(BF16)
            y2, snew2, dec2 = [], [], []
            for r in range(2):
                h = h0 + r
                col = acs[:, h:h + 1]
                row = acs_t[h:h + 1, :]
                last = acs_t[h:h + 1, ch - 1:ch]
                decay = jnp.exp(jnp.where(causal, col - row, -jnp.inf))
                y = _dot((cb * decay).astype(BF16), xdt)
                y = y + _dot((cg * jnp.exp(col)).astype(BF16), s_prev_b)
                snew = _dot((bg_t * jnp.exp(last - row)).astype(BF16), xdt)
                y2.append(y)
                snew2.append(snew)
                dec2.append(jnp.exp(last))
            y = jnp.where(first, y2[0], y2[1])
            state_ref[pair] = (s_prev * jnp.where(first_n, dec2[0], dec2[1])
                               + jnp.where(first_n, snew2[0], snew2[1]))
            ys.append(y + dskip_ref[:, lo:lo + 2 * SSD_HEAD_DIM] * xs)
        gw = pairs_per_group * 2 * SSD_HEAD_DIM
        yg = jnp.concatenate(ys, axis=-1) * _silu(z_ref[:, g * gw:(g + 1) * gw])
        yg = yg * lax.rsqrt(jnp.mean(yg * yg, axis=-1, keepdims=True) + RMS_EPS)
        yn_ref[:, g * gw:(g + 1) * gw] = (yg * ng_ref[:, g * gw:(g + 1) * gw]).astype(yn_ref.dtype)


def _ssd_prompt(xbc, z, dt, conv_w, conv_b, dt_bias, a_log, d_skip, norm_g, expand, *, batch, seq):
    t, conv_dim = xbc.shape
    d_ssd = z.shape[1]
    n_pairs = d_ssd // (2 * SSD_HEAD_DIM)
    nc = seq // SSD_CHUNK
    n_taps = conv_w.shape[0]
    g = SSD_SEQS_PER_STEP if batch % SSD_SEQS_PER_STEP == 0 else 1
    tile = lambda n: pl.BlockSpec((1, g, SSD_CHUNK, n), lambda b, c: (b, 0, c, 0))
    by_seq = lambda a: a.reshape(batch // g, g, seq, a.shape[1])
    yn, st = pl.pallas_call(
        functools.partial(_ssd_prompt_kernel, d_ssd=d_ssd, n_taps=n_taps),
        grid=(batch // g, nc),
        in_specs=[tile(conv_dim), tile(d_ssd), tile(V7X_LANES), _resident(conv_w.shape), _resident(conv_b.shape),
                  _resident(dt_bias.shape), _resident(a_log.shape), _resident(d_skip.shape),
                  _resident(norm_g.shape), _resident(expand.shape)],
        out_specs=[tile(d_ssd),
                   pl.BlockSpec((g, n_pairs, 2 * SSD_HEAD_DIM, SSD_STATE), lambda b, c: (b, 0, 0, 0))],
        out_shape=[jax.ShapeDtypeStruct((batch // g, g, seq, d_ssd), BF16),
                   jax.ShapeDtypeStruct((batch, n_pairs, 2 * SSD_HEAD_DIM, SSD_STATE), F32)],
        scratch_shapes=[pltpu.VMEM((g, SSD_CHUNK + V7X_SUBLANES, conv_dim), F32),
                        pltpu.VMEM((g, n_pairs, SSD_STATE, 2 * SSD_HEAD_DIM), F32),
                        pltpu.VMEM((g, SSD_CHUNK, conv_dim), F32)],
        compiler_params=_params("parallel", "arbitrary"),
        name="ssd_prompt",
    )(by_seq(xbc), by_seq(z), by_seq(dt), conv_w, conv_b, dt_bias, a_log, d_skip, norm_g, expand)
    return yn.reshape(t, d_ssd), st


def _gmlp_gate(ug_ref, vn_ref, ws_ref, bs_ref, out_ref):
    ch = GMLP_CHUNK
    n_groups = ws_ref.shape[0]
    gd = vn_ref.shape[1] // n_groups
    ii = lax.broadcasted_iota(jnp.int32, (ch, ch), 0)
    jj = lax.broadcasted_iota(jnp.int32, (ch, ch), 1)
    for g in range(n_groups):
        ws = jnp.where(ii >= jj, ws_ref[g], 0.0).astype(BF16)
        cols = slice(g * gd, (g + 1) * gd)
        for r0 in range(0, vn_ref.shape[0], ch):
            rows = slice(r0, r0 + ch)
            mixed = _dot(ws, vn_ref[rows, cols].astype(BF16)) + bs_ref[:, g:g + 1]
            out_ref[rows, cols] = (ug_ref[rows, cols].astype(F32) * mixed).astype(out_ref.dtype)


def _ssd_sample_prep_kernel(xbc_ref, buf_ref, dt_ref, cw_ref, cb_ref, dtb_ref, alog_ref, expand_ref,
                            newbuf_ref, xs_ref, xdt_ref, bc_ref, da_ref, *, d_ssd, n_taps):
    x_new = xbc_ref[...]
    acc = cb_ref[...] + cw_ref[n_taps - 1:n_taps, :] * x_new
    for k in range(n_taps - 1):
        acc = acc + cw_ref[k:k + 1, :] * buf_ref[k]
    newbuf_ref[0:n_taps - 2] = buf_ref[1:n_taps - 1]
    newbuf_ref[n_taps - 2] = x_new
    xc = _silu(acc)
    xs = xc[:, :d_ssd]
    dt = _softplus(dt_ref[...] + dtb_ref[...])
    da_ref[...] = jnp.exp(dt * (-jnp.exp(alog_ref[...])))
    dt_wide = jnp.dot(dt, expand_ref[...], precision=HIGHEST, preferred_element_type=F32)
    xs_ref[...] = xs
    xdt_ref[...] = xs * dt_wide
    bc_ref[...] = xc[:, d_ssd:]


def _ssd_sample_prep(xbc, buf, dt, conv_w, conv_b, dt_bias, a_log, expand):
    n, conv_dim = xbc.shape
    d_ssd = expand.shape[1]
    n_taps = conv_w.shape[0]
    args = (xbc, buf, dt, conv_w, conv_b, dt_bias, a_log, expand)
    return pl.pallas_call(
        functools.partial(_ssd_sample_prep_kernel, d_ssd=d_ssd, n_taps=n_taps),
        grid=(1,),
        in_specs=[_resident(a.shape) for a in args],
        out_specs=[_whole(buf.shape), _whole((n, d_ssd)), _whole((n, d_ssd)),
                   _whole((n, conv_dim - d_ssd)), _whole((n, V7X_LANES))],
        out_shape=[jax.ShapeDtypeStruct(buf.shape, F32), jax.ShapeDtypeStruct((n, d_ssd), F32),
                   jax.ShapeDtypeStruct((n, d_ssd), F32), jax.ShapeDtypeStruct((n, conv_dim - d_ssd), F32),
                   jax.ShapeDtypeStruct((n, V7X_LANES), F32)],
        compiler_params=_params("arbitrary"),
        name="ssd_sample_prep",
    )(*args)


def _ssd_sample_state_kernel(h_ref, xdt_ref, da_ref, bc_ref, hnew_ref, y_ref, xcol_ref, ccol_ref):
    bs, n_heads, p, s = h_ref.shape
    n_pairs = n_heads // 2
    pairs_per_group = n_pairs // SSD_GROUPS
    gn = SSD_GROUPS * s
    w = 2 * p
    eye = lax.broadcasted_iota(jnp.int32, (w, w), 0) == lax.broadcasted_iota(jnp.int32, (w, w), 1)
    ones = jnp.ones((w, V7X_LANES), BF16)

    def stacked_diag(v):
        return jnp.concatenate(
            [jnp.where(eye, jnp.broadcast_to(v[j:j + 1], (w, w)), 0.0) for j in range(bs)], axis=0).astype(BF16)

    for q in range(n_pairs):
        x = xdt_ref[:, q * w:(q + 1) * w]
        hi = x.astype(BF16).astype(F32)
        mid = (x - hi).astype(BF16).astype(F32)
        lo = (x - hi) - mid
        xcol_ref[q] = _dot(stacked_diag(hi), ones) + (_dot(stacked_diag(mid), ones) + _dot(stacked_diag(lo), ones))
    for g in range(SSD_GROUPS):
        ccol_ref[g] = _dot(stacked_diag(bc_ref[:, gn + g * s:gn + (g + 1) * s]), ones).astype(BF16)

    for j in range(bs):
        for g in range(SSD_GROUPS):
            brow = bc_ref[j:j + 1, g * s:(g + 1) * s]
            h_new = []
            for q in range(g * pairs_per_group, (g + 1) * pairs_per_group):
                for r in range(2):
                    h = 2 * q + r
                    xcol = xcol_ref[q, j * w + r * p:j * w + (r + 1) * p, :]
                    h_new.append(h_ref[j, h] * da_ref[j:j + 1, h:h + 1] + xcol * brow)
                    hnew_ref[j, h] = h_new[-1]
            hc = _dot(jnp.concatenate(h_new, axis=0).astype(BF16), ccol_ref[g, j * s:(j + 1) * s, :])
            for i in range(pairs_per_group):
                y = jnp.sum(jnp.where(eye, hc[i * w:(i + 1) * w], 0.0), axis=0, keepdims=True)
                q = g * pairs_per_group + i
                y_ref[j, q:q + 1, :] = y


def _ssd_sample_state(h, xdt, da, bc):
    n, n_heads, p, s = h.shape
    assert 2 * p == V7X_LANES and s == V7X_LANES
    bs = SAMPLE_STATE_TILE
    rows = lambda w: pl.BlockSpec((bs, w), lambda i: (i, 0))
    return pl.pallas_call(
        _ssd_sample_state_kernel,
        grid=(n // bs,),
        in_specs=[pl.BlockSpec((bs, n_heads, p, s), lambda i: (i, 0, 0, 0)), rows(xdt.shape[1]), rows(da.shape[1]),
                  rows(bc.shape[1])],
        out_specs=[pl.BlockSpec((bs, n_heads, p, s), lambda i: (i, 0, 0, 0)),
                   pl.BlockSpec((bs, n_heads // 2, 2 * p), lambda i: (i, 0, 0))],
        out_shape=[jax.ShapeDtypeStruct(h.shape, F32), jax.ShapeDtypeStruct((n, n_heads // 2, 2 * p), F32)],
        scratch_shapes=[pltpu.VMEM((n_heads // 2, bs * 2 * p, V7X_LANES), F32),
                        pltpu.VMEM((SSD_GROUPS, bs * s, V7X_LANES), BF16)],
        compiler_params=_params("parallel"),
        name="ssd_sample_state",
    )(h, xdt, da, bc)


def _even_sample_post_kernel(y_ref, xs_ref, z_ref, dskip_ref, ng_ref, ug_ref, vn_ref, w0_ref, b0_ref,
                             yn_ref, gout_ref):
    d_ssd = y_ref.shape[1]
    gw = d_ssd // SSD_GROUPS
    y = (y_ref[...] + dskip_ref[...] * xs_ref[...]) * _silu(z_ref[...])
    for g in range(SSD_GROUPS):
        yg = y[:, g * gw:(g + 1) * gw]
        yg = yg * lax.rsqrt(jnp.mean(yg * yg, axis=-1, keepdims=True) + RMS_EPS)
        yn_ref[:, g * gw:(g + 1) * gw] = (yg * ng_ref[:, g * gw:(g + 1) * gw]).astype(yn_ref.dtype)
    mixed = w0_ref[...] * vn_ref[...] + b0_ref[...]
    gout_ref[...] = (ug_ref[...].astype(F32) * mixed).astype(gout_ref.dtype)


def _even_sample_post(y, xs, z, d_skip, norm_g, ug, vn, w0, b0):
    args = (y, xs, z, d_skip, norm_g, ug, vn, w0, b0)
    return pl.pallas_call(
        _even_sample_post_kernel,
        grid=(1,),
        in_specs=[_resident(a.shape) for a in args],
        out_specs=[_whole(y.shape), _whole(vn.shape)],
        out_shape=[jax.ShapeDtypeStruct(y.shape, BF16), jax.ShapeDtypeStruct(vn.shape, BF16)],
        compiler_params=_params("arbitrary"),
        name="even_sample_post",
    )(*args)


def _even_out_kernel(*refs, fuse_gmlp):
    if fuse_gmlp:
        (x_ref, yn_ref, ug_ref, vn_ref, ws_ref, bs_ref, wo_ref, g_ref, wg_ref, wu_ref, wd_ref, gc_ref, w1_ref,
         b1_ref, out_ref, a_ref, gout_ref) = refs
        _gmlp_gate(ug_ref, vn_ref, ws_ref, bs_ref, gout_ref)
    else:
        (x_ref, yn_ref, gout_ref, wo_ref, g_ref, wg_ref, wu_ref, wd_ref, gc_ref, w1_ref, b1_ref,
         out_ref, a_ref) = refs
    d_ssd = yn_ref.shape[1]
    x = x_ref[...] + _dot(yn_ref[...], wo_ref[0:d_ssd, :]) + _dot(gout_ref[...], wo_ref[d_ssd:, :])
    h = _rms(x, g_ref[...]).astype(BF16)
    mid = (_silu(_dot(h, wg_ref[...])) * _dot(h, wu_ref[...])).astype(BF16)
    x = x + _dot(mid, wd_ref[...])
    out_ref[...] = x
    dc = a_ref.shape[1]
    h = _rms(x, gc_ref[...]).astype(BF16)
    val = _dot(h, w1_ref[:, 0:dc]) + b1_ref[:, 0:dc]
    gate = _dot(h, w1_ref[:, dc:]) + b1_ref[:, dc:]
    a_ref[...] = val * _sigmoid(gate)


def _even_out(x, yn, gate_in, weights, *, seq=None):
    t, d = x.shape
    dc = weights[-2].shape[1] // 2
    fuse_gmlp = isinstance(gate_in, tuple)
    tm = _row_tile(t if seq is None else seq, 256)
    row = lambda n: pl.BlockSpec((tm, n), lambda i: (i, 0))
    if fuse_gmlp:
        ug, vn, w_s, b_s_t = gate_in
        assert tm % GMLP_CHUNK == 0
        gate_args = (ug, vn, w_s, b_s_t)
        gate_specs = [row(ug.shape[1]), row(vn.shape[1]), _resident(w_s.shape), _resident(b_s_t.shape)]
        scratch = [pltpu.VMEM((tm, ug.shape[1]), BF16)]
    else:
        gate_args, gate_specs, scratch = (gate_in,), [row(gate_in.shape[1])], []
    return pl.pallas_call(
        functools.partial(_even_out_kernel, fuse_gmlp=fuse_gmlp),
        grid=(t // tm,),
        in_specs=[row(d), row(yn.shape[1])] + gate_specs + [_resident(w.shape) for w in weights],
        out_specs=[row(d), row(dc)],
        out_shape=[jax.ShapeDtypeStruct((t, d), F32), jax.ShapeDtypeStruct((t, dc), F32)],
        scratch_shapes=scratch,
        compiler_params=_params("parallel"),
        name="even_out_ffn",
    )(x, yn, *gate_args, *weights)


def _route(logits, n_experts, route_ref, route_t_ref):
    lane = lax.broadcasted_iota(jnp.int32, logits.shape, 1)
    lane_f = lane.astype(F32)
    big = jnp.float32(V7X_LANES)
    lg = jnp.where(lane < n_experts, logits, -jnp.inf)
    m1 = jnp.max(lg, axis=-1, keepdims=True)
    i1 = jnp.min(jnp.where(lg == m1, lane_f, big), axis=-1, keepdims=True)
    lg2 = jnp.where(lane_f == i1, -jnp.inf, lg)
    m2 = jnp.max(lg2, axis=-1, keepdims=True)
    i2 = jnp.min(jnp.where(lg2 == m2, lane_f, big), axis=-1, keepdims=True)
    e2 = jnp.exp(m2 - m1)
    g1 = 1.0 / (1.0 + e2)
    g2 = e2 / (1.0 + e2)
    route = jnp.where(lane == 0, i1, jnp.where(lane == 1, i2, jnp.where(lane == 2, g1,
                      jnp.where(lane == 3, g2, 0.0))))
    route_ref[...] = route
    route_t_ref[...] = route.T[0:V7X_SUBLANES, :]


def _conf_finish(c, x, lng_ref, lnb_ref, w2_ref, b2_ref, ng_ref, rhi_ref, rlo_ref, out_ref, hn_ref, route_ref,
                 route_t_ref, *, n_experts):
    hmid = _silu(_layernorm(c, lng_ref[...], lnb_ref[...])).astype(BF16)
    x_new = x + _dot(hmid, w2_ref[...]) + b2_ref[...]
    out_ref[...] = x_new
    hn = _rms(x_new, ng_ref[...])
    hn_ref[...] = _pack_bf16_pairs(hn)
    hi = hn.astype(BF16)
    lo = (hn - hi.astype(F32)).astype(BF16)
    logits = _dot(hi, rhi_ref[...]) + (_dot(lo, rhi_ref[...]) + _dot(hi, rlo_ref[...]))
    _route(logits, n_experts, route_ref, route_t_ref)


def _conf_prompt_kernel(a_ref, x_ref, xs_ref, hns_ref, routes_ref, dw_ref, db_ref, lng_ref, lnb_ref, w2_ref, b2_ref,
                        ng_ref, rhi_ref, rlo_ref, out_ref, hn_ref, route_ref, route_t_ref, xp_ref, c_ref, win_ref,
                        *, n_taps, n_experts, tiles_per_seq, n_prompt_tiles):
    tl, d = a_ref.shape
    halo = xp_ref.shape[0] - tl
    base = halo - (n_taps - 1)
    rows = CONF_ACC_ROWS
    step = pl.program_id(0)

    @pl.when(step < n_prompt_tiles)
    def _():
        @pl.when(step % tiles_per_seq == 0)
        def _():
            xp_ref[0:halo, :] = jnp.zeros((halo, d), F32)

        xp_ref[halo:halo + tl, :] = a_ref[...]

        def lane_block(j, carry):
            cols = pl.ds(pl.multiple_of(j * V7X_LANES, V7X_LANES), V7X_LANES)
            for r0 in range(0, tl, rows):
                acc = jnp.broadcast_to(db_ref[:, cols], (rows, V7X_LANES))
                for phase in range(V7X_SUBLANES):
                    taps = [k for k in range(n_taps) if (base + k) % V7X_SUBLANES == phase]
                    if not taps:
                        continue
                    span = max(base + k - phase for k in taps) + rows
                    if phase:
                        win_ref[0:span, :] = xp_ref[pl.ds(r0 + phase, span), cols]
                    for k in taps:
                        o = base + k - phase
                        win = win_ref[o:o + rows, :] if phase else xp_ref[pl.ds(r0 + o, rows), cols]
                        acc = acc + dw_ref[k:k + 1, cols] * win
                c_ref[r0:r0 + rows, cols] = acc
            return carry

        lax.fori_loop(0, d // V7X_LANES, lane_block, 0)
        xp_ref[0:halo, :] = xp_ref[tl:tl + halo, :]
        _conf_finish(c_ref[...], x_ref[...], lng_ref, lnb_ref, w2_ref, b2_ref, ng_ref, rhi_ref, rlo_ref,
                     out_ref, hn_ref, route_ref, route_t_ref, n_experts=n_experts)

    @pl.when(step == n_prompt_tiles)
    def _():
        ns = xs_ref.shape[0]
        out_ref[0:ns, :] = xs_ref[...]
        hn_ref[0:ns, :] = hns_ref[...]
        route_ref[0:ns, :] = routes_ref[...]


def _conf_prompt(a, x, sample_rows, consts, *, seq, n_experts):
    t, d = a.shape
    n_s = sample_rows[0].shape[0]
    tl = _row_tile(seq, CONF_TILE)
    assert tl % CONF_ACC_ROWS == 0 and n_s <= tl
    n_prompt_tiles = t // tl
    n_taps = consts[0].shape[0]
    halo = -(-(n_taps - 1) // V7X_SUBLANES) * V7X_SUBLANES
    in_tile = lambda n: pl.BlockSpec((tl, n), lambda s: (jnp.minimum(s, n_prompt_tiles - 1), 0))
    out_tile = lambda n: pl.BlockSpec((tl, n), lambda s: (s, 0))
    small = sample_rows + tuple(consts)
    return pl.pallas_call(
        functools.partial(_conf_prompt_kernel, n_taps=n_taps, n_experts=n_experts, tiles_per_seq=seq // tl,
                          n_prompt_tiles=n_prompt_tiles),
        grid=(n_prompt_tiles + 1,),
        in_specs=[in_tile(d), in_tile(d)] + [_resident(c.shape) for c in small],
        out_specs=[out_tile(d), out_tile(d // 2), out_tile(V7X_LANES),
                   pl.BlockSpec((V7X_SUBLANES, tl), lambda s: (0, jnp.minimum(s, n_prompt_tiles - 1)))],
        out_shape=[jax.ShapeDtypeStruct((t + n_s, d), F32), jax.ShapeDtypeStruct((t + n_s, d // 2), jnp.uint32),
                   jax.ShapeDtypeStruct((t + n_s, V7X_LANES), F32), jax.ShapeDtypeStruct((V7X_SUBLANES, t), F32)],
        scratch_shapes=[pltpu.VMEM((tl + halo, d), F32), pltpu.VMEM((tl, d), F32),
                        pltpu.VMEM((CONF_ACC_ROWS + halo, V7X_LANES), F32)],
        compiler_params=_params("arbitrary"),
        name="conf_prompt",
    )(a, x, *small)


def _conf_sample_conv_kernel(a_ref, buf_ref, dw_ref, db_ref, newbuf_ref, c_ref, *, n_taps):
    a_new = a_ref[...]
    acc = db_ref[...] + dw_ref[n_taps - 1:n_taps, :] * a_new
    for k in range(n_taps - 1):
        acc = acc + dw_ref[k:k + 1, :] * buf_ref[k]
    c_ref[...] = acc
    newbuf_ref[0:n_taps - 2] = buf_ref[1:n_taps - 1]
    newbuf_ref[n_taps - 2] = a_new


def _conf_sample_conv(a, buf, dw_w, dw_b):
    n, d = a.shape
    bs = CONF_SAMPLE_TILE
    hist = pl.BlockSpec((buf.shape[0], bs, d), lambda i: (0, i, 0))
    rows = pl.BlockSpec((bs, d), lambda i: (i, 0))
    return pl.pallas_call(
        functools.partial(_conf_sample_conv_kernel, n_taps=dw_w.shape[0]),
        grid=(n // bs,),
        in_specs=[rows, hist, _resident(dw_w.shape), _resident(dw_b.shape)],
        out_specs=[hist, rows],
        out_shape=[jax.ShapeDtypeStruct(buf.shape, F32), jax.ShapeDtypeStruct((n, d), F32)],
        compiler_params=_params("parallel"),
        name="conf_sample_conv",
    )(a, buf, dw_w, dw_b)


def _conf_sample_kernel(c_ref, x_ref, lng_ref, lnb_ref, w2_ref, b2_ref, ng_ref, rhi_ref, rlo_ref,
                        out_ref, hn_ref, route_ref, route_t_ref, *, n_experts):
    _conf_finish(c_ref[...], x_ref[...], lng_ref, lnb_ref, w2_ref, b2_ref, ng_ref, rhi_ref, rlo_ref,
                 out_ref, hn_ref, route_ref, route_t_ref, n_experts=n_experts)


def _conf_sample(c, x, consts, *, n_experts):
    n, d = c.shape
    args = (c, x) + tuple(consts)
    return pl.pallas_call(
        functools.partial(_conf_sample_kernel, n_experts=n_experts),
        grid=(1,),
        in_specs=[_resident(v.shape) for v in args],
        out_specs=[_whole((n, d)), _whole((n, d // 2)), _whole((n, V7X_LANES)), _whole((V7X_SUBLANES, n))],
        out_shape=[jax.ShapeDtypeStruct((n, d), F32), jax.ShapeDtypeStruct((n, d // 2), jnp.uint32),
                   jax.ShapeDtypeStruct((n, V7X_LANES), F32), jax.ShapeDtypeStruct((V7X_SUBLANES, n), F32)],
        compiler_params=_params("arbitrary"),
        name="conf_sample",
    )(*args)


def _sc_window_rows(x):
    assert x.dtype.itemsize == 4, "the SparseCore indirect copies move 32-bit elements"
    return SC_WINDOW_BYTES // (x.shape[1] * x.dtype.itemsize)


def _sc_gather_rows(x, idx):
    n, d = idx.shape[0], x.shape[1]
    w = _sc_window_rows(x)
    assert n % w == 0
    mesh = plsc.VectorSubcoreMesh(core_axis_name="core", subcore_axis_name="subcore")

    @functools.partial(pl.kernel, out_type=jax.ShapeDtypeStruct((n, d), x.dtype), mesh=mesh, scratch_types=[])
    def gather(x_hbm, idx_hbm, out_hbm):
        def body(idx_vmem, out_vmem):
            pltpu.sync_copy(x_hbm.at[idx_vmem.at[0]], out_vmem)

        pltpu.emit_pipeline(
            body,
            grid=(n // w,),
            in_specs=[pl.BlockSpec((1, w), lambda i: (i, 0))],
            out_specs=[pl.BlockSpec((w, d), lambda i: (i, 0))],
            core_axis_name=("core", "subcore"),
            dimension_semantics=(pltpu.PARALLEL,),
        )(idx_hbm, out_hbm)

    return gather(x, idx.reshape(n // w, w))


def _sc_dispatch_rows(x, pos, pad_pos, n_out):
    t, d = x.shape
    w = _sc_window_rows(x)
    n_pad = pad_pos.shape[0]
    assert t % w == 0 and n_pad % w == 0 and TOP_K * t + n_pad == n_out
    mesh = plsc.VectorSubcoreMesh(core_axis_name="core", subcore_axis_name="subcore")

    @functools.partial(pl.kernel, out_type=jax.ShapeDtypeStruct((n_out, d), x.dtype), mesh=mesh, scratch_types=[])
    def dispatch(x_hbm, pos_hbm, pad_hbm, zero_hbm, out_hbm):
        def body(rows_vmem, idx_vmem):
            pltpu.sync_copy(rows_vmem, out_hbm.at[idx_vmem.at[0]])

        def scatter(src_hbm, idx_hbm, src_map):
            pltpu.emit_pipeline(
                body,
                grid=(idx_hbm.shape[0],),
                in_specs=[pl.BlockSpec((w, d), src_map), pl.BlockSpec((1, w), lambda i: (i, 0))],
                out_specs=[],
                core_axis_name=("core", "subcore"),
                dimension_semantics=(pltpu.PARALLEL,),
            )(src_hbm, idx_hbm)

        for k in range(TOP_K):
            scatter(x_hbm, pos_hbm.at[k], lambda i: (i, 0))
        scatter(zero_hbm, pad_hbm, lambda i: (0, 0))

    return dispatch(x, pos.reshape(TOP_K, t // w, w), pad_pos.reshape(n_pad // w, w), jnp.zeros((w, d), x.dtype))


def _moe_ffn_kernel(tile_expert_ref, first_ref, bank_ref, next_expert_ref, n_tiles_ref, x_ref, wg_hbm, wu_hbm,
                    wd_hbm, out_ref, wg_v, wu_v, wd_v, stage_g, stage_u, stage_d, acc_ref, sems):
    i = pl.program_id(0)
    dff = wg_v.shape[2]
    ck = stage_g.shape[2]
    n_chunks = dff // ck
    active = i < n_tiles_ref[0]
    starts_run = active & (first_ref[i] == 1)
    bank = bank_ref[i]
    nxt = next_expert_ref[i]

    def chunk_copies(expert, c, slot):
        cols = slice(c * ck, (c + 1) * ck)
        return (pltpu.make_async_copy(wg_hbm.at[expert, :, cols], stage_g.at[slot], sems.at[0, slot]),
                pltpu.make_async_copy(wu_hbm.at[expert, :, cols], stage_u.at[slot], sems.at[1, slot]),
                pltpu.make_async_copy(wd_hbm.at[expert, cols, :], stage_d.at[slot], sems.at[2, slot]))

    def stream_weights(expert, dst_bank, between_chunks):
        for cp in chunk_copies(expert, 0, 0):
            cp.start()
        for c in range(n_chunks):
            slot = c % 2
            if c + 1 < n_chunks:
                for cp in chunk_copies(expert, c + 1, 1 - slot):
                    cp.start()
            for cp in chunk_copies(expert, c, slot):
                cp.wait()
            cols = slice(c * ck, (c + 1) * ck)
            wg_v[dst_bank, :, cols] = stage_g[slot].astype(BF16)
            wu_v[dst_bank, :, cols] = stage_u[slot].astype(BF16)
            wd_v[dst_bank, cols, :] = stage_d[slot].astype(BF16)
            between_chunks(c)

    def ffn_part(xb, cols):
        mid = (_silu(_dot(xb, wg_v[bank, :, cols])) * _dot(xb, wu_v[bank, :, cols])).astype(BF16)
        return _dot(mid, wd_v[bank, cols, :])

    @pl.when(starts_run & (i == 0))
    def _():
        stream_weights(tile_expert_ref[i], bank, lambda c: None)

    overlapped = starts_run & (nxt >= 0)

    @pl.when(overlapped)
    def _():
        xb = _unpack_bf16_pairs(x_ref[...]).astype(BF16)
        acc_ref[...] = jnp.zeros_like(acc_ref)

        def piece(c):
            acc_ref[...] += ffn_part(xb, slice(c * ck, (c + 1) * ck))

        stream_weights(nxt, 1 - bank, piece)
        out_ref[...] = _pack_bf16_pairs(acc_ref[...])

    @pl.when(active & jnp.logical_not(overlapped))
    def _():
        xb = _unpack_bf16_pairs(x_ref[...]).astype(BF16)
        half = dff // 2
        acc = ffn_part(xb, slice(0, half)) + ffn_part(xb, slice(half, dff))
        out_ref[...] = _pack_bf16_pairs(acc)

    @pl.when(jnp.logical_not(active))
    def _():
        out_ref[...] = jnp.zeros_like(out_ref)


def _moe_ffn(tile_expert, n_tiles, xs, wg, wu, wd, *, tm):
    p, d = xs.shape[0], wg.shape[1]
    dff = wg.shape[2]
    ck = MOE_FF_CHUNK
    assert dff % ck == 0
    nt = p // tm
    i32 = jnp.int32
    tile = jnp.arange(nt, dtype=i32)
    first = jnp.concatenate([jnp.ones((1,), i32), (tile_expert[1:] != tile_expert[:-1]).astype(i32)])
    bank = (jnp.cumsum(first) - 1) % 2
    run_start = jnp.where((first == 1) & (tile < n_tiles[0]), tile, nt)
    next_start = jnp.concatenate([lax.cummin(run_start, reverse=True)[1:], jnp.full((1,), nt, i32)])
    next_expert = jnp.sum((next_start[:, None] == tile[None, :]).astype(i32) * tile_expert[None, :], axis=1)
    next_expert = jnp.where(next_start < nt, next_expert, -1)
    any_spec = pl.BlockSpec(memory_space=pl.ANY)
    row_tile = pl.BlockSpec((tm, d // 2), lambda i, *_: (i, 0))
    grid_spec = pltpu.PrefetchScalarGridSpec(
        num_scalar_prefetch=5,
        grid=(nt,),
        in_specs=[row_tile, any_spec, any_spec, any_spec],
        out_specs=row_tile,
        scratch_shapes=[pltpu.VMEM((2, d, dff), BF16), pltpu.VMEM((2, d, dff), BF16), pltpu.VMEM((2, dff, d), BF16),
                        pltpu.VMEM((2, d, ck), F32), pltpu.VMEM((2, d, ck), F32), pltpu.VMEM((2, ck, d), F32),
                        pltpu.VMEM((tm, d), F32), pltpu.SemaphoreType.DMA((3, 2))],
    )
    return pl.pallas_call(
        _moe_ffn_kernel,
        grid_spec=grid_spec,
        out_shape=jax.ShapeDtypeStruct((p, d // 2), jnp.uint32),
        compiler_params=_params("arbitrary"),
        name="moe_ffn",
    )(tile_expert, first, bank.astype(i32), next_expert.astype(i32), n_tiles, xs, wg, wu, wd)


def _moe_combine_kernel(x_ref, route_ref, gf_ref, a_ref, b_ref, y_ref):
    route = route_ref[...]
    x = (x_ref[...] + route[:, 2:3] * _unpack_bf16_pairs(a_ref[...])
         + route[:, 3:4] * _unpack_bf16_pairs(b_ref[...]))
    y_ref[...] = _rms(x, gf_ref[...])


def _moe_combine(x, route, gf, picked, *, row0, n_rows, a_row0, b_row0):
    d = x.shape[1]
    tc = _row_tile(n_rows, COMBINE_TILE)
    assert row0 % tc == 0 and a_row0 % tc == 0 and b_row0 % tc == 0
    rows = lambda w, r0: pl.BlockSpec((tc, w), lambda i: (i + r0 // tc, 0))
    return pl.pallas_call(
        _moe_combine_kernel,
        grid=(n_rows // tc,),
        in_specs=[rows(d, row0), rows(V7X_LANES, row0), _resident(gf.shape), rows(d // 2, a_row0),
                  rows(d // 2, b_row0)],
        out_specs=pl.BlockSpec((tc, d), lambda i: (i, 0)),
        out_shape=jax.ShapeDtypeStruct((n_rows, d), F32),
        compiler_params=_params("parallel"),
        name="moe_combine",
    )(x, route, gf, picked, picked)


def _moe_plan(route_t, n_experts, tm):
    t = route_t.shape[1]
    n_slots = t * TOP_K
    i32 = jnp.int32
    expert = route_t[:TOP_K].astype(i32).reshape(1, n_slots)
    onehot = (expert == jnp.arange(n_experts, dtype=i32)[:, None]).astype(i32)
    counts = jnp.sum(onehot, axis=1)
    tiles = (counts + tm - 1) // tm
    tile_end = jnp.cumsum(tiles)
    start = (tile_end - tiles) * tm
    rank = jnp.cumsum(onehot, axis=1) - onehot
    pos = jnp.sum(onehot * (rank + start[:, None]), axis=0)
    nt = -(-n_slots // tm) + n_experts
    n_tiles = tile_end[-1]
    tile_id = jnp.minimum(jnp.arange(nt, dtype=i32), n_tiles - 1)
    tile_expert = jnp.sum((tile_id[:, None] >= tile_end[None, :]).astype(i32), axis=1)
    seg_start = jnp.concatenate([start + counts, (n_tiles * tm)[None]])
    seg_len = jnp.concatenate([tiles * tm - counts, (nt * tm - n_tiles * tm)[None]])
    seg_end = jnp.cumsum(seg_len)
    q = jnp.arange(nt * tm - n_slots, dtype=i32)
    in_seg = (q[:, None] >= (seg_end - seg_len)[None, :]) & (q[:, None] < seg_end[None, :])
    pad_pos = jnp.sum(in_seg.astype(i32) * (seg_start - (seg_end - seg_len))[None, :], axis=1) + q
    return (tile_expert.astype(i32), n_tiles.astype(i32).reshape(1), pos.reshape(TOP_K, t).astype(i32),
            pad_pos.astype(i32), nt * tm)


def _row(v):
    return v.reshape(1, -1).astype(F32)


def _pad_lanes(v):
    return jnp.pad(_row(v), ((0, 0), (0, V7X_LANES - v.shape[-1])))


def kernel(x_prompt, x_sample, state_ssm, state_ssd_conv, state_conf_conv, norm_mix_even, w_in_even, ssd_conv_w, ssd_conv_b, ssd_dt_bias, ssd_a_log, ssd_d, ssd_norm, gmlp_ln_g, gmlp_ln_b, gmlp_w_s, gmlp_b_s, w_out_even, norm_ffn_even, ffn_w_gate, ffn_w_up, ffn_w_down, norm_mix_odd, conf_w1, conf_b1, conf_dw_w, conf_dw_b, conf_ln_g, conf_ln_b, conf_w2, conf_b2, norm_ffn_odd, moe_router, moe_w_gate, moe_w_up, moe_w_down, final_norm):
    batch, seq, d_model = x_prompt.shape
    n_dec, dec_seq, _ = x_sample.shape
    assert dec_seq == 1, "the sample group advances one token per sequence"
    assert seq % SSD_CHUNK == 0 and n_dec % V7X_LANES == 0
    n_even, n_odd = w_in_even.shape[0], conf_w1.shape[0]
    assert (n_even, n_odd) == (1, 1), "the final norm is fused into the last (odd) layer's MoE combine"
    n_heads = ssd_dt_bias.shape[1]
    d_ssd = n_heads * SSD_HEAD_DIM
    conv_dim = ssd_conv_w.shape[2]
    d_gmlp = gmlp_ln_g.shape[1]
    n_groups_gmlp = gmlp_w_s.shape[1]
    n_experts = moe_router.shape[2]
    tp = batch * seq

    xp = x_prompt.reshape(tp, d_model)
    xs = x_sample.reshape(n_dec, d_model)
    outs = dict(ssm_p=[], conv_p=[], conf_p=[], ssm_s=[], conv_s=[], conf_s=[], v_s=[])

    for layer in range(n_even + n_odd):
        i = layer // 2
        if layer % 2 == 0:
            w = w_in_even[i]
            o2, o3 = d_ssd + conv_dim, d_ssd + conv_dim + n_heads
            w_parts = (w[:, :o2].astype(BF16), w[:, o3:].astype(BF16),
                       jnp.pad(w[:, o2:o3], ((0, 0), (0, V7X_LANES - n_heads))).astype(BF16))
            in_args = (_row(norm_mix_even[i]), *w_parts, _row(gmlp_ln_g[i]), _row(gmlp_ln_b[i]))
            dims = dict(d_ssd=d_ssd, conv_dim=conv_dim, d_gmlp=d_gmlp)
            dt_bias, a_log = _pad_lanes(ssd_dt_bias[i]), _pad_lanes(ssd_a_log[i])
            d_skip = _row(jnp.repeat(ssd_d[i], SSD_HEAD_DIM))
            norm_g = _row(ssd_norm[i])
            conv_w, conv_b = ssd_conv_w[i], _row(ssd_conv_b[i])
            z, xbc, ug, vn, dt = _even_in(xp, *in_args, **dims)
            expand = (jnp.arange(d_ssd)[None, :] // SSD_HEAD_DIM == jnp.arange(V7X_LANES)[:, None]).astype(F32)
            yn, st = _ssd_prompt(xbc, z, dt, conv_w, conv_b, dt_bias, a_log, d_skip, norm_g, expand.astype(BF16),
                                 batch=batch, seq=seq)
            outs['ssm_p'].append(st.reshape(batch, n_heads, SSD_HEAD_DIM, SSD_STATE))
            outs['conv_p'].append(xbc.reshape(batch, seq, conv_dim)[:, seq - (conv_w.shape[0] - 1):])
            tail_w = (w_out_even[i].astype(BF16), _row(norm_ffn_even[i]), ffn_w_gate[i].astype(BF16),
                      ffn_w_up[i].astype(BF16), ffn_w_down[i].astype(BF16), _row(norm_mix_odd[i]),
                      conf_w1[i].astype(BF16), _row(conf_b1[i]))
            xp, a_p = _even_out(xp, yn, (ug, vn, gmlp_w_s[i], gmlp_b_s[i].T), tail_w, seq=seq)
            z, xbc, ug, vn, dt = _even_in(xs, *in_args, **dims)
            newbuf, xc, xdt, bc, da = _ssd_sample_prep(xbc, state_ssd_conv[i].transpose(1, 0, 2), dt, conv_w, conv_b,
                                                        dt_bias, a_log, expand)
            h_new, y4 = _ssd_sample_state(state_ssm[i], xdt, da, bc)
            gd = d_gmlp // n_groups_gmlp
            w0 = _row(jnp.repeat(gmlp_w_s[i][:, 0, 0], gd))
            b0 = _row(jnp.repeat(gmlp_b_s[i][:, 0], gd))
            yn, gout = _even_sample_post(y4.reshape(n_dec, d_ssd), xc, z, d_skip, norm_g, ug, vn, w0, b0)
            outs['ssm_s'].append(h_new)
            outs['conv_s'].append(newbuf.transpose(1, 0, 2))
            outs['v_s'].append(vn.reshape(n_dec, 1, d_gmlp))
            xs, a_s = _even_out(xs, yn, gout, tail_w)
        else:
            router = jnp.pad(moe_router[i], ((0, 0), (0, V7X_LANES - n_experts)))
            router_hi = router.astype(BF16)
            router_lo = (router - router_hi.astype(F32)).astype(BF16)
            tail = (conf_dw_w[i], _row(conf_dw_b[i]), _row(conf_ln_g[i]), _row(conf_ln_b[i]),
                    conf_w2[i].astype(BF16), _row(conf_b2[i]), _row(norm_ffn_odd[i]), router_hi, router_lo)
            n_keep = conf_dw_w.shape[1] - 1
            outs['conf_p'].append(a_p.reshape(batch, seq, -1)[:, seq - n_keep:])
            newbuf, c_s = _conf_sample_conv(a_s, state_conf_conv[i].transpose(1, 0, 2), tail[0], tail[1])
            outs['conf_s'].append(newbuf.transpose(1, 0, 2))
            *sample_rows, route_t_s = _conf_sample(c_s, xs, tail[2:], n_experts=n_experts)
            x_all, hn, route, route_t_p = _conf_prompt(a_p, xp, tuple(sample_rows), tail, seq=seq,
                                                       n_experts=n_experts)
            route_t = jnp.concatenate([route_t_p, route_t_s], axis=1)
            tile_expert, n_tiles, pos, pad_pos, n_rows = _moe_plan(route_t, n_experts, MOE_TILE)
            xs_sorted = _sc_dispatch_rows(hn, pos, pad_pos, n_rows)
            ys = _moe_ffn(tile_expert, n_tiles, xs_sorted, moe_w_gate[i], moe_w_up[i], moe_w_down[i], tm=MOE_TILE)
            picked = _sc_gather_rows(ys, jnp.concatenate([pos[:, :tp].reshape(-1), pos[:, tp:].reshape(-1)]))
            gf = _row(final_norm)
            xp = _moe_combine(x_all, route, gf, picked, row0=0, n_rows=tp, a_row0=0, b_row0=tp)
            xs = _moe_combine(x_all, route, gf, picked, row0=tp, n_rows=n_dec, a_row0=TOP_K * tp,
                              b_row0=TOP_K * tp + n_dec)

    y_prompt = xp.reshape(batch, seq, d_model)
    y_sample = xs.reshape(n_dec, 1, d_model)
    return (y_prompt, y_sample, jnp.stack(outs['ssm_p']), jnp.stack(outs['conv_p']), jnp.stack(outs['conf_p']),
            jnp.stack(outs['ssm_s']), jnp.stack(outs['conv_s']), jnp.stack(outs['conf_s']), jnp.stack(outs['v_s']))
```
